```python
import math
import jax, jax.numpy as jnp
from jax import lax
import numpy as np

D_MODEL = 1024
BATCH = 4
SEQ = 4096
DEPTH = 2

GRID_W = 64
CTX_LEN = 256
N_MOD = 9
D_FF = 2816
EPS = 1e-6
ROPE_BASE = 10000.0
RET_WIDTH = D_MODEL // 2
RET_HEADS = 4
RET_HEAD_DIM = RET_WIDTH // RET_HEADS
RET_CHUNK = 128
S5_WIDTH = D_MODEL - RET_WIDTH
S5_GROUP = 16
S5_GROUPS = S5_WIDTH // S5_GROUP
S5_STATE = 64
S5_DT_MIN = 0.001
S5_DT_MAX = 0.1
AB_IN_WIDTH = 4 * RET_WIDTH + S5_WIDTH
NA_HEADS = 16
NA_HEAD_DIM = D_MODEL // NA_HEADS
NA_KH_MAX = 8
NA_KW = 16
NA_QBLOCK_W = 16
NA_KSPAN_W = 2 * NA_KW
NEG_INF = -1e30

kernel_name = "hybrid_retention_s5_natten_macaron_dit"


def rms_norm(x, g):
    xf = x.astype(jnp.float32)
    y = xf * lax.rsqrt(jnp.mean(xf * xf, axis=-1, keepdims=True) + EPS)
    return (y * g.astype(jnp.float32)).astype(x.dtype)


def modulate(x, g, shift, scale):
    return rms_norm(x, g) * (1 + scale) + shift


def swiglu(h, w1, w2):
    a, b = jnp.split(h @ w1, 2, axis=-1)
    return (jax.nn.silu(a) * b) @ w2


def axial_rope(x, row, col):
    dh = x.shape[-1]
    half = dh // 2
    quarter = half // 2
    inv = ROPE_BASE ** (-jnp.arange(0, half, 2, dtype=jnp.float32) / half)

    def rot(xp, pos):
        ang = pos.astype(jnp.float32)[:, None] * inv[None, :]
        cos = jnp.cos(ang)[None, :, None, :].astype(x.dtype)
        sin = jnp.sin(ang)[None, :, None, :].astype(x.dtype)
        x1, x2 = xp[..., :quarter], xp[..., quarter:]
        return jnp.concatenate([x1 * cos - x2 * sin, x2 * cos + x1 * sin], axis=-1)

    return jnp.concatenate([rot(x[..., :half], row), rot(x[..., half:], col)], axis=-1)


def retention_scan(q, k, v, log_gamma, s0):
    B, H, L, dk = q.shape
    dv = v.shape[-1]
    C = RET_CHUNK
    N = L // C
    qc = q.reshape(B, H, N, C, dk)
    kc = k.reshape(B, H, N, C, dk)
    vc = v.reshape(B, H, N, C, dv)
    pos = jnp.arange(C, dtype=jnp.float32)
    lg = log_gamma[:, None]
    diff = pos[:, None] - pos[None, :]
    dmask = jnp.where(diff >= 0, jnp.exp(lg[:, :, None] * jnp.maximum(diff, 0.0)), 0.0)
    inner = jnp.einsum('bhnid,bhnjd->bhnij', qc, kc) * dmask[None, :, None]
    o_inner = jnp.einsum('bhnij,bhnje->bhnie', inner, vc)
    k_w = jnp.exp(lg * (C - 1 - pos))
    kv = jnp.einsum('bhnjd,bhnje->nbhde', kc * k_w[None, :, None, :, None], vc)
    chunk_decay = jnp.exp(log_gamma * C)[None, :, None, None]

    def step(s, kv_n):
        return chunk_decay * s + kv_n, s

    s_final, s_prev = lax.scan(step, s0, kv)
    q_w = jnp.exp(lg * (pos + 1.0))
    o_cross = jnp.einsum('bhnid,nbhde->bhnie', qc * q_w[None, :, None, :, None], s_prev)
    return (o_inner + o_cross).reshape(B, H, L, dv), s_final


def retention_mixer(q_l, k_l, v_l, g_l, q_c, k_c, v_c, g_c, decay_logit, row, col, need_ctx):
    def heads(t):
        return t.reshape(t.shape[0], t.shape[1], RET_HEADS, RET_HEAD_DIM)

    def tr(t):
        return jnp.swapaxes(t, 1, 2).astype(jnp.float32)

    kscale = RET_HEAD_DIM ** -0.5
    ql = tr(axial_rope(heads(q_l), row, col))
    kl = tr(axial_rope(heads(k_l), row, col) * kscale)
    vl = tr(heads(v_l))
    qc = tr(heads(q_c))
    kc = tr(heads(k_c) * kscale)
    vc = tr(heads(v_c))
    log_gamma = jax.nn.log_sigmoid(decay_logit.astype(jnp.float32))
    s0 = jnp.zeros((q_l.shape[0], RET_HEADS, RET_HEAD_DIM, RET_HEAD_DIM), jnp.float32)

    def flip(t):
        return jnp.flip(t, axis=2)

    oc_f, sc_f = retention_scan(qc, kc, vc, log_gamma[0], s0)
    oc_b, sc_b = retention_scan(flip(qc), flip(kc), flip(vc), log_gamma[1], s0)
    ol_f, _ = retention_scan(ql, kl, vl, log_gamma[0], sc_f)
    ol_b, _ = retention_scan(flip(ql), flip(kl), flip(vl), log_gamma[1], sc_b)

    def finish(o, g):
        o = o * lax.rsqrt(jnp.mean(o * o, axis=-1, keepdims=True) + EPS)
        Bo, Ho, Lo, do = o.shape
        o = jnp.swapaxes(o, 1, 2).reshape(Bo, Lo, Ho * do)
        return (o * jax.nn.silu(g.astype(jnp.float32))).astype(g.dtype)

    y_l = finish(ol_f + flip(ol_b), g_l)
    y_c = finish(oc_f + flip(oc_b), g_c) if need_ctx else None
    return y_l, y_c


def _ssm_combine(e1, e2):
    a1, b1 = e1
    a2, b2 = e2
    return a1 * a2, a2 * b1 + b2


def ssm_scan(lam_bar, bu):
    a = jnp.broadcast_to(lam_bar, bu.shape)
    _, xs = lax.associative_scan(_ssm_combine, (a, bu), axis=1)
    return xs


def s5_mixer(u_lat, u_ctx, lam_re, lam_im, log_dt, b_re, b_im, c_re, c_im, d_skip, glu_w, glu_b, need_ctx):
    B, L, _ = u_lat.shape
    Lc = u_ctx.shape[1]
    ul = u_lat.astype(jnp.float32).reshape(B, L, S5_GROUPS, S5_GROUP)
    uc = u_ctx.astype(jnp.float32).reshape(B, Lc, S5_GROUPS, S5_GROUP)
    dsk = d_skip.astype(jnp.float32).reshape(S5_GROUPS, S5_GROUP)
    ys_l = dsk * ul
    ys_c = dsk * uc
    for direction in range(2):
        lam = lax.complex(jnp.minimum(lam_re[direction].astype(jnp.float32), -1e-4),
                          lam_im[direction].astype(jnp.float32))
        dt = jnp.exp(log_dt[direction].astype(jnp.float32))[:, None]
        lam_bar = jnp.exp(lam * dt)
        b_bar = ((lam_bar - 1.0) / lam)[..., None] * lax.complex(
            b_re[direction].astype(jnp.float32), b_im[direction].astype(jnp.float32))
        c_mat = lax.complex(c_re[direction].astype(jnp.float32), c_im[direction].astype(jnp.float32))
        if direction == 0:
            order = lambda t: t
        else:
            order = lambda t: jnp.flip(t, axis=1)
        bu_c = jnp.einsum('gpk,blgk->blgp', b_bar, order(uc).astype(jnp.complex64))
        x_c = ssm_scan(lam_bar, bu_c)
        bu_l = jnp.einsum('gpk,blgk->blgp', b_bar, order(ul).astype(jnp.complex64))
        bu_l = bu_l.at[:, 0].add(lam_bar * x_c[:, -1])
        x_l = ssm_scan(lam_bar, bu_l)
        ys_l = ys_l + order(jnp.einsum('gkp,blgp->blgk', c_mat, x_l).real)
        if need_ctx:
            ys_c = ys_c + order(jnp.einsum('gkp,blgp->blgk', c_mat, x_c).real)

    def glu(y, Ly):
        y = jax.nn.gelu(y.reshape(B, Ly, S5_WIDTH))
        y = y * jax.nn.sigmoid(y @ glu_w.astype(jnp.float32) + glu_b.astype(jnp.float32))
        return y.astype(u_lat.dtype)

    y_l = glu(ys_l, L)
    y_c = glu(ys_c, Lc) if need_ctx else None
    return y_l, y_c


def ab_mixer(h_lat, h_ctx, w_in, w_out, decay_logit, lam_re, lam_im, log_dt, b_re, b_im,
             c_re, c_im, d_skip, glu_w, glu_b, row, col, need_ctx):
    cuts = [RET_WIDTH, 2 * RET_WIDTH, 3 * RET_WIDTH, 4 * RET_WIDTH]
    ql, kl, vl, gl, ul = jnp.split(h_lat @ w_in, cuts, axis=-1)
    qc, kc, vc, gc, uc = jnp.split(h_ctx @ w_in, cuts, axis=-1)
    r_l, r_c = retention_mixer(ql, kl, vl, gl, qc, kc, vc, gc, decay_logit, row, col, need_ctx)
    s_l, s_c = s5_mixer(ul, uc, lam_re, lam_im, log_dt, b_re, b_im, c_re, c_im, d_skip, glu_w, glu_b, need_ctx)
    y_l = jnp.concatenate([r_l, s_l], axis=-1) @ w_out
    y_c = (jnp.concatenate([r_c, s_c], axis=-1) @ w_out) if need_ctx else None
    return y_l, y_c


def na_mixer(h_lat, h_ctx, w_qkv, w_o, rpb, need_ctx):
    B, S, _ = h_lat.shape
    rows = S // GRID_W
    kh = min(NA_KH_MAX, rows)
    H, dh = NA_HEADS, NA_HEAD_DIM
    scale = dh ** -0.5
    q, k, v = jnp.split(h_lat @ w_qkv, 3, axis=-1)
    qc, kc, vc = jnp.split(h_ctx @ w_qkv, 3, axis=-1)
    q_g = q.reshape(B, rows, GRID_W, H, dh) * scale
    k_g = k.reshape(B, rows, GRID_W, H, dh)
    v_g = v.reshape(B, rows, GRID_W, H, dh)
    k_c = kc.reshape(B, -1, H, dh)
    v_c = vc.reshape(B, -1, H, dh)
    n_cb = GRID_W // NA_QBLOCK_W
    row_start = jnp.clip(jnp.arange(rows) - kh // 2, 0, rows - kh)
    qcol = np.arange(GRID_W).reshape(n_cb, NA_QBLOCK_W)
    kcol0 = np.clip(np.arange(n_cb) * NA_QBLOCK_W - NA_KW // 2, 0, GRID_W - NA_KSPAN_W)
    kcol = kcol0[:, None] + np.arange(NA_KSPAN_W)
    wstart = np.clip(qcol - NA_KW // 2, 0, GRID_W - NA_KW)
    kc3 = kcol[:, None, :]
    col_valid = jnp.asarray((kc3 >= wstart[..., None]) & (kc3 < wstart[..., None] + NA_KW))
    col_rel = np.clip(kc3 - qcol[..., None], -(NA_KW - 1), NA_KW - 1) + NA_KW - 1
    bias_cols = rpb.astype(jnp.float32)[:, :, col_rel]
    n_loc = kh * NA_KSPAN_W

    def row_block(args):
        r, r0, q_r = args
        k_rows = lax.dynamic_slice_in_dim(k_g, r0, kh, axis=1)
        v_rows = lax.dynamic_slice_in_dim(v_g, r0, kh, axis=1)
        k_blk = k_rows[:, :, kcol]
        v_blk = v_rows[:, :, kcol]
        qb = q_r.reshape(B, n_cb, NA_QBLOCK_W, H, dh)
        s_loc = jnp.einsum('bjqhd,bijlhd->bhjqil', qb, k_blk).astype(jnp.float32)
        rel_r = r0 + jnp.arange(kh) - r + NA_KH_MAX - 1
        bias = jnp.moveaxis(bias_cols[:, rel_r], 1, 3)
        s_loc = jnp.where(col_valid[:, :, None, :], s_loc + bias, NEG_INF)
        s_ctx = jnp.einsum('bjqhd,bkhd->bhjqk', qb, k_c).astype(jnp.float32)
        s = jnp.concatenate([s_loc.reshape(B, H, n_cb, NA_QBLOCK_W, n_loc), s_ctx], axis=-1)
        p = jax.nn.softmax(s, axis=-1).astype(v_g.dtype)
        p_loc = p[..., :n_loc].reshape(B, H, n_cb, NA_QBLOCK_W, kh, NA_KSPAN_W)
        p_ctx = p[..., n_loc:]
        o = (jnp.einsum('bhjqil,bijlhd->bjqhd', p_loc, v_blk)
             + jnp.einsum('bhjqk,bkhd->bjqhd', p_ctx, v_c))
        return o.reshape(B, GRID_W, H * dh)

    o = lax.map(row_block, (jnp.arange(rows), row_start, jnp.moveaxis(q_g, 1, 0)))
    y_l = jnp.moveaxis(o, 0, 1).reshape(B, S, H * dh) @ w_o
    y_c = None
    if need_ctx:
        q_cc = qc.reshape(B, -1, H, dh) * scale
        s = jnp.einsum('bqhd,bkhd->bhqk', q_cc, k_c).astype(jnp.float32)
        p = jax.nn.softmax(s, axis=-1).astype(v_c.dtype)
        oc = jnp.einsum('bhqk,bkhd->bqhd', p, v_c)
        y_c = oc.reshape(B, -1, H * dh) @ w_o
    return y_l, y_c


def setup_inputs(seed: int = 0) -> dict:
    key = jax.random.key(seed)
    ks = jax.random.split(key, 32)
    n_even = (DEPTH + 1) // 2
    n_odd = DEPTH // 2
    nrm = jax.random.normal
    D, F = D_MODEL, D_FF
    x = nrm(ks[0], (BATCH, SEQ, D), jnp.float32)
    c = nrm(ks[1], (BATCH, D), jnp.float32)
    ctx = nrm(ks[2], (BATCH, CTX_LEN, D), jnp.float32)
    c_ctx = nrm(ks[3], (D,), jnp.float32)
    w_mod = nrm(ks[4], (DEPTH, D, N_MOD * D), jnp.float32) * (0.5 * D ** -0.5)
    b_mod = nrm(ks[5], (DEPTH, N_MOD * D), jnp.float32) * 0.02
    norm_g = 1.0 + 0.02 * nrm(ks[6], (DEPTH, 3, D), jnp.float32)
    ffn_w1 = nrm(ks[7], (DEPTH, 2, D, 2 * F), jnp.float32) * D ** -0.5
    ffn_w2 = nrm(ks[8], (DEPTH, 2, F, D), jnp.float32) * F ** -0.5
    w_in_ab = nrm(ks[9], (n_even, D, AB_IN_WIDTH), jnp.float32) * D ** -0.5
    w_out_ab = nrm(ks[10], (n_even, D, D), jnp.float32) * D ** -0.5
    base_logit = jnp.log(2.0 ** (5.0 + jnp.arange(RET_HEADS, dtype=jnp.float32)) - 1.0)
    ret_decay_logit = base_logit + 0.01 * nrm(ks[11], (n_even, 2, RET_HEADS), jnp.float32)
    gp = (n_even, 2, S5_GROUPS, S5_STATE)
    s5_lam_re = -0.5 + 0.01 * nrm(ks[12], gp, jnp.float32)
    s5_lam_im = math.pi * jnp.arange(S5_STATE, dtype=jnp.float32) + 0.01 * nrm(ks[13], gp, jnp.float32)
    s5_log_dt = jax.random.uniform(ks[14], (n_even, 2, S5_GROUPS), jnp.float32,
                                   math.log(S5_DT_MIN), math.log(S5_DT_MAX))
    bsh = (n_even, 2, S5_GROUPS, S5_STATE, S5_GROUP)
    s5_b_re = nrm(ks[15], bsh, jnp.float32) * (2 * S5_GROUP) ** -0.5
    s5_b_im = nrm(ks[16], bsh, jnp.float32) * (2 * S5_GROUP) ** -0.5
    csh = (n_even, 2, S5_GROUPS, S5_GROUP, S5_STATE)
    s5_c_re = nrm(ks[17], csh, jnp.float32) * (2 * S5_STATE) ** -0.5
    s5_c_im = nrm(ks[18], csh, jnp.float32) * (2 * S5_STATE) ** -0.5
    s5_d = nrm(ks[19], (n_even, S5_WIDTH), jnp.float32)
    s5_glu_w = nrm(ks[20], (n_even, S5_WIDTH, S5_WIDTH), jnp.float32) * S5_WIDTH ** -0.5
    s5_glu_b = 0.02 * nrm(ks[21], (n_even, S5_WIDTH), jnp.float32)
    na_w_qkv = nrm(ks[22], (n_odd, D, 3 * D), jnp.float32) * D ** -0.5
    na_w_o = nrm(ks[23], (n_odd, D, D), jnp.float32) * D ** -0.5
    na_rpb = 0.02 * nrm(ks[24], (n_odd, NA_HEADS, 2 * NA_KH_MAX - 1, 2 * NA_KW - 1), jnp.float32)
    final_g = 1.0 + 0.02 * nrm(ks[25], (D,), jnp.float32)
    return {"x": x, "c": c, "ctx": ctx, "c_ctx": c_ctx, "w_mod": w_mod, "b_mod": b_mod,
            "norm_g": norm_g, "ffn_w1": ffn_w1, "ffn_w2": ffn_w2, "w_in_ab": w_in_ab,
            "w_out_ab": w_out_ab, "ret_decay_logit": ret_decay_logit, "s5_lam_re": s5_lam_re,
            "s5_lam_im": s5_lam_im, "s5_log_dt": s5_log_dt, "s5_b_re": s5_b_re, "s5_b_im": s5_b_im,
            "s5_c_re": s5_c_re, "s5_c_im": s5_c_im, "s5_d": s5_d, "s5_glu_w": s5_glu_w,
            "s5_glu_b": s5_glu_b, "na_w_qkv": na_w_qkv, "na_w_o": na_w_o, "na_rpb": na_rpb,
            "final_g": final_g}


def reference(x, c, ctx, c_ctx, w_mod, b_mod, norm_g, ffn_w1, ffn_w2, w_in_ab, w_out_ab,
              ret_decay_logit, s5_lam_re, s5_lam_im, s5_log_dt, s5_b_re, s5_b_im, s5_c_re, s5_c_im,
              s5_d, s5_glu_w, s5_glu_b, na_w_qkv, na_w_o, na_rpb, final_g):
    S = x.shape[1]
    t = jnp.arange(S)
    row = t // GRID_W
    col = t % GRID_W
    h = x
    hc = ctx
    sc = jax.nn.silu(c)
    scc = jax.nn.silu(c_ctx)
    for layer in range(DEPTH):
        last = layer == DEPTH - 1
        i = layer // 2
        ml = jnp.split((sc @ w_mod[layer] + b_mod[layer])[:, None, :], N_MOD, axis=-1)
        mc = jnp.split((scc @ w_mod[layer] + b_mod[layer])[None, None, :], N_MOD, axis=-1)
        h = h + 0.5 * ml[2] * swiglu(modulate(h, norm_g[layer, 0], ml[0], ml[1]), ffn_w1[layer, 0], ffn_w2[layer, 0])
        hc = hc + 0.5 * mc[2] * swiglu(modulate(hc, norm_g[layer, 0], mc[0], mc[1]), ffn_w1[layer, 0], ffn_w2[layer, 0])
        a_l = modulate(h, norm_g[layer, 1], ml[3], ml[4])
        a_c = modulate(hc, norm_g[layer, 1], mc[3], mc[4])
        if layer % 2 == 0:
            y_l, y_c = ab_mixer(a_l, a_c, w_in_ab[i], w_out_ab[i], ret_decay_logit[i], s5_lam_re[i],
                                s5_lam_im[i], s5_log_dt[i], s5_b_re[i], s5_b_im[i], s5_c_re[i], s5_c_im[i],
                                s5_d[i], s5_glu_w[i], s5_glu_b[i], row, col, not last)
        else:
            y_l, y_c = na_mixer(a_l, a_c, na_w_qkv[i], na_w_o[i], na_rpb[i], not last)
        h = h + ml[5] * y_l
        h = h + 0.5 * ml[8] * swiglu(modulate(h, norm_g[layer, 2], ml[6], ml[7]), ffn_w1[layer, 1], ffn_w2[layer, 1])
        if not last:
            hc = hc + mc[5] * y_c
            hc = hc + 0.5 * mc[8] * swiglu(modulate(hc, norm_g[layer, 2], mc[6], mc[7]), ffn_w1[layer, 1], ffn_w2[layer, 1])
    return rms_norm(h, final_g)
```

```python
import functools
import math

import numpy as np
import jax
import jax.numpy as jnp
from jax import lax
from jax.experimental import pallas as pl
from jax.experimental.pallas import tpu as pltpu

F32 = jnp.float32
BF16 = jnp.bfloat16

EPS = 1e-6
ROPE_BASE = 10000.0
GRID_W = 64
N_MOD = 9
RET_HEADS = 4
RET_HEAD_DIM = 128
RET_CHUNK = 128
S5_GROUP = 16
S5_STATE = 64
S5_CHUNK = 16
S5_TILE = 8
NA_HEADS = 16
NA_HEAD_DIM = 64
NA_KH = 8
NA_KW = 16
NEG_INF = -1e30

TOKEN_TILE = 512
VMEM_LIMIT = 56 * 1024 * 1024
MOD_ROWS = 8


def _cparams(*sem):
    return pltpu.CompilerParams(dimension_semantics=sem, vmem_limit_bytes=VMEM_LIMIT)


def _resident(shape):
    nd = len(shape)
    return pl.BlockSpec(shape, lambda *_: (0,) * nd, pipeline_mode=pl.Buffered(1))


def _mod_kernel(c_ref, w_ref, b_ref, o_ref):
    c = c_ref[...]
    s = c * jax.nn.sigmoid(c)
    o_ref[0] = jnp.dot(s, w_ref[0], preferred_element_type=F32,
                       precision=lax.Precision.HIGHEST) + b_ref[0]


def _modulation(cvec, w_mod, b_mod):
    depth, d, nd = w_mod.shape
    tn = d
    return pl.pallas_call(
        _mod_kernel,
        grid=(depth, nd // tn),
        in_specs=[
            pl.BlockSpec((MOD_ROWS, d), lambda l, j: (0, 0)),
            pl.BlockSpec((1, d, tn), lambda l, j: (l, 0, j)),
            pl.BlockSpec((1, 1, tn), lambda l, j: (l, 0, j)),
        ],
        out_specs=pl.BlockSpec((1, MOD_ROWS, tn), lambda l, j: (l, 0, j)),
        out_shape=jax.ShapeDtypeStruct((depth, MOD_ROWS, nd), F32),
        compiler_params=_cparams("parallel", "parallel"),
        name="modulation",
    )(cvec, w_mod, b_mod.reshape(depth, 1, nd))


def _rms(x):
    return x * lax.rsqrt(jnp.mean(x * x, axis=-1, keepdims=True) + EPS)


def _modulated(h, m, g, mi):
    return (_rms(h) * g) * (1.0 + m[mi + 1:mi + 2]) + m[mi:mi + 1]


def _tile_specs(n_lat_tiles_per_batch, n_batch, d):
    def mod_idx(i):
        return (jnp.minimum(i // n_lat_tiles_per_batch, n_batch), 0, 0)
    h_spec = pl.BlockSpec((TOKEN_TILE, d), lambda i: (i, 0))
    m_spec = pl.BlockSpec((1, N_MOD, d), mod_idx)
    return h_spec, m_spec


def _ffn_kernel(h_ref, m_ref, g_ref, w1a_ref, w1b_ref, w2_ref, *rest, mi, final):
    o_ref = rest[-1]
    h = h_ref[...]
    m = m_ref[0]
    xm = _modulated(h, m, g_ref[...], mi).astype(BF16)
    a = jnp.dot(xm, w1a_ref[...], preferred_element_type=F32)
    b = jnp.dot(xm, w1b_ref[...], preferred_element_type=F32)
    hid = (a * jax.nn.sigmoid(a) * b).astype(BF16)
    y = jnp.dot(hid, w2_ref[...], preferred_element_type=F32)
    out = h + (0.5 * m[mi + 2:mi + 3]) * y
    if final:
        out = _rms(out) * rest[0][...]
    o_ref[...] = out


def _ffn(h, mod, g, w1a, w1b, w2, *, mi, n_rows, lat_tiles, n_batch, final_g=None):
    d = h.shape[1]
    f = w2.shape[0]
    h_spec, m_spec = _tile_specs(lat_tiles, n_batch, d)
    in_specs = [h_spec, m_spec, _resident((1, d)), _resident((d, f)), _resident((d, f)),
                _resident((f, d))]
    args = [h, mod, g.reshape(1, d), w1a, w1b, w2]
    if final_g is not None:
        in_specs.append(_resident((1, d)))
        args.append(final_g.reshape(1, d))
    return pl.pallas_call(
        functools.partial(_ffn_kernel, mi=mi, final=final_g is not None),
        grid=(n_rows // TOKEN_TILE,),
        in_specs=in_specs,
        out_specs=pl.BlockSpec((TOKEN_TILE, d), lambda i: (i, 0)),
        out_shape=jax.ShapeDtypeStruct((n_rows, d), F32),
        compiler_params=_cparams("parallel"),
        name="ffn",
    )(*args)


def _proj_kernel(h_ref, m_ref, g_ref, w_ref, o_ref, *, mi):
    xm = _modulated(h_ref[...], m_ref[0], g_ref[...], mi).astype(BF16)
    o_ref[...] = jnp.dot(xm, w_ref[...], preferred_element_type=F32).astype(o_ref.dtype)


def _proj(h, mod, g, w, *, mi, lat_tiles, n_batch, out_dtype):
    n_rows, d = h.shape
    n = w.shape[1]
    h_spec, m_spec = _tile_specs(lat_tiles, n_batch, d)
    return pl.pallas_call(
        functools.partial(_proj_kernel, mi=mi),
        grid=(n_rows // TOKEN_TILE,),
        in_specs=[h_spec, m_spec, _resident((1, d)), _resident((d, n))],
        out_specs=pl.BlockSpec((TOKEN_TILE, n), lambda i: (i, 0)),
        out_shape=jax.ShapeDtypeStruct((n_rows, n), out_dtype),
        compiler_params=_cparams("parallel"),
        name="mixer_in_proj",
    )(h, mod, g.reshape(1, d), w)


def _rope_tables(seq):
    half = RET_HEAD_DIM // 2
    quarter = half // 2
    inv = ROPE_BASE ** (-np.arange(0, half, 2, dtype=np.float64) / half)
    t = np.arange(seq)
    lane = np.arange(RET_HEAD_DIM)
    pos = np.where(lane[None, :] < half, (t // GRID_W)[:, None], (t % GRID_W)[:, None])
    ang = pos.astype(np.float32).astype(np.float64) * inv.astype(np.float32)[lane % quarter][None, :]
    first = (lane % half) < quarter
    cos = np.cos(ang)
    sin = np.sin(ang)
    sa = np.where(first[None, :], -sin, 0.0)
    sb = np.where(first[None, :], 0.0, sin)
    pad1 = np.ones((RET_CHUNK, RET_HEAD_DIM))
    pad0 = np.zeros((RET_CHUNK, RET_HEAD_DIM))
    tabs = [np.concatenate([cos, pad1]), np.concatenate([sa, pad0]), np.concatenate([sb, pad0])]
    return [jnp.asarray(x, F32) for x in tabs]


def _rope(x, cos, sa, sb):
    quarter = RET_HEAD_DIM // 4
    up = pltpu.roll(x, RET_HEAD_DIM - quarter, axis=1)
    dn = pltpu.roll(x, quarter, axis=1)
    return x * cos + up * sa + dn * sb


def _retention_kernel(dec_ref, q_ref, k_ref, v_ref, g_ref, cos_ref, sa_ref, sb_ref,
                      dmask_ref, qwf_ref, qwb_ref, kwf_ref, kwb_ref, o_ref,
                      s_run, s_bwd, *, n_chunks, n_ctx_chunks):
    ph = pl.program_id(1)
    i = pl.program_id(2)
    hd = RET_HEAD_DIM
    kscale = hd ** -0.5

    @pl.when(i == 0)
    def _():
        s_run[...] = jnp.zeros_like(s_run)

    cos, sa, sb = cos_ref[...], sa_ref[...], sb_ref[...]
    tn = (((0,), (0,)), ((), ()))
    nt = (((1,), (1,)), ((), ()))

    @pl.when(ph == 0)
    def _():
        cid = jnp.where(i < n_ctx_chunks, n_ctx_chunks - 1 - i, n_chunks + n_ctx_chunks - 1 - i)
        for h in range(RET_HEADS):
            sl = slice(h * hd, (h + 1) * hd)
            k = _rope(k_ref[:, sl], cos, sa, sb) * kscale
            v = v_ref[:, sl].astype(BF16)
            s_old = s_run[h]
            s_bwd[cid, h] = s_old
            kv = lax.dot_general((k * kwb_ref[h]).astype(BF16), v, tn, preferred_element_type=F32)
            s_run[h] = dec_ref[1, h] * s_old + kv

    @pl.when(ph == 1)
    def _():
        for h in range(RET_HEADS):
            sl = slice(h * hd, (h + 1) * hd)
            q = _rope(q_ref[:, sl], cos, sa, sb)
            k = _rope(k_ref[:, sl], cos, sa, sb) * kscale
            v = v_ref[:, sl].astype(BF16)
            s_old = s_run[h]
            a = lax.dot_general(q.astype(BF16), k.astype(BF16), nt, preferred_element_type=F32)
            o = jnp.dot((a * dmask_ref[h]).astype(BF16), v, preferred_element_type=F32)
            o += jnp.dot((q * qwf_ref[h]).astype(BF16), s_old.astype(BF16),
                         preferred_element_type=F32)
            o += jnp.dot((q * qwb_ref[h]).astype(BF16), s_bwd[i, h].astype(BF16),
                         preferred_element_type=F32)
            kv = lax.dot_general((k * kwf_ref[h]).astype(BF16), v, tn, preferred_element_type=F32)
            s_run[h] = dec_ref[0, h] * s_old + kv
            o = o * lax.rsqrt(jnp.mean(o * o, axis=-1, keepdims=True) + EPS)
            g = g_ref[:, sl]
            o_ref[:, sl] = o * (g * jax.nn.sigmoid(g))


def _retention(p, decay_logit, *, n_batch, seq, ctx_len):
    c = RET_CHUNK
    width = RET_HEADS * RET_HEAD_DIM
    n_lat = seq // c
    n_ctx = ctx_len // c
    n_ch = n_lat + n_ctx
    n_rows = p.shape[0]
    lat_blocks = n_batch * n_lat

    log_gamma = jax.nn.log_sigmoid(decay_logit.astype(F32))
    pos = jnp.arange(c, dtype=F32)
    diff = pos[:, None] - pos[None, :]
    lf = log_gamma[0][:, None, None]
    lb = log_gamma[1][:, None, None]
    dmask = (jnp.where(diff >= 0, jnp.exp(lf * jnp.maximum(diff, 0.0)), 0.0)
             + jnp.where(diff <= 0, jnp.exp(lb * jnp.maximum(-diff, 0.0)), 0.0))
    ones = jnp.ones((1, 1, RET_HEAD_DIM), F32)
    col = pos[None, :, None]
    qwf = jnp.exp(lf * (col + 1.0)) * ones
    qwb = jnp.exp(lb * (c - col)) * ones
    kwf = jnp.exp(lf * (c - 1.0 - col)) * ones
    kwb = jnp.exp(lb * col) * ones
    chunk_decay = jnp.exp(log_gamma * c)
    cos, sa, sb = _rope_tables(seq)

    def chunk_id(ph, i):
        bwd = jnp.where(i < n_ctx, n_ctx - 1 - i, n_ch + n_ctx - 1 - i)
        return jnp.where(ph == 0, bwd, i)

    def row_block(b, cid):
        return jnp.where(cid < n_ctx, lat_blocks + b * n_ctx + cid, b * n_lat + cid - n_ctx)

    def col_spec(j, forward_only):
        def idx(b, ph, i):
            cid = chunk_id(ph, i)
            if forward_only:
                cid = jnp.where(ph == 0, 0, cid)
            return (row_block(b, cid), j)
        return pl.BlockSpec((c, width), idx)

    def tab_idx(b, ph, i):
        cid = chunk_id(ph, i)
        return (jnp.where(cid < n_ctx, n_lat, cid - n_ctx), 0)

    tab_spec = pl.BlockSpec((c, RET_HEAD_DIM), tab_idx)
    head_tab = _resident((RET_HEADS, c, RET_HEAD_DIM))
    kern = functools.partial(_retention_kernel, n_chunks=n_ch, n_ctx_chunks=n_ctx)
    return pl.pallas_call(
        kern,
        grid=(n_batch, 2, n_ch),
        in_specs=[
            pl.BlockSpec(memory_space=pltpu.SMEM),
            col_spec(0, True), col_spec(1, False), col_spec(2, False), col_spec(3, True),
            tab_spec, tab_spec, tab_spec,
            head_tab, head_tab, head_tab, head_tab, head_tab,
        ],
        out_specs=col_spec(0, True),
        out_shape=jax.ShapeDtypeStruct((n_rows, width), F32),
        scratch_shapes=[
            pltpu.VMEM((RET_HEADS, RET_HEAD_DIM, RET_HEAD_DIM), F32),
            pltpu.VMEM((n_ch, RET_HEADS, RET_HEAD_DIM, RET_HEAD_DIM), F32),
        ],
        compiler_params=_cparams("parallel", "arbitrary", "arbitrary"),
        name="retention",
    )(chunk_decay, p, p, p, p, cos, sa, sb, dmask, qwf, qwb, kwf, kwb)


def _cmul(a, b):
    return a[0] * b[0] - a[1] * b[1], a[0] * b[1] + a[1] * b[0]


def _pair_blockdiag(a):
    g, r, c = a.shape
    a = a.reshape(g // 2, 2, r, c)
    eye = jnp.eye(2, dtype=a.dtype)
    return jnp.einsum('jarc,ab->jarbc', a, eye).reshape(g // 2, 2 * r, 2 * c)


def _pair_lanes(a):
    g, r, p = a.shape
    return a.reshape(g // 2, 2, r, p).transpose(0, 2, 1, 3).reshape(g // 2, r, 2 * p)


def _s5_weights(lam_re, lam_im, log_dt, b_re, b_im, c_re, c_im, d_skip):
    q = S5_CHUNK
    n_g = lam_re.shape[1]
    hi = lax.Precision.HIGHEST
    kall = None
    zw = []
    rw = []
    tabs = []
    for d in range(2):
        lr = jnp.minimum(lam_re[d].astype(F32), -1e-4)
        li = lam_im[d].astype(F32)
        dt = jnp.exp(log_dt[d].astype(F32))[:, None]
        mag = jnp.exp(lr * dt)
        a = (mag * jnp.cos(li * dt), mag * jnp.sin(li * dt))
        den = lr * lr + li * li
        am1 = (a[0] - 1.0, a[1])
        coef = ((am1[0] * lr + am1[1] * li) / den, (am1[1] * lr - am1[0] * li) / den)
        bb = _cmul((coef[0][..., None], coef[1][..., None]),
                   (b_re[d].astype(F32), b_im[d].astype(F32)))
        cm = (c_re[d].astype(F32), c_im[d].astype(F32))
        pw = [(jnp.ones_like(a[0]), jnp.zeros_like(a[0]))]
        for _ in range(q):
            pw.append(_cmul(pw[-1], a))
        pw_re = jnp.stack([x[0] for x in pw], axis=1)
        pw_im = jnp.stack([x[1] for x in pw], axis=1)
        e = _cmul((pw_re[:, :q, :, None], pw_im[:, :q, :, None]),
                  (bb[0][:, None], bb[1][:, None]))
        taps = (jnp.einsum('gkp,gtpj->gtkj', cm[0], e[0], precision=hi)
                - jnp.einsum('gkp,gtpj->gtkj', cm[1], e[1], precision=hi))
        if d == 0:
            kall = jnp.concatenate([jnp.zeros_like(taps[:, 1:]), taps], axis=1)
        else:
            kall = kall + jnp.concatenate([taps[:, ::-1], jnp.zeros_like(taps[:, 1:])], axis=1)
        tsel = np.arange(q)[::-1] if d == 0 else np.arange(q)
        zr = pw_re[:, tsel][..., None] * bb[0][:, None] - pw_im[:, tsel][..., None] * bb[1][:, None]
        zi = pw_re[:, tsel][..., None] * bb[1][:, None] + pw_im[:, tsel][..., None] * bb[0][:, None]
        zw.append((zr.transpose(0, 1, 3, 2).reshape(n_g, q * S5_GROUP, S5_STATE),
                   zi.transpose(0, 1, 3, 2).reshape(n_g, q * S5_GROUP, S5_STATE)))
        rsel = np.arange(1, q + 1) if d == 0 else np.arange(q, 0, -1)
        rr = cm[0][:, None] * pw_re[:, rsel][:, :, None, :] - cm[1][:, None] * pw_im[:, rsel][:, :, None, :]
        ri = cm[0][:, None] * pw_im[:, rsel][:, :, None, :] + cm[1][:, None] * pw_re[:, rsel][:, :, None, :]
        rw.append((rr.transpose(0, 3, 1, 2).reshape(n_g, S5_STATE, q * S5_GROUP),
                   (-ri).transpose(0, 3, 1, 2).reshape(n_g, S5_STATE, q * S5_GROUP)))
        a16 = (pw_re[:, q], pw_im[:, q])
        a32 = _cmul(a16, a16)
        a64 = _cmul(a32, a32)
        a128 = _cmul(a64, a64)
        ramp = [(jnp.ones_like(a[0]), jnp.zeros_like(a[0]))]
        for _ in range(S5_TILE - 1):
            ramp.append(_cmul(ramp[-1], a16))
        if d == 1:
            ramp = ramp[::-1]
        rows = [a16[0], a16[1], a32[0], a32[1], a64[0], a64[1], a128[0], a128[1]]
        rows += [x[0] for x in ramp] + [x[1] for x in ramp]
        tabs.append(jnp.stack(rows, axis=1))
    dsk = d_skip.astype(F32).reshape(n_g, S5_GROUP)
    eye = jnp.eye(S5_GROUP, dtype=F32)
    kall = kall.at[:, q - 1].add(dsk[:, :, None] * eye[None])
    delta = np.arange(q)[None, :] - np.arange(q)[:, None] + (q - 1)
    m = kall[:, delta]
    m = m.transpose(0, 1, 4, 2, 3).reshape(n_g, q * S5_GROUP, q * S5_GROUP)
    w1 = jnp.concatenate([_pair_blockdiag(m)] + [_pair_blockdiag(z) for pair in zw for z in pair],
                         axis=2)
    w2 = jnp.concatenate([_pair_blockdiag(r) for pair in rw for r in pair], axis=1)
    tab = jnp.concatenate([_pair_lanes(t) for t in tabs], axis=1)
    return w1.astype(BF16), w2.astype(BF16), tab


def _tile_shift(x, s, down):
    row = lax.broadcasted_iota(jnp.int32, x.shape, 1)
    if down:
        return jnp.where(row >= s, pltpu.roll(x, s, axis=1), 0.0)
    return jnp.where(row < S5_TILE - s, pltpu.roll(x, S5_TILE - s, axis=1), 0.0)


def _s5_kernel(u_ref, w1_ref, w2_ref, tab_ref, o_ref, zx, xin, *, n_batch, lat_tiles, ctx_tiles):
    lanes = 2 * S5_STATE
    width = u_ref.shape[2]
    y = jnp.dot(u_ref[0].astype(BF16), w1_ref[0], preferred_element_type=F32)
    o_ref[0] = y[:, :width]
    n_tiles = y.shape[0] // S5_TILE

    def tab_row(r):
        return tab_ref[0, r:r + 1, :]

    for d in range(2):
        down = d == 0
        t0 = 24 * d
        lo = width + 2 * d * lanes
        z = (y[:, lo:lo + lanes].reshape(n_tiles, S5_TILE, lanes),
             y[:, lo + lanes:lo + 2 * lanes].reshape(n_tiles, S5_TILE, lanes))
        loc = (_tile_shift(z[0], 1, down), _tile_shift(z[1], 1, down))
        for k, s in enumerate((1, 2, 4)):
            mul = (tab_row(t0 + 2 * k)[None], tab_row(t0 + 2 * k + 1)[None])
            inc = _cmul(mul, (_tile_shift(loc[0], s, down), _tile_shift(loc[1], s, down)))
            loc = (loc[0] + inc[0], loc[1] + inc[1])
        zx[2 * d] = loc[0]
        zx[2 * d + 1] = loc[1]
        zx[4 + 2 * d] = z[0]
        zx[4 + 2 * d + 1] = z[1]

    for d in range(2):
        t0 = 24 * d
        edge = S5_TILE - 1 if d == 0 else 0
        a16 = (tab_row(t0), tab_row(t0 + 1))
        a128 = (tab_row(t0 + 6), tab_row(t0 + 7))
        ramp = (tab_ref[0, t0 + 8:t0 + 16, :], tab_ref[0, t0 + 16:t0 + 24, :])
        ctx0 = n_batch * lat_tiles

        def tile_step(j, carry, d=d, edge=edge, a16=a16, a128=a128, ramp=ramp):
            loc = (zx[2 * d, j], zx[2 * d + 1, j])
            z = (zx[4 + 2 * d, j], zx[4 + 2 * d + 1, j])
            inc = _cmul(ramp, carry)
            xin[j, :, (2 * d) * lanes:(2 * d + 1) * lanes] = loc[0] + inc[0]
            xin[j, :, (2 * d + 1) * lanes:(2 * d + 2) * lanes] = loc[1] + inc[1]
            e_loc = _cmul(a16, (loc[0][edge:edge + 1], loc[1][edge:edge + 1]))
            nxt = _cmul(a128, carry)
            return (nxt[0] + e_loc[0] + z[0][edge:edge + 1], nxt[1] + e_loc[1] + z[1][edge:edge + 1])

        for b in range(n_batch):
            carry = (jnp.zeros((1, lanes), F32), jnp.zeros((1, lanes), F32))
            cb = ctx0 + b * ctx_tiles
            lb = b * lat_tiles
            if d == 0:
                carry = lax.fori_loop(0, ctx_tiles, lambda t, c, cb=cb: tile_step(cb + t, c), carry)
                carry = lax.fori_loop(0, lat_tiles, lambda t, c, lb=lb: tile_step(lb + t, c), carry)
            else:
                carry = lax.fori_loop(
                    0, ctx_tiles, lambda t, c, cb=cb: tile_step(cb + ctx_tiles - 1 - t, c), carry)
                carry = lax.fori_loop(
                    0, lat_tiles, lambda t, c, lb=lb: tile_step(lb + lat_tiles - 1 - t, c), carry)

    x = xin[...].reshape(n_tiles * S5_TILE, 4 * lanes)
    o_ref[0] += jnp.dot(x.astype(BF16), w2_ref[0], preferred_element_type=F32)


def _s5(u, weights, *, n_batch, seq, ctx_len):
    w1, w2, tab = weights
    n_rows, width = u.shape
    q = S5_CHUNK
    n_pairs = w1.shape[0]
    pw = 2 * q * S5_GROUP
    n_sup = n_rows // q
    u5 = u.reshape(n_sup, q, n_pairs, 2, S5_GROUP).transpose(2, 0, 3, 1, 4).reshape(n_pairs, n_sup, pw)
    n_tiles = n_sup // S5_TILE
    lanes = 2 * S5_STATE
    kern = functools.partial(_s5_kernel, n_batch=n_batch, lat_tiles=seq // q // S5_TILE,
                             ctx_tiles=ctx_len // q // S5_TILE)
    y5 = pl.pallas_call(
        kern,
        grid=(n_pairs,),
        in_specs=[
            pl.BlockSpec((1, n_sup, pw), lambda j: (j, 0, 0)),
            pl.BlockSpec((1, pw, 2 * pw), lambda j: (j, 0, 0)),
            pl.BlockSpec((1, 4 * lanes, pw), lambda j: (j, 0, 0)),
            pl.BlockSpec((1, tab.shape[1], lanes), lambda j: (j, 0, 0)),
        ],
        out_specs=pl.BlockSpec((1, n_sup, pw), lambda j: (j, 0, 0)),
        out_shape=jax.ShapeDtypeStruct((n_pairs, n_sup, pw), F32),
        scratch_shapes=[
            pltpu.VMEM((8, n_tiles, S5_TILE, lanes), F32),
            pltpu.VMEM((n_tiles, S5_TILE, 4 * lanes), F32),
        ],
        compiler_params=_cparams("parallel"),
        name="s5",
    )(u5, w1, w2, tab)
    return y5.reshape(n_pairs, n_sup, 2, q, S5_GROUP).transpose(1, 3, 0, 2, 4).reshape(n_rows, width)


def _ab_out_kernel(h_ref, m_ref, r_ref, y_ref, gw_ref, gb_ref, wr_ref, ws_ref, o_ref):
    y = y_ref[...]
    g = 0.5 * y * (1.0 + jnp.tanh(math.sqrt(2.0 / math.pi) * (y + 0.044715 * (y * y * y))))
    s = g * jax.nn.sigmoid(jnp.dot(g.astype(BF16), gw_ref[...], preferred_element_type=F32)
                           + gb_ref[...])
    out = jnp.dot(r_ref[...].astype(BF16), wr_ref[...], preferred_element_type=F32)
    out += jnp.dot(s.astype(BF16), ws_ref[...], preferred_element_type=F32)
    o_ref[...] = h_ref[...] + m_ref[0][5:6] * out


def _ab_out(h, mod, r, ys, glu_w, glu_b, w_out, *, lat_tiles, n_batch):
    n_rows, d = h.shape
    w = r.shape[1]
    h_spec, m_spec = _tile_specs(lat_tiles, n_batch, d)
    half = pl.BlockSpec((TOKEN_TILE, w), lambda i: (i, 0))
    return pl.pallas_call(
        _ab_out_kernel,
        grid=(n_rows // TOKEN_TILE,),
        in_specs=[h_spec, m_spec, half, half, _resident((w, w)), _resident((1, w)),
                  _resident((w, d)), _resident((w, d))],
        out_specs=pl.BlockSpec((TOKEN_TILE, d), lambda i: (i, 0)),
        out_shape=jax.ShapeDtypeStruct((n_rows, d), F32),
        compiler_params=_cparams("parallel"),
        name="ab_out_proj",
    )(h, mod, r, ys, glu_w.astype(BF16), glu_b.reshape(1, w).astype(F32),
      w_out[:w].astype(BF16), w_out[w:].astype(BF16))


def _na_bias_table(rpb):
    w = GRID_W
    qcol = np.arange(w)
    kcol = np.arange(w)
    wstart = np.clip(qcol - NA_KW // 2, 0, w - NA_KW)
    valid = (kcol[None, :] >= wstart[:, None]) & (kcol[None, :] < wstart[:, None] + NA_KW)
    rel = np.clip(kcol[None, :] - qcol[:, None], -(NA_KW - 1), NA_KW - 1) + NA_KW - 1
    rel_row = np.arange(NA_KH)[:, None] + np.arange(NA_KH)[None, :]
    bias = rpb.astype(F32)[:, rel_row][:, :, :, rel]
    bias = jnp.where(jnp.asarray(valid)[None, None, None], bias, NEG_INF)
    bias = bias.transpose(0, 1, 3, 2, 4).reshape(NA_HEADS, NA_KH, w, NA_KH * w)
    return bias.reshape(NA_HEADS // 2, 2, NA_KH, w, NA_KH * w).transpose(0, 2, 1, 3, 4).reshape(
        NA_HEADS // 2, NA_KH, 2 * w, NA_KH * w)


def _na_kernel(q_ref, k_ref, v_ref, kc_ref, vc_ref, bias_ref, o_ref, *, rows):
    w = GRID_W
    dh = NA_HEAD_DIM
    nt = (((1,), (1,)), ((), ()))
    scale = dh ** -0.5
    kc = kc_ref[...]
    vc = vc_ref[...]
    lane = lax.broadcasted_iota(jnp.int32, (w, 2 * dh), 1)
    first = lane < dh

    def body(r, carry):
        r0 = jnp.clip(r - NA_KH // 2, 0, rows - NA_KH)
        rel0 = r0 - r + NA_KH - 1
        q = q_ref[pl.ds(pl.multiple_of(r * w, w), w), :] * scale
        zero = jnp.zeros_like(q)
        qs = jnp.concatenate([jnp.where(first, q, zero), jnp.where(first, zero, q)], axis=0)
        koff = pl.multiple_of(r0 * w, w)
        kl = k_ref[pl.ds(koff, NA_KH * w), :]
        vl = v_ref[pl.ds(koff, NA_KH * w), :]
        s_loc = lax.dot_general(qs, kl, nt, preferred_element_type=F32) + bias_ref[0, rel0]
        s_ctx = lax.dot_general(qs, kc, nt, preferred_element_type=F32)
        m = jnp.maximum(jnp.max(s_loc, axis=-1, keepdims=True), jnp.max(s_ctx, axis=-1, keepdims=True))
        p_loc = jnp.exp(s_loc - m)
        p_ctx = jnp.exp(s_ctx - m)
        den = jnp.sum(p_loc, axis=-1, keepdims=True) + jnp.sum(p_ctx, axis=-1, keepdims=True)
        o = jnp.dot(p_loc.astype(BF16), vl, preferred_element_type=F32)
        o += jnp.dot(p_ctx.astype(BF16), vc, preferred_element_type=F32)
        o = o / den
        o_ref[pl.ds(pl.multiple_of(r * w, w), w), :] = jnp.where(first, o[:w], o[w:]).astype(o_ref.dtype)
        return carry

    lax.fori_loop(0, rows, body, 0)


def _natten(p, bias, *, n_batch, seq, ctx_len):
    d = NA_HEADS * NA_HEAD_DIM
    lb = 2 * NA_HEAD_DIM
    n_pairs = NA_HEADS // 2
    ctx_blk0 = n_batch * seq // ctx_len
    lat = lambda part: pl.BlockSpec((seq, lb), lambda b, j: (b, part * n_pairs + j))
    ctx = lambda part: pl.BlockSpec((ctx_len, lb), lambda b, j: (ctx_blk0 + b, part * n_pairs + j))
    return pl.pallas_call(
        functools.partial(_na_kernel, rows=seq // GRID_W),
        grid=(n_batch, n_pairs),
        in_specs=[lat(0), lat(1), lat(2), ctx(1), ctx(2),
                  pl.BlockSpec((1, NA_KH, 2 * GRID_W, NA_KH * GRID_W), lambda b, j: (j, 0, 0, 0))],
        out_specs=pl.BlockSpec((seq, lb), lambda b, j: (b, j)),
        out_shape=jax.ShapeDtypeStruct((n_batch * seq, d), BF16),
        compiler_params=_cparams("parallel", "parallel"),
        name="natten",
    )(p, p, p, p, p, bias)


def _na_out_kernel(h_ref, m_ref, a_ref, w_ref, o_ref):
    y = jnp.dot(a_ref[...], w_ref[...], preferred_element_type=F32)
    o_ref[...] = h_ref[...] + m_ref[0][5:6] * y


def _na_out(h, mod, att, w_o, *, lat_tiles, n_batch):
    n_rows, d = att.shape
    h_spec, m_spec = _tile_specs(lat_tiles, n_batch, d)
    return pl.pallas_call(
        _na_out_kernel,
        grid=(n_rows // TOKEN_TILE,),
        in_specs=[h_spec, m_spec, pl.BlockSpec((TOKEN_TILE, d), lambda i: (i, 0)), _resident((d, d))],
        out_specs=pl.BlockSpec((TOKEN_TILE, d), lambda i: (i, 0)),
        out_shape=jax.ShapeDtypeStruct((n_rows, d), F32),
        compiler_params=_cparams("parallel"),
        name="na_out_proj",
    )(h, mod, att, w_o.astype(BF16))


def kernel(x, c, ctx, c_ctx, w_mod, b_mod, norm_g, ffn_w1, ffn_w2, w_in_ab, w_out_ab, ret_decay_logit, s5_lam_re, s5_lam_im, s5_log_dt, s5_b_re, s5_b_im, s5_c_re, s5_c_im, s5_d, s5_glu_w, s5_glu_b, na_w_qkv, na_w_o, na_rpb, final_g):
    n_batch, seq, d = x.shape
    ctx_len = ctx.shape[1]
    depth = w_mod.shape[0]
    f = ffn_w2.shape[2]
    n_lat = n_batch * seq
    n_all = n_lat + n_batch * ctx_len
    lat_tiles = seq // TOKEN_TILE
    assert seq % TOKEN_TILE == 0 and (n_batch * ctx_len) % TOKEN_TILE == 0
    assert n_batch + 1 <= MOD_ROWS and seq % (GRID_W * NA_KH) == 0

    cvec = jnp.concatenate([c, c_ctx[None], jnp.zeros((MOD_ROWS - n_batch - 1, d), F32)], axis=0)
    mod = _modulation(cvec, w_mod, b_mod).reshape(depth, MOD_ROWS, N_MOD, d)
    h = jnp.concatenate([x.reshape(n_lat, d), ctx.reshape(n_batch * ctx_len, d)], axis=0)
    common = dict(lat_tiles=lat_tiles, n_batch=n_batch)
    dims = dict(n_batch=n_batch, seq=seq, ctx_len=ctx_len)

    for layer in range(depth):
        last = layer == depth - 1
        i = layer // 2
        m = mod[layer]
        w1 = ffn_w1[layer].astype(BF16)
        w2 = ffn_w2[layer].astype(BF16)
        h = _ffn(h, m, norm_g[layer, 0], w1[0, :, :f], w1[0, :, f:], w2[0], mi=0,
                 n_rows=n_all, **common)
        if layer % 2 == 0:
            p = _proj(h, m, norm_g[layer, 1], w_in_ab[i].astype(BF16), mi=3, out_dtype=F32, **common)
            r = _retention(p, ret_decay_logit[i], **dims)
            weights = _s5_weights(s5_lam_re[i], s5_lam_im[i], s5_log_dt[i], s5_b_re[i], s5_b_im[i],
                                  s5_c_re[i], s5_c_im[i], s5_d[i])
            ys = _s5(p[:, 4 * RET_HEADS * RET_HEAD_DIM:], weights, **dims)
            h = _ab_out(h, m, r, ys, s5_glu_w[i], s5_glu_b[i], w_out_ab[i], **common)
        else:
            assert last
            p = _proj(h, m, norm_g[layer, 1], na_w_qkv[i].astype(BF16), mi=3, out_dtype=BF16, **common)
            att = _natten(p, _na_bias_table(na_rpb[i]), **dims)
            h = _na_out(h, m, att, na_w_o[i], **common)
        n_rows = n_lat if last else n_all
        h = _ffn(h, m, norm_g[layer, 2], w1[1, :, :f], w1[1, :, f:], w2[1], mi=6, n_rows=n_rows,
                 final_g=final_g if last else None, **common)
    return h[:n_lat].reshape(n_batch, seq, d)
```

```python
import functools
import math

import numpy as np
import jax
import jax.numpy as jnp
from jax import lax
from jax.experimental import pallas as pl
from jax.experimental.pallas import tpu as pltpu

F32 = jnp.float32
BF16 = jnp.bfloat16

EPS = 1e-6
ROPE_BASE = 10000.0
GRID_W = 64
N_MOD = 9
RET_HEADS = 4
RET_HEAD_DIM = 128
RET_CHUNK = 128
S5_GROUP = 16
S5_STATE = 64
S5_CHUNK = 16
S5_TILE = 8
NA_HEADS = 16
NA_HEAD_DIM = 64
NA_KH = 8
NA_KW = 16
NEG_INF = -1e30

TOKEN_TILE = 512
VMEM_LIMIT = 56 * 1024 * 1024
MOD_ROWS = 8


def _cparams(*sem):
    return pltpu.CompilerParams(dimension_semantics=sem, vmem_limit_bytes=VMEM_LIMIT)


def _resident(shape):
    nd = len(shape)
    return pl.BlockSpec(shape, lambda *_: (0,) * nd, pipeline_mode=pl.Buffered(1))


def _mod_kernel(c_ref, w_ref, b_ref, o_ref):
    c = c_ref[...]
    s = c * jax.nn.sigmoid(c)
    o_ref[0] = jnp.dot(s, w_ref[0], preferred_element_type=F32,
                       precision=lax.Precision.HIGHEST) + b_ref[0]


def _modulation(cvec, w_mod, b_mod):
    depth, d, nd = w_mod.shape
    tn = d
    return pl.pallas_call(
        _mod_kernel,
        grid=(depth, nd // tn),
        in_specs=[
            pl.BlockSpec((MOD_ROWS, d), lambda l, j: (0, 0)),
            pl.BlockSpec((1, d, tn), lambda l, j: (l, 0, j)),
            pl.BlockSpec((1, 1, tn), lambda l, j: (l, 0, j)),
        ],
        out_specs=pl.BlockSpec((1, MOD_ROWS, tn), lambda l, j: (l, 0, j)),
        out_shape=jax.ShapeDtypeStruct((depth, MOD_ROWS, nd), F32),
        compiler_params=_cparams("parallel", "parallel"),
        name="modulation",
    )(cvec, w_mod, b_mod.reshape(depth, 1, nd))


def _rms(x):
    return x * lax.rsqrt(jnp.mean(x * x, axis=-1, keepdims=True) + EPS)


def _modulated(h, m, g, mi):
    return (_rms(h) * g) * (1.0 + m[mi + 1:mi + 2]) + m[mi:mi + 1]


def _tile_specs(n_lat_tiles_per_batch, n_batch, d):
    def mod_idx(i):
        return (jnp.minimum(i // n_lat_tiles_per_batch, n_batch), 0, 0)
    h_spec = pl.BlockSpec((TOKEN_TILE, d), lambda i: (i, 0))
    m_spec = pl.BlockSpec((1, N_MOD, d), mod_idx)
    return h_spec, m_spec


def _ffn_kernel(h_ref, m_ref, g_ref, w1a_ref, w1b_ref, w2_ref, *rest, mi, final):
    o_ref = rest[-1]
    h = h_ref[...]
    m = m_ref[0]
    xm = _modulated(h, m, g_ref[...], mi).astype(BF16)
    a = jnp.dot(xm, w1a_ref[...], preferred_element_type=F32)
    b = jnp.dot(xm, w1b_ref[...], preferred_element_type=F32)
    hid = (a * jax.nn.sigmoid(a) * b).astype(BF16)
    y = jnp.dot(hid, w2_ref[...], preferred_element_type=F32)
    out = h + (0.5 * m[mi + 2:mi + 3]) * y
    if final:
        out = _rms(out) * rest[0][...]
    o_ref[...] = out


def _ffn(h, mod, g, w1a, w1b, w2, *, mi, n_rows, lat_tiles, n_batch, final_g=None):
    d = h.shape[1]
    f = w2.shape[0]
    h_spec, m_spec = _tile_specs(lat_tiles, n_batch, d)
    in_specs = [h_spec, m_spec, _resident((1, d)), _resident((d, f)), _resident((d, f)),
                _resident((f, d))]
    args = [h, mod, g.reshape(1, d), w1a, w1b, w2]
    if final_g is not None:
        in_specs.append(_resident((1, d)))
        args.append(final_g.reshape(1, d))
    return pl.pallas_call(
        functools.partial(_ffn_kernel, mi=mi, final=final_g is not None),
        grid=(n_rows // TOKEN_TILE,),
        in_specs=in_specs,
        out_specs=pl.BlockSpec((TOKEN_TILE, d), lambda i: (i, 0)),
        out_shape=jax.ShapeDtypeStruct((n_rows, d), F32),
        compiler_params=_cparams("parallel"),
        name="ffn",
    )(*args)


def _proj_kernel(h_ref, m_ref, g_ref, w_ref, o_ref, *, mi):
    xm = _modulated(h_ref[...], m_ref[0], g_ref[...], mi).astype(BF16)
    o_ref[...] = jnp.dot(xm, w_ref[...], preferred_element_type=F32).astype(o_ref.dtype)


def _proj(h, mod, g, w, *, mi, lat_tiles, n_batch, out_dtype):
    n_rows, d = h.shape
    n = w.shape[1]
    h_spec, m_spec = _tile_specs(lat_tiles, n_batch, d)
    return pl.pallas_call(
        functools.partial(_proj_kernel, mi=mi),
        grid=(n_rows // TOKEN_TILE,),
        in_specs=[h_spec, m_spec, _resident((1, d)), _resident((d, n))],
        out_specs=pl.BlockSpec((TOKEN_TILE, n), lambda i: (i, 0)),
        out_shape=jax.ShapeDtypeStruct((n_rows, n), out_dtype),
        compiler_params=_cparams("parallel"),
        name="mixer_in_proj",
    )(h, mod, g.reshape(1, d), w)


def _rope_tables(seq):
    half = RET_HEAD_DIM // 2
    quarter = half // 2
    inv = ROPE_BASE ** (-np.arange(0, half, 2, dtype=np.float64) / half)
    t = np.arange(seq)
    lane = np.arange(RET_HEAD_DIM)
    pos = np.where(lane[None, :] < half, (t // GRID_W)[:, None], (t % GRID_W)[:, None])
    ang = pos.astype(np.float32).astype(np.float64) * inv.astype(np.float32)[lane % quarter][None, :]
    first = (lane % half) < quarter
    cos = np.cos(ang)
    sin = np.sin(ang)
    sa = np.where(first[None, :], -sin, 0.0)
    sb = np.where(first[None, :], 0.0, sin)
    pad1 = np.ones((RET_CHUNK, RET_HEAD_DIM))
    pad0 = np.zeros((RET_CHUNK, RET_HEAD_DIM))
    tabs = [np.concatenate([cos, pad1]), np.concatenate([sa, pad0]), np.concatenate([sb, pad0])]
    return [jnp.asarray(x, F32) for x in tabs]


def _rope(x, cos, sa, sb):
    quarter = RET_HEAD_DIM // 4
    up = pltpu.roll(x, RET_HEAD_DIM - quarter, axis=1)
    dn = pltpu.roll(x, quarter, axis=1)
    return x * cos + up * sa + dn * sb


def _retention_kernel(dec_ref, q_ref, k_ref, v_ref, g_ref, cos_ref, sa_ref, sb_ref,
                      dmask_ref, qwf_ref, qwb_ref, kwf_ref, kwb_ref, o_ref,
                      s_run, s_bwd, *, n_chunks, n_ctx_chunks):
    ph = pl.program_id(1)
    i = pl.program_id(2)
    hd = RET_HEAD_DIM
    kscale = hd ** -0.5

    @pl.when(i == 0)
    def _():
        s_run[...] = jnp.zeros_like(s_run)

    cos, sa, sb = cos_ref[...], sa_ref[...], sb_ref[...]
    tn = (((0,), (0,)), ((), ()))
    nt = (((1,), (1,)), ((), ()))

    @pl.when(ph == 0)
    def _():
        cid = jnp.where(i < n_ctx_chunks, n_ctx_chunks - 1 - i, n_chunks + n_ctx_chunks - 1 - i)
        for h in range(RET_HEADS):
            sl = slice(h * hd, (h + 1) * hd)
            k = _rope(k_ref[:, sl], cos, sa, sb) * kscale
            v = v_ref[:, sl].astype(BF16)
            s_old = s_run[h]
            s_bwd[cid, h] = s_old
            kv = lax.dot_general((k * kwb_ref[h]).astype(BF16), v, tn, preferred_element_type=F32)
            s_run[h] = dec_ref[1, h] * s_old + kv

    @pl.when(ph == 1)
    def _():
        for h in range(RET_HEADS):
            sl = slice(h * hd, (h + 1) * hd)
            q = _rope(q_ref[:, sl], cos, sa, sb)
            k = _rope(k_ref[:, sl], cos, sa, sb) * kscale
            v = v_ref[:, sl].astype(BF16)
            s_old = s_run[h]
            a = lax.dot_general(q.astype(BF16), k.astype(BF16), nt, preferred_element_type=F32)
            o = jnp.dot((a * dmask_ref[h]).astype(BF16), v, preferred_element_type=F32)
            o += jnp.dot((q * qwf_ref[h]).astype(BF16), s_old.astype(BF16),
                         preferred_element_type=F32)
            o += jnp.dot((q * qwb_ref[h]).astype(BF16), s_bwd[i, h].astype(BF16),
                         preferred_element_type=F32)
            kv = lax.dot_general((k * kwf_ref[h]).astype(BF16), v, tn, preferred_element_type=F32)
            s_run[h] = dec_ref[0, h] * s_old + kv
            o = o * lax.rsqrt(jnp.mean(o * o, axis=-1, keepdims=True) + EPS)
            g = g_ref[:, sl]
            o_ref[:, sl] = o * (g * jax.nn.sigmoid(g))


def _retention(p, decay_logit, *, n_batch, seq, ctx_len):
    c = RET_CHUNK
    width = RET_HEADS * RET_HEAD_DIM
    n_lat = seq // c
    n_ctx = ctx_len // c
    n_ch = n_lat + n_ctx
    n_rows = p.shape[0]
    lat_blocks = n_batch * n_lat

    log_gamma = jax.nn.log_sigmoid(decay_logit.astype(F32))
    pos = jnp.arange(c, dtype=F32)
    diff = pos[:, None] - pos[None, :]
    lf = log_gamma[0][:, None, None]
    lb = log_gamma[1][:, None, None]
    dmask = (jnp.where(diff >= 0, jnp.exp(lf * jnp.maximum(diff, 0.0)), 0.0)
             + jnp.where(diff <= 0, jnp.exp(lb * jnp.maximum(-diff, 0.0)), 0.0))
    ones = jnp.ones((1, 1, RET_HEAD_DIM), F32)
    col = pos[None, :, None]
    qwf = jnp.exp(lf * (col + 1.0)) * ones
    qwb = jnp.exp(lb * (c - col)) * ones
    kwf = jnp.exp(lf * (c - 1.0 - col)) * ones
    kwb = jnp.exp(lb * col) * ones
    chunk_decay = jnp.exp(log_gamma * c)
    cos, sa, sb = _rope_tables(seq)

    def chunk_id(ph, i):
        bwd = jnp.where(i < n_ctx, n_ctx - 1 - i, n_ch + n_ctx - 1 - i)
        return jnp.where(ph == 0, bwd, i)

    def row_block(b, cid):
        return jnp.where(cid < n_ctx, lat_blocks + b * n_ctx + cid, b * n_lat + cid - n_ctx)

    def col_spec(j, forward_only):
        def idx(b, ph, i):
            cid = chunk_id(ph, i)
            if forward_only:
                cid = jnp.where(ph == 0, 0, cid)
            return (row_block(b, cid), j)
        return pl.BlockSpec((c, width), idx)

    def tab_idx(b, ph, i):
        cid = chunk_id(ph, i)
        return (jnp.where(cid < n_ctx, n_lat, cid - n_ctx), 0)

    tab_spec = pl.BlockSpec((c, RET_HEAD_DIM), tab_idx)
    head_tab = _resident((RET_HEADS, c, RET_HEAD_DIM))
    kern = functools.partial(_retention_kernel, n_chunks=n_ch, n_ctx_chunks=n_ctx)
    return pl.pallas_call(
        kern,
        grid=(n_batch, 2, n_ch),
        in_specs=[
            pl.BlockSpec(memory_space=pltpu.SMEM),
            col_spec(0, True), col_spec(1, False), col_spec(2, False), col_spec(3, True),
            tab_spec, tab_spec, tab_spec,
            head_tab, head_tab, head_tab, head_tab, head_tab,
        ],
        out_specs=col_spec(0, True),
        out_shape=jax.ShapeDtypeStruct((n_rows, width), F32),
        scratch_shapes=[
            pltpu.VMEM((RET_HEADS, RET_HEAD_DIM, RET_HEAD_DIM), F32),
            pltpu.VMEM((n_ch, RET_HEADS, RET_HEAD_DIM, RET_HEAD_DIM), F32),
        ],
        compiler_params=_cparams("parallel", "arbitrary", "arbitrary"),
        name="retention",
    )(chunk_decay, p, p, p, p, cos, sa, sb, dmask, qwf, qwb, kwf, kwb)


def _cmul(a, b):
    return a[0] * b[0] - a[1] * b[1], a[0] * b[1] + a[1] * b[0]


def _pair_blockdiag(a):
    g, r, c = a.shape
    a = a.reshape(g // 2, 2, r, c)
    eye = jnp.eye(2, dtype=a.dtype)
    return jnp.einsum('jarc,ab->jarbc', a, eye).reshape(g // 2, 2 * r, 2 * c)


def _pair_lanes(a):
    g, r, p = a.shape
    return a.reshape(g // 2, 2, r, p).transpose(0, 2, 1, 3).reshape(g // 2, r, 2 * p)


def _s5_weights(lam_re, lam_im, log_dt, b_re, b_im, c_re, c_im, d_skip):
    q = S5_CHUNK
    n_g = lam_re.shape[1]
    hi = lax.Precision.HIGHEST
    kall = None
    zw = []
    rw = []
    tabs = []
    for d in range(2):
        lr = jnp.minimum(lam_re[d].astype(F32), -1e-4)
        li = lam_im[d].astype(F32)
        dt = jnp.exp(log_dt[d].astype(F32))[:, None]
        mag = jnp.exp(lr * dt)
        a = (mag * jnp.cos(li * dt), mag * jnp.sin(li * dt))
        den = lr * lr + li * li
        am1 = (a[0] - 1.0, a[1])
        coef = ((am1[0] * lr + am1[1] * li) / den, (am1[1] * lr - am1[0] * li) / den)
        bb = _cmul((coef[0][..., None], coef[1][..., None]),
                   (b_re[d].astype(F32), b_im[d].astype(F32)))
        cm = (c_re[d].astype(F32), c_im[d].astype(F32))
        pw = [(jnp.ones_like(a[0]), jnp.zeros_like(a[0]))]
        for _ in range(q):
            pw.append(_cmul(pw[-1], a))
        pw_re = jnp.stack([x[0] for x in pw], axis=1)
        pw_im = jnp.stack([x[1] for x in pw], axis=1)
        e = _cmul((pw_re[:, :q, :, None], pw_im[:, :q, :, None]),
                  (bb[0][:, None], bb[1][:, None]))
        taps = (jnp.einsum('gkp,gtpj->gtkj', cm[0], e[0], precision=hi)
                - jnp.einsum('gkp,gtpj->gtkj', cm[1], e[1], precision=hi))
        if d == 0:
            kall = jnp.concatenate([jnp.zeros_like(taps[:, 1:]), taps], axis=1)
        else:
            kall = kall + jnp.concatenate([taps[:, ::-1], jnp.zeros_like(taps[:, 1:])], axis=1)
        tsel = np.arange(q)[::-1] if d == 0 else np.arange(q)
        zr = pw_re[:, tsel][..., None] * bb[0][:, None] - pw_im[:, tsel][..., None] * bb[1][:, None]
        zi = pw_re[:, tsel][..., None] * bb[1][:, None] + pw_im[:, tsel][..., None] * bb[0][:, None]
        zw.append((zr.transpose(0, 1, 3, 2).reshape(n_g, q * S5_GROUP, S5_STATE),
                   zi.transpose(0, 1, 3, 2).reshape(n_g, q * S5_GROUP, S5_STATE)))
        rsel = np.arange(1, q + 1) if d == 0 else np.arange(q, 0, -1)
        rr = cm[0][:, None] * pw_re[:, rsel][:, :, None, :] - cm[1][:, None] * pw_im[:, rsel][:, :, None, :]
        ri = cm[0][:, None] * pw_im[:, rsel][:, :, None, :] + cm[1][:, None] * pw_re[:, rsel][:, :, None, :]
        rw.append((rr.transpose(0, 3, 1, 2).reshape(n_g, S5_STATE, q * S5_GROUP),
                   (-ri).transpose(0, 3, 1, 2).reshape(n_g, S5_STATE, q * S5_GROUP)))
        a16 = (pw_re[:, q], pw_im[:, q])
        a32 = _cmul(a16, a16)
        a64 = _cmul(a32, a32)
        a128 = _cmul(a64, a64)
        ramp = [(jnp.ones_like(a[0]), jnp.zeros_like(a[0]))]
        for _ in range(S5_TILE - 1):
            ramp.append(_cmul(ramp[-1], a16))
        if d == 1:
            ramp = ramp[::-1]
        rows = [a16[0], a16[1], a32[0], a32[1], a64[0], a64[1], a128[0], a128[1]]
        rows += [x[0] for x in ramp] + [x[1] for x in ramp]
        tabs.append(jnp.stack(rows, axis=1))
    dsk = d_skip.astype(F32).reshape(n_g, S5_GROUP)
    eye = jnp.eye(S5_GROUP, dtype=F32)
    kall = kall.at[:, q - 1].add(dsk[:, :, None] * eye[None])
    delta = np.arange(q)[None, :] - np.arange(q)[:, None] + (q - 1)
    m = kall[:, delta]
    m = m.transpose(0, 1, 4, 2, 3).reshape(n_g, q * S5_GROUP, q * S5_GROUP)
    w1 = jnp.concatenate([_pair_blockdiag(m)] + [_pair_blockdiag(z) for pair in zw for z in pair],
                         axis=2)
    w2 = jnp.concatenate([_pair_blockdiag(r) for pair in rw for r in pair], axis=1)
    tab = jnp.concatenate([_pair_lanes(t) for t in tabs], axis=1)
    return w1.astype(BF16), w2.astype(BF16), tab


def _tile_shift(x, s, down):
    row = lax.broadcasted_iota(jnp.int32, x.shape, 1)
    if down:
        return jnp.where(row >= s, pltpu.roll(x, s, axis=1), 0.0)
    return jnp.where(row < S5_TILE - s, pltpu.roll(x, S5_TILE - s, axis=1), 0.0)


S5_LANE_GROUPS = 128 // S5_GROUP
S5_SLAB = 64


def _block_transpose(arrs):
    lane = lax.broadcasted_iota(jnp.int32, arrs[0].shape, 1)
    blk = lane // S5_GROUP
    a = list(arrs)
    for bit in range(3):
        s = 1 << bit
        hi = (blk & s) != 0
        new = list(a)
        for i in range(S5_LANE_GROUPS):
            if i & s:
                continue
            new[i] = jnp.where(hi, pltpu.roll(a[i + s], S5_GROUP * s, axis=1), a[i])
            new[i + s] = jnp.where(hi, a[i + s], pltpu.roll(a[i], 128 - S5_GROUP * s, axis=1))
        a = new
    return a


def _s5_kernel(ul_ref, uc_ref, w1_ref, w2_ref, tab_ref, yl_ref, yc_ref, v_scr, y_scr, zx, xin,
               *, lat_tiles, ctx_tiles):
    q = S5_CHUNK
    lanes = 2 * S5_STATE
    pw = 2 * q * S5_GROUP
    n_tiles = ctx_tiles + lat_tiles
    ctx_rows = ctx_tiles * S5_TILE
    lat_rows = lat_tiles * S5_TILE
    slab = min(S5_SLAB, lat_rows)
    half = S5_TILE

    def load_slab(src_ref, r_src, r_dst, n):
        for th in range(q // half):
            arrs = [src_ref[pl.ds(r_src * q + th * half + tl, n, stride=q), :] for tl in range(half)]
            outs = _block_transpose(arrs)
            for g in range(S5_LANE_GROUPS):
                lo = (g % 2) * (pw // 2) + th * 128
                v_scr[g // 2, pl.ds(r_dst, n), lo:lo + 128] = outs[g]

    def store_slab(dst_ref, r_dst, r_src, n):
        for th in range(q // half):
            arrs = []
            for g in range(S5_LANE_GROUPS):
                lo = (g % 2) * (pw // 2) + th * 128
                arrs.append(y_scr[g // 2, pl.ds(r_src, n), lo:lo + 128])
            outs = _block_transpose(arrs)
            for tl in range(half):
                dst_ref[pl.ds(r_dst * q + th * half + tl, n, stride=q), :] = outs[tl]

    load_slab(uc_ref, 0, 0, ctx_rows)

    def load_body(i, carry):
        r = pl.multiple_of(i * slab, slab)
        load_slab(ul_ref, r, ctx_rows + r, slab)
        return carry

    lax.fori_loop(0, lat_rows // slab, load_body, 0)

    def pair_body(p, carry):
        y = jnp.dot(v_scr[p].astype(BF16), w1_ref[p], preferred_element_type=F32)
        y_scr[p] = y[:, :pw]

        def tab_row(r):
            return tab_ref[p, r:r + 1, :]

        for d in range(2):
            down = d == 0
            t0 = 24 * d
            lo = pw + 2 * d * lanes
            z = (y[:, lo:lo + lanes].reshape(n_tiles, S5_TILE, lanes),
                 y[:, lo + lanes:lo + 2 * lanes].reshape(n_tiles, S5_TILE, lanes))
            loc = (_tile_shift(z[0], 1, down), _tile_shift(z[1], 1, down))
            for k, s in enumerate((1, 2, 4)):
                mul = (tab_row(t0 + 2 * k)[None], tab_row(t0 + 2 * k + 1)[None])
                inc = _cmul(mul, (_tile_shift(loc[0], s, down), _tile_shift(loc[1], s, down)))
                loc = (loc[0] + inc[0], loc[1] + inc[1])
            zx[2 * d] = loc[0]
            zx[2 * d + 1] = loc[1]
            zx[4 + 2 * d] = z[0]
            zx[4 + 2 * d + 1] = z[1]

        for d in range(2):
            t0 = 24 * d
            edge = S5_TILE - 1 if d == 0 else 0
            a16 = (tab_row(t0), tab_row(t0 + 1))
            a128 = (tab_row(t0 + 6), tab_row(t0 + 7))
            ramp = (tab_ref[p, t0 + 8:t0 + 16, :], tab_ref[p, t0 + 16:t0 + 24, :])

            def tile_step(j, carry, d=d, edge=edge, a16=a16, a128=a128, ramp=ramp):
                loc = (zx[2 * d, j], zx[2 * d + 1, j])
                z = (zx[4 + 2 * d, j], zx[4 + 2 * d + 1, j])
                inc = _cmul(ramp, carry)
                xin[j, :, (2 * d) * lanes:(2 * d + 1) * lanes] = loc[0] + inc[0]
                xin[j, :, (2 * d + 1) * lanes:(2 * d + 2) * lanes] = loc[1] + inc[1]
                e_loc = _cmul(a16, (loc[0][edge:edge + 1], loc[1][edge:edge + 1]))
                nxt = _cmul(a128, carry)
                return (nxt[0] + e_loc[0] + z[0][edge:edge + 1],
                        nxt[1] + e_loc[1] + z[1][edge:edge + 1])

            zero = (jnp.zeros((1, lanes), F32), jnp.zeros((1, lanes), F32))
            if d == 0:
                lax.fori_loop(0, n_tiles, tile_step, zero)
            else:
                mid = lax.fori_loop(0, ctx_tiles, lambda t, c: tile_step(ctx_tiles - 1 - t, c), zero)
                lax.fori_loop(0, lat_tiles, lambda t, c: tile_step(n_tiles - 1 - t, c), mid)

        x = xin[...].reshape(n_tiles * S5_TILE, 4 * lanes)
        y_scr[p] += jnp.dot(x.astype(BF16), w2_ref[p], preferred_element_type=F32)
        return carry

    lax.fori_loop(0, v_scr.shape[0], pair_body, 0)

    store_slab(yc_ref, 0, 0, ctx_rows)

    def store_body(i, carry):
        r = pl.multiple_of(i * slab, slab)
        store_slab(yl_ref, r, ctx_rows + r, slab)
        return carry

    lax.fori_loop(0, lat_rows // slab, store_body, 0)


def _s5(p, col0, weights, *, n_batch, seq, ctx_len):
    w1, w2, tab = weights
    q = S5_CHUNK
    lanes = 2 * S5_STATE
    pw = 2 * q * S5_GROUP
    n_pairs = w1.shape[0]
    ppb = S5_LANE_GROUPS // 2
    n_blocks = n_pairs // ppb
    width = n_blocks * 128
    lat_tiles = seq // q // S5_TILE
    ctx_tiles = ctx_len // q // S5_TILE
    n_tiles = lat_tiles + ctx_tiles
    n_sup = n_tiles * S5_TILE
    cb0 = col0 // 128
    ctx_blk0 = n_batch * seq // ctx_len
    kern = functools.partial(_s5_kernel, lat_tiles=lat_tiles, ctx_tiles=ctx_tiles)
    return pl.pallas_call(
        kern,
        grid=(n_blocks, n_batch),
        in_specs=[
            pl.BlockSpec((seq, 128), lambda j, b: (b, cb0 + j)),
            pl.BlockSpec((ctx_len, 128), lambda j, b: (ctx_blk0 + b, cb0 + j)),
            pl.BlockSpec((ppb, pw, 2 * pw), lambda j, b: (j, 0, 0)),
            pl.BlockSpec((ppb, 4 * lanes, pw), lambda j, b: (j, 0, 0)),
            pl.BlockSpec((ppb, tab.shape[1], lanes), lambda j, b: (j, 0, 0)),
        ],
        out_specs=[pl.BlockSpec((seq, 128), lambda j, b: (b, j)),
                   pl.BlockSpec((ctx_len, 128), lambda j, b: (b, j))],
        out_shape=[jax.ShapeDtypeStruct((n_batch * seq, width), F32),
                   jax.ShapeDtypeStruct((n_batch * ctx_len, width), F32)],
        scratch_shapes=[
            pltpu.VMEM((ppb, n_sup, pw), F32),
            pltpu.VMEM((ppb, n_sup, pw), F32),
            pltpu.VMEM((8, n_tiles, S5_TILE, lanes), F32),
            pltpu.VMEM((n_tiles, S5_TILE, 4 * lanes), F32),
        ],
        compiler_params=_cparams("parallel", "parallel"),
        name="s5",
    )(p, p, w1, w2, tab)


def _ab_out_kernel(h_ref, m_ref, r_ref, yl_ref, yc_ref, gw_ref, gb_ref, wr_ref, ws_ref, o_ref,
                   *, n_lat_tiles):
    y = jnp.where(pl.program_id(0) < n_lat_tiles, yl_ref[...], yc_ref[...])
    g = 0.5 * y * (1.0 + jnp.tanh(math.sqrt(2.0 / math.pi) * (y + 0.044715 * (y * y * y))))
    s = g * jax.nn.sigmoid(jnp.dot(g.astype(BF16), gw_ref[...], preferred_element_type=F32)
                           + gb_ref[...])
    out = jnp.dot(r_ref[...].astype(BF16), wr_ref[...], preferred_element_type=F32)
    out += jnp.dot(s.astype(BF16), ws_ref[...], preferred_element_type=F32)
    o_ref[...] = h_ref[...] + m_ref[0][5:6] * out


def _ab_out(h, mod, r, ys_lat, ys_ctx, glu_w, glu_b, w_out, *, lat_tiles, n_batch):
    n_rows, d = h.shape
    w = r.shape[1]
    n_lat_tiles = ys_lat.shape[0] // TOKEN_TILE
    h_spec, m_spec = _tile_specs(lat_tiles, n_batch, d)
    half = pl.BlockSpec((TOKEN_TILE, w), lambda i: (i, 0))
    lat = pl.BlockSpec((TOKEN_TILE, w), lambda i: (jnp.minimum(i, n_lat_tiles - 1), 0))
    ctx = pl.BlockSpec((TOKEN_TILE, w), lambda i: (jnp.maximum(i - n_lat_tiles, 0), 0))
    return pl.pallas_call(
        functools.partial(_ab_out_kernel, n_lat_tiles=n_lat_tiles),
        grid=(n_rows // TOKEN_TILE,),
        in_specs=[h_spec, m_spec, half, lat, ctx, _resident((w, w)), _resident((1, w)),
                  _resident((w, d)), _resident((w, d))],
        out_specs=pl.BlockSpec((TOKEN_TILE, d), lambda i: (i, 0)),
        out_shape=jax.ShapeDtypeStruct((n_rows, d), F32),
        compiler_params=_cparams("parallel"),
        name="ab_out_proj",
    )(h, mod, r, ys_lat, ys_ctx, glu_w.astype(BF16), glu_b.reshape(1, w).astype(F32),
      w_out[:w].astype(BF16), w_out[w:].astype(BF16))


NA_QROWS = 4
NA_KROWS = NA_QROWS + NA_KH


def _na_bias_table(rpb):
    w = GRID_W
    qcol = np.arange(w)
    kcol = np.arange(w)
    wstart = np.clip(qcol - NA_KW // 2, 0, w - NA_KW)
    valid = (kcol[None, :] >= wstart[:, None]) & (kcol[None, :] < wstart[:, None] + NA_KW)
    rel = np.clip(kcol[None, :] - qcol[:, None], -(NA_KW - 1), NA_KW - 1) + NA_KW - 1
    onehot = (rel[None] == np.arange(2 * NA_KW - 1)[:, None, None]).astype(np.float32)
    tiles = jnp.einsum('hrj,jqk->hrqk', rpb.astype(F32), jnp.asarray(onehot),
                       precision=lax.Precision.HIGHEST)
    tiles = jnp.where(jnp.asarray(valid)[None, None], tiles, NEG_INF)
    neg = jnp.full((NA_HEADS, w, w), NEG_INF, F32)
    types = [lambda a: (0, NA_KH - 1 - a), lambda a: (a, NA_QROWS - 1 - a),
             lambda a: (NA_QROWS, -1 - a)]
    slabs = []
    for ty in types:
        for a in range(NA_QROWS):
            m0, rel0 = ty(a)
            row = [tiles[:, rel0 + m] if m0 <= m < m0 + NA_KH else neg for m in range(NA_KROWS)]
            slabs.append(jnp.concatenate(row, axis=-1))
    bias = jnp.stack(slabs, axis=1).reshape(NA_HEADS // 2, 2, 3, NA_QROWS * w, NA_KROWS * w)
    return bias.transpose(0, 2, 1, 3, 4).reshape(NA_HEADS // 2, 3, 2 * NA_QROWS * w, NA_KROWS * w)


def _na_kernel(q_ref, k_ref, v_ref, kc_ref, vc_ref, bias_ref, o_ref, *, rows):
    w = GRID_W
    dh = NA_HEAD_DIM
    nq = NA_QROWS * w
    nk = NA_KROWS * w
    n_blocks = rows // NA_QROWS
    nt = (((1,), (1,)), ((), ()))
    scale = dh ** -0.5
    kc = kc_ref[...]
    vc = vc_ref[...]
    lane = lax.broadcasted_iota(jnp.int32, (nq, 2 * dh), 1)
    first = lane < dh

    def body(i, carry):
        r0 = jnp.clip(NA_QROWS * i - NA_KH // 2, 0, rows - NA_KROWS)
        kind = jnp.where(i == 0, 0, jnp.where(i == n_blocks - 1, 2, 1))
        qoff = pl.multiple_of(i * nq, nq)
        q = q_ref[pl.ds(qoff, nq), :] * scale
        zero = jnp.zeros_like(q)
        qs = jnp.concatenate([jnp.where(first, q, zero), jnp.where(first, zero, q)], axis=0)
        koff = pl.multiple_of(r0 * w, NA_QROWS * w)
        kl = k_ref[pl.ds(koff, nk), :]
        vl = v_ref[pl.ds(koff, nk), :]
        s_loc = lax.dot_general(qs, kl, nt, preferred_element_type=F32) + bias_ref[0, kind]
        s_ctx = lax.dot_general(qs, kc, nt, preferred_element_type=F32)
        m = jnp.maximum(jnp.max(s_loc, axis=-1, keepdims=True), jnp.max(s_ctx, axis=-1, keepdims=True))
        p_loc = jnp.exp(s_loc - m)
        p_ctx = jnp.exp(s_ctx - m)
        den = jnp.sum(p_loc, axis=-1, keepdims=True) + jnp.sum(p_ctx, axis=-1, keepdims=True)
        o = jnp.dot(p_loc.astype(BF16), vl, preferred_element_type=F32)
        o += jnp.dot(p_ctx.astype(BF16), vc, preferred_element_type=F32)
        o = o / den
        o_ref[pl.ds(qoff, nq), :] = jnp.where(first, o[:nq], o[nq:]).astype(o_ref.dtype)
        return carry

    lax.fori_loop(0, n_blocks, body, 0, unroll=8)


def _natten(p, bias, *, n_batch, seq, ctx_len):
    d = NA_HEADS * NA_HEAD_DIM
    lb = 2 * NA_HEAD_DIM
    n_pairs = NA_HEADS // 2
    ctx_blk0 = n_batch * seq // ctx_len
    rows = seq // GRID_W
    assert rows % NA_QROWS == 0 and rows >= NA_KROWS + NA_QROWS
    lat = lambda part: pl.BlockSpec((seq, lb), lambda j, b: (b, part * n_pairs + j))
    ctx = lambda part: pl.BlockSpec((ctx_len, lb), lambda j, b: (ctx_blk0 + b, part * n_pairs + j))
    return pl.pallas_call(
        functools.partial(_na_kernel, rows=rows),
        grid=(n_pairs, n_batch),
        in_specs=[lat(0), lat(1), lat(2), ctx(1), ctx(2),
                  pl.BlockSpec((1,) + bias.shape[1:], lambda j, b: (j, 0, 0, 0))],
        out_specs=pl.BlockSpec((seq, lb), lambda j, b: (b, j)),
        out_shape=jax.ShapeDtypeStruct((n_batch * seq, d), BF16),
        compiler_params=_cparams("parallel", "parallel"),
        name="natten",
    )(p, p, p, p, p, bias)


def _na_out_kernel(h_ref, m_ref, a_ref, w_ref, o_ref):
    y = jnp.dot(a_ref[...], w_ref[...], preferred_element_type=F32)
    o_ref[...] = h_ref[...] + m_ref[0][5:6] * y


def _na_out(h, mod, att, w_o, *, lat_tiles, n_batch):
    n_rows, d = att.shape
    h_spec, m_spec = _tile_specs(lat_tiles, n_batch, d)
    return pl.pallas_call(
        _na_out_kernel,
        grid=(n_rows // TOKEN_TILE,),
        in_specs=[h_spec, m_spec, pl.BlockSpec((TOKEN_TILE, d), lambda i: (i, 0)), _resident((d, d))],
        out_specs=pl.BlockSpec((TOKEN_TILE, d), lambda i: (i, 0)),
        out_shape=jax.ShapeDtypeStruct((n_rows, d), F32),
        compiler_params=_cparams("parallel"),
        name="na_out_proj",
    )(h, mod, att, w_o.astype(BF16))


def kernel(x, c, ctx, c_ctx, w_mod, b_mod, norm_g, ffn_w1, ffn_w2, w_in_ab, w_out_ab, ret_decay_logit, s5_lam_re, s5_lam_im, s5_log_dt, s5_b_re, s5_b_im, s5_c_re, s5_c_im, s5_d, s5_glu_w, s5_glu_b, na_w_qkv, na_w_o, na_rpb, final_g):
    n_batch, seq, d = x.shape
    ctx_len = ctx.shape[1]
    depth = w_mod.shape[0]
    f = ffn_w2.shape[2]
    n_lat = n_batch * seq
    n_all = n_lat + n_batch * ctx_len
    lat_tiles = seq // TOKEN_TILE
    assert seq % TOKEN_TILE == 0 and (n_batch * ctx_len) % TOKEN_TILE == 0
    assert n_batch + 1 <= MOD_ROWS and seq % (GRID_W * NA_KH) == 0

    cvec = jnp.concatenate([c, c_ctx[None], jnp.zeros((MOD_ROWS - n_batch - 1, d), F32)], axis=0)
    mod = _modulation(cvec, w_mod, b_mod).reshape(depth, MOD_ROWS, N_MOD, d)
    h = jnp.concatenate([x.reshape(n_lat, d), ctx.reshape(n_batch * ctx_len, d)], axis=0)
    common = dict(lat_tiles=lat_tiles, n_batch=n_batch)
    dims = dict(n_batch=n_batch, seq=seq, ctx_len=ctx_len)

    for layer in range(depth):
        last = layer == depth - 1
        i = layer // 2
        m = mod[layer]
        w1 = ffn_w1[layer].astype(BF16)
        w2 = ffn_w2[layer].astype(BF16)
        h = _ffn(h, m, norm_g[layer, 0], w1[0, :, :f], w1[0, :, f:], w2[0], mi=0,
                 n_rows=n_all, **common)
        if layer % 2 == 0:
            p = _proj(h, m, norm_g[layer, 1], w_in_ab[i].astype(BF16), mi=3, out_dtype=F32, **common)
            r = _retention(p, ret_decay_logit[i], **dims)
            weights = _s5_weights(s5_lam_re[i], s5_lam_im[i], s5_log_dt[i], s5_b_re[i], s5_b_im[i],
                                  s5_c_re[i], s5_c_im[i], s5_d[i])
            ys_lat, ys_ctx = _s5(p, 4 * RET_HEADS * RET_HEAD_DIM, weights, **dims)
            h = _ab_out(h, m, r, ys_lat, ys_ctx, s5_glu_w[i], s5_glu_b[i], w_out_ab[i], **common)
        else:
            assert last
            p = _proj(h, m, norm_g[layer, 1], na_w_qkv[i].astype(BF16), mi=3, out_dtype=BF16, **common)
            att = _natten(p, _na_bias_table(na_rpb[i]), **dims)
            h = _na_out(h, m, att, na_w_o[i], **common)
        n_rows = n_lat if last else n_all
        h = _ffn(h, m, norm_g[layer, 2], w1[1, :, :f], w1[1, :, f:], w2[1], mi=6, n_rows=n_rows,
                 final_g=final_g if last else None, **common)
    return h[:n_lat].reshape(n_batch, seq, d)
```

```python
import functools
import math

import numpy as np
import jax
import jax.numpy as jnp
from jax import lax
from jax.experimental import pallas as pl
from jax.experimental.pallas import tpu as pltpu

F32 = jnp.float32
BF16 = jnp.bfloat16

EPS = 1e-6
ROPE_BASE = 10000.0
GRID_W = 64
N_MOD = 9
RET_HEADS = 4
RET_HEAD_DIM = 128
RET_CHUNK = 256
S5_GROUP = 16
S5_STATE = 64
S5_CHUNK = 16
S5_TILE = 8
NA_HEADS = 16
NA_HEAD_DIM = 64
NA_KH = 8
NA_KW = 16
NEG_INF = -1e30

TOKEN_TILE = 512
VMEM_LIMIT = 56 * 1024 * 1024
MOD_ROWS = 8


def _cparams(*sem):
    return pltpu.CompilerParams(dimension_semantics=sem, vmem_limit_bytes=VMEM_LIMIT)


def _resident(shape):
    nd = len(shape)
    return pl.BlockSpec(shape, lambda *_: (0,) * nd, pipeline_mode=pl.Buffered(1))


def _mod_kernel(c_ref, w_ref, b_ref, o_ref):
    c = c_ref[...]
    s = c * jax.nn.sigmoid(c)
    o_ref[0] = jnp.dot(s, w_ref[0], preferred_element_type=F32,
                       precision=lax.Precision.HIGHEST) + b_ref[0]


def _modulation(cvec, w_mod, b_mod):
    depth, d, nd = w_mod.shape
    tn = d
    return pl.pallas_call(
        _mod_kernel,
        grid=(depth, nd // tn),
        in_specs=[
            pl.BlockSpec((MOD_ROWS, d), lambda l, j: (0, 0)),
            pl.BlockSpec((1, d, tn), lambda l, j: (l, 0, j)),
            pl.BlockSpec((1, 1, tn), lambda l, j: (l, 0, j)),
        ],
        out_specs=pl.BlockSpec((1, MOD_ROWS, tn), lambda l, j: (l, 0, j)),
        out_shape=jax.ShapeDtypeStruct((depth, MOD_ROWS, nd), F32),
        compiler_params=_cparams("parallel", "parallel"),
        name="modulation",
    )(cvec, w_mod, b_mod.reshape(depth, 1, nd))


def _rms(x):
    return x * lax.rsqrt(jnp.mean(x * x, axis=-1, keepdims=True) + EPS)


def _modulated(h, m, g, mi):
    return (_rms(h) * g) * (1.0 + m[mi + 1:mi + 2]) + m[mi:mi + 1]


def _tile_specs(n_lat_tiles_per_batch, n_batch, d):
    def mod_idx(i):
        return (jnp.minimum(i // n_lat_tiles_per_batch, n_batch), 0, 0)
    h_spec = pl.BlockSpec((TOKEN_TILE, d), lambda i: (i, 0))
    m_spec = pl.BlockSpec((1, N_MOD, d), mod_idx)
    return h_spec, m_spec


def _pinned(block_shape, index):
    return pl.BlockSpec(block_shape, lambda *_: index, pipeline_mode=pl.Buffered(1))


def _stream_specs(parts, width):
    if len(parts) == 1:
        return [pl.BlockSpec((TOKEN_TILE, width), lambda i: (i, 0))], 0
    n0 = parts[0].shape[0] // TOKEN_TILE
    return [pl.BlockSpec((TOKEN_TILE, width), lambda i: (jnp.minimum(i, n0 - 1), 0)),
            pl.BlockSpec((TOKEN_TILE, width), lambda i: (jnp.maximum(i - n0, 0), 0))], n0


def _stream_tile(refs, n0):
    if len(refs) == 1:
        return refs[0][...]
    return jnp.where(pl.program_id(0) < n0, refs[0][...], refs[1][...])


def _ffn_kernel(*refs, n_h, n0, mi, final):
    h_refs = refs[:n_h]
    m_ref, g_ref, w1a_ref, w1b_ref, w2_ref = refs[n_h:n_h + 5]
    o_ref = refs[-1]
    h = _stream_tile(h_refs, n0)
    m = m_ref[0]
    xm = _modulated(h, m, g_ref[...], mi).astype(BF16)
    a = jnp.dot(xm, w1a_ref[...], preferred_element_type=F32)
    b = jnp.dot(xm, w1b_ref[...], preferred_element_type=F32)
    hid = (a * jax.nn.sigmoid(a) * b).astype(BF16)
    y = jnp.dot(hid, w2_ref[...], preferred_element_type=F32)
    out = h + (0.5 * m[mi + 2:mi + 3]) * y
    if final:
        out = _rms(out) * refs[-2][...]
    o_ref[...] = out


def _ffn(h_parts, mod, g, w1, w2, *, layer, k, n_rows, lat_tiles, n_batch, final_g=None):
    d = h_parts[0].shape[1]
    f = w2.shape[2]
    h_specs, n0 = _stream_specs(h_parts, d)
    _, m_spec = _tile_specs(lat_tiles, n_batch, d)
    in_specs = h_specs + [
        m_spec,
        _pinned((None, None, 1, d), (layer, 2 * k, 0, 0)),
        _pinned((None, None, d, f), (layer, k, 0, 0)),
        _pinned((None, None, d, f), (layer, k, 0, 1)),
        _pinned((None, None, f, d), (layer, k, 0, 0)),
    ]
    args = list(h_parts) + [mod, g, w1, w1, w2]
    if final_g is not None:
        in_specs.append(_resident((1, d)))
        args.append(final_g.reshape(1, d))
    return pl.pallas_call(
        functools.partial(_ffn_kernel, n_h=len(h_parts), n0=n0, mi=6 * k, final=final_g is not None),
        grid=(n_rows // TOKEN_TILE,),
        in_specs=in_specs,
        out_specs=pl.BlockSpec((TOKEN_TILE, d), lambda i: (i, 0)),
        out_shape=jax.ShapeDtypeStruct((n_rows, d), F32),
        compiler_params=_cparams("parallel"),
        name="ffn",
    )(*args)


def _proj_kernel(h_ref, m_ref, g_ref, w_ref, o_ref, *, mi):
    xm = _modulated(h_ref[...], m_ref[0], g_ref[...], mi).astype(BF16)
    o_ref[...] = jnp.dot(xm, w_ref[...], preferred_element_type=F32).astype(o_ref.dtype)


def _proj(h, mod, g, w, *, layer, wi, lat_tiles, n_batch, out_dtype):
    n_rows, d = h.shape
    n = w.shape[2]
    h_spec, m_spec = _tile_specs(lat_tiles, n_batch, d)
    return pl.pallas_call(
        functools.partial(_proj_kernel, mi=3),
        grid=(n_rows // TOKEN_TILE,),
        in_specs=[h_spec, m_spec, _pinned((None, None, 1, d), (layer, 1, 0, 0)),
                  _pinned((None, d, n), (wi, 0, 0))],
        out_specs=pl.BlockSpec((TOKEN_TILE, n), lambda i: (i, 0)),
        out_shape=jax.ShapeDtypeStruct((n_rows, n), out_dtype),
        compiler_params=_cparams("parallel"),
        name="mixer_in_proj",
    )(h, mod, g, w)


def _rope_tables(seq):
    half = RET_HEAD_DIM // 2
    quarter = half // 2
    inv = ROPE_BASE ** (-np.arange(0, half, 2, dtype=np.float64) / half)
    t = np.arange(seq)
    lane = np.arange(RET_HEAD_DIM)
    pos = np.where(lane[None, :] < half, (t // GRID_W)[:, None], (t % GRID_W)[:, None])
    ang = pos.astype(np.float32).astype(np.float64) * inv.astype(np.float32)[lane % quarter][None, :]
    first = (lane % half) < quarter
    cos = np.cos(ang)
    sin = np.sin(ang)
    sa = np.where(first[None, :], -sin, 0.0)
    sb = np.where(first[None, :], 0.0, sin)
    pad1 = np.ones((RET_CHUNK, RET_HEAD_DIM))
    pad0 = np.zeros((RET_CHUNK, RET_HEAD_DIM))
    tabs = [np.concatenate([cos, pad1]), np.concatenate([sa, pad0]), np.concatenate([sb, pad0])]
    return [jnp.asarray(x, F32) for x in tabs]


def _rope(x, cos, sa, sb):
    quarter = RET_HEAD_DIM // 4
    up = pltpu.roll(x, RET_HEAD_DIM - quarter, axis=1)
    dn = pltpu.roll(x, quarter, axis=1)
    return x * cos + up * sa + dn * sb


def _retention_kernel(dec_ref, q_ref, k_ref, v_ref, g_ref, cos_ref, sa_ref, sb_ref,
                      dmask_ref, qwf_ref, qwb_ref, kwf_ref, kwb_ref, o_ref,
                      s_run, s_bwd, *, n_chunks, n_ctx_chunks):
    ph = pl.program_id(1)
    i = pl.program_id(2)
    hd = RET_HEAD_DIM
    kscale = hd ** -0.5

    @pl.when(i == 0)
    def _():
        s_run[...] = jnp.zeros_like(s_run)

    cos, sa, sb = cos_ref[...], sa_ref[...], sb_ref[...]
    tn = (((0,), (0,)), ((), ()))
    nt = (((1,), (1,)), ((), ()))

    @pl.when(ph == 0)
    def _():
        cid = jnp.where(i < n_ctx_chunks, n_ctx_chunks - 1 - i, n_chunks + n_ctx_chunks - 1 - i)
        for h in range(RET_HEADS):
            sl = slice(h * hd, (h + 1) * hd)
            k = _rope(k_ref[:, sl], cos, sa, sb) * kscale
            v = v_ref[:, sl].astype(BF16)
            s_old = s_run[h]
            s_bwd[cid, h] = s_old
            kv = lax.dot_general((k * kwb_ref[h]).astype(BF16), v, tn, preferred_element_type=F32)
            s_run[h] = dec_ref[1, h] * s_old + kv

    @pl.when(ph == 1)
    def _():
        for h in range(RET_HEADS):
            sl = slice(h * hd, (h + 1) * hd)
            q = _rope(q_ref[:, sl], cos, sa, sb)
            k = _rope(k_ref[:, sl], cos, sa, sb) * kscale
            v = v_ref[:, sl].astype(BF16)
            s_old = s_run[h]
            a = lax.dot_general(q.astype(BF16), k.astype(BF16), nt, preferred_element_type=F32)
            o = jnp.dot((a * dmask_ref[h]).astype(BF16), v, preferred_element_type=F32)
            o += jnp.dot((q * qwf_ref[h]).astype(BF16), s_old.astype(BF16),
                         preferred_element_type=F32)
            o += jnp.dot((q * qwb_ref[h]).astype(BF16), s_bwd[i, h].astype(BF16),
                         preferred_element_type=F32)
            kv = lax.dot_general((k * kwf_ref[h]).astype(BF16), v, tn, preferred_element_type=F32)
            s_run[h] = dec_ref[0, h] * s_old + kv
            o = o * lax.rsqrt(jnp.mean(o * o, axis=-1, keepdims=True) + EPS)
            g = g_ref[:, sl]
            o_ref[:, sl] = o * (g * jax.nn.sigmoid(g))


def _retention(p, decay_logit, *, n_batch, seq, ctx_len):
    c = RET_CHUNK
    width = RET_HEADS * RET_HEAD_DIM
    n_lat = seq // c
    n_ctx = ctx_len // c
    n_ch = n_lat + n_ctx
    n_rows = p.shape[0]
    lat_blocks = n_batch * n_lat

    log_gamma = jax.nn.log_sigmoid(decay_logit.astype(F32))
    pos = jnp.arange(c, dtype=F32)
    diff = pos[:, None] - pos[None, :]
    lf = log_gamma[0][:, None, None]
    lb = log_gamma[1][:, None, None]
    dmask = (jnp.where(diff >= 0, jnp.exp(lf * jnp.maximum(diff, 0.0)), 0.0)
             + jnp.where(diff <= 0, jnp.exp(lb * jnp.maximum(-diff, 0.0)), 0.0))
    ones = jnp.ones((1, 1, RET_HEAD_DIM), F32)
    col = pos[None, :, None]
    qwf = jnp.exp(lf * (col + 1.0)) * ones
    qwb = jnp.exp(lb * (c - col)) * ones
    kwf = jnp.exp(lf * (c - 1.0 - col)) * ones
    kwb = jnp.exp(lb * col) * ones
    chunk_decay = jnp.exp(log_gamma * c)
    cos, sa, sb = _rope_tables(seq)

    def chunk_id(ph, i):
        bwd = jnp.where(i < n_ctx, n_ctx - 1 - i, n_ch + n_ctx - 1 - i)
        return jnp.where(ph == 0, bwd, i)

    def row_block(b, cid):
        return jnp.where(cid < n_ctx, lat_blocks + b * n_ctx + cid, b * n_lat + cid - n_ctx)

    def col_spec(j, forward_only):
        def idx(b, ph, i):
            cid = chunk_id(ph, i)
            if forward_only:
                cid = jnp.where(ph == 0, 0, cid)
            return (row_block(b, cid), j)
        return pl.BlockSpec((c, width), idx)

    def tab_idx(b, ph, i):
        cid = chunk_id(ph, i)
        return (jnp.where(cid < n_ctx, n_lat, cid - n_ctx), 0)

    tab_spec = pl.BlockSpec((c, RET_HEAD_DIM), tab_idx)
    head_tab = _resident((RET_HEADS, c, RET_HEAD_DIM))
    kern = functools.partial(_retention_kernel, n_chunks=n_ch, n_ctx_chunks=n_ctx)
    return pl.pallas_call(
        kern,
        grid=(n_batch, 2, n_ch),
        in_specs=[
            pl.BlockSpec(memory_space=pltpu.SMEM),
            col_spec(0, True), col_spec(1, False), col_spec(2, False), col_spec(3, True),
            tab_spec, tab_spec, tab_spec,
            _resident((RET_HEADS, c, c)), head_tab, head_tab, head_tab, head_tab,
        ],
        out_specs=col_spec(0, True),
        out_shape=jax.ShapeDtypeStruct((n_rows, width), F32),
        scratch_shapes=[
            pltpu.VMEM((RET_HEADS, RET_HEAD_DIM, RET_HEAD_DIM), F32),
            pltpu.VMEM((n_ch, RET_HEADS, RET_HEAD_DIM, RET_HEAD_DIM), F32),
        ],
        compiler_params=_cparams("parallel", "arbitrary", "arbitrary"),
        name="retention",
    )(chunk_decay, p, p, p, p, cos, sa, sb, dmask, qwf, qwb, kwf, kwb)


def _cmul(a, b):
    return a[0] * b[0] - a[1] * b[1], a[0] * b[1] + a[1] * b[0]


def _pair_blockdiag(a):
    a0, a1 = a[0::2], a[1::2]
    z = jnp.zeros_like(a0)
    return jnp.concatenate([jnp.concatenate([a0, z], axis=2), jnp.concatenate([z, a1], axis=2)],
                           axis=1)


def _pair_lanes(a):
    return jnp.concatenate([a[0::2], a[1::2]], axis=2)


def _powers(a, n):
    one = (jnp.ones_like(a[0]), jnp.zeros_like(a[0]))
    pw = (jnp.stack([one[0], a[0]], axis=1), jnp.stack([one[1], a[1]], axis=1))
    step = a
    while pw[0].shape[1] < n:
        step = _cmul(step, step)
        nxt = _cmul(pw, (step[0][:, None], step[1][:, None]))
        pw = (jnp.concatenate([pw[0], nxt[0]], axis=1), jnp.concatenate([pw[1], nxt[1]], axis=1))
    top = _cmul(step, step)
    return (jnp.concatenate([pw[0], top[0][:, None]], axis=1),
            jnp.concatenate([pw[1], top[1][:, None]], axis=1))


def _s5_weights(lam_re, lam_im, log_dt, b_re, b_im, c_re, c_im, d_skip):
    q = S5_CHUNK
    n_g = lam_re.shape[1]
    hi = lax.Precision.HIGHEST
    kall = None
    zw = []
    rw = []
    tabs = []
    for d in range(2):
        lr = jnp.minimum(lam_re[d].astype(F32), -1e-4)
        li = lam_im[d].astype(F32)
        dt = jnp.exp(log_dt[d].astype(F32))[:, None]
        mag = jnp.exp(lr * dt)
        a = (mag * jnp.cos(li * dt), mag * jnp.sin(li * dt))
        den = lr * lr + li * li
        am1 = (a[0] - 1.0, a[1])
        coef = ((am1[0] * lr + am1[1] * li) / den, (am1[1] * lr - am1[0] * li) / den)
        bt = (b_re[d].astype(F32).transpose(0, 2, 1), b_im[d].astype(F32).transpose(0, 2, 1))
        bb = _cmul((coef[0][:, None], coef[1][:, None]), bt)
        cm = (c_re[d].astype(F32), c_im[d].astype(F32))
        pw_re, pw_im = _powers(a, q)
        e = _cmul((pw_re[:, :q, None, :], pw_im[:, :q, None, :]),
                  (bb[0][:, None], bb[1][:, None]))
        taps = (jnp.einsum('gkp,gtjp->gjtk', cm[0], e[0], precision=hi)
                - jnp.einsum('gkp,gtjp->gjtk', cm[1], e[1], precision=hi))
        if d == 0:
            kall = jnp.concatenate([jnp.zeros_like(taps[:, :, 1:]), taps], axis=2)
        else:
            kall = kall + jnp.concatenate([taps[:, :, ::-1], jnp.zeros_like(taps[:, :, 1:])], axis=2)
        tsel = np.arange(q)[::-1] if d == 0 else np.arange(q)
        z = _cmul((pw_re[:, tsel][:, :, None, :], pw_im[:, tsel][:, :, None, :]),
                  (bb[0][:, None], bb[1][:, None]))
        zw.append((z[0].reshape(n_g, q * S5_GROUP, S5_STATE),
                   z[1].reshape(n_g, q * S5_GROUP, S5_STATE)))
        rsel = np.arange(1, q + 1) if d == 0 else np.arange(q, 0, -1)
        ct = (cm[0].transpose(0, 2, 1)[:, :, None, :], cm[1].transpose(0, 2, 1)[:, :, None, :])
        pt = (pw_re[:, rsel].transpose(0, 2, 1)[..., None], pw_im[:, rsel].transpose(0, 2, 1)[..., None])
        r = _cmul(ct, pt)
        rw.append((r[0].reshape(n_g, S5_STATE, q * S5_GROUP),
                   (-r[1]).reshape(n_g, S5_STATE, q * S5_GROUP)))
        a16 = (pw_re[:, q], pw_im[:, q])
        ramp_re, ramp_im = _powers(a16, S5_TILE)
        a32 = (ramp_re[:, 2], ramp_im[:, 2])
        a64 = (ramp_re[:, 4], ramp_im[:, 4])
        a128 = (ramp_re[:, 8], ramp_im[:, 8])
        rsl = slice(0, S5_TILE) if d == 0 else slice(S5_TILE - 1, None, -1)
        rows = [a16[0], a16[1], a32[0], a32[1], a64[0], a64[1], a128[0], a128[1]]
        tabs.append(jnp.concatenate([jnp.stack(rows, axis=1), ramp_re[:, rsl], ramp_im[:, rsl]],
                                    axis=1))
    dsk = d_skip.astype(F32).reshape(n_g, S5_GROUP)
    eye = jnp.eye(S5_GROUP, dtype=F32)
    kall = kall.at[:, :, q - 1].add(dsk[:, :, None] * eye[None])
    kflat = kall.reshape(n_g, S5_GROUP, (2 * q - 1) * S5_GROUP)
    m = jnp.stack([kflat[:, :, S5_GROUP * (q - 1 - ti):S5_GROUP * (2 * q - 1 - ti)] for ti in range(q)],
                  axis=1).reshape(n_g, q * S5_GROUP, q * S5_GROUP)
    w1 = jnp.concatenate([_pair_blockdiag(m)] + [_pair_blockdiag(z) for pair in zw for z in pair],
                         axis=2)
    w2 = jnp.concatenate([_pair_blockdiag(r) for pair in rw for r in pair], axis=1)
    tab = jnp.concatenate([_pair_lanes(t) for t in tabs], axis=1)
    return w1.astype(BF16), w2.astype(BF16), tab


def _tile_shift(x, s, down):
    row = lax.broadcasted_iota(jnp.int32, x.shape, 1)
    if down:
        return jnp.where(row >= s, pltpu.roll(x, s, axis=1), 0.0)
    return jnp.where(row < S5_TILE - s, pltpu.roll(x, S5_TILE - s, axis=1), 0.0)


S5_LANE_GROUPS = 128 // S5_GROUP
S5_SLAB = 64


def _block_transpose(arrs):
    lane = lax.broadcasted_iota(jnp.int32, arrs[0].shape, 1)
    blk = lane // S5_GROUP
    a = list(arrs)
    for bit in range(3):
        s = 1 << bit
        hi = (blk & s) != 0
        new = list(a)
        for i in range(S5_LANE_GROUPS):
            if i & s:
                continue
            new[i] = jnp.where(hi, pltpu.roll(a[i + s], S5_GROUP * s, axis=1), a[i])
            new[i + s] = jnp.where(hi, a[i + s], pltpu.roll(a[i], 128 - S5_GROUP * s, axis=1))
        a = new
    return a


def _s5_kernel(ul_ref, uc_ref, w1_ref, w2_ref, tab_ref, yl_ref, yc_ref, v_scr, y_scr, zx, xin,
               *, lat_tiles, ctx_tiles):
    q = S5_CHUNK
    lanes = 2 * S5_STATE
    pw = 2 * q * S5_GROUP
    n_tiles = ctx_tiles + lat_tiles
    ctx_rows = ctx_tiles * S5_TILE
    lat_rows = lat_tiles * S5_TILE
    slab = min(S5_SLAB, lat_rows)
    half = S5_TILE

    def load_slab(src_ref, r_src, r_dst, n):
        for th in range(q // half):
            arrs = [src_ref[pl.ds(r_src * q + th * half + tl, n, stride=q), :] for tl in range(half)]
            outs = _block_transpose(arrs)
            for g in range(S5_LANE_GROUPS):
                lo = (g % 2) * (pw // 2) + th * 128
                v_scr[g // 2, pl.ds(r_dst, n), lo:lo + 128] = outs[g]

    def store_slab(dst_ref, r_dst, r_src, n):
        for th in range(q // half):
            arrs = []
            for g in range(S5_LANE_GROUPS):
                lo = (g % 2) * (pw // 2) + th * 128
                arrs.append(y_scr[g // 2, pl.ds(r_src, n), lo:lo + 128])
            outs = _block_transpose(arrs)
            for tl in range(half):
                dst_ref[pl.ds(r_dst * q + th * half + tl, n, stride=q), :] = outs[tl]

    load_slab(uc_ref, 0, 0, ctx_rows)

    def load_body(i, carry):
        r = pl.multiple_of(i * slab, slab)
        load_slab(ul_ref, r, ctx_rows + r, slab)
        return carry

    lax.fori_loop(0, lat_rows // slab, load_body, 0)

    def pair_body(p, carry):
        y = jnp.dot(v_scr[p].astype(BF16), w1_ref[p], preferred_element_type=F32)
        y_scr[p] = y[:, :pw]

        def tab_row(r):
            return tab_ref[p, r:r + 1, :]

        for d in range(2):
            down = d == 0
            t0 = 24 * d
            lo = pw + 2 * d * lanes
            z = (y[:, lo:lo + lanes].reshape(n_tiles, S5_TILE, lanes),
                 y[:, lo + lanes:lo + 2 * lanes].reshape(n_tiles, S5_TILE, lanes))
            loc = (_tile_shift(z[0], 1, down), _tile_shift(z[1], 1, down))
            for k, s in enumerate((1, 2, 4)):
                mul = (tab_row(t0 + 2 * k)[None], tab_row(t0 + 2 * k + 1)[None])
                inc = _cmul(mul, (_tile_shift(loc[0], s, down), _tile_shift(loc[1], s, down)))
                loc = (loc[0] + inc[0], loc[1] + inc[1])
            zx[2 * d] = loc[0]
            zx[2 * d + 1] = loc[1]
            zx[4 + 2 * d] = z[0]
            zx[4 + 2 * d + 1] = z[1]

        for d in range(2):
            t0 = 24 * d
            edge = S5_TILE - 1 if d == 0 else 0
            a16 = (tab_row(t0), tab_row(t0 + 1))
            a128 = (tab_row(t0 + 6), tab_row(t0 + 7))
            ramp = (tab_ref[p, t0 + 8:t0 + 16, :], tab_ref[p, t0 + 16:t0 + 24, :])

            def tile_step(j, carry, d=d, edge=edge, a16=a16, a128=a128, ramp=ramp):
                loc = (zx[2 * d, j], zx[2 * d + 1, j])
                z = (zx[4 + 2 * d, j], zx[4 + 2 * d + 1, j])
                inc = _cmul(ramp, carry)
                xin[j, :, (2 * d) * lanes:(2 * d + 1) * lanes] = loc[0] + inc[0]
                xin[j, :, (2 * d + 1) * lanes:(2 * d + 2) * lanes] = loc[1] + inc[1]
                e_loc = _cmul(a16, (loc[0][edge:edge + 1], loc[1][edge:edge + 1]))
                nxt = _cmul(a128, carry)
                return (nxt[0] + e_loc[0] + z[0][edge:edge + 1],
                        nxt[1] + e_loc[1] + z[1][edge:edge + 1])

            zero = (jnp.zeros((1, lanes), F32), jnp.zeros((1, lanes), F32))
            if d == 0:
                lax.fori_loop(0, n_tiles, tile_step, zero)
            else:
                mid = lax.fori_loop(0, ctx_tiles, lambda t, c: tile_step(ctx_tiles - 1 - t, c), zero)
                lax.fori_loop(0, lat_tiles, lambda t, c: tile_step(n_tiles - 1 - t, c), mid)

        x = xin[...].reshape(n_tiles * S5_TILE, 4 * lanes)
        y_scr[p] += jnp.dot(x.astype(BF16), w2_ref[p], preferred_element_type=F32)
        return carry

    lax.fori_loop(0, v_scr.shape[0], pair_body, 0)

    store_slab(yc_ref, 0, 0, ctx_rows)

    def store_body(i, carry):
        r = pl.multiple_of(i * slab, slab)
        store_slab(yl_ref, r, ctx_rows + r, slab)
        return carry

    lax.fori_loop(0, lat_rows // slab, store_body, 0)


def _s5(p, col0, weights, *, n_batch, seq, ctx_len):
    w1, w2, tab = weights
    q = S5_CHUNK
    lanes = 2 * S5_STATE
    pw = 2 * q * S5_GROUP
    n_pairs = w1.shape[0]
    ppb = S5_LANE_GROUPS // 2
    n_blocks = n_pairs // ppb
    width = n_blocks * 128
    lat_tiles = seq // q // S5_TILE
    ctx_tiles = ctx_len // q // S5_TILE
    n_tiles = lat_tiles + ctx_tiles
    n_sup = n_tiles * S5_TILE
    cb0 = col0 // 128
    ctx_blk0 = n_batch * seq // ctx_len
    kern = functools.partial(_s5_kernel, lat_tiles=lat_tiles, ctx_tiles=ctx_tiles)
    return pl.pallas_call(
        kern,
        grid=(n_blocks, n_batch),
        in_specs=[
            pl.BlockSpec((seq, 128), lambda j, b: (b, cb0 + j)),
            pl.BlockSpec((ctx_len, 128), lambda j, b: (ctx_blk0 + b, cb0 + j)),
            pl.BlockSpec((ppb, pw, 2 * pw), lambda j, b: (j, 0, 0)),
            pl.BlockSpec((ppb, 4 * lanes, pw), lambda j, b: (j, 0, 0)),
            pl.BlockSpec((ppb, tab.shape[1], lanes), lambda j, b: (j, 0, 0)),
        ],
        out_specs=[pl.BlockSpec((seq, 128), lambda j, b: (b, j)),
                   pl.BlockSpec((ctx_len, 128), lambda j, b: (b, j))],
        out_shape=[jax.ShapeDtypeStruct((n_batch * seq, width), F32),
                   jax.ShapeDtypeStruct((n_batch * ctx_len, width), F32)],
        scratch_shapes=[
            pltpu.VMEM((ppb, n_sup, pw), F32),
            pltpu.VMEM((ppb, n_sup, pw), F32),
            pltpu.VMEM((8, n_tiles, S5_TILE, lanes), F32),
            pltpu.VMEM((n_tiles, S5_TILE, 4 * lanes), F32),
        ],
        compiler_params=_cparams("parallel", "parallel"),
        name="s5",
    )(p, p, w1, w2, tab)


def _ab_out_kernel(h_ref, m_ref, r_ref, yl_ref, yc_ref, gw_ref, gb_ref, wr_ref, ws_ref, o_ref,
                   *, n_lat_tiles):
    y = jnp.where(pl.program_id(0) < n_lat_tiles, yl_ref[...], yc_ref[...])
    g = 0.5 * y * (1.0 + jnp.tanh(math.sqrt(2.0 / math.pi) * (y + 0.044715 * (y * y * y))))
    s = g * jax.nn.sigmoid(jnp.dot(g.astype(BF16), gw_ref[...], preferred_element_type=F32)
                           + gb_ref[...])
    out = jnp.dot(r_ref[...].astype(BF16), wr_ref[...], preferred_element_type=F32)
    out += jnp.dot(s.astype(BF16), ws_ref[...], preferred_element_type=F32)
    o_ref[...] = h_ref[...] + m_ref[0][5:6] * out


def _ab_out(h, mod, r, ys_parts, glu_w, glu_b, w_out, *, wi, lat_tiles, n_batch):
    n_rows, d = h.shape
    w = r.shape[1]
    h_spec, m_spec = _tile_specs(lat_tiles, n_batch, d)
    half = pl.BlockSpec((TOKEN_TILE, w), lambda i: (i, 0))
    y_specs, n0 = _stream_specs(ys_parts, w)
    return pl.pallas_call(
        functools.partial(_ab_out_kernel, n_lat_tiles=n0),
        grid=(n_rows // TOKEN_TILE,),
        in_specs=[h_spec, m_spec, half] + y_specs + [
            _pinned((None, w, w), (wi, 0, 0)), _pinned((None, 1, w), (wi, 0, 0)),
            _pinned((None, w, d), (wi, 0, 0)), _pinned((None, w, d), (wi, 1, 0))],
        out_specs=pl.BlockSpec((TOKEN_TILE, d), lambda i: (i, 0)),
        out_shape=jax.ShapeDtypeStruct((n_rows, d), F32),
        compiler_params=_cparams("parallel"),
        name="ab_out_proj",
    )(h, mod, r, *ys_parts, glu_w, glu_b, w_out, w_out)


NA_QROWS = 4
NA_KROWS = NA_QROWS + NA_KH


def _na_bias_table(rpb):
    w = GRID_W
    qcol = np.arange(w)
    kcol = np.arange(w)
    wstart = np.clip(qcol - NA_KW // 2, 0, w - NA_KW)
    valid = (kcol[None, :] >= wstart[:, None]) & (kcol[None, :] < wstart[:, None] + NA_KW)
    rel = np.clip(kcol[None, :] - qcol[:, None], -(NA_KW - 1), NA_KW - 1) + NA_KW - 1
    onehot = (rel[None] == np.arange(2 * NA_KW - 1)[:, None, None]).astype(np.float32)
    tiles = jnp.einsum('hrj,jqk->hrqk', rpb.astype(F32), jnp.asarray(onehot),
                       precision=lax.Precision.HIGHEST)
    tiles = jnp.where(jnp.asarray(valid)[None, None], tiles, NEG_INF)
    neg = jnp.full((NA_HEADS, w, w), NEG_INF, F32)
    types = [lambda a: (0, NA_KH - 1 - a), lambda a: (a, NA_QROWS - 1 - a),
             lambda a: (NA_QROWS, -1 - a)]
    slabs = []
    for ty in types:
        for a in range(NA_QROWS):
            m0, rel0 = ty(a)
            row = [tiles[:, rel0 + m] if m0 <= m < m0 + NA_KH else neg for m in range(NA_KROWS)]
            slabs.append(jnp.concatenate(row, axis=-1))
    bias = jnp.stack(slabs, axis=1).reshape(NA_HEADS // 2, 2, 3, NA_QROWS * w, NA_KROWS * w)
    return bias.transpose(0, 2, 1, 3, 4).reshape(NA_HEADS // 2, 3, 2 * NA_QROWS * w, NA_KROWS * w)


def _na_kernel(q_ref, k_ref, v_ref, kc_ref, vc_ref, bias_ref, o_ref, *, rows):
    w = GRID_W
    dh = NA_HEAD_DIM
    nq = NA_QROWS * w
    nk = NA_KROWS * w
    n_blocks = rows // NA_QROWS
    nt = (((1,), (1,)), ((), ()))
    scale = dh ** -0.5
    kc = kc_ref[...]
    vc = vc_ref[...]
    lane = lax.broadcasted_iota(jnp.int32, (nq, 2 * dh), 1)
    first = lane < dh

    def body(i, carry):
        r0 = jnp.clip(NA_QROWS * i - NA_KH // 2, 0, rows - NA_KROWS)
        kind = jnp.where(i == 0, 0, jnp.where(i == n_blocks - 1, 2, 1))
        qoff = pl.multiple_of(i * nq, nq)
        q = q_ref[pl.ds(qoff, nq), :] * scale
        zero = jnp.zeros_like(q)
        qs = jnp.concatenate([jnp.where(first, q, zero), jnp.where(first, zero, q)], axis=0)
        koff = pl.multiple_of(r0 * w, NA_QROWS * w)
        kl = k_ref[pl.ds(koff, nk), :]
        vl = v_ref[pl.ds(koff, nk), :]
        s_loc = lax.dot_general(qs, kl, nt, preferred_element_type=F32) + bias_ref[0, kind]
        s_ctx = lax.dot_general(qs, kc, nt, preferred_element_type=F32)
        m = jnp.maximum(jnp.max(s_loc, axis=-1, keepdims=True), jnp.max(s_ctx, axis=-1, keepdims=True))
        p_loc = jnp.exp(s_loc - m)
        p_ctx = jnp.exp(s_ctx - m)
        den = jnp.sum(p_loc, axis=-1, keepdims=True) + jnp.sum(p_ctx, axis=-1, keepdims=True)
        o = jnp.dot(p_loc.astype(BF16), vl, preferred_element_type=F32)
        o += jnp.dot(p_ctx.astype(BF16), vc, preferred_element_type=F32)
        o = o / den
        o_ref[pl.ds(qoff, nq), :] = jnp.where(first, o[:nq], o[nq:]).astype(o_ref.dtype)
        return carry

    lax.fori_loop(0, n_blocks, body, 0, unroll=8)


def _natten(p, bias, *, n_batch, seq, ctx_len):
    d = NA_HEADS * NA_HEAD_DIM
    lb = 2 * NA_HEAD_DIM
    n_pairs = NA_HEADS // 2
    ctx_blk0 = n_batch * seq // ctx_len
    rows = seq // GRID_W
    assert rows % NA_QROWS == 0 and rows >= NA_KROWS + NA_QROWS
    lat = lambda part: pl.BlockSpec((seq, lb), lambda j, b: (b, part * n_pairs + j))
    ctx = lambda part: pl.BlockSpec((ctx_len, lb), lambda j, b: (ctx_blk0 + b, part * n_pairs + j))
    return pl.pallas_call(
        functools.partial(_na_kernel, rows=rows),
        grid=(n_pairs, n_batch),
        in_specs=[lat(0), lat(1), lat(2), ctx(1), ctx(2),
                  pl.BlockSpec((1,) + bias.shape[1:], lambda j, b: (j, 0, 0, 0))],
        out_specs=pl.BlockSpec((seq, lb), lambda j, b: (b, j)),
        out_shape=jax.ShapeDtypeStruct((n_batch * seq, d), BF16),
        compiler_params=_cparams("parallel", "parallel"),
        name="natten",
    )(p, p, p, p, p, bias)


def _na_out_kernel(h_ref, m_ref, a_ref, w_ref, o_ref):
    y = jnp.dot(a_ref[...], w_ref[...], preferred_element_type=F32)
    o_ref[...] = h_ref[...] + m_ref[0][5:6] * y


def _na_out(h, mod, att, w_o, *, wi, lat_tiles, n_batch):
    n_rows, d = att.shape
    h_spec, m_spec = _tile_specs(lat_tiles, n_batch, d)
    return pl.pallas_call(
        _na_out_kernel,
        grid=(n_rows // TOKEN_TILE,),
        in_specs=[h_spec, m_spec, pl.BlockSpec((TOKEN_TILE, d), lambda i: (i, 0)),
                  _pinned((None, d, d), (wi, 0, 0))],
        out_specs=pl.BlockSpec((TOKEN_TILE, d), lambda i: (i, 0)),
        out_shape=jax.ShapeDtypeStruct((n_rows, d), F32),
        compiler_params=_cparams("parallel"),
        name="na_out_proj",
    )(h, mod, att, w_o)


def kernel(x, c, ctx, c_ctx, w_mod, b_mod, norm_g, ffn_w1, ffn_w2, w_in_ab, w_out_ab, ret_decay_logit, s5_lam_re, s5_lam_im, s5_log_dt, s5_b_re, s5_b_im, s5_c_re, s5_c_im, s5_d, s5_glu_w, s5_glu_b, na_w_qkv, na_w_o, na_rpb, final_g):
    n_batch, seq, d = x.shape
    ctx_len = ctx.shape[1]
    depth = w_mod.shape[0]
    n_lat = n_batch * seq
    n_all = n_lat + n_batch * ctx_len
    lat_tiles = seq // TOKEN_TILE
    assert seq % TOKEN_TILE == 0 and (n_batch * ctx_len) % TOKEN_TILE == 0
    assert n_batch + 1 <= MOD_ROWS and seq % (GRID_W * NA_KH) == 0

    cvec = jnp.concatenate([c, c_ctx[None], jnp.zeros((MOD_ROWS - n_batch - 1, d), F32)], axis=0)
    mod = _modulation(cvec, w_mod, b_mod).reshape(depth, MOD_ROWS, N_MOD, d)
    h_parts = (x.reshape(n_lat, d), ctx.reshape(n_batch * ctx_len, d))
    common = dict(lat_tiles=lat_tiles, n_batch=n_batch)
    dims = dict(n_batch=n_batch, seq=seq, ctx_len=ctx_len)
    gains = norm_g.astype(F32).reshape(depth, 3, 1, d)
    w1 = ffn_w1.astype(BF16)
    w2 = ffn_w2.astype(BF16)

    for layer in range(depth):
        last = layer == depth - 1
        i = layer // 2
        m = mod[layer]
        h = _ffn(h_parts, m, gains, w1, w2, layer=layer, k=0, n_rows=n_all, **common)
        if layer % 2 == 0:
            p = _proj(h, m, gains, w_in_ab.astype(BF16), layer=layer, wi=i, out_dtype=F32, **common)
            r = _retention(p, ret_decay_logit[i], **dims)
            weights = _s5_weights(s5_lam_re[i], s5_lam_im[i], s5_log_dt[i], s5_b_re[i], s5_b_im[i],
                                  s5_c_re[i], s5_c_im[i], s5_d[i])
            ys_parts = _s5(p, 4 * RET_HEADS * RET_HEAD_DIM, weights, **dims)
            h = _ab_out(h, m, r, ys_parts, s5_glu_w.astype(BF16),
                        s5_glu_b.astype(F32)[:, None, :], w_out_ab.astype(BF16), wi=i, **common)
        else:
            assert last
            p = _proj(h, m, gains, na_w_qkv.astype(BF16), layer=layer, wi=i, out_dtype=BF16, **common)
            att = _natten(p, _na_bias_table(na_rpb[i]), **dims)
            h = _na_out(h, m, att, na_w_o.astype(BF16), wi=i, **common)
        n_rows = n_lat if last else n_all
        h = _ffn((h,), m, gains, w1, w2, layer=layer, k=1, n_rows=n_rows,
                 final_g=final_g if last else None, **common)
        h_parts = (h,)
    return h[:n_lat].reshape(n_batch, seq, d)
```

```python
import functools
import math

import numpy as np
import jax
import jax.numpy as jnp
from jax import lax
from jax.experimental import pallas as pl
from jax.experimental.pallas import tpu as pltpu

F32 = jnp.float32
BF16 = jnp.bfloat16

EPS = 1e-6
ROPE_BASE = 10000.0
GRID_W = 64
N_MOD = 9
RET_HEADS = 4
RET_HEAD_DIM = 128
RET_CHUNK = 256
S5_GROUP = 16
S5_STATE = 64
S5_CHUNK = 16
S5_TILE = 8
NA_HEADS = 16
NA_HEAD_DIM = 64
NA_KH = 8
NA_KW = 16
NEG_INF = -1e30

TOKEN_TILE = 512
VMEM_LIMIT = 56 * 1024 * 1024
MOD_ROWS = 8


def _cparams(*sem):
    return pltpu.CompilerParams(dimension_semantics=sem, vmem_limit_bytes=VMEM_LIMIT)


def _resident(shape):
    nd = len(shape)
    return pl.BlockSpec(shape, lambda *_: (0,) * nd, pipeline_mode=pl.Buffered(1))


def _mod_kernel(c_ref, w_ref, b_ref, o_ref):
    c = c_ref[...]
    s = c * jax.nn.sigmoid(c)
    o_ref[0] = jnp.dot(s, w_ref[0], preferred_element_type=F32,
                       precision=lax.Precision.HIGHEST) + b_ref[0]


def _modulation(cvec, w_mod, b_mod):
    depth, d, nd = w_mod.shape
    tn = d
    return pl.pallas_call(
        _mod_kernel,
        grid=(depth, nd // tn),
        in_specs=[
            pl.BlockSpec((MOD_ROWS, d), lambda l, j: (0, 0)),
            pl.BlockSpec((1, d, tn), lambda l, j: (l, 0, j)),
            pl.BlockSpec((1, 1, tn), lambda l, j: (l, 0, j)),
        ],
        out_specs=pl.BlockSpec((1, MOD_ROWS, tn), lambda l, j: (l, 0, j)),
        out_shape=jax.ShapeDtypeStruct((depth, MOD_ROWS, nd), F32),
        compiler_params=_cparams("parallel", "parallel"),
        name="modulation",
    )(cvec, w_mod, b_mod.reshape(depth, 1, nd))


def _rms(x):
    return x * lax.rsqrt(jnp.mean(x * x, axis=-1, keepdims=True) + EPS)


def _modulated(h, m, g, mi):
    return (_rms(h) * g) * (1.0 + m[mi + 1:mi + 2]) + m[mi:mi + 1]


def _tile_specs(n_lat_tiles_per_batch, n_batch, d):
    def mod_idx(i):
        return (jnp.minimum(i // n_lat_tiles_per_batch, n_batch), 0, 0)
    h_spec = pl.BlockSpec((TOKEN_TILE, d), lambda i: (i, 0))
    m_spec = pl.BlockSpec((1, N_MOD, d), mod_idx)
    return h_spec, m_spec


def _pinned(block_shape, index):
    return pl.BlockSpec(block_shape, lambda *_: index, pipeline_mode=pl.Buffered(1))


def _stream_specs(parts, width):
    if len(parts) == 1:
        return [pl.BlockSpec((TOKEN_TILE, width), lambda i: (i, 0))], 0
    n0 = parts[0].shape[0] // TOKEN_TILE
    return [pl.BlockSpec((TOKEN_TILE, width), lambda i: (jnp.minimum(i, n0 - 1), 0)),
            pl.BlockSpec((TOKEN_TILE, width), lambda i: (jnp.maximum(i - n0, 0), 0))], n0


def _stream_tile(refs, n0):
    if len(refs) == 1:
        return refs[0][...]
    return jnp.where(pl.program_id(0) < n0, refs[0][...], refs[1][...])


def _ffn_kernel(*refs, n_h, n0, mi, final):
    h_refs = refs[:n_h]
    m_ref, g_ref, w1a_ref, w1b_ref, w2_ref = refs[n_h:n_h + 5]
    o_ref = refs[-1]
    h = _stream_tile(h_refs, n0)
    m = m_ref[0]
    xm = _modulated(h, m, g_ref[...], mi).astype(BF16)
    a = jnp.dot(xm, w1a_ref[...], preferred_element_type=F32)
    b = jnp.dot(xm, w1b_ref[...], preferred_element_type=F32)
    hid = (a * jax.nn.sigmoid(a) * b).astype(BF16)
    y = jnp.dot(hid, w2_ref[...], preferred_element_type=F32)
    out = h + (0.5 * m[mi + 2:mi + 3]) * y
    if final:
        out = _rms(out) * refs[-2][...]
    o_ref[...] = out


def _ffn(h_parts, mod, g, w1, w2, *, layer, k, n_rows, lat_tiles, n_batch, final_g=None):
    d = h_parts[0].shape[1]
    f = w2.shape[2]
    h_specs, n0 = _stream_specs(h_parts, d)
    _, m_spec = _tile_specs(lat_tiles, n_batch, d)
    in_specs = h_specs + [
        m_spec,
        _pinned((None, None, 1, d), (layer, 2 * k, 0, 0)),
        _pinned((None, None, d, f), (layer, k, 0, 0)),
        _pinned((None, None, d, f), (layer, k, 0, 1)),
        _pinned((None, None, f, d), (layer, k, 0, 0)),
    ]
    args = list(h_parts) + [mod, g, w1, w1, w2]
    if final_g is not None:
        in_specs.append(_resident((1, d)))
        args.append(final_g.reshape(1, d))
    return pl.pallas_call(
        functools.partial(_ffn_kernel, n_h=len(h_parts), n0=n0, mi=6 * k, final=final_g is not None),
        grid=(n_rows // TOKEN_TILE,),
        in_specs=in_specs,
        out_specs=pl.BlockSpec((TOKEN_TILE, d), lambda i: (i, 0)),
        out_shape=jax.ShapeDtypeStruct((n_rows, d), F32),
        compiler_params=_cparams("parallel"),
        name="ffn",
    )(*args)


def _proj_kernel(h_ref, m_ref, g_ref, w_ref, o_ref, *, mi):
    xm = _modulated(h_ref[...], m_ref[0], g_ref[...], mi).astype(BF16)
    o_ref[...] = jnp.dot(xm, w_ref[...], preferred_element_type=F32).astype(o_ref.dtype)


def _proj(h, mod, g, w, *, layer, wi, lat_tiles, n_batch, out_dtype):
    n_rows, d = h.shape
    n = w.shape[2]
    h_spec, m_spec = _tile_specs(lat_tiles, n_batch, d)
    return pl.pallas_call(
        functools.partial(_proj_kernel, mi=3),
        grid=(n_rows // TOKEN_TILE,),
        in_specs=[h_spec, m_spec, _pinned((None, None, 1, d), (layer, 1, 0, 0)),
                  _pinned((None, d, n), (wi, 0, 0))],
        out_specs=pl.BlockSpec((TOKEN_TILE, n), lambda i: (i, 0)),
        out_shape=jax.ShapeDtypeStruct((n_rows, n), out_dtype),
        compiler_params=_cparams("parallel"),
        name="mixer_in_proj",
    )(h, mod, g, w)


def _rope_tables(seq):
    half = RET_HEAD_DIM // 2
    quarter = half // 2
    inv = ROPE_BASE ** (-np.arange(0, half, 2, dtype=np.float64) / half)
    t = np.arange(seq)
    lane = np.arange(RET_HEAD_DIM)
    pos = np.where(lane[None, :] < half, (t // GRID_W)[:, None], (t % GRID_W)[:, None])
    ang = pos.astype(np.float32).astype(np.float64) * inv.astype(np.float32)[lane % quarter][None, :]
    first = (lane % half) < quarter
    cos = np.cos(ang)
    sin = np.sin(ang)
    sa = np.where(first[None, :], -sin, 0.0)
    sb = np.where(first[None, :], 0.0, sin)
    pad1 = np.ones((RET_CHUNK, RET_HEAD_DIM))
    pad0 = np.zeros((RET_CHUNK, RET_HEAD_DIM))
    tabs = [np.concatenate([cos, pad1]), np.concatenate([sa, pad0]), np.concatenate([sb, pad0])]
    return [jnp.asarray(x, F32) for x in tabs]


def _rope(x, cos, sa, sb):
    quarter = RET_HEAD_DIM // 4
    up = pltpu.roll(x, RET_HEAD_DIM - quarter, axis=1)
    dn = pltpu.roll(x, quarter, axis=1)
    return x * cos + up * sa + dn * sb


def _retention_kernel(dec_ref, q_ref, k_ref, v_ref, g_ref, cos_ref, sa_ref, sb_ref,
                      dmask_ref, qwf_ref, qwb_ref, kwf_ref, kwb_ref, o_ref,
                      s_run, s_bwd, *, n_chunks, n_ctx_chunks):
    ph = pl.program_id(1)
    i = pl.program_id(2)
    hd = RET_HEAD_DIM
    kscale = hd ** -0.5

    @pl.when(i == 0)
    def _():
        s_run[...] = jnp.zeros_like(s_run)

    cos, sa, sb = cos_ref[...], sa_ref[...], sb_ref[...]
    tn = (((0,), (0,)), ((), ()))
    nt = (((1,), (1,)), ((), ()))

    @pl.when(ph == 0)
    def _():
        cid = jnp.where(i < n_ctx_chunks, n_ctx_chunks - 1 - i, n_chunks + n_ctx_chunks - 1 - i)
        for h in range(RET_HEADS):
            sl = slice(h * hd, (h + 1) * hd)
            k = _rope(k_ref[:, sl], cos, sa, sb) * kscale
            v = v_ref[:, sl].astype(BF16)
            s_old = s_run[h]
            s_bwd[cid, h] = s_old
            kv = lax.dot_general((k * kwb_ref[h]).astype(BF16), v, tn, preferred_element_type=F32)
            s_run[h] = dec_ref[1, h] * s_old + kv

    @pl.when(ph == 1)
    def _():
        for h in range(RET_HEADS):
            sl = slice(h * hd, (h + 1) * hd)
            q = _rope(q_ref[:, sl], cos, sa, sb)
            k = _rope(k_ref[:, sl], cos, sa, sb) * kscale
            v = v_ref[:, sl].astype(BF16)
            s_old = s_run[h]
            a = lax.dot_general(q.astype(BF16), k.astype(BF16), nt, preferred_element_type=F32)
            o = jnp.dot((a * dmask_ref[h]).astype(BF16), v, preferred_element_type=F32)
            o += jnp.dot((q * qwf_ref[h]).astype(BF16), s_old.astype(BF16),
                         preferred_element_type=F32)
            o += jnp.dot((q * qwb_ref[h]).astype(BF16), s_bwd[i, h].astype(BF16),
                         preferred_element_type=F32)
            kv = lax.dot_general((k * kwf_ref[h]).astype(BF16), v, tn, preferred_element_type=F32)
            s_run[h] = dec_ref[0, h] * s_old + kv
            o = o * lax.rsqrt(jnp.mean(o * o, axis=-1, keepdims=True) + EPS)
            g = g_ref[:, sl]
            o_ref[:, sl] = o * (g * jax.nn.sigmoid(g))


def _retention(p, decay_logit, *, n_batch, seq, ctx_len):
    c = RET_CHUNK
    width = RET_HEADS * RET_HEAD_DIM
    n_lat = seq // c
    n_ctx = ctx_len // c
    n_ch = n_lat + n_ctx
    n_rows = p.shape[0]
    lat_blocks = n_batch * n_lat

    log_gamma = jax.nn.log_sigmoid(decay_logit.astype(F32))
    pos = jnp.arange(c, dtype=F32)
    diff = pos[:, None] - pos[None, :]
    lf = log_gamma[0][:, None, None]
    lb = log_gamma[1][:, None, None]
    dmask = (jnp.where(diff >= 0, jnp.exp(lf * jnp.maximum(diff, 0.0)), 0.0)
             + jnp.where(diff <= 0, jnp.exp(lb * jnp.maximum(-diff, 0.0)), 0.0))
    ones = jnp.ones((1, 1, RET_HEAD_DIM), F32)
    col = pos[None, :, None]
    qwf = jnp.exp(lf * (col + 1.0)) * ones
    qwb = jnp.exp(lb * (c - col)) * ones
    kwf = jnp.exp(lf * (c - 1.0 - col)) * ones
    kwb = jnp.exp(lb * col) * ones
    chunk_decay = jnp.exp(log_gamma * c)
    cos, sa, sb = _rope_tables(seq)

    def chunk_id(ph, i):
        bwd = jnp.where(i < n_ctx, n_ctx - 1 - i, n_ch + n_ctx - 1 - i)
        return jnp.where(ph == 0, bwd, i)

    def row_block(b, cid):
        return jnp.where(cid < n_ctx, lat_blocks + b * n_ctx + cid, b * n_lat + cid - n_ctx)

    def col_spec(j, forward_only):
        def idx(b, ph, i):
            cid = chunk_id(ph, i)
            if forward_only:
                cid = jnp.where(ph == 0, 0, cid)
            return (row_block(b, cid), j)
        return pl.BlockSpec((c, width), idx)

    def tab_idx(b, ph, i):
        cid = chunk_id(ph, i)
        return (jnp.where(cid < n_ctx, n_lat, cid - n_ctx), 0)

    tab_spec = pl.BlockSpec((c, RET_HEAD_DIM), tab_idx)
    head_tab = _resident((RET_HEADS, c, RET_HEAD_DIM))
    kern = functools.partial(_retention_kernel, n_chunks=n_ch, n_ctx_chunks=n_ctx)
    return pl.pallas_call(
        kern,
        grid=(n_batch, 2, n_ch),
        in_specs=[
            pl.BlockSpec(memory_space=pltpu.SMEM),
            col_spec(0, True), col_spec(1, False), col_spec(2, False), col_spec(3, True),
            tab_spec, tab_spec, tab_spec,
            _resident((RET_HEADS, c, c)), head_tab, head_tab, head_tab, head_tab,
        ],
        out_specs=col_spec(0, True),
        out_shape=jax.ShapeDtypeStruct((n_rows, width), F32),
        scratch_shapes=[
            pltpu.VMEM((RET_HEADS, RET_HEAD_DIM, RET_HEAD_DIM), F32),
            pltpu.VMEM((n_ch, RET_HEADS, RET_HEAD_DIM, RET_HEAD_DIM), F32),
        ],
        compiler_params=_cparams("parallel", "arbitrary", "arbitrary"),
        name="retention",
    )(chunk_decay, p, p, p, p, cos, sa, sb, dmask, qwf, qwb, kwf, kwb)


def _cmul(a, b):
    return a[0] * b[0] - a[1] * b[1], a[0] * b[1] + a[1] * b[0]


def _pair_blockdiag(a):
    a0, a1 = a[0::2], a[1::2]
    z = jnp.zeros_like(a0)
    return jnp.concatenate([jnp.concatenate([a0, z], axis=2), jnp.concatenate([z, a1], axis=2)],
                           axis=1)


def _pair_lanes(a):
    return jnp.concatenate([a[0::2], a[1::2]], axis=2)


def _powers(a, n):
    one = (jnp.ones_like(a[0]), jnp.zeros_like(a[0]))
    pw = (jnp.stack([one[0], a[0]], axis=1), jnp.stack([one[1], a[1]], axis=1))
    step = a
    while pw[0].shape[1] < n:
        step = _cmul(step, step)
        nxt = _cmul(pw, (step[0][:, None], step[1][:, None]))
        pw = (jnp.concatenate([pw[0], nxt[0]], axis=1), jnp.concatenate([pw[1], nxt[1]], axis=1))
    top = _cmul(step, step)
    return (jnp.concatenate([pw[0], top[0][:, None]], axis=1),
            jnp.concatenate([pw[1], top[1][:, None]], axis=1))


def _s5_weights(lam_re, lam_im, log_dt, b_re, b_im, c_re, c_im, d_skip):
    q = S5_CHUNK
    n_g = lam_re.shape[1]
    hi = lax.Precision.HIGHEST
    kall = None
    zw = []
    rw = []
    tabs = []
    for d in range(2):
        lr = jnp.minimum(lam_re[d].astype(F32), -1e-4)
        li = lam_im[d].astype(F32)
        dt = jnp.exp(log_dt[d].astype(F32))[:, None]
        mag = jnp.exp(lr * dt)
        a = (mag * jnp.cos(li * dt), mag * jnp.sin(li * dt))
        den = lr * lr + li * li
        am1 = (a[0] - 1.0, a[1])
        coef = ((am1[0] * lr + am1[1] * li) / den, (am1[1] * lr - am1[0] * li) / den)
        bt = (b_re[d].astype(F32).transpose(0, 2, 1), b_im[d].astype(F32).transpose(0, 2, 1))
        bb = _cmul((coef[0][:, None], coef[1][:, None]), bt)
        cm = (c_re[d].astype(F32), c_im[d].astype(F32))
        pw_re, pw_im = _powers(a, q)
        e = _cmul((pw_re[:, :q, None, :], pw_im[:, :q, None, :]),
                  (bb[0][:, None], bb[1][:, None]))
        taps = (jnp.einsum('gkp,gtjp->gjtk', cm[0], e[0], precision=hi)
                - jnp.einsum('gkp,gtjp->gjtk', cm[1], e[1], precision=hi))
        if d == 0:
            kall = jnp.concatenate([jnp.zeros_like(taps[:, :, 1:]), taps], axis=2)
        else:
            kall = kall + jnp.concatenate([taps[:, :, ::-1], jnp.zeros_like(taps[:, :, 1:])], axis=2)
        tsel = np.arange(q)[::-1] if d == 0 else np.arange(q)
        z = _cmul((pw_re[:, tsel][:, :, None, :], pw_im[:, tsel][:, :, None, :]),
                  (bb[0][:, None], bb[1][:, None]))
        zw.append((z[0].reshape(n_g, q * S5_GROUP, S5_STATE),
                   z[1].reshape(n_g, q * S5_GROUP, S5_STATE)))
        rsel = np.arange(1, q + 1) if d == 0 else np.arange(q, 0, -1)
        ct = (cm[0].transpose(0, 2, 1)[:, :, None, :], cm[1].transpose(0, 2, 1)[:, :, None, :])
        pt = (pw_re[:, rsel].transpose(0, 2, 1)[..., None], pw_im[:, rsel].transpose(0, 2, 1)[..., None])
        r = _cmul(ct, pt)
        rw.append((r[0].reshape(n_g, S5_STATE, q * S5_GROUP),
                   (-r[1]).reshape(n_g, S5_STATE, q * S5_GROUP)))
        a16 = (pw_re[:, q], pw_im[:, q])
        ramp_re, ramp_im = _powers(a16, S5_TILE)
        a32 = (ramp_re[:, 2], ramp_im[:, 2])
        a64 = (ramp_re[:, 4], ramp_im[:, 4])
        a128 = (ramp_re[:, 8], ramp_im[:, 8])
        rsl = slice(0, S5_TILE) if d == 0 else slice(S5_TILE - 1, None, -1)
        rows = [a16[0], a16[1], a32[0], a32[1], a64[0], a64[1], a128[0], a128[1]]
        tabs.append(jnp.concatenate([jnp.stack(rows, axis=1), ramp_re[:, rsl], ramp_im[:, rsl]],
                                    axis=1))
    dsk = d_skip.astype(F32).reshape(n_g, S5_GROUP)
    eye = jnp.eye(S5_GROUP, dtype=F32)
    kall = kall.at[:, :, q - 1].add(dsk[:, :, None] * eye[None])
    kflat = kall.reshape(n_g, S5_GROUP, (2 * q - 1) * S5_GROUP)
    m = jnp.stack([kflat[:, :, S5_GROUP * (q - 1 - ti):S5_GROUP * (2 * q - 1 - ti)] for ti in range(q)],
                  axis=1).reshape(n_g, q * S5_GROUP, q * S5_GROUP)
    w1 = jnp.concatenate([_pair_blockdiag(m)] + [_pair_blockdiag(z) for pair in zw for z in pair],
                         axis=2)
    w2 = jnp.concatenate([_pair_blockdiag(r) for pair in rw for r in pair], axis=1)
    tab = jnp.concatenate([_pair_lanes(t) for t in tabs], axis=1)
    return w1.astype(BF16), w2.astype(BF16), tab


def _tile_shift(x, s, down):
    row = lax.broadcasted_iota(jnp.int32, x.shape, 1)
    if down:
        return jnp.where(row >= s, pltpu.roll(x, s, axis=1), 0.0)
    return jnp.where(row < S5_TILE - s, pltpu.roll(x, S5_TILE - s, axis=1), 0.0)


S5_LANE_GROUPS = 128 // S5_GROUP
S5_SLAB = 64


def _block_transpose(arrs):
    lane = lax.broadcasted_iota(jnp.int32, arrs[0].shape, 1)
    blk = lane // S5_GROUP
    a = list(arrs)
    for bit in range(3):
        s = 1 << bit
        hi = (blk & s) != 0
        new = list(a)
        for i in range(S5_LANE_GROUPS):
            if i & s:
                continue
            new[i] = jnp.where(hi, pltpu.roll(a[i + s], S5_GROUP * s, axis=1), a[i])
            new[i + s] = jnp.where(hi, a[i + s], pltpu.roll(a[i], 128 - S5_GROUP * s, axis=1))
        a = new
    return a


def _s5_kernel(ul_ref, uc_ref, w1_ref, w2_ref, tab_ref, yl_ref, yc_ref, v_scr, y_scr, zx, xin,
               *, lat_tiles, ctx_tiles):
    q = S5_CHUNK
    lanes = 2 * S5_STATE
    pw = 2 * q * S5_GROUP
    n_tiles = ctx_tiles + lat_tiles
    ctx_rows = ctx_tiles * S5_TILE
    lat_rows = lat_tiles * S5_TILE
    slab = min(S5_SLAB, lat_rows)
    half = S5_TILE

    def load_slab(src_ref, r_src, r_dst, n):
        for th in range(q // half):
            arrs = [src_ref[pl.ds(r_src * q + th * half + tl, n, stride=q), :] for tl in range(half)]
            outs = _block_transpose(arrs)
            for g in range(S5_LANE_GROUPS):
                lo = (g % 2) * (pw // 2) + th * 128
                v_scr[g // 2, pl.ds(r_dst, n), lo:lo + 128] = outs[g]

    def store_slab(dst_ref, r_dst, r_src, n):
        for th in range(q // half):
            arrs = []
            for g in range(S5_LANE_GROUPS):
                lo = (g % 2) * (pw // 2) + th * 128
                arrs.append(y_scr[g // 2, pl.ds(r_src, n), lo:lo + 128])
            outs = _block_transpose(arrs)
            for tl in range(half):
                dst_ref[pl.ds(r_dst * q + th * half + tl, n, stride=q), :] = outs[tl]

    load_slab(uc_ref, 0, 0, ctx_rows)

    def load_body(i, carry):
        r = pl.multiple_of(i * slab, slab)
        load_slab(ul_ref, r, ctx_rows + r, slab)
        return carry

    lax.fori_loop(0, lat_rows // slab, load_body, 0)

    def pair_body(p, carry):
        y = jnp.dot(v_scr[p].astype(BF16), w1_ref[p], preferred_element_type=F32)
        y_scr[p] = y[:, :pw]

        def tab_row(r):
            return tab_ref[p, r:r + 1, :]

        for d in range(2):
            down = d == 0
            t0 = 24 * d
            lo = pw + 2 * d * lanes
            z = (y[:, lo:lo + lanes].reshape(n_tiles, S5_TILE, lanes),
                 y[:, lo + lanes:lo + 2 * lanes].reshape(n_tiles, S5_TILE, lanes))
            loc = (_tile_shift(z[0], 1, down), _tile_shift(z[1], 1, down))
            for k, s in enumerate((1, 2, 4)):
                mul = (tab_row(t0 + 2 * k)[None], tab_row(t0 + 2 * k + 1)[None])
                inc = _cmul(mul, (_tile_shift(loc[0], s, down), _tile_shift(loc[1], s, down)))
                loc = (loc[0] + inc[0], loc[1] + inc[1])
            zx[2 * d] = loc[0]
            zx[2 * d + 1] = loc[1]
            zx[4 + 2 * d] = z[0]
            zx[4 + 2 * d + 1] = z[1]

        for d in range(2):
            t0 = 24 * d
            edge = S5_TILE - 1 if d == 0 else 0
            a16 = (tab_row(t0), tab_row(t0 + 1))
            a128 = (tab_row(t0 + 6), tab_row(t0 + 7))
            ramp = (tab_ref[p, t0 + 8:t0 + 16, :], tab_ref[p, t0 + 16:t0 + 24, :])

            def tile_step(j, carry, d=d, edge=edge, a16=a16, a128=a128, ramp=ramp):
                loc = (zx[2 * d, j], zx[2 * d + 1, j])
                z = (zx[4 + 2 * d, j], zx[4 + 2 * d + 1, j])
                inc = _cmul(ramp, carry)
                xin[j, :, (2 * d) * lanes:(2 * d + 1) * lanes] = loc[0] + inc[0]
                xin[j, :, (2 * d + 1) * lanes:(2 * d + 2) * lanes] = loc[1] + inc[1]
                e_loc = _cmul(a16, (loc[0][edge:edge + 1], loc[1][edge:edge + 1]))
                nxt = _cmul(a128, carry)
                return (nxt[0] + e_loc[0] + z[0][edge:edge + 1],
                        nxt[1] + e_loc[1] + z[1][edge:edge + 1])

            zero = (jnp.zeros((1, lanes), F32), jnp.zeros((1, lanes), F32))
            if d == 0:
                lax.fori_loop(0, n_tiles, tile_step, zero)
            else:
                mid = lax.fori_loop(0, ctx_tiles, lambda t, c: tile_step(ctx_tiles - 1 - t, c), zero)
                lax.fori_loop(0, lat_tiles, lambda t, c: tile_step(n_tiles - 1 - t, c), mid)

        x = xin[...].reshape(n_tiles * S5_TILE, 4 * lanes)
        y_scr[p] += jnp.dot(x.astype(BF16), w2_ref[p], preferred_element_type=F32)
        return carry

    lax.fori_loop(0, v_scr.shape[0], pair_body, 0)

    store_slab(yc_ref, 0, 0, ctx_rows)

    def store_body(i, carry):
        r = pl.multiple_of(i * slab, slab)
        store_slab(yl_ref, r, ctx_rows + r, slab)
        return carry

    lax.fori_loop(0, lat_rows // slab, store_body, 0)


def _s5(p, col0, weights, *, n_batch, seq, ctx_len):
    w1, w2, tab = weights
    q = S5_CHUNK
    lanes = 2 * S5_STATE
    pw = 2 * q * S5_GROUP
    n_pairs = w1.shape[0]
    ppb = S5_LANE_GROUPS // 2
    n_blocks = n_pairs // ppb
    width = n_blocks * 128
    lat_tiles = seq // q // S5_TILE
    ctx_tiles = ctx_len // q // S5_TILE
    n_tiles = lat_tiles + ctx_tiles
    n_sup = n_tiles * S5_TILE
    cb0 = col0 // 128
    ctx_blk0 = n_batch * seq // ctx_len
    kern = functools.partial(_s5_kernel, lat_tiles=lat_tiles, ctx_tiles=ctx_tiles)
    return pl.pallas_call(
        kern,
        grid=(n_blocks, n_batch),
        in_specs=[
            pl.BlockSpec((seq, 128), lambda j, b: (b, cb0 + j)),
            pl.BlockSpec((ctx_len, 128), lambda j, b: (ctx_blk0 + b, cb0 + j)),
            pl.BlockSpec((ppb, pw, 2 * pw), lambda j, b: (j, 0, 0)),
            pl.BlockSpec((ppb, 4 * lanes, pw), lambda j, b: (j, 0, 0)),
            pl.BlockSpec((ppb, tab.shape[1], lanes), lambda j, b: (j, 0, 0)),
        ],
        out_specs=[pl.BlockSpec((seq, 128), lambda j, b: (b, j)),
                   pl.BlockSpec((ctx_len, 128), lambda j, b: (b, j))],
        out_shape=[jax.ShapeDtypeStruct((n_batch * seq, width), F32),
                   jax.ShapeDtypeStruct((n_batch * ctx_len, width), F32)],
        scratch_shapes=[
            pltpu.VMEM((ppb, n_sup, pw), F32),
            pltpu.VMEM((ppb, n_sup, pw), F32),
            pltpu.VMEM((8, n_tiles, S5_TILE, lanes), F32),
            pltpu.VMEM((n_tiles, S5_TILE, 4 * lanes), F32),
        ],
        compiler_params=_cparams("parallel", "parallel"),
        name="s5",
    )(p, p, w1, w2, tab)


def _ab_out_kernel(h_ref, m_ref, r_ref, yl_ref, yc_ref, gw_ref, gb_ref, wr_ref, ws_ref, o_ref,
                   *, n_lat_tiles):
    y = jnp.where(pl.program_id(0) < n_lat_tiles, yl_ref[...], yc_ref[...])
    g = 0.5 * y * (1.0 + jnp.tanh(math.sqrt(2.0 / math.pi) * (y + 0.044715 * (y * y * y))))
    s = g * jax.nn.sigmoid(jnp.dot(g.astype(BF16), gw_ref[...], preferred_element_type=F32)
                           + gb_ref[...])
    out = jnp.dot(r_ref[...].astype(BF16), wr_ref[...], preferred_element_type=F32)
    out += jnp.dot(s.astype(BF16), ws_ref[...], preferred_element_type=F32)
    o_ref[...] = h_ref[...] + m_ref[0][5:6] * out


def _ab_out(h, mod, r, ys_parts, glu_w, glu_b, w_out, *, wi, lat_tiles, n_batch):
    n_rows, d = h.shape
    w = r.shape[1]
    h_spec, m_spec = _tile_specs(lat_tiles, n_batch, d)
    half = pl.BlockSpec((TOKEN_TILE, w), lambda i: (i, 0))
    y_specs, n0 = _stream_specs(ys_parts, w)
    return pl.pallas_call(
        functools.partial(_ab_out_kernel, n_lat_tiles=n0),
        grid=(n_rows // TOKEN_TILE,),
        in_specs=[h_spec, m_spec, half] + y_specs + [
            _pinned((None, w, w), (wi, 0, 0)), _pinned((None, 1, w), (wi, 0, 0)),
            _pinned((None, w, d), (wi, 0, 0)), _pinned((None, w, d), (wi, 1, 0))],
        out_specs=pl.BlockSpec((TOKEN_TILE, d), lambda i: (i, 0)),
        out_shape=jax.ShapeDtypeStruct((n_rows, d), F32),
        compiler_params=_cparams("parallel"),
        name="ab_out_proj",
    )(h, mod, r, *ys_parts, glu_w, glu_b, w_out, w_out)


NA_QROWS = 4
NA_KROWS = NA_QROWS + NA_KH


def _na_bias_table(rpb):
    w = GRID_W
    qcol = np.arange(w)
    kcol = np.arange(w)
    wstart = np.clip(qcol - NA_KW // 2, 0, w - NA_KW)
    valid = (kcol[None, :] >= wstart[:, None]) & (kcol[None, :] < wstart[:, None] + NA_KW)
    rel = np.clip(kcol[None, :] - qcol[:, None], -(NA_KW - 1), NA_KW - 1) + NA_KW - 1
    onehot = (rel[None] == np.arange(2 * NA_KW - 1)[:, None, None]).astype(np.float32)
    tiles = jnp.einsum('hrj,jqk->hrqk', rpb.astype(F32), jnp.asarray(onehot),
                       precision=lax.Precision.HIGHEST)
    tiles = jnp.where(jnp.asarray(valid)[None, None], tiles, NEG_INF)
    neg = jnp.full((NA_HEADS, w, w), NEG_INF, F32)
    types = [lambda a: (0, NA_KH - 1 - a), lambda a: (a, NA_QROWS - 1 - a),
             lambda a: (NA_QROWS, -1 - a)]
    slabs = []
    for ty in types:
        for a in range(NA_QROWS):
            m0, rel0 = ty(a)
            row = [tiles[:, rel0 + m] if m0 <= m < m0 + NA_KH else neg for m in range(NA_KROWS)]
            slabs.append(jnp.concatenate(row, axis=-1))
    bias = jnp.stack(slabs, axis=1).reshape(NA_HEADS // 2, 2, 3, NA_QROWS * w, NA_KROWS * w)
    return bias.transpose(0, 2, 1, 3, 4).reshape(NA_HEADS // 2, 3, 2 * NA_QROWS * w, NA_KROWS * w)


def _na_kernel(q_ref, k_ref, v_ref, kc_ref, vc_ref, bias_ref, o_ref, vx, vcx, s_even, s_odd,
               p_even, p_odd, *, rows):
    w = GRID_W
    dh = NA_HEAD_DIM
    lb = 2 * dh
    nq = NA_QROWS * w
    nk = NA_KROWS * w
    n_blocks = rows // NA_QROWS
    nt = (((1,), (1,)), ((), ()))
    scale = dh ** -0.5
    kc = kc_ref[...]
    lane = lax.broadcasted_iota(jnp.int32, (nq, lb), 1)
    first = lane < dh

    for dst, src in ((vx, v_ref), (vcx, vc_ref)):
        n = src.shape[0]
        dst[:, :lb] = src[...]
        dst[:, lb:] = (lax.broadcasted_iota(jnp.int32, (n, lb), 1) == 0).astype(BF16)

    def key_offset(i):
        r0 = jnp.clip(NA_QROWS * i - NA_KH // 2, 0, rows - NA_KROWS)
        return pl.multiple_of(r0 * w, NA_QROWS * w)

    def scores(i, s_ref):
        kind = jnp.where(i == 0, 0, jnp.where(i == n_blocks - 1, 2, 1))
        q = q_ref[pl.ds(pl.multiple_of(i * nq, nq), nq), :] * scale
        zero = jnp.zeros_like(q)
        qs = jnp.concatenate([jnp.where(first, q, zero), jnp.where(first, zero, q)], axis=0)
        kl = k_ref[pl.ds(key_offset(i), nk), :]
        s_ref[:, :nk] = lax.dot_general(qs, kl, nt, preferred_element_type=F32) + bias_ref[0, kind]
        s_ref[:, nk:] = lax.dot_general(qs, kc, nt, preferred_element_type=F32)

    def softmax(s_ref, p_ref):
        s = s_ref[...]
        p_ref[...] = jnp.exp(s - jnp.max(s, axis=-1, keepdims=True)).astype(BF16)

    def attend(i, p_ref):
        o = jnp.dot(p_ref[:, :nk], vx[pl.ds(key_offset(i), nk), :], preferred_element_type=F32)
        o += jnp.dot(p_ref[:, nk:], vcx[...], preferred_element_type=F32)
        o = o[:, :lb] / o[:, lb:lb + 1]
        o_ref[pl.ds(pl.multiple_of(i * nq, nq), nq), :] = jnp.where(first, o[:nq], o[nq:]).astype(
            o_ref.dtype)

    scores(0, s_even)
    softmax(s_even, p_even)
    scores(1, s_odd)

    def body(j, carry):
        attend(2 * j - 2, p_even)
        softmax(s_odd, p_odd)
        scores(2 * j, s_even)
        attend(2 * j - 1, p_odd)
        softmax(s_even, p_even)
        scores(2 * j + 1, s_odd)
        return carry

    lax.fori_loop(1, n_blocks // 2, body, 0, unroll=True)
    attend(n_blocks - 2, p_even)
    softmax(s_odd, p_odd)
    attend(n_blocks - 1, p_odd)


def _natten(p, bias, *, n_batch, seq, ctx_len):
    d = NA_HEADS * NA_HEAD_DIM
    lb = 2 * NA_HEAD_DIM
    n_pairs = NA_HEADS // 2
    ctx_blk0 = n_batch * seq // ctx_len
    rows = seq // GRID_W
    assert rows % (2 * NA_QROWS) == 0 and rows >= NA_KROWS + NA_QROWS
    stacked = 2 * NA_QROWS * GRID_W
    n_keys = NA_KROWS * GRID_W + ctx_len
    lat = lambda part: pl.BlockSpec((seq, lb), lambda j, b: (b, part * n_pairs + j))
    ctx = lambda part: pl.BlockSpec((ctx_len, lb), lambda j, b: (ctx_blk0 + b, part * n_pairs + j))
    return pl.pallas_call(
        functools.partial(_na_kernel, rows=rows),
        grid=(n_pairs, n_batch),
        in_specs=[lat(0), lat(1), lat(2), ctx(1), ctx(2),
                  pl.BlockSpec((1,) + bias.shape[1:], lambda j, b: (j, 0, 0, 0))],
        out_specs=pl.BlockSpec((seq, lb), lambda j, b: (b, j)),
        out_shape=jax.ShapeDtypeStruct((n_batch * seq, d), BF16),
        scratch_shapes=[pltpu.VMEM((seq, 2 * lb), BF16), pltpu.VMEM((ctx_len, 2 * lb), BF16),
                        pltpu.VMEM((stacked, n_keys), F32), pltpu.VMEM((stacked, n_keys), F32),
                        pltpu.VMEM((stacked, n_keys), BF16), pltpu.VMEM((stacked, n_keys), BF16)],
        compiler_params=_cparams("parallel", "parallel"),
        name="natten",
    )(p, p, p, p, p, bias)


def _na_out_kernel(h_ref, m_ref, a_ref, w_ref, o_ref):
    y = jnp.dot(a_ref[...], w_ref[...], preferred_element_type=F32)
    o_ref[...] = h_ref[...] + m_ref[0][5:6] * y


def _na_out(h, mod, att, w_o, *, wi, lat_tiles, n_batch):
    n_rows, d = att.shape
    h_spec, m_spec = _tile_specs(lat_tiles, n_batch, d)
    return pl.pallas_call(
        _na_out_kernel,
        grid=(n_rows // TOKEN_TILE,),
        in_specs=[h_spec, m_spec, pl.BlockSpec((TOKEN_TILE, d), lambda i: (i, 0)),
                  _pinned((None, d, d), (wi, 0, 0))],
        out_specs=pl.BlockSpec((TOKEN_TILE, d), lambda i: (i, 0)),
        out_shape=jax.ShapeDtypeStruct((n_rows, d), F32),
        compiler_params=_cparams("parallel"),
        name="na_out_proj",
    )(h, mod, att, w_o)


def kernel(x, c, ctx, c_ctx, w_mod, b_mod, norm_g, ffn_w1, ffn_w2, w_in_ab, w_out_ab, ret_decay_logit, s5_lam_re, s5_lam_im, s5_log_dt, s5_b_re, s5_b_im, s5_c_re, s5_c_im, s5_d, s5_glu_w, s5_glu_b, na_w_qkv, na_w_o, na_rpb, final_g):
    n_batch, seq, d = x.shape
    ctx_len = ctx.shape[1]
    depth = w_mod.shape[0]
    n_lat = n_batch * seq
    n_all = n_lat + n_batch * ctx_len
    lat_tiles = seq // TOKEN_TILE
    assert seq % TOKEN_TILE == 0 and (n_batch * ctx_len) % TOKEN_TILE == 0
    assert n_batch + 1 <= MOD_ROWS and seq % (GRID_W * NA_KH) == 0

    cvec = jnp.concatenate([c, c_ctx[None], jnp.zeros((MOD_ROWS - n_batch - 1, d), F32)], axis=0)
    mod = _modulation(cvec, w_mod, b_mod).reshape(depth, MOD_ROWS, N_MOD, d)
    h_parts = (x.reshape(n_lat, d), ctx.reshape(n_batch * ctx_len, d))
    common = dict(lat_tiles=lat_tiles, n_batch=n_batch)
    dims = dict(n_batch=n_batch, seq=seq, ctx_len=ctx_len)
    gains = norm_g.astype(F32).reshape(depth, 3, 1, d)
    w1 = ffn_w1.astype(BF16)
    w2 = ffn_w2.astype(BF16)

    for layer in range(depth):
        last = layer == depth - 1
        i = layer // 2
        m = mod[layer]
        h = _ffn(h_parts, m, gains, w1, w2, layer=layer, k=0, n_rows=n_all, **common)
        if layer % 2 == 0:
            p = _proj(h, m, gains, w_in_ab.astype(BF16), layer=layer, wi=i, out_dtype=F32, **common)
            r = _retention(p, ret_decay_logit[i], **dims)
            weights = _s5_weights(s5_lam_re[i], s5_lam_im[i], s5_log_dt[i], s5_b_re[i], s5_b_im[i],
                                  s5_c_re[i], s5_c_im[i], s5_d[i])
            ys_parts = _s5(p, 4 * RET_HEADS * RET_HEAD_DIM, weights, **dims)
            h = _ab_out(h, m, r, ys_parts, s5_glu_w.astype(BF16),
                        s5_glu_b.astype(F32)[:, None, :], w_out_ab.astype(BF16), wi=i, **common)
        else:
            assert last
            p = _proj(h, m, gains, na_w_qkv.astype(BF16), layer=layer, wi=i, out_dtype=BF16, **common)
            att = _natten(p, _na_bias_table(na_rpb[i]), **dims)
            h = _na_out(h, m, att, na_w_o.astype(BF16), wi=i, **common)
        n_rows = n_lat if last else n_all
        h = _ffn((h,), m, gains, w1, w2, layer=layer, k=1, n_rows=n_rows,
                 final_g=final_g if last else None, **common)
        h_parts = (h,)
    return h[:n_lat].reshape(n_batch, seq, d)
```

```python
import functools
import math
from typing import NamedTuple

import numpy as np
import jax
import jax.numpy as jnp
from jax import lax
from jax.experimental import pallas as pl
from jax.experimental.pallas import tpu as pltpu

F32 = jnp.float32
BF16 = jnp.bfloat16

EPS = 1e-6
ROPE_BASE = 10000.0
GRID_W = 64
N_MOD = 9
RET_HEADS = 4
RET_HEAD_DIM = 128
RET_CHUNK = 256
S5_GROUP = 16
S5_STATE = 64
S5_CHUNK = 16
S5_TILE = 8
NA_HEADS = 16
NA_HEAD_DIM = 64
NA_KH = 8
NA_KW = 16
NEG_INF = -1e30

TOKEN_TILE = 512
VMEM_LIMIT = 56 * 1024 * 1024
MOD_ROWS = 8


def _cparams(*sem):
    return pltpu.CompilerParams(dimension_semantics=sem, vmem_limit_bytes=VMEM_LIMIT)


def _resident(shape):
    nd = len(shape)
    return pl.BlockSpec(shape, lambda *_: (0,) * nd, pipeline_mode=pl.Buffered(1))


def _mod_kernel(c_ref, w_ref, b_ref, o_ref):
    c = c_ref[...]
    s = c * jax.nn.sigmoid(c)
    o_ref[0] = jnp.dot(s, w_ref[0], preferred_element_type=F32,
                       precision=lax.Precision.HIGHEST) + b_ref[0]


def _modulation(cvec, w_mod, b_mod):
    depth, d, nd = w_mod.shape
    tn = d
    return pl.pallas_call(
        _mod_kernel,
        grid=(depth, nd // tn),
        in_specs=[
            pl.BlockSpec((MOD_ROWS, d), lambda l, j: (0, 0)),
            pl.BlockSpec((1, d, tn), lambda l, j: (l, 0, j)),
            pl.BlockSpec((1, 1, tn), lambda l, j: (l, 0, j)),
        ],
        out_specs=pl.BlockSpec((1, MOD_ROWS, tn), lambda l, j: (l, 0, j)),
        out_shape=jax.ShapeDtypeStruct((depth, MOD_ROWS, nd), F32),
        compiler_params=_cparams("parallel", "parallel"),
        name="modulation",
    )(cvec, w_mod, b_mod.reshape(depth, 1, nd))


def _rms(x):
    return x * lax.rsqrt(jnp.mean(x * x, axis=-1, keepdims=True) + EPS)


def _modulated(h, m, g, mi):
    return (_rms(h) * g) * (1.0 + m[mi + 1:mi + 2]) + m[mi:mi + 1]


def _tile_specs(n_lat_tiles_per_batch, n_batch, d):
    def mod_idx(i):
        return (jnp.minimum(i // n_lat_tiles_per_batch, n_batch), 0, 0)
    h_spec = pl.BlockSpec((TOKEN_TILE, d), lambda i: (i, 0))
    m_spec = pl.BlockSpec((1, N_MOD, d), mod_idx)
    return h_spec, m_spec


def _pinned(block_shape, index):
    return pl.BlockSpec(block_shape, lambda *_: index, pipeline_mode=pl.Buffered(1))


def _stream_specs(parts, width):
    if len(parts) == 1:
        return [pl.BlockSpec((TOKEN_TILE, width), lambda i: (i, 0))], 0
    n0 = parts[0].shape[0] // TOKEN_TILE
    return [pl.BlockSpec((TOKEN_TILE, width), lambda i: (jnp.minimum(i, n0 - 1), 0)),
            pl.BlockSpec((TOKEN_TILE, width), lambda i: (jnp.maximum(i - n0, 0), 0))], n0


def _stream_tile(refs, n0):
    if len(refs) == 1:
        return refs[0][...]
    return jnp.where(pl.program_id(0) < n0, refs[0][...], refs[1][...])


class _HalfCfg(NamedTuple):
    n_h: int
    n0: int
    pre: str
    n_y: int
    ny0: int
    post: str
    final: bool


def _gelu_tanh(y):
    return 0.5 * y * (1.0 + jnp.tanh(math.sqrt(2.0 / math.pi) * (y + 0.044715 * (y * y * y))))


def _half_kernel(*refs, cfg):
    it = iter(refs)
    take = lambda n: [next(it) for _ in range(n)]
    h_refs = take(cfg.n_h)
    m_ref, g_ref = take(2)
    h = _stream_tile(h_refs, cfg.n0)
    m = m_ref[0]
    gate_mix = m[5:6]
    if cfg.pre == "ab":
        (r_ref,) = take(1)
        y_refs = take(cfg.n_y)
        gw_ref, gb_ref, wr_ref, ws_ref = take(4)
        g = _gelu_tanh(_stream_tile(y_refs, cfg.ny0))
        s = g * jax.nn.sigmoid(jnp.dot(g.astype(BF16), gw_ref[...], preferred_element_type=F32)
                               + gb_ref[...])
        mix = jnp.dot(r_ref[...].astype(BF16), wr_ref[...], preferred_element_type=F32)
        mix += jnp.dot(s.astype(BF16), ws_ref[...], preferred_element_type=F32)
        h = h + gate_mix * mix
    elif cfg.pre == "na":
        a_ref, wo_ref = take(2)
        h = h + gate_mix * jnp.dot(a_ref[...], wo_ref[...], preferred_element_type=F32)
    w1a_ref, w1b_ref, w2_ref = take(3)
    k = 1 if cfg.pre else 0
    mi = 6 * k
    xm = _modulated(h, m, g_ref[2 * k], mi).astype(BF16)
    a = jnp.dot(xm, w1a_ref[...], preferred_element_type=F32)
    b = jnp.dot(xm, w1b_ref[...], preferred_element_type=F32)
    hid = (a * jax.nn.sigmoid(a) * b).astype(BF16)
    h = h + (0.5 * m[mi + 2:mi + 3]) * jnp.dot(hid, w2_ref[...], preferred_element_type=F32)
    if cfg.post:
        (wp_ref,) = take(1)
    if cfg.post == "ab":
        cos_ref, sa_ref, sb_ref = take(3)
    if cfg.final:
        (fg_ref,) = take(1)
    outs = list(it)
    outs[0][...] = _rms(h) * fg_ref[...] if cfg.final else h
    if not cfg.post:
        return
    xm = _modulated(h, m, g_ref[1], 3).astype(BF16)
    pr = jnp.dot(xm, wp_ref[...], preferred_element_type=F32)
    if cfg.post == "na":
        outs[1][...] = pr.astype(BF16)
        return
    width = RET_HEADS * RET_HEAD_DIM
    cos, sa, sb = cos_ref[...], sa_ref[...], sb_ref[...]
    for hh in range(RET_HEADS):
        lo = hh * RET_HEAD_DIM
        hi = lo + RET_HEAD_DIM
        outs[1][:, lo:hi] = _rope(pr[:, lo:hi], cos, sa, sb).astype(BF16)
        kr = _rope(pr[:, width + lo:width + hi], cos, sa, sb) * (RET_HEAD_DIM ** -0.5)
        outs[1][:, width + lo:width + hi] = kr.astype(BF16)
    outs[1][:, 2 * width:] = pr[:, 2 * width:3 * width].astype(BF16)
    outs[2][...] = pr[:, 3 * width:]


def _half_layer(h_parts, mod, gains, w1, w2, *, layer, n_rows, lat_tiles, n_batch,
                pre="", pre_args=(), post="", post_w=None, post_wi=0, rope=None, final_g=None):
    d = h_parts[0].shape[1]
    f = w2.shape[2]
    k = 1 if pre else 0
    h_specs, n0 = _stream_specs(h_parts, d)
    _, m_spec = _tile_specs(lat_tiles, n_batch, d)
    in_specs = h_specs + [m_spec, _pinned((None, 3, 1, d), (layer, 0, 0, 0))]
    args = list(h_parts) + [mod, gains]
    n_y = ny0 = 0
    if pre == "ab":
        r, ys_parts, glu_w, glu_b, w_out, wi = pre_args
        w = r.shape[1]
        y_specs, ny0 = _stream_specs(ys_parts, w)
        n_y = len(ys_parts)
        in_specs += [pl.BlockSpec((TOKEN_TILE, w), lambda i: (i, 0))] + y_specs + [
            _pinned((None, w, w), (wi, 0, 0)), _pinned((None, 1, w), (wi, 0, 0)),
            _pinned((None, w, d), (wi, 0, 0)), _pinned((None, w, d), (wi, 1, 0))]
        args += [r, *ys_parts, glu_w, glu_b, w_out, w_out]
    elif pre == "na":
        att, w_o, wi = pre_args
        in_specs += [pl.BlockSpec((TOKEN_TILE, d), lambda i: (i, 0)), _pinned((None, d, d), (wi, 0, 0))]
        args += [att, w_o]
    in_specs += [_pinned((None, None, d, f), (layer, k, 0, 0)),
                 _pinned((None, None, d, f), (layer, k, 0, 1)),
                 _pinned((None, None, f, d), (layer, k, 0, 0))]
    args += [w1, w1, w2]
    out_specs = [pl.BlockSpec((TOKEN_TILE, d), lambda i: (i, 0))]
    out_shape = [jax.ShapeDtypeStruct((n_rows, d), F32)]
    if post:
        n = post_w.shape[2]
        in_specs.append(_pinned((None, d, n), (post_wi, 0, 0)))
        args.append(post_w)
    if post == "ab":
        n_lat_tiles = lat_tiles * n_batch
        tab = pl.BlockSpec((TOKEN_TILE, RET_HEAD_DIM),
                           lambda i: (jnp.where(i < n_lat_tiles, i % lat_tiles, lat_tiles), 0))
        in_specs += [tab, tab, tab]
        args += list(rope)
        n_bf = 3 * RET_HEADS * RET_HEAD_DIM
        out_specs += [pl.BlockSpec((TOKEN_TILE, n_bf), lambda i: (i, 0)),
                      pl.BlockSpec((TOKEN_TILE, n - n_bf), lambda i: (i, 0))]
        out_shape += [jax.ShapeDtypeStruct((n_rows, n_bf), BF16),
                      jax.ShapeDtypeStruct((n_rows, n - n_bf), F32)]
    elif post == "na":
        out_specs.append(pl.BlockSpec((TOKEN_TILE, n), lambda i: (i, 0)))
        out_shape.append(jax.ShapeDtypeStruct((n_rows, n), BF16))
    if final_g is not None:
        in_specs.append(_resident((1, d)))
        args.append(final_g.reshape(1, d))
    cfg = _HalfCfg(n_h=len(h_parts), n0=n0, pre=pre, n_y=n_y, ny0=ny0, post=post,
                   final=final_g is not None)
    return pl.pallas_call(
        functools.partial(_half_kernel, cfg=cfg),
        grid=(n_rows // TOKEN_TILE,),
        in_specs=in_specs,
        out_specs=out_specs,
        out_shape=out_shape,
        compiler_params=_cparams("parallel"),
        name="half_layer",
    )(*args)


def _rope_tables(seq):
    half = RET_HEAD_DIM // 2
    quarter = half // 2
    inv = ROPE_BASE ** (-np.arange(0, half, 2, dtype=np.float64) / half)
    t = np.arange(seq)
    lane = np.arange(RET_HEAD_DIM)
    pos = np.where(lane[None, :] < half, (t // GRID_W)[:, None], (t % GRID_W)[:, None])
    ang = pos.astype(np.float32).astype(np.float64) * inv.astype(np.float32)[lane % quarter][None, :]
    first = (lane % half) < quarter
    cos = np.cos(ang)
    sin = np.sin(ang)
    sa = np.where(first[None, :], -sin, 0.0)
    sb = np.where(first[None, :], 0.0, sin)
    pad1 = np.ones((TOKEN_TILE, RET_HEAD_DIM))
    pad0 = np.zeros((TOKEN_TILE, RET_HEAD_DIM))
    tabs = [np.concatenate([cos, pad1]), np.concatenate([sa, pad0]), np.concatenate([sb, pad0])]
    return [jnp.asarray(x, F32) for x in tabs]


def _rope(x, cos, sa, sb):
    quarter = RET_HEAD_DIM // 4
    up = pltpu.roll(x, RET_HEAD_DIM - quarter, axis=1)
    dn = pltpu.roll(x, quarter, axis=1)
    return x * cos + up * sa + dn * sb


def _retention_kernel(dec_ref, q_ref, k_ref, v_ref, g_ref,
                      dmask_ref, qwf_ref, qwb_ref, kwf_ref, kwb_ref, o_ref,
                      s_run, s_bwd, *, n_chunks, n_ctx_chunks):
    ph = pl.program_id(1)
    i = pl.program_id(2)
    hd = RET_HEAD_DIM

    @pl.when(i == 0)
    def _():
        s_run[...] = jnp.zeros_like(s_run)

    tn = (((0,), (0,)), ((), ()))
    nt = (((1,), (1,)), ((), ()))

    @pl.when(ph == 0)
    def _():
        cid = jnp.where(i < n_ctx_chunks, n_ctx_chunks - 1 - i, n_chunks + n_ctx_chunks - 1 - i)
        for h in range(RET_HEADS):
            sl = slice(h * hd, (h + 1) * hd)
            s_old = s_run[h]
            s_bwd[cid, h] = s_old
            kw = (k_ref[:, sl] * kwb_ref[h]).astype(BF16)
            kv = lax.dot_general(kw, v_ref[:, sl], tn, preferred_element_type=F32)
            s_run[h] = dec_ref[1, h] * s_old + kv

    @pl.when(ph == 1)
    def _():
        for h in range(RET_HEADS):
            sl = slice(h * hd, (h + 1) * hd)
            q = q_ref[:, sl]
            k = k_ref[:, sl]
            v = v_ref[:, sl]
            s_old = s_run[h]
            a = lax.dot_general(q, k, nt, preferred_element_type=F32)
            o = jnp.dot((a * dmask_ref[h]).astype(BF16), v, preferred_element_type=F32)
            o += qwf_ref[h] * jnp.dot(q, s_old.astype(BF16), preferred_element_type=F32)
            o += qwb_ref[h] * jnp.dot(q, s_bwd[i, h].astype(BF16), preferred_element_type=F32)
            kv = lax.dot_general((k * kwf_ref[h]).astype(BF16), v, tn, preferred_element_type=F32)
            s_run[h] = dec_ref[0, h] * s_old + kv
            o = o * lax.rsqrt(jnp.mean(o * o, axis=-1, keepdims=True) + EPS)
            g = g_ref[:, sl]
            o_ref[:, sl] = o * (g * jax.nn.sigmoid(g))


def _retention(pq, pg, decay_logit, *, n_batch, seq, ctx_len):
    c = RET_CHUNK
    width = RET_HEADS * RET_HEAD_DIM
    n_lat = seq // c
    n_ctx = ctx_len // c
    n_ch = n_lat + n_ctx
    n_rows = pq.shape[0]
    lat_blocks = n_batch * n_lat

    log_gamma = jax.nn.log_sigmoid(decay_logit.astype(F32))
    pos = jnp.arange(c, dtype=F32)
    diff = pos[:, None] - pos[None, :]
    lf = log_gamma[0][:, None, None]
    lb = log_gamma[1][:, None, None]
    dmask = (jnp.where(diff >= 0, jnp.exp(lf * jnp.maximum(diff, 0.0)), 0.0)
             + jnp.where(diff <= 0, jnp.exp(lb * jnp.maximum(-diff, 0.0)), 0.0))
    ones = jnp.ones((1, 1, RET_HEAD_DIM), F32)
    col = pos[None, :, None]
    qwf = jnp.exp(lf * (col + 1.0)) * ones
    qwb = jnp.exp(lb * (c - col)) * ones
    kwf = jnp.exp(lf * (c - 1.0 - col)) * ones
    kwb = jnp.exp(lb * col) * ones
    chunk_decay = jnp.exp(log_gamma * c)

    def chunk_id(ph, i):
        bwd = jnp.where(i < n_ctx, n_ctx - 1 - i, n_ch + n_ctx - 1 - i)
        return jnp.where(ph == 0, bwd, i)

    def row_block(b, cid):
        return jnp.where(cid < n_ctx, lat_blocks + b * n_ctx + cid, b * n_lat + cid - n_ctx)

    def col_spec(j, forward_only):
        def idx(b, ph, i):
            cid = chunk_id(ph, i)
            if forward_only:
                cid = jnp.where(ph == 0, 0, cid)
            return (row_block(b, cid), j)
        return pl.BlockSpec((c, width), idx)

    head_tab = _resident((RET_HEADS, c, RET_HEAD_DIM))
    kern = functools.partial(_retention_kernel, n_chunks=n_ch, n_ctx_chunks=n_ctx)
    return pl.pallas_call(
        kern,
        grid=(n_batch, 2, n_ch),
        in_specs=[
            pl.BlockSpec(memory_space=pltpu.SMEM),
            col_spec(0, True), col_spec(1, False), col_spec(2, False), col_spec(0, True),
            _resident((RET_HEADS, c, c)), head_tab, head_tab, head_tab, head_tab,
        ],
        out_specs=col_spec(0, True),
        out_shape=jax.ShapeDtypeStruct((n_rows, width), F32),
        scratch_shapes=[
            pltpu.VMEM((RET_HEADS, RET_HEAD_DIM, RET_HEAD_DIM), F32),
            pltpu.VMEM((n_ch, RET_HEADS, RET_HEAD_DIM, RET_HEAD_DIM), F32),
        ],
        compiler_params=_cparams("parallel", "arbitrary", "arbitrary"),
        name="retention",
    )(chunk_decay, pq, pq, pq, pg, dmask, qwf, qwb, kwf, kwb)


def _cmul(a, b):
    return a[0] * b[0] - a[1] * b[1], a[0] * b[1] + a[1] * b[0]


def _pair_blockdiag(a):
    a0, a1 = a[0::2], a[1::2]
    z = jnp.zeros_like(a0)
    return jnp.concatenate([jnp.concatenate([a0, z], axis=2), jnp.concatenate([z, a1], axis=2)],
                           axis=1)


def _pair_lanes(a):
    return jnp.concatenate([a[0::2], a[1::2]], axis=2)


def _powers(a, n):
    one = (jnp.ones_like(a[0]), jnp.zeros_like(a[0]))
    pw = (jnp.stack([one[0], a[0]], axis=1), jnp.stack([one[1], a[1]], axis=1))
    step = a
    while pw[0].shape[1] < n:
        step = _cmul(step, step)
        nxt = _cmul(pw, (step[0][:, None], step[1][:, None]))
        pw = (jnp.concatenate([pw[0], nxt[0]], axis=1), jnp.concatenate([pw[1], nxt[1]], axis=1))
    top = _cmul(step, step)
    return (jnp.concatenate([pw[0], top[0][:, None]], axis=1),
            jnp.concatenate([pw[1], top[1][:, None]], axis=1))


def _s5_weights(lam_re, lam_im, log_dt, b_re, b_im, c_re, c_im, d_skip):
    q = S5_CHUNK
    n_g = lam_re.shape[1]
    hi = lax.Precision.HIGHEST
    kall = None
    zw = []
    rw = []
    tabs = []
    for d in range(2):
        lr = jnp.minimum(lam_re[d].astype(F32), -1e-4)
        li = lam_im[d].astype(F32)
        dt = jnp.exp(log_dt[d].astype(F32))[:, None]
        mag = jnp.exp(lr * dt)
        a = (mag * jnp.cos(li * dt), mag * jnp.sin(li * dt))
        den = lr * lr + li * li
        am1 = (a[0] - 1.0, a[1])
        coef = ((am1[0] * lr + am1[1] * li) / den, (am1[1] * lr - am1[0] * li) / den)
        bt = (b_re[d].astype(F32).transpose(0, 2, 1), b_im[d].astype(F32).transpose(0, 2, 1))
        bb = _cmul((coef[0][:, None], coef[1][:, None]), bt)
        cm = (c_re[d].astype(F32), c_im[d].astype(F32))
        pw_re, pw_im = _powers(a, q)
        e = _cmul((pw_re[:, :q, None, :], pw_im[:, :q, None, :]),
                  (bb[0][:, None], bb[1][:, None]))
        taps = (jnp.einsum('gkp,gtjp->gjtk', cm[0], e[0], precision=hi)
                - jnp.einsum('gkp,gtjp->gjtk', cm[1], e[1], precision=hi))
        if d == 0:
            kall = jnp.concatenate([jnp.zeros_like(taps[:, :, 1:]), taps], axis=2)
        else:
            kall = kall + jnp.concatenate([taps[:, :, ::-1], jnp.zeros_like(taps[:, :, 1:])], axis=2)
        tsel = np.arange(q)[::-1] if d == 0 else np.arange(q)
        z = _cmul((pw_re[:, tsel][:, :, None, :], pw_im[:, tsel][:, :, None, :]),
                  (bb[0][:, None], bb[1][:, None]))
        zw.append((z[0].reshape(n_g, q * S5_GROUP, S5_STATE),
                   z[1].reshape(n_g, q * S5_GROUP, S5_STATE)))
        rsel = np.arange(1, q + 1) if d == 0 else np.arange(q, 0, -1)
        ct = (cm[0].transpose(0, 2, 1)[:, :, None, :], cm[1].transpose(0, 2, 1)[:, :, None, :])
        pt = (pw_re[:, rsel].transpose(0, 2, 1)[..., None], pw_im[:, rsel].transpose(0, 2, 1)[..., None])
        r = _cmul(ct, pt)
        rw.append((r[0].reshape(n_g, S5_STATE, q * S5_GROUP),
                   (-r[1]).reshape(n_g, S5_STATE, q * S5_GROUP)))
        a16 = (pw_re[:, q], pw_im[:, q])
        ramp_re, ramp_im = _powers(a16, S5_TILE)
        a32 = (ramp_re[:, 2], ramp_im[:, 2])
        a64 = (ramp_re[:, 4], ramp_im[:, 4])
        a128 = (ramp_re[:, 8], ramp_im[:, 8])
        rsl = slice(0, S5_TILE) if d == 0 else slice(S5_TILE - 1, None, -1)
        rows = [a16[0], a16[1], a32[0], a32[1], a64[0], a64[1], a128[0], a128[1]]
        tabs.append(jnp.concatenate([jnp.stack(rows, axis=1), ramp_re[:, rsl], ramp_im[:, rsl]],
                                    axis=1))
    dsk = d_skip.astype(F32).reshape(n_g, S5_GROUP)
    eye = jnp.eye(S5_GROUP, dtype=F32)
    kall = kall.at[:, :, q - 1].add(dsk[:, :, None] * eye[None])
    kflat = kall.reshape(n_g, S5_GROUP, (2 * q - 1) * S5_GROUP)
    m = jnp.stack([kflat[:, :, S5_GROUP * (q - 1 - ti):S5_GROUP * (2 * q - 1 - ti)] for ti in range(q)],
                  axis=1).reshape(n_g, q * S5_GROUP, q * S5_GROUP)
    w1 = jnp.concatenate([_pair_blockdiag(m)] + [_pair_blockdiag(z) for pair in zw for z in pair],
                         axis=2)
    w2 = jnp.concatenate([_pair_blockdiag(r) for pair in rw for r in pair], axis=1)
    tab = jnp.concatenate([_pair_lanes(t) for t in tabs], axis=1)
    return w1.astype(BF16), w2.astype(BF16), tab


def _tile_shift(x, s, down):
    row = lax.broadcasted_iota(jnp.int32, x.shape, 1)
    if down:
        return jnp.where(row >= s, pltpu.roll(x, s, axis=1), 0.0)
    return jnp.where(row < S5_TILE - s, pltpu.roll(x, S5_TILE - s, axis=1), 0.0)


S5_LANE_GROUPS = 128 // S5_GROUP
S5_SLAB = 64


def _block_transpose(arrs):
    lane = lax.broadcasted_iota(jnp.int32, arrs[0].shape, 1)
    blk = lane // S5_GROUP
    a = list(arrs)
    for bit in range(3):
        s = 1 << bit
        hi = (blk & s) != 0
        new = list(a)
        for i in range(S5_LANE_GROUPS):
            if i & s:
                continue
            new[i] = jnp.where(hi, pltpu.roll(a[i + s], S5_GROUP * s, axis=1), a[i])
            new[i + s] = jnp.where(hi, a[i + s], pltpu.roll(a[i], 128 - S5_GROUP * s, axis=1))
        a = new
    return a


def _s5_kernel(ul_ref, uc_ref, w1_ref, w2_ref, tab_ref, yl_ref, yc_ref, v_scr, y_scr, zx, xin,
               *, lat_tiles, ctx_tiles):
    q = S5_CHUNK
    lanes = 2 * S5_STATE
    pw = 2 * q * S5_GROUP
    n_tiles = ctx_tiles + lat_tiles
    ctx_rows = ctx_tiles * S5_TILE
    lat_rows = lat_tiles * S5_TILE
    slab = min(S5_SLAB, lat_rows)
    half = S5_TILE

    def load_slab(src_ref, r_src, r_dst, n):
        for th in range(q // half):
            arrs = [src_ref[pl.ds(r_src * q + th * half + tl, n, stride=q), :] for tl in range(half)]
            outs = _block_transpose(arrs)
            for g in range(S5_LANE_GROUPS):
                lo = (g % 2) * (pw // 2) + th * 128
                v_scr[g // 2, pl.ds(r_dst, n), lo:lo + 128] = outs[g]

    def store_slab(dst_ref, r_dst, r_src, n):
        for th in range(q // half):
            arrs = []
            for g in range(S5_LANE_GROUPS):
                lo = (g % 2) * (pw // 2) + th * 128
                arrs.append(y_scr[g // 2, pl.ds(r_src, n), lo:lo + 128])
            outs = _block_transpose(arrs)
            for tl in range(half):
                dst_ref[pl.ds(r_dst * q + th * half + tl, n, stride=q), :] = outs[tl]

    load_slab(uc_ref, 0, 0, ctx_rows)

    def load_body(i, carry):
        r = pl.multiple_of(i * slab, slab)
        load_slab(ul_ref, r, ctx_rows + r, slab)
        return carry

    lax.fori_loop(0, lat_rows // slab, load_body, 0)

    def pair_body(p, carry):
        y = jnp.dot(v_scr[p].astype(BF16), w1_ref[p], preferred_element_type=F32)
        y_scr[p] = y[:, :pw]

        def tab_row(r):
            return tab_ref[p, r:r + 1, :]

        for d in range(2):
            down = d == 0
            t0 = 24 * d
            lo = pw + 2 * d * lanes
            z = (y[:, lo:lo + lanes].reshape(n_tiles, S5_TILE, lanes),
                 y[:, lo + lanes:lo + 2 * lanes].reshape(n_tiles, S5_TILE, lanes))
            loc = (_tile_shift(z[0], 1, down), _tile_shift(z[1], 1, down))
            for k, s in enumerate((1, 2, 4)):
                mul = (tab_row(t0 + 2 * k)[None], tab_row(t0 + 2 * k + 1)[None])
                inc = _cmul(mul, (_tile_shift(loc[0], s, down), _tile_shift(loc[1], s, down)))
                loc = (loc[0] + inc[0], loc[1] + inc[1])
            zx[2 * d] = loc[0]
            zx[2 * d + 1] = loc[1]
            zx[4 + 2 * d] = z[0]
            zx[4 + 2 * d + 1] = z[1]

        for d in range(2):
            t0 = 24 * d
            edge = S5_TILE - 1 if d == 0 else 0
            a16 = (tab_row(t0), tab_row(t0 + 1))
            a128 = (tab_row(t0 + 6), tab_row(t0 + 7))
            ramp = (tab_ref[p, t0 + 8:t0 + 16, :], tab_ref[p, t0 + 16:t0 + 24, :])

            def tile_step(j, carry, d=d, edge=edge, a16=a16, a128=a128, ramp=ramp):
                loc = (zx[2 * d, j], zx[2 * d + 1, j])
                z = (zx[4 + 2 * d, j], zx[4 + 2 * d + 1, j])
                inc = _cmul(ramp, carry)
                xin[j, :, (2 * d) * lanes:(2 * d + 1) * lanes] = loc[0] + inc[0]
                xin[j, :, (2 * d + 1) * lanes:(2 * d + 2) * lanes] = loc[1] + inc[1]
                e_loc = _cmul(a16, (loc[0][edge:edge + 1], loc[1][edge:edge + 1]))
                nxt = _cmul(a128, carry)
                return (nxt[0] + e_loc[0] + z[0][edge:edge + 1],
                        nxt[1] + e_loc[1] + z[1][edge:edge + 1])

            zero = (jnp.zeros((1, lanes), F32), jnp.zeros((1, lanes), F32))
            if d == 0:
                lax.fori_loop(0, n_tiles, tile_step, zero)
            else:
                mid = lax.fori_loop(0, ctx_tiles, lambda t, c: tile_step(ctx_tiles - 1 - t, c), zero)
                lax.fori_loop(0, lat_tiles, lambda t, c: tile_step(n_tiles - 1 - t, c), mid)

        x = xin[...].reshape(n_tiles * S5_TILE, 4 * lanes)
        y_scr[p] += jnp.dot(x.astype(BF16), w2_ref[p], preferred_element_type=F32)
        return carry

    lax.fori_loop(0, v_scr.shape[0], pair_body, 0)

    store_slab(yc_ref, 0, 0, ctx_rows)

    def store_body(i, carry):
        r = pl.multiple_of(i * slab, slab)
        store_slab(yl_ref, r, ctx_rows + r, slab)
        return carry

    lax.fori_loop(0, lat_rows // slab, store_body, 0)


def _s5(p, col0, weights, *, n_batch, seq, ctx_len):
    w1, w2, tab = weights
    q = S5_CHUNK
    lanes = 2 * S5_STATE
    pw = 2 * q * S5_GROUP
    n_pairs = w1.shape[0]
    ppb = S5_LANE_GROUPS // 2
    n_blocks = n_pairs // ppb
    width = n_blocks * 128
    lat_tiles = seq // q // S5_TILE
    ctx_tiles = ctx_len // q // S5_TILE
    n_tiles = lat_tiles + ctx_tiles
    n_sup = n_tiles * S5_TILE
    cb0 = col0 // 128
    ctx_blk0 = n_batch * seq // ctx_len
    kern = functools.partial(_s5_kernel, lat_tiles=lat_tiles, ctx_tiles=ctx_tiles)
    return pl.pallas_call(
        kern,
        grid=(n_blocks, n_batch),
        in_specs=[
            pl.BlockSpec((seq, 128), lambda j, b: (b, cb0 + j)),
            pl.BlockSpec((ctx_len, 128), lambda j, b: (ctx_blk0 + b, cb0 + j)),
            pl.BlockSpec((ppb, pw, 2 * pw), lambda j, b: (j, 0, 0)),
            pl.BlockSpec((ppb, 4 * lanes, pw), lambda j, b: (j, 0, 0)),
            pl.BlockSpec((ppb, tab.shape[1], lanes), lambda j, b: (j, 0, 0)),
        ],
        out_specs=[pl.BlockSpec((seq, 128), lambda j, b: (b, j)),
                   pl.BlockSpec((ctx_len, 128), lambda j, b: (b, j))],
        out_shape=[jax.ShapeDtypeStruct((n_batch * seq, width), F32),
                   jax.ShapeDtypeStruct((n_batch * ctx_len, width), F32)],
        scratch_shapes=[
            pltpu.VMEM((ppb, n_sup, pw), F32),
            pltpu.VMEM((ppb, n_sup, pw), F32),
            pltpu.VMEM((8, n_tiles, S5_TILE, lanes), F32),
            pltpu.VMEM((n_tiles, S5_TILE, 4 * lanes), F32),
        ],
        compiler_params=_cparams("parallel", "parallel"),
        name="s5",
    )(p, p, w1, w2, tab)


NA_QROWS = 4
NA_KROWS = NA_QROWS + NA_KH


def _na_bias_table(rpb):
    w = GRID_W
    qcol = np.arange(w)
    kcol = np.arange(w)
    wstart = np.clip(qcol - NA_KW // 2, 0, w - NA_KW)
    valid = (kcol[None, :] >= wstart[:, None]) & (kcol[None, :] < wstart[:, None] + NA_KW)
    rel = np.clip(kcol[None, :] - qcol[:, None], -(NA_KW - 1), NA_KW - 1) + NA_KW - 1
    onehot = (rel[None] == np.arange(2 * NA_KW - 1)[:, None, None]).astype(np.float32)
    tiles = jnp.einsum('hrj,jqk->hrqk', rpb.astype(F32), jnp.asarray(onehot),
                       precision=lax.Precision.HIGHEST)
    tiles = jnp.where(jnp.asarray(valid)[None, None], tiles, NEG_INF)
    neg = jnp.full((NA_HEADS, w, w), NEG_INF, F32)
    types = [lambda a: (0, NA_KH - 1 - a), lambda a: (a, NA_QROWS - 1 - a),
             lambda a: (NA_QROWS, -1 - a)]
    slabs = []
    for ty in types:
        for a in range(NA_QROWS):
            m0, rel0 = ty(a)
            row = [tiles[:, rel0 + m] if m0 <= m < m0 + NA_KH else neg for m in range(NA_KROWS)]
            slabs.append(jnp.concatenate(row, axis=-1))
    bias = jnp.stack(slabs, axis=1).reshape(NA_HEADS // 2, 2, 3, NA_QROWS * w, NA_KROWS * w)
    return bias.transpose(0, 2, 1, 3, 4).reshape(NA_HEADS // 2, 3, 2 * NA_QROWS * w, NA_KROWS * w)


def _na_kernel(q_ref, k_ref, v_ref, kc_ref, vc_ref, bias_ref, o_ref, vx, vcx, s_even, s_odd,
               p_even, p_odd, *, rows):
    w = GRID_W
    dh = NA_HEAD_DIM
    lb = 2 * dh
    nq = NA_QROWS * w
    nk = NA_KROWS * w
    n_blocks = rows // NA_QROWS
    nt = (((1,), (1,)), ((), ()))
    scale = dh ** -0.5
    kc = kc_ref[...]
    lane = lax.broadcasted_iota(jnp.int32, (nq, lb), 1)
    first = lane < dh

    for dst, src in ((vx, v_ref), (vcx, vc_ref)):
        n = src.shape[0]
        dst[:, :lb] = src[...]
        dst[:, lb:] = (lax.broadcasted_iota(jnp.int32, (n, lb), 1) == 0).astype(BF16)

    def key_offset(i):
        r0 = jnp.clip(NA_QROWS * i - NA_KH // 2, 0, rows - NA_KROWS)
        return pl.multiple_of(r0 * w, NA_QROWS * w)

    def scores(i, s_ref):
        kind = jnp.where(i == 0, 0, jnp.where(i == n_blocks - 1, 2, 1))
        q = q_ref[pl.ds(pl.multiple_of(i * nq, nq), nq), :] * scale
        zero = jnp.zeros_like(q)
        qs = jnp.concatenate([jnp.where(first, q, zero), jnp.where(first, zero, q)], axis=0)
        kl = k_ref[pl.ds(key_offset(i), nk), :]
        s_ref[:, :nk] = lax.dot_general(qs, kl, nt, preferred_element_type=F32) + bias_ref[0, kind]
        s_ref[:, nk:] = lax.dot_general(qs, kc, nt, preferred_element_type=F32)

    def softmax(s_ref, p_ref):
        s = s_ref[...]
        p_ref[...] = jnp.exp(s - jnp.max(s, axis=-1, keepdims=True)).astype(BF16)

    def attend(i, p_ref):
        o = jnp.dot(p_ref[:, :nk], vx[pl.ds(key_offset(i), nk), :], preferred_element_type=F32)
        o += jnp.dot(p_ref[:, nk:], vcx[...], preferred_element_type=F32)
        o = o[:, :lb] / o[:, lb:lb + 1]
        o_ref[pl.ds(pl.multiple_of(i * nq, nq), nq), :] = jnp.where(first, o[:nq], o[nq:]).astype(
            o_ref.dtype)

    scores(0, s_even)
    softmax(s_even, p_even)
    scores(1, s_odd)

    def body(j, carry):
        attend(2 * j - 2, p_even)
        softmax(s_odd, p_odd)
        scores(2 * j, s_even)
        attend(2 * j - 1, p_odd)
        softmax(s_even, p_even)
        scores(2 * j + 1, s_odd)
        return carry

    lax.fori_loop(1, n_blocks // 2, body, 0, unroll=True)
    attend(n_blocks - 2, p_even)
    softmax(s_odd, p_odd)
    attend(n_blocks - 1, p_odd)


def _natten(p, bias, *, n_batch, seq, ctx_len):
    d = NA_HEADS * NA_HEAD_DIM
    lb = 2 * NA_HEAD_DIM
    n_pairs = NA_HEADS // 2
    ctx_blk0 = n_batch * seq // ctx_len
    rows = seq // GRID_W
    assert rows % (2 * NA_QROWS) == 0 and rows >= NA_KROWS + NA_QROWS
    stacked = 2 * NA_QROWS * GRID_W
    n_keys = NA_KROWS * GRID_W + ctx_len
    lat = lambda part: pl.BlockSpec((seq, lb), lambda j, b: (b, part * n_pairs + j))
    ctx = lambda part: pl.BlockSpec((ctx_len, lb), lambda j, b: (ctx_blk0 + b, part * n_pairs + j))
    return pl.pallas_call(
        functools.partial(_na_kernel, rows=rows),
        grid=(n_pairs, n_batch),
        in_specs=[lat(0), lat(1), lat(2), ctx(1), ctx(2),
                  pl.BlockSpec((1,) + bias.shape[1:], lambda j, b: (j, 0, 0, 0))],
        out_specs=pl.BlockSpec((seq, lb), lambda j, b: (b, j)),
        out_shape=jax.ShapeDtypeStruct((n_batch * seq, d), BF16),
        scratch_shapes=[pltpu.VMEM((seq, 2 * lb), BF16), pltpu.VMEM((ctx_len, 2 * lb), BF16),
                        pltpu.VMEM((stacked, n_keys), F32), pltpu.VMEM((stacked, n_keys), F32),
                        pltpu.VMEM((stacked, n_keys), BF16), pltpu.VMEM((stacked, n_keys), BF16)],
        compiler_params=_cparams("parallel", "parallel"),
        name="natten",
    )(p, p, p, p, p, bias)


def kernel(x, c, ctx, c_ctx, w_mod, b_mod, norm_g, ffn_w1, ffn_w2, w_in_ab, w_out_ab, ret_decay_logit, s5_lam_re, s5_lam_im, s5_log_dt, s5_b_re, s5_b_im, s5_c_re, s5_c_im, s5_d, s5_glu_w, s5_glu_b, na_w_qkv, na_w_o, na_rpb, final_g):
    n_batch, seq, d = x.shape
    ctx_len = ctx.shape[1]
    depth = w_mod.shape[0]
    n_lat = n_batch * seq
    n_all = n_lat + n_batch * ctx_len
    lat_tiles = seq // TOKEN_TILE
    assert seq % TOKEN_TILE == 0 and (n_batch * ctx_len) % TOKEN_TILE == 0
    assert n_batch + 1 <= MOD_ROWS and seq % (GRID_W * NA_KH) == 0

    cvec = jnp.concatenate([c, c_ctx[None], jnp.zeros((MOD_ROWS - n_batch - 1, d), F32)], axis=0)
    mod = _modulation(cvec, w_mod, b_mod).reshape(depth, MOD_ROWS, N_MOD, d)
    h_parts = (x.reshape(n_lat, d), ctx.reshape(n_batch * ctx_len, d))
    common = dict(lat_tiles=lat_tiles, n_batch=n_batch)
    dims = dict(n_batch=n_batch, seq=seq, ctx_len=ctx_len)
    gains = norm_g.astype(F32).reshape(depth, 3, 1, d)
    w1 = ffn_w1.astype(BF16)
    w2 = ffn_w2.astype(BF16)

    for layer in range(depth):
        last = layer == depth - 1
        i = layer // 2
        half = functools.partial(_half_layer, mod=mod[layer], gains=gains, w1=w1, w2=w2, layer=layer,
                                 **common)
        if layer % 2 == 0:
            h, pq, pg = half(h_parts, n_rows=n_all, post="ab", post_w=w_in_ab.astype(BF16), post_wi=i,
                             rope=_rope_tables(seq))
            r = _retention(pq, pg, ret_decay_logit[i], **dims)
            weights = _s5_weights(s5_lam_re[i], s5_lam_im[i], s5_log_dt[i], s5_b_re[i], s5_b_im[i],
                                  s5_c_re[i], s5_c_im[i], s5_d[i])
            ys_parts = _s5(pg, RET_HEADS * RET_HEAD_DIM, weights, **dims)
            pre = dict(pre="ab", pre_args=(r, ys_parts, s5_glu_w.astype(BF16),
                                           s5_glu_b.astype(F32)[:, None, :], w_out_ab.astype(BF16), i))
        else:
            assert last
            h, p = half(h_parts, n_rows=n_all, post="na", post_w=na_w_qkv.astype(BF16), post_wi=i)
            att = _natten(p, _na_bias_table(na_rpb[i]), **dims)
            pre = dict(pre="na", pre_args=(att, na_w_o.astype(BF16), i))
        (h,) = half((h,), n_rows=n_lat if last else n_all, final_g=final_g if last else None, **pre)
        h_parts = (h,)
    return h[:n_lat].reshape(n_batch, seq, d)
```

```python
import functools
import math
from typing import NamedTuple

import numpy as np
import jax
import jax.numpy as jnp
from jax import lax
from jax.experimental import pallas as pl
from jax.experimental.pallas import tpu as pltpu

F32 = jnp.float32
BF16 = jnp.bfloat16

EPS = 1e-6
ROPE_BASE = 10000.0
GRID_W = 64
N_MOD = 9
RET_HEADS = 4
RET_HEAD_DIM = 128
RET_CHUNK = 256
S5_GROUP = 16
S5_STATE = 64
S5_CHUNK = 16
S5_TILE = 8
NA_HEADS = 16
NA_HEAD_DIM = 64
NA_KH = 8
NA_KW = 16
NEG_INF = -1e30

TOKEN_TILE = 512
VMEM_LIMIT = 56 * 1024 * 1024
MOD_ROWS = 8


def _cparams(*sem):
    return pltpu.CompilerParams(dimension_semantics=sem, vmem_limit_bytes=VMEM_LIMIT)


def _resident(shape):
    nd = len(shape)
    return pl.BlockSpec(shape, lambda *_: (0,) * nd, pipeline_mode=pl.Buffered(1))


def _mod_kernel(c_ref, w_ref, b_ref, o_ref):
    c = c_ref[...]
    s = c * jax.nn.sigmoid(c)
    o_ref[0] = jnp.dot(s, w_ref[0], preferred_element_type=F32,
                       precision=lax.Precision.HIGHEST) + b_ref[0]


def _modulation(cvec, w_mod, b_mod):
    depth, d, nd = w_mod.shape
    tn = nd // 4 if nd % 512 == 0 else d
    return pl.pallas_call(
        _mod_kernel,
        grid=(depth, nd // tn),
        in_specs=[
            pl.BlockSpec((MOD_ROWS, d), lambda l, j: (0, 0)),
            pl.BlockSpec((1, d, tn), lambda l, j: (l, 0, j)),
            pl.BlockSpec((1, 1, tn), lambda l, j: (l, 0, j)),
        ],
        out_specs=pl.BlockSpec((1, MOD_ROWS, tn), lambda l, j: (l, 0, j)),
        out_shape=jax.ShapeDtypeStruct((depth, MOD_ROWS, nd), F32),
        compiler_params=_cparams("parallel", "parallel"),
        name="modulation",
    )(cvec, w_mod, b_mod.reshape(depth, 1, nd))


def _rms(x):
    return x * lax.rsqrt(jnp.mean(x * x, axis=-1, keepdims=True) + EPS)


def _modulated(h, m, g, mi):
    return (_rms(h) * g) * (1.0 + m[mi + 1:mi + 2]) + m[mi:mi + 1]


def _tile_specs(n_lat_tiles_per_batch, n_batch, d):
    def mod_idx(i):
        return (jnp.minimum(i // n_lat_tiles_per_batch, n_batch), 0, 0)
    h_spec = pl.BlockSpec((TOKEN_TILE, d), lambda i: (i, 0))
    m_spec = pl.BlockSpec((1, N_MOD, d), mod_idx)
    return h_spec, m_spec


def _pinned(block_shape, index):
    return pl.BlockSpec(block_shape, lambda *_: index, pipeline_mode=pl.Buffered(1))


def _stream_specs(parts, width):
    if len(parts) == 1:
        return [pl.BlockSpec((TOKEN_TILE, width), lambda i: (i, 0))], 0
    n0 = parts[0].shape[0] // TOKEN_TILE
    return [pl.BlockSpec((TOKEN_TILE, width), lambda i: (jnp.minimum(i, n0 - 1), 0)),
            pl.BlockSpec((TOKEN_TILE, width), lambda i: (jnp.maximum(i - n0, 0), 0))], n0


def _stream_tile(refs, n0):
    if len(refs) == 1:
        return refs[0][...]
    return jnp.where(pl.program_id(0) < n0, refs[0][...], refs[1][...])


class _HalfCfg(NamedTuple):
    n_h: int
    n0: int
    pre: str
    n_y: int
    ny0: int
    post: str
    final: bool


def _gelu_tanh(y):
    return 0.5 * y * (1.0 + jnp.tanh(math.sqrt(2.0 / math.pi) * (y + 0.044715 * (y * y * y))))


def _half_kernel(*refs, cfg):
    it = iter(refs)
    take = lambda n: [next(it) for _ in range(n)]
    h_refs = take(cfg.n_h)
    m_ref, g_ref = take(2)
    h = _stream_tile(h_refs, cfg.n0)
    m = m_ref[0]
    gate_mix = m[5:6]
    if cfg.pre == "ab":
        (r_ref,) = take(1)
        y_refs = take(cfg.n_y)
        gw_ref, gb_ref, wr_ref, ws_ref = take(4)
        g = _gelu_tanh(_stream_tile(y_refs, cfg.ny0))
        s = g * jax.nn.sigmoid(jnp.dot(g.astype(BF16), gw_ref[...], preferred_element_type=F32)
                               + gb_ref[...])
        mix = jnp.dot(r_ref[...].astype(BF16), wr_ref[...], preferred_element_type=F32)
        mix += jnp.dot(s.astype(BF16), ws_ref[...], preferred_element_type=F32)
        h = h + gate_mix * mix
    elif cfg.pre == "na":
        a_ref, wo_ref = take(2)
        h = h + gate_mix * jnp.dot(a_ref[...], wo_ref[...], preferred_element_type=F32)
    w1a_ref, w1b_ref, w2_ref = take(3)
    k = 1 if cfg.pre else 0
    mi = 6 * k
    xm = _modulated(h, m, g_ref[2 * k], mi).astype(BF16)
    a = jnp.dot(xm, w1a_ref[...], preferred_element_type=F32)
    b = jnp.dot(xm, w1b_ref[...], preferred_element_type=F32)
    hid = (a * jax.nn.sigmoid(a) * b).astype(BF16)
    h = h + (0.5 * m[mi + 2:mi + 3]) * jnp.dot(hid, w2_ref[...], preferred_element_type=F32)
    if cfg.post:
        (wp_ref,) = take(1)
    if cfg.post == "ab":
        cos_ref, sa_ref, sb_ref = take(3)
    if cfg.final:
        (fg_ref,) = take(1)
    outs = list(it)
    outs[0][...] = _rms(h) * fg_ref[...] if cfg.final else h
    if not cfg.post:
        return
    xm = _modulated(h, m, g_ref[1], 3).astype(BF16)
    pr = jnp.dot(xm, wp_ref[...], preferred_element_type=F32)
    if cfg.post == "na":
        outs[1][...] = pr.astype(BF16)
        return
    width = RET_HEADS * RET_HEAD_DIM
    cos, sa, sb = cos_ref[...], sa_ref[...], sb_ref[...]
    for hh in range(RET_HEADS):
        lo = hh * RET_HEAD_DIM
        hi = lo + RET_HEAD_DIM
        outs[1][:, lo:hi] = _rope(pr[:, lo:hi], cos, sa, sb).astype(BF16)
        kr = _rope(pr[:, width + lo:width + hi], cos, sa, sb) * (RET_HEAD_DIM ** -0.5)
        outs[1][:, width + lo:width + hi] = kr.astype(BF16)
    outs[1][:, 2 * width:] = pr[:, 2 * width:3 * width].astype(BF16)
    outs[2][...] = pr[:, 3 * width:]


def _half_layer(h_parts, mod, gains, w1, w2, *, layer, n_rows, lat_tiles, n_batch,
                pre="", pre_args=(), post="", post_w=None, post_wi=0, rope=None, final_g=None):
    d = h_parts[0].shape[1]
    f = w2.shape[2]
    k = 1 if pre else 0
    h_specs, n0 = _stream_specs(h_parts, d)
    _, m_spec = _tile_specs(lat_tiles, n_batch, d)
    in_specs = h_specs + [m_spec, _pinned((None, 3, 1, d), (layer, 0, 0, 0))]
    args = list(h_parts) + [mod, gains]
    n_y = ny0 = 0
    if pre == "ab":
        r, ys_parts, glu_w, glu_b, w_out, wi = pre_args
        w = r.shape[1]
        y_specs, ny0 = _stream_specs(ys_parts, w)
        n_y = len(ys_parts)
        in_specs += [pl.BlockSpec((TOKEN_TILE, w), lambda i: (i, 0))] + y_specs + [
            _pinned((None, w, w), (wi, 0, 0)), _pinned((None, 1, w), (wi, 0, 0)),
            _pinned((None, w, d), (wi, 0, 0)), _pinned((None, w, d), (wi, 1, 0))]
        args += [r, *ys_parts, glu_w, glu_b, w_out, w_out]
    elif pre == "na":
        att, w_o, wi = pre_args
        in_specs += [pl.BlockSpec((TOKEN_TILE, d), lambda i: (i, 0)), _pinned((None, d, d), (wi, 0, 0))]
        args += [att, w_o]
    in_specs += [_pinned((None, None, d, f), (layer, k, 0, 0)),
                 _pinned((None, None, d, f), (layer, k, 0, 1)),
                 _pinned((None, None, f, d), (layer, k, 0, 0))]
    args += [w1, w1, w2]
    out_specs = [pl.BlockSpec((TOKEN_TILE, d), lambda i: (i, 0))]
    out_shape = [jax.ShapeDtypeStruct((n_rows, d), F32)]
    if post:
        n = post_w.shape[2]
        in_specs.append(_pinned((None, d, n), (post_wi, 0, 0)))
        args.append(post_w)
    if post == "ab":
        n_lat_tiles = lat_tiles * n_batch
        tab = pl.BlockSpec((TOKEN_TILE, RET_HEAD_DIM),
                           lambda i: (jnp.where(i < n_lat_tiles, i % lat_tiles, lat_tiles), 0))
        in_specs += [tab, tab, tab]
        args += list(rope)
        n_bf = 3 * RET_HEADS * RET_HEAD_DIM
        out_specs += [pl.BlockSpec((TOKEN_TILE, n_bf), lambda i: (i, 0)),
                      pl.BlockSpec((TOKEN_TILE, n - n_bf), lambda i: (i, 0))]
        out_shape += [jax.ShapeDtypeStruct((n_rows, n_bf), BF16),
                      jax.ShapeDtypeStruct((n_rows, n - n_bf), F32)]
    elif post == "na":
        out_specs.append(pl.BlockSpec((TOKEN_TILE, n), lambda i: (i, 0)))
        out_shape.append(jax.ShapeDtypeStruct((n_rows, n), BF16))
    if final_g is not None:
        in_specs.append(_resident((1, d)))
        args.append(final_g.reshape(1, d))
    cfg = _HalfCfg(n_h=len(h_parts), n0=n0, pre=pre, n_y=n_y, ny0=ny0, post=post,
                   final=final_g is not None)
    return pl.pallas_call(
        functools.partial(_half_kernel, cfg=cfg),
        grid=(n_rows // TOKEN_TILE,),
        in_specs=in_specs,
        out_specs=out_specs,
        out_shape=out_shape,
        compiler_params=_cparams("parallel"),
        name="half_layer",
    )(*args)


def _rope_tables(seq):
    half = RET_HEAD_DIM // 2
    quarter = half // 2
    inv = ROPE_BASE ** (-np.arange(0, half, 2, dtype=np.float64) / half)
    t = np.arange(seq)
    lane = np.arange(RET_HEAD_DIM)
    pos = np.where(lane[None, :] < half, (t // GRID_W)[:, None], (t % GRID_W)[:, None])
    ang = pos.astype(np.float32).astype(np.float64) * inv.astype(np.float32)[lane % quarter][None, :]
    first = (lane % half) < quarter
    cos = np.cos(ang)
    sin = np.sin(ang)
    sa = np.where(first[None, :], -sin, 0.0)
    sb = np.where(first[None, :], 0.0, sin)
    pad1 = np.ones((TOKEN_TILE, RET_HEAD_DIM))
    pad0 = np.zeros((TOKEN_TILE, RET_HEAD_DIM))
    tabs = [np.concatenate([cos, pad1]), np.concatenate([sa, pad0]), np.concatenate([sb, pad0])]
    return [jnp.asarray(x, F32) for x in tabs]


def _rope(x, cos, sa, sb):
    quarter = RET_HEAD_DIM // 4
    up = pltpu.roll(x, RET_HEAD_DIM - quarter, axis=1)
    dn = pltpu.roll(x, quarter, axis=1)
    return x * cos + up * sa + dn * sb


def _retention_kernel(dec_ref, q_ref, k_ref, v_ref, g_ref,
                      dmask_ref, qwf_ref, qwb_ref, kwf_ref, kwb_ref, o_ref,
                      s_run, s_bwd, *, n_chunks, n_ctx_chunks):
    ph = pl.program_id(1)
    i = pl.program_id(2)
    hd = RET_HEAD_DIM

    @pl.when(i == 0)
    def _():
        s_run[...] = jnp.zeros_like(s_run)

    tn = (((0,), (0,)), ((), ()))
    nt = (((1,), (1,)), ((), ()))

    @pl.when(ph == 0)
    def _():
        cid = jnp.where(i < n_ctx_chunks, n_ctx_chunks - 1 - i, n_chunks + n_ctx_chunks - 1 - i)
        for h in range(RET_HEADS):
            sl = slice(h * hd, (h + 1) * hd)
            s_old = s_run[h]
            s_bwd[cid, h] = s_old
            kw = (k_ref[:, sl] * kwb_ref[h]).astype(BF16)
            kv = lax.dot_general(kw, v_ref[:, sl], tn, preferred_element_type=F32)
            s_run[h] = dec_ref[1, h] * s_old + kv

    @pl.when(ph == 1)
    def _():
        for h in range(RET_HEADS):
            sl = slice(h * hd, (h + 1) * hd)
            q = q_ref[:, sl]
            k = k_ref[:, sl]
            v = v_ref[:, sl]
            s_old = s_run[h]
            a = lax.dot_general(q, k, nt, preferred_element_type=F32)
            o = jnp.dot((a * dmask_ref[h]).astype(BF16), v, preferred_element_type=F32)
            o += qwf_ref[h] * jnp.dot(q, s_old.astype(BF16), preferred_element_type=F32)
            o += qwb_ref[h] * jnp.dot(q, s_bwd[i, h].astype(BF16), preferred_element_type=F32)
            kv = lax.dot_general((k * kwf_ref[h]).astype(BF16), v, tn, preferred_element_type=F32)
            s_run[h] = dec_ref[0, h] * s_old + kv
            o = o * lax.rsqrt(jnp.mean(o * o, axis=-1, keepdims=True) + EPS)
            g = g_ref[:, sl]
            o_ref[:, sl] = o * (g * jax.nn.sigmoid(g))


def _retention(pq, pg, decay_logit, *, n_batch, seq, ctx_len):
    c = RET_CHUNK
    width = RET_HEADS * RET_HEAD_DIM
    n_lat = seq // c
    n_ctx = ctx_len // c
    n_ch = n_lat + n_ctx
    n_rows = pq.shape[0]
    lat_blocks = n_batch * n_lat

    log_gamma = jax.nn.log_sigmoid(decay_logit.astype(F32))
    pos = jnp.arange(c, dtype=F32)
    diff = pos[:, None] - pos[None, :]
    lf = log_gamma[0][:, None, None]
    lb = log_gamma[1][:, None, None]
    dmask = (jnp.where(diff >= 0, jnp.exp(lf * jnp.maximum(diff, 0.0)), 0.0)
             + jnp.where(diff <= 0, jnp.exp(lb * jnp.maximum(-diff, 0.0)), 0.0))
    ones = jnp.ones((1, 1, RET_HEAD_DIM), F32)
    col = pos[None, :, None]
    qwf = jnp.exp(lf * (col + 1.0)) * ones
    qwb = jnp.exp(lb * (c - col)) * ones
    kwf = jnp.exp(lf * (c - 1.0 - col)) * ones
    kwb = jnp.exp(lb * col) * ones
    chunk_decay = jnp.exp(log_gamma * c)

    def chunk_id(ph, i):
        bwd = jnp.where(i < n_ctx, n_ctx - 1 - i, n_ch + n_ctx - 1 - i)
        return jnp.where(ph == 0, bwd, i)

    def row_block(b, cid):
        return jnp.where(cid < n_ctx, lat_blocks + b * n_ctx + cid, b * n_lat + cid - n_ctx)

    def col_spec(j, forward_only):
        def idx(b, ph, i):
            cid = chunk_id(ph, i)
            if forward_only:
                cid = jnp.where(ph == 0, 0, cid)
            return (row_block(b, cid), j)
        return pl.BlockSpec((c, width), idx)

    head_tab = _resident((RET_HEADS, c, RET_HEAD_DIM))
    kern = functools.partial(_retention_kernel, n_chunks=n_ch, n_ctx_chunks=n_ctx)
    return pl.pallas_call(
        kern,
        grid=(n_batch, 2, n_ch),
        in_specs=[
            pl.BlockSpec(memory_space=pltpu.SMEM),
            col_spec(0, True), col_spec(1, False), col_spec(2, False), col_spec(0, True),
            _resident((RET_HEADS, c, c)), head_tab, head_tab, head_tab, head_tab,
        ],
        out_specs=col_spec(0, True),
        out_shape=jax.ShapeDtypeStruct((n_rows, width), F32),
        scratch_shapes=[
            pltpu.VMEM((RET_HEADS, RET_HEAD_DIM, RET_HEAD_DIM), F32),
            pltpu.VMEM((n_ch, RET_HEADS, RET_HEAD_DIM, RET_HEAD_DIM), F32),
        ],
        compiler_params=_cparams("parallel", "arbitrary", "arbitrary"),
        name="retention",
    )(chunk_decay, pq, pq, pq, pg, dmask, qwf, qwb, kwf, kwb)


def _cmul(a, b):
    return a[0] * b[0] - a[1] * b[1], a[0] * b[1] + a[1] * b[0]


def _pair_blockdiag(a):
    a0, a1 = a[0::2], a[1::2]
    z = jnp.zeros_like(a0)
    return jnp.concatenate([jnp.concatenate([a0, z], axis=2), jnp.concatenate([z, a1], axis=2)],
                           axis=1)


def _pair_lanes(a):
    return jnp.concatenate([a[0::2], a[1::2]], axis=2)


def _powers(a, n):
    one = (jnp.ones_like(a[0]), jnp.zeros_like(a[0]))
    pw = (jnp.stack([one[0], a[0]], axis=1), jnp.stack([one[1], a[1]], axis=1))
    step = a
    while pw[0].shape[1] < n:
        step = _cmul(step, step)
        nxt = _cmul(pw, (step[0][:, None], step[1][:, None]))
        pw = (jnp.concatenate([pw[0], nxt[0]], axis=1), jnp.concatenate([pw[1], nxt[1]], axis=1))
    top = _cmul(step, step)
    return (jnp.concatenate([pw[0], top[0][:, None]], axis=1),
            jnp.concatenate([pw[1], top[1][:, None]], axis=1))


def _s5_weights(lam_re, lam_im, log_dt, b_re, b_im, c_re, c_im, d_skip):
    q = S5_CHUNK
    n_g = lam_re.shape[1]
    hi = lax.Precision.HIGHEST
    kall = None
    zw = []
    rw = []
    tabs = []
    for d in range(2):
        lr = jnp.minimum(lam_re[d].astype(F32), -1e-4)
        li = lam_im[d].astype(F32)
        dt = jnp.exp(log_dt[d].astype(F32))[:, None]
        mag = jnp.exp(lr * dt)
        a = (mag * jnp.cos(li * dt), mag * jnp.sin(li * dt))
        den = lr * lr + li * li
        am1 = (a[0] - 1.0, a[1])
        coef = ((am1[0] * lr + am1[1] * li) / den, (am1[1] * lr - am1[0] * li) / den)
        bt = (b_re[d].astype(F32).transpose(0, 2, 1), b_im[d].astype(F32).transpose(0, 2, 1))
        bb = _cmul((coef[0][:, None], coef[1][:, None]), bt)
        cm = (c_re[d].astype(F32), c_im[d].astype(F32))
        pw_re, pw_im = _powers(a, q)
        e = _cmul((pw_re[:, :q, None, :], pw_im[:, :q, None, :]),
                  (bb[0][:, None], bb[1][:, None]))
        taps = (jnp.einsum('gkp,gtjp->gjtk', cm[0], e[0], precision=hi)
                - jnp.einsum('gkp,gtjp->gjtk', cm[1], e[1], precision=hi))
        if d == 0:
            kall = jnp.concatenate([jnp.zeros_like(taps[:, :, 1:]), taps], axis=2)
        else:
            kall = kall + jnp.concatenate([taps[:, :, ::-1], jnp.zeros_like(taps[:, :, 1:])], axis=2)
        tsel = np.arange(q)[::-1] if d == 0 else np.arange(q)
        z = _cmul((pw_re[:, tsel][:, :, None, :], pw_im[:, tsel][:, :, None, :]),
                  (bb[0][:, None], bb[1][:, None]))
        zw.append((z[0].reshape(n_g, q * S5_GROUP, S5_STATE),
                   z[1].reshape(n_g, q * S5_GROUP, S5_STATE)))
        rsel = np.arange(1, q + 1) if d == 0 else np.arange(q, 0, -1)
        ct = (cm[0].transpose(0, 2, 1)[:, :, None, :], cm[1].transpose(0, 2, 1)[:, :, None, :])
        pt = (pw_re[:, rsel].transpose(0, 2, 1)[..., None], pw_im[:, rsel].transpose(0, 2, 1)[..., None])
        r = _cmul(ct, pt)
        rw.append((r[0].reshape(n_g, S5_STATE, q * S5_GROUP),
                   (-r[1]).reshape(n_g, S5_STATE, q * S5_GROUP)))
        a16 = (pw_re[:, q], pw_im[:, q])
        ramp_re, ramp_im = _powers(a16, S5_TILE)
        a32 = (ramp_re[:, 2], ramp_im[:, 2])
        a64 = (ramp_re[:, 4], ramp_im[:, 4])
        a128 = (ramp_re[:, 8], ramp_im[:, 8])
        rsl = slice(0, S5_TILE) if d == 0 else slice(S5_TILE - 1, None, -1)
        rows = [a16[0], a16[1], a32[0], a32[1], a64[0], a64[1], a128[0], a128[1]]
        tabs.append(jnp.concatenate([jnp.stack(rows, axis=1), ramp_re[:, rsl], ramp_im[:, rsl]],
                                    axis=1))
    dsk = d_skip.astype(F32).reshape(n_g, S5_GROUP)
    eye = jnp.eye(S5_GROUP, dtype=F32)
    kall = kall.at[:, :, q - 1].add(dsk[:, :, None] * eye[None])
    kall = jnp.concatenate([kall, jnp.zeros_like(kall[:, :, :1])], axis=2)
    kflat = kall.reshape(n_g, S5_GROUP, 2 * q * S5_GROUP)
    wz = jnp.concatenate([_pair_blockdiag(z) for pair in zw for z in pair], axis=2)
    w2 = jnp.concatenate([_pair_blockdiag(r) for pair in rw for r in pair], axis=1)
    tab = jnp.concatenate([_pair_lanes(t) for t in tabs], axis=1)
    return kflat, wz.astype(BF16), w2.astype(BF16), tab


def _tile_shift(x, s, down):
    row = lax.broadcasted_iota(jnp.int32, x.shape, 1)
    if down:
        return jnp.where(row >= s, pltpu.roll(x, s, axis=1), 0.0)
    return jnp.where(row < S5_TILE - s, pltpu.roll(x, S5_TILE - s, axis=1), 0.0)


S5_LANE_GROUPS = 128 // S5_GROUP
S5_SLAB = 64


def _block_transpose(arrs):
    lane = lax.broadcasted_iota(jnp.int32, arrs[0].shape, 1)
    blk = lane // S5_GROUP
    a = list(arrs)
    for bit in range(3):
        s = 1 << bit
        hi = (blk & s) != 0
        new = list(a)
        for i in range(S5_LANE_GROUPS):
            if i & s:
                continue
            new[i] = jnp.where(hi, pltpu.roll(a[i + s], S5_GROUP * s, axis=1), a[i])
            new[i + s] = jnp.where(hi, a[i + s], pltpu.roll(a[i], 128 - S5_GROUP * s, axis=1))
        a = new
    return a


def _s5_kernel(ul_ref, uc_ref, kf_ref, wz_ref, w2_ref, tab_ref, yl_ref, yc_ref, m_scr, v_scr, y_scr,
               zx, xin, *, lat_tiles, ctx_tiles):
    q = S5_CHUNK
    lanes = 2 * S5_STATE
    pw = 2 * q * S5_GROUP
    gw = q * S5_GROUP

    @pl.when(pl.program_id(1) == 0)
    def _():
        m_scr[...] = jnp.zeros_like(m_scr)
        for g in range(S5_LANE_GROUPS):
            taps = kf_ref[g]
            base = (g % 2) * gw
            for ti in range(q):
                lo = S5_GROUP * (q - 1 - ti)
                m_scr[g // 2, base + ti * S5_GROUP:base + (ti + 1) * S5_GROUP, base:base + gw] = (
                    taps[:, lo:lo + gw].astype(BF16))

    n_tiles = ctx_tiles + lat_tiles
    ctx_rows = ctx_tiles * S5_TILE
    lat_rows = lat_tiles * S5_TILE
    slab = min(S5_SLAB, lat_rows)
    half = S5_TILE

    def load_slab(src_ref, r_src, r_dst, n):
        for th in range(q // half):
            arrs = [src_ref[pl.ds(r_src * q + th * half + tl, n, stride=q), :] for tl in range(half)]
            outs = _block_transpose(arrs)
            for g in range(S5_LANE_GROUPS):
                lo = (g % 2) * (pw // 2) + th * 128
                v_scr[g // 2, pl.ds(r_dst, n), lo:lo + 128] = outs[g]

    def store_slab(dst_ref, r_dst, r_src, n):
        for th in range(q // half):
            arrs = []
            for g in range(S5_LANE_GROUPS):
                lo = (g % 2) * (pw // 2) + th * 128
                arrs.append(y_scr[g // 2, pl.ds(r_src, n), lo:lo + 128])
            outs = _block_transpose(arrs)
            for tl in range(half):
                dst_ref[pl.ds(r_dst * q + th * half + tl, n, stride=q), :] = outs[tl]

    load_slab(uc_ref, 0, 0, ctx_rows)

    def load_body(i, carry):
        r = pl.multiple_of(i * slab, slab)
        load_slab(ul_ref, r, ctx_rows + r, slab)
        return carry

    lax.fori_loop(0, lat_rows // slab, load_body, 0)

    def pair_body(p, carry):
        v = v_scr[p].astype(BF16)
        y_scr[p] = jnp.dot(v, m_scr[p], preferred_element_type=F32)
        y = jnp.dot(v, wz_ref[p], preferred_element_type=F32)

        def tab_row(r):
            return tab_ref[p, r:r + 1, :]

        for d in range(2):
            down = d == 0
            t0 = 24 * d
            lo = 2 * d * lanes
            z = (y[:, lo:lo + lanes].reshape(n_tiles, S5_TILE, lanes),
                 y[:, lo + lanes:lo + 2 * lanes].reshape(n_tiles, S5_TILE, lanes))
            loc = (_tile_shift(z[0], 1, down), _tile_shift(z[1], 1, down))
            for k, s in enumerate((1, 2, 4)):
                mul = (tab_row(t0 + 2 * k)[None], tab_row(t0 + 2 * k + 1)[None])
                inc = _cmul(mul, (_tile_shift(loc[0], s, down), _tile_shift(loc[1], s, down)))
                loc = (loc[0] + inc[0], loc[1] + inc[1])
            zx[2 * d] = loc[0]
            zx[2 * d + 1] = loc[1]
            zx[4 + 2 * d] = z[0]
            zx[4 + 2 * d + 1] = z[1]

        for d in range(2):
            t0 = 24 * d
            edge = S5_TILE - 1 if d == 0 else 0
            a16 = (tab_row(t0), tab_row(t0 + 1))
            a128 = (tab_row(t0 + 6), tab_row(t0 + 7))
            ramp = (tab_ref[p, t0 + 8:t0 + 16, :], tab_ref[p, t0 + 16:t0 + 24, :])

            def tile_step(j, carry, d=d, edge=edge, a16=a16, a128=a128, ramp=ramp):
                loc = (zx[2 * d, j], zx[2 * d + 1, j])
                z = (zx[4 + 2 * d, j], zx[4 + 2 * d + 1, j])
                inc = _cmul(ramp, carry)
                xin[j, :, (2 * d) * lanes:(2 * d + 1) * lanes] = loc[0] + inc[0]
                xin[j, :, (2 * d + 1) * lanes:(2 * d + 2) * lanes] = loc[1] + inc[1]
                e_loc = _cmul(a16, (loc[0][edge:edge + 1], loc[1][edge:edge + 1]))
                nxt = _cmul(a128, carry)
                return (nxt[0] + e_loc[0] + z[0][edge:edge + 1],
                        nxt[1] + e_loc[1] + z[1][edge:edge + 1])

            zero = (jnp.zeros((1, lanes), F32), jnp.zeros((1, lanes), F32))
            if d == 0:
                lax.fori_loop(0, n_tiles, tile_step, zero)
            else:
                mid = lax.fori_loop(0, ctx_tiles, lambda t, c: tile_step(ctx_tiles - 1 - t, c), zero)
                lax.fori_loop(0, lat_tiles, lambda t, c: tile_step(n_tiles - 1 - t, c), mid)

        x = xin[...].reshape(n_tiles * S5_TILE, 4 * lanes)
        y_scr[p] += jnp.dot(x.astype(BF16), w2_ref[p], preferred_element_type=F32)
        return carry

    lax.fori_loop(0, v_scr.shape[0], pair_body, 0)

    store_slab(yc_ref, 0, 0, ctx_rows)

    def store_body(i, carry):
        r = pl.multiple_of(i * slab, slab)
        store_slab(yl_ref, r, ctx_rows + r, slab)
        return carry

    lax.fori_loop(0, lat_rows // slab, store_body, 0)


def _s5(p, col0, weights, *, n_batch, seq, ctx_len):
    kflat, wz, w2, tab = weights
    q = S5_CHUNK
    lanes = 2 * S5_STATE
    pw = 2 * q * S5_GROUP
    n_pairs = w2.shape[0]
    ppb = S5_LANE_GROUPS // 2
    n_blocks = n_pairs // ppb
    width = n_blocks * 128
    lat_tiles = seq // q // S5_TILE
    ctx_tiles = ctx_len // q // S5_TILE
    n_tiles = lat_tiles + ctx_tiles
    n_sup = n_tiles * S5_TILE
    cb0 = col0 // 128
    ctx_blk0 = n_batch * seq // ctx_len
    kern = functools.partial(_s5_kernel, lat_tiles=lat_tiles, ctx_tiles=ctx_tiles)
    return pl.pallas_call(
        kern,
        grid=(n_blocks, n_batch),
        in_specs=[
            pl.BlockSpec((seq, 128), lambda j, b: (b, cb0 + j)),
            pl.BlockSpec((ctx_len, 128), lambda j, b: (ctx_blk0 + b, cb0 + j)),
            pl.BlockSpec((S5_LANE_GROUPS,) + kflat.shape[1:], lambda j, b: (j, 0, 0)),
            pl.BlockSpec((ppb, pw, 4 * lanes), lambda j, b: (j, 0, 0)),
            pl.BlockSpec((ppb, 4 * lanes, pw), lambda j, b: (j, 0, 0)),
            pl.BlockSpec((ppb, tab.shape[1], lanes), lambda j, b: (j, 0, 0)),
        ],
        out_specs=[pl.BlockSpec((seq, 128), lambda j, b: (b, j)),
                   pl.BlockSpec((ctx_len, 128), lambda j, b: (b, j))],
        out_shape=[jax.ShapeDtypeStruct((n_batch * seq, width), F32),
                   jax.ShapeDtypeStruct((n_batch * ctx_len, width), F32)],
        scratch_shapes=[
            pltpu.VMEM((ppb, pw, pw), BF16),
            pltpu.VMEM((ppb, n_sup, pw), F32),
            pltpu.VMEM((ppb, n_sup, pw), F32),
            pltpu.VMEM((8, n_tiles, S5_TILE, lanes), F32),
            pltpu.VMEM((n_tiles, S5_TILE, 4 * lanes), F32),
        ],
        compiler_params=_cparams("arbitrary", "arbitrary"),
        name="s5",
    )(p, p, kflat, wz, w2, tab)


NA_QROWS = 4
NA_KROWS = NA_QROWS + NA_KH


NA_REL_ROWS = 2 * NA_KH - 1


def _na_tile_index(kind, a, m):
    first, rel0 = ((0, NA_KH - 1 - a), (a, NA_QROWS - 1 - a), (NA_QROWS, -1 - a))[kind]
    return rel0 + m if first <= m < first + NA_KH else NA_REL_ROWS


def _na_bias_tiles(rpb):
    w = GRID_W
    qcol = np.arange(w)
    kcol = np.arange(w)
    wstart = np.clip(qcol - NA_KW // 2, 0, w - NA_KW)
    valid = (kcol[None, :] >= wstart[:, None]) & (kcol[None, :] < wstart[:, None] + NA_KW)
    rel = np.clip(kcol[None, :] - qcol[:, None], -(NA_KW - 1), NA_KW - 1) + NA_KW - 1
    onehot = (rel[None] == np.arange(2 * NA_KW - 1)[:, None, None]).astype(np.float32)
    tiles = jnp.einsum('hrj,jqk->hrqk', rpb.astype(F32), jnp.asarray(onehot),
                       precision=lax.Precision.HIGHEST)
    tiles = jnp.where(jnp.asarray(valid)[None, None], tiles, NEG_INF)
    return jnp.concatenate([tiles, jnp.full((NA_HEADS, 1, w, w), NEG_INF, F32)], axis=1)


def _na_kernel(q_ref, k_ref, v_ref, kc_ref, vc_ref, tiles_ref, o_ref, bias_ref, vx, vcx, s_even, s_odd,
               p_even, p_odd, *, rows):
    w = GRID_W
    dh = NA_HEAD_DIM
    lb = 2 * dh
    nq = NA_QROWS * w
    nk = NA_KROWS * w
    n_blocks = rows // NA_QROWS

    @pl.when(pl.program_id(1) == 0)
    def _():
        for kind in range(3):
            for hh in range(2):
                for a in range(NA_QROWS):
                    for m in range(0, NA_KROWS, 2):
                        pair = [tiles_ref[hh, _na_tile_index(kind, a, m + e)] for e in range(2)]
                        bias_ref[kind, hh * nq + a * w:hh * nq + (a + 1) * w, m * w:(m + 2) * w] = (
                            jnp.concatenate(pair, axis=1))

    nt = (((1,), (1,)), ((), ()))
    scale = dh ** -0.5
    kc = kc_ref[...]
    lane = lax.broadcasted_iota(jnp.int32, (nq, lb), 1)
    first = lane < dh

    for dst, src in ((vx, v_ref), (vcx, vc_ref)):
        n = src.shape[0]
        dst[:, :lb] = src[...]
        dst[:, lb:] = (lax.broadcasted_iota(jnp.int32, (n, lb), 1) == 0).astype(BF16)

    def key_offset(i):
        r0 = jnp.clip(NA_QROWS * i - NA_KH // 2, 0, rows - NA_KROWS)
        return pl.multiple_of(r0 * w, NA_QROWS * w)

    def scores(i, s_ref):
        kind = jnp.where(i == 0, 0, jnp.where(i == n_blocks - 1, 2, 1))
        q = q_ref[pl.ds(pl.multiple_of(i * nq, nq), nq), :] * scale
        zero = jnp.zeros_like(q)
        qs = jnp.concatenate([jnp.where(first, q, zero), jnp.where(first, zero, q)], axis=0)
        kl = k_ref[pl.ds(key_offset(i), nk), :]
        s_ref[:, :nk] = lax.dot_general(qs, kl, nt, preferred_element_type=F32) + bias_ref[kind]
        s_ref[:, nk:] = lax.dot_general(qs, kc, nt, preferred_element_type=F32)

    def softmax(s_ref, p_ref):
        s = s_ref[...]
        p_ref[...] = jnp.exp(s - jnp.max(s, axis=-1, keepdims=True)).astype(BF16)

    def attend(i, p_ref):
        o = jnp.dot(p_ref[:, :nk], vx[pl.ds(key_offset(i), nk), :], preferred_element_type=F32)
        o += jnp.dot(p_ref[:, nk:], vcx[...], preferred_element_type=F32)
        o = o[:, :lb] / o[:, lb:lb + 1]
        o_ref[pl.ds(pl.multiple_of(i * nq, nq), nq), :] = jnp.where(first, o[:nq], o[nq:]).astype(
            o_ref.dtype)

    scores(0, s_even)
    softmax(s_even, p_even)
    scores(1, s_odd)

    def body(j, carry):
        attend(2 * j - 2, p_even)
        softmax(s_odd, p_odd)
        scores(2 * j, s_even)
        attend(2 * j - 1, p_odd)
        softmax(s_even, p_even)
        scores(2 * j + 1, s_odd)
        return carry

    lax.fori_loop(1, n_blocks // 2, body, 0, unroll=True)
    attend(n_blocks - 2, p_even)
    softmax(s_odd, p_odd)
    attend(n_blocks - 1, p_odd)


def _natten(p, tiles, *, n_batch, seq, ctx_len):
    d = NA_HEADS * NA_HEAD_DIM
    lb = 2 * NA_HEAD_DIM
    n_pairs = NA_HEADS // 2
    ctx_blk0 = n_batch * seq // ctx_len
    rows = seq // GRID_W
    assert rows % (2 * NA_QROWS) == 0 and rows >= NA_KROWS + NA_QROWS
    stacked = 2 * NA_QROWS * GRID_W
    n_keys = NA_KROWS * GRID_W + ctx_len
    lat = lambda part: pl.BlockSpec((seq, lb), lambda j, b: (b, part * n_pairs + j))
    ctx = lambda part: pl.BlockSpec((ctx_len, lb), lambda j, b: (ctx_blk0 + b, part * n_pairs + j))
    return pl.pallas_call(
        functools.partial(_na_kernel, rows=rows),
        grid=(n_pairs, n_batch),
        in_specs=[lat(0), lat(1), lat(2), ctx(1), ctx(2),
                  pl.BlockSpec((2,) + tiles.shape[1:], lambda j, b: (j, 0, 0, 0))],
        out_specs=pl.BlockSpec((seq, lb), lambda j, b: (b, j)),
        out_shape=jax.ShapeDtypeStruct((n_batch * seq, d), BF16),
        scratch_shapes=[pltpu.VMEM((3, stacked, NA_KROWS * GRID_W), F32),
                        pltpu.VMEM((seq, 2 * lb), BF16), pltpu.VMEM((ctx_len, 2 * lb), BF16),
                        pltpu.VMEM((stacked, n_keys), F32), pltpu.VMEM((stacked, n_keys), F32),
                        pltpu.VMEM((stacked, n_keys), BF16), pltpu.VMEM((stacked, n_keys), BF16)],
        compiler_params=_cparams("arbitrary", "arbitrary"),
        name="natten",
    )(p, p, p, p, p, tiles)


def kernel(x, c, ctx, c_ctx, w_mod, b_mod, norm_g, ffn_w1, ffn_w2, w_in_ab, w_out_ab, ret_decay_logit, s5_lam_re, s5_lam_im, s5_log_dt, s5_b_re, s5_b_im, s5_c_re, s5_c_im, s5_d, s5_glu_w, s5_glu_b, na_w_qkv, na_w_o, na_rpb, final_g):
    n_batch, seq, d = x.shape
    ctx_len = ctx.shape[1]
    depth = w_mod.shape[0]
    n_lat = n_batch * seq
    n_all = n_lat + n_batch * ctx_len
    lat_tiles = seq // TOKEN_TILE
    assert seq % TOKEN_TILE == 0 and (n_batch * ctx_len) % TOKEN_TILE == 0
    assert n_batch + 1 <= MOD_ROWS and seq % (GRID_W * NA_KH) == 0

    cvec = jnp.concatenate([c, c_ctx[None], jnp.zeros((MOD_ROWS - n_batch - 1, d), F32)], axis=0)
    mod = _modulation(cvec, w_mod, b_mod).reshape(depth, MOD_ROWS, N_MOD, d)
    h_parts = (x.reshape(n_lat, d), ctx.reshape(n_batch * ctx_len, d))
    common = dict(lat_tiles=lat_tiles, n_batch=n_batch)
    dims = dict(n_batch=n_batch, seq=seq, ctx_len=ctx_len)
    gains = norm_g.astype(F32).reshape(depth, 3, 1, d)
    w1 = ffn_w1.astype(BF16)
    w2 = ffn_w2.astype(BF16)

    for layer in range(depth):
        last = layer == depth - 1
        i = layer // 2
        half = functools.partial(_half_layer, mod=mod[layer], gains=gains, w1=w1, w2=w2, layer=layer,
                                 **common)
        if layer % 2 == 0:
            h, pq, pg = half(h_parts, n_rows=n_all, post="ab", post_w=w_in_ab.astype(BF16), post_wi=i,
                             rope=_rope_tables(seq))
            r = _retention(pq, pg, ret_decay_logit[i], **dims)
            weights = _s5_weights(s5_lam_re[i], s5_lam_im[i], s5_log_dt[i], s5_b_re[i], s5_b_im[i],
                                  s5_c_re[i], s5_c_im[i], s5_d[i])
            ys_parts = _s5(pg, RET_HEADS * RET_HEAD_DIM, weights, **dims)
            pre = dict(pre="ab", pre_args=(r, ys_parts, s5_glu_w.astype(BF16),
                                           s5_glu_b.astype(F32)[:, None, :], w_out_ab.astype(BF16), i))
        else:
            assert last
            h, p = half(h_parts, n_rows=n_all, post="na", post_w=na_w_qkv.astype(BF16), post_wi=i)
            att = _natten(p, _na_bias_tiles(na_rpb[i]), **dims)
            pre = dict(pre="na", pre_args=(att, na_w_o.astype(BF16), i))
        (h,) = half((h,), n_rows=n_lat if last else n_all, final_g=final_g if last else None, **pre)
        h_parts = (h,)
    return h[:n_lat].reshape(n_batch, seq, d)
```

```python
import functools
import math
from typing import NamedTuple

import numpy as np
import jax
import jax.numpy as jnp
from jax import lax
from jax.experimental import pallas as pl
from jax.experimental.pallas import tpu as pltpu

F32 = jnp.float32
BF16 = jnp.bfloat16

EPS = 1e-6
ROPE_BASE = 10000.0
GRID_W = 64
N_MOD = 9
RET_HEADS = 4
RET_HEAD_DIM = 128
RET_CHUNK = 256
S5_GROUP = 16
S5_STATE = 64
S5_CHUNK = 16
S5_TILE = 8
NA_HEADS = 16
NA_HEAD_DIM = 64
NA_KH = 8
NA_KW = 16
NEG_INF = -1e30

TOKEN_TILE = 512
VMEM_LIMIT = 56 * 1024 * 1024
MOD_ROWS = 8


def _cparams(*sem):
    return pltpu.CompilerParams(dimension_semantics=sem, vmem_limit_bytes=VMEM_LIMIT)


def _resident(shape):
    nd = len(shape)
    return pl.BlockSpec(shape, lambda *_: (0,) * nd, pipeline_mode=pl.Buffered(1))


def _mod_kernel(c_ref, w_ref, b_ref, o_ref):
    c = c_ref[...]
    s = c * jax.nn.sigmoid(c)
    o_ref[0] = jnp.dot(s, w_ref[0], preferred_element_type=F32,
                       precision=lax.Precision.HIGHEST) + b_ref[0]


def _modulation(cvec, w_mod, b_mod):
    depth, d, nd = w_mod.shape
    tn = nd // 4 if nd % 512 == 0 else d
    return pl.pallas_call(
        _mod_kernel,
        grid=(depth, nd // tn),
        in_specs=[
            pl.BlockSpec((MOD_ROWS, d), lambda l, j: (0, 0)),
            pl.BlockSpec((1, d, tn), lambda l, j: (l, 0, j)),
            pl.BlockSpec((1, 1, tn), lambda l, j: (l, 0, j)),
        ],
        out_specs=pl.BlockSpec((1, MOD_ROWS, tn), lambda l, j: (l, 0, j)),
        out_shape=jax.ShapeDtypeStruct((depth, MOD_ROWS, nd), F32),
        compiler_params=_cparams("parallel", "parallel"),
        name="modulation",
    )(cvec, w_mod, b_mod.reshape(depth, 1, nd))


def _rms(x):
    return x * lax.rsqrt(jnp.mean(x * x, axis=-1, keepdims=True) + EPS)


def _modulated(h, m, g, mi):
    return (_rms(h) * g) * (1.0 + m[mi + 1:mi + 2]) + m[mi:mi + 1]


def _tile_specs(n_lat_tiles_per_batch, n_batch, d):
    def mod_idx(i):
        return (jnp.minimum(i // n_lat_tiles_per_batch, n_batch), 0, 0)
    h_spec = pl.BlockSpec((TOKEN_TILE, d), lambda i: (i, 0))
    m_spec = pl.BlockSpec((1, N_MOD, d), mod_idx)
    return h_spec, m_spec


def _pinned(block_shape, index):
    return pl.BlockSpec(block_shape, lambda *_: index, pipeline_mode=pl.Buffered(1))


def _stream_specs(parts, width, rows=TOKEN_TILE):
    if len(parts) == 1:
        return [pl.BlockSpec((rows, width), lambda i: (i, 0))], 0
    n0 = parts[0].shape[0] // rows
    return [pl.BlockSpec((rows, width), lambda i: (jnp.minimum(i, n0 - 1), 0)),
            pl.BlockSpec((rows, width), lambda i: (jnp.maximum(i - n0, 0), 0))], n0


def _stream_tile(refs, n0):
    if len(refs) == 1:
        return refs[0][...]
    return jnp.where(pl.program_id(0) < n0, refs[0][...], refs[1][...])


class _HalfCfg(NamedTuple):
    n_h: int
    n0: int
    pre: str
    n_y: int
    ny0: int
    post: str
    final: bool


def _gelu_tanh(y):
    return 0.5 * y * (1.0 + jnp.tanh(math.sqrt(2.0 / math.pi) * (y + 0.044715 * (y * y * y))))


S5_CHUNKS_PER_TILE = TOKEN_TILE // S5_CHUNK


def _to_chunk_rows(u_scr, out_ref):
    half = 128 // S5_GROUP
    gw = S5_CHUNK * S5_GROUP
    for blk in range(u_scr.shape[0]):
        for th in range(S5_CHUNK // half):
            arrs = [u_scr[blk, pl.ds(th * half + tl, S5_CHUNKS_PER_TILE, stride=S5_CHUNK), :]
                    for tl in range(half)]
            for g, x in enumerate(_block_transpose(arrs)):
                lo = (blk * half + g) * gw + th * 128
                out_ref[:, lo:lo + 128] = x


def _from_chunk_rows(y, y_scr):
    half = 128 // S5_GROUP
    gw = S5_CHUNK * S5_GROUP
    for blk in range(y_scr.shape[0]):
        for th in range(S5_CHUNK // half):
            arrs = [y[:, (blk * half + g) * gw + th * 128:(blk * half + g) * gw + (th + 1) * 128]
                    for g in range(half)]
            for tl, x in enumerate(_block_transpose(arrs)):
                y_scr[blk, pl.ds(th * half + tl, S5_CHUNKS_PER_TILE, stride=S5_CHUNK), :] = x


def _half_kernel(*refs, cfg):
    it = iter(refs)
    take = lambda n: [next(it) for _ in range(n)]
    h_refs = take(cfg.n_h)
    m_ref, g_ref = take(2)
    h = _stream_tile(h_refs, cfg.n0)
    m = m_ref[0]
    gate_mix = m[5:6]
    relayout_scr = refs[-1]
    if cfg.pre == "ab":
        (r_ref,) = take(1)
        y_refs = take(cfg.n_y)
        gw_ref, gb_ref, wr_ref, ws_ref = take(4)
        _from_chunk_rows(_stream_tile(y_refs, cfg.ny0), relayout_scr)
        g = _gelu_tanh(jnp.concatenate([relayout_scr[j] for j in range(relayout_scr.shape[0])], axis=1))
        s = g * jax.nn.sigmoid(jnp.dot(g.astype(BF16), gw_ref[...], preferred_element_type=F32)
                               + gb_ref[...])
        mix = jnp.dot(r_ref[...].astype(BF16), wr_ref[...], preferred_element_type=F32)
        mix += jnp.dot(s.astype(BF16), ws_ref[...], preferred_element_type=F32)
        h = h + gate_mix * mix
    elif cfg.pre == "na":
        a_ref, wo_ref = take(2)
        h = h + gate_mix * jnp.dot(a_ref[...], wo_ref[...], preferred_element_type=F32)
    w1a_ref, w1b_ref, w2_ref = take(3)
    k = 1 if cfg.pre else 0
    mi = 6 * k
    xm = _modulated(h, m, g_ref[2 * k], mi).astype(BF16)
    a = jnp.dot(xm, w1a_ref[...], preferred_element_type=F32)
    b = jnp.dot(xm, w1b_ref[...], preferred_element_type=F32)
    hid = (a * jax.nn.sigmoid(a) * b).astype(BF16)
    h = h + (0.5 * m[mi + 2:mi + 3]) * jnp.dot(hid, w2_ref[...], preferred_element_type=F32)
    if cfg.post:
        (wp_ref,) = take(1)
    if cfg.post == "ab":
        cos_ref, sa_ref, sb_ref = take(3)
    if cfg.final:
        (fg_ref,) = take(1)
    outs = list(it)
    if "ab" in (cfg.pre, cfg.post):
        outs.pop()
    outs[0][...] = _rms(h) * fg_ref[...] if cfg.final else h
    if not cfg.post:
        return
    xm = _modulated(h, m, g_ref[1], 3).astype(BF16)
    pr = jnp.dot(xm, wp_ref[...], preferred_element_type=F32)
    if cfg.post == "na":
        outs[1][...] = pr.astype(BF16)
        return
    width = RET_HEADS * RET_HEAD_DIM
    cos, sa, sb = cos_ref[...], sa_ref[...], sb_ref[...]
    for hh in range(RET_HEADS):
        lo = hh * RET_HEAD_DIM
        hi = lo + RET_HEAD_DIM
        outs[1][:, lo:hi] = _rope(pr[:, lo:hi], cos, sa, sb).astype(BF16)
        kr = _rope(pr[:, width + lo:width + hi], cos, sa, sb) * (RET_HEAD_DIM ** -0.5)
        outs[1][:, width + lo:width + hi] = kr.astype(BF16)
    outs[1][:, 2 * width:] = pr[:, 2 * width:3 * width].astype(BF16)
    outs[2][...] = pr[:, 3 * width:4 * width]
    for j in range(relayout_scr.shape[0]):
        relayout_scr[j] = pr[:, 4 * width + j * 128:4 * width + (j + 1) * 128]
    _to_chunk_rows(relayout_scr, outs[3])


def _half_layer(h_parts, mod, gains, w1, w2, *, layer, n_rows, lat_tiles, n_batch,
                pre="", pre_args=(), post="", post_w=None, post_wi=0, rope=None, final_g=None):
    d = h_parts[0].shape[1]
    f = w2.shape[2]
    k = 1 if pre else 0
    h_specs, n0 = _stream_specs(h_parts, d)
    _, m_spec = _tile_specs(lat_tiles, n_batch, d)
    in_specs = h_specs + [m_spec, _pinned((None, 3, 1, d), (layer, 0, 0, 0))]
    args = list(h_parts) + [mod, gains]
    n_y = ny0 = 0
    scratch = []
    if pre == "ab":
        r, ys_parts, glu_w, glu_b, w_out, wi = pre_args
        w = r.shape[1]
        y_specs, ny0 = _stream_specs(ys_parts, S5_CHUNK * w, rows=S5_CHUNKS_PER_TILE)
        n_y = len(ys_parts)
        scratch = [pltpu.VMEM((w // 128, TOKEN_TILE, 128), F32)]
        in_specs += [pl.BlockSpec((TOKEN_TILE, w), lambda i: (i, 0))] + y_specs + [
            _pinned((None, w, w), (wi, 0, 0)), _pinned((None, 1, w), (wi, 0, 0)),
            _pinned((None, w, d), (wi, 0, 0)), _pinned((None, w, d), (wi, 1, 0))]
        args += [r, *ys_parts, glu_w, glu_b, w_out, w_out]
    elif pre == "na":
        att, w_o, wi = pre_args
        in_specs += [pl.BlockSpec((TOKEN_TILE, d), lambda i: (i, 0)), _pinned((None, d, d), (wi, 0, 0))]
        args += [att, w_o]
    in_specs += [_pinned((None, None, d, f), (layer, k, 0, 0)),
                 _pinned((None, None, d, f), (layer, k, 0, 1)),
                 _pinned((None, None, f, d), (layer, k, 0, 0))]
    args += [w1, w1, w2]
    out_specs = [pl.BlockSpec((TOKEN_TILE, d), lambda i: (i, 0))]
    out_shape = [jax.ShapeDtypeStruct((n_rows, d), F32)]
    if post:
        n = post_w.shape[2]
        in_specs.append(_pinned((None, d, n), (post_wi, 0, 0)))
        args.append(post_w)
    if post == "ab":
        n_lat_tiles = lat_tiles * n_batch
        tab = pl.BlockSpec((TOKEN_TILE, RET_HEAD_DIM),
                           lambda i: (jnp.where(i < n_lat_tiles, i % lat_tiles, lat_tiles), 0))
        in_specs += [tab, tab, tab]
        args += list(rope)
        width = RET_HEADS * RET_HEAD_DIM
        n_u = n - 4 * width
        out_specs += [pl.BlockSpec((TOKEN_TILE, 3 * width), lambda i: (i, 0)),
                      pl.BlockSpec((TOKEN_TILE, width), lambda i: (i, 0)),
                      pl.BlockSpec((S5_CHUNKS_PER_TILE, S5_CHUNK * n_u), lambda i: (i, 0))]
        out_shape += [jax.ShapeDtypeStruct((n_rows, 3 * width), BF16),
                      jax.ShapeDtypeStruct((n_rows, width), F32),
                      jax.ShapeDtypeStruct((n_rows // S5_CHUNK, S5_CHUNK * n_u), F32)]
        scratch = [pltpu.VMEM((n_u // 128, TOKEN_TILE, 128), F32)]
    elif post == "na":
        out_specs.append(pl.BlockSpec((TOKEN_TILE, n), lambda i: (i, 0)))
        out_shape.append(jax.ShapeDtypeStruct((n_rows, n), BF16))
    if final_g is not None:
        in_specs.append(_resident((1, d)))
        args.append(final_g.reshape(1, d))
    cfg = _HalfCfg(n_h=len(h_parts), n0=n0, pre=pre, n_y=n_y, ny0=ny0, post=post,
                   final=final_g is not None)
    return pl.pallas_call(
        functools.partial(_half_kernel, cfg=cfg),
        grid=(n_rows // TOKEN_TILE,),
        in_specs=in_specs,
        out_specs=out_specs,
        out_shape=out_shape,
        scratch_shapes=scratch,
        compiler_params=_cparams("parallel"),
        name="half_layer",
    )(*args)


def _rope_tables(seq):
    half = RET_HEAD_DIM // 2
    quarter = half // 2
    inv = ROPE_BASE ** (-np.arange(0, half, 2, dtype=np.float64) / half)
    t = np.arange(seq)
    lane = np.arange(RET_HEAD_DIM)
    pos = np.where(lane[None, :] < half, (t // GRID_W)[:, None], (t % GRID_W)[:, None])
    ang = pos.astype(np.float32).astype(np.float64) * inv.astype(np.float32)[lane % quarter][None, :]
    first = (lane % half) < quarter
    cos = np.cos(ang)
    sin = np.sin(ang)
    sa = np.where(first[None, :], -sin, 0.0)
    sb = np.where(first[None, :], 0.0, sin)
    pad1 = np.ones((TOKEN_TILE, RET_HEAD_DIM))
    pad0 = np.zeros((TOKEN_TILE, RET_HEAD_DIM))
    tabs = [np.concatenate([cos, pad1]), np.concatenate([sa, pad0]), np.concatenate([sb, pad0])]
    return [jnp.asarray(x, F32) for x in tabs]


def _rope(x, cos, sa, sb):
    quarter = RET_HEAD_DIM // 4
    up = pltpu.roll(x, RET_HEAD_DIM - quarter, axis=1)
    dn = pltpu.roll(x, quarter, axis=1)
    return x * cos + up * sa + dn * sb


def _retention_kernel(dec_ref, q_ref, k_ref, v_ref, g_ref,
                      dmask_ref, qwf_ref, qwb_ref, kwf_ref, kwb_ref, o_ref,
                      s_run, s_bwd, *, n_chunks, n_ctx_chunks):
    ph = pl.program_id(1)
    i = pl.program_id(2)
    hd = RET_HEAD_DIM

    @pl.when(i == 0)
    def _():
        s_run[...] = jnp.zeros_like(s_run)

    tn = (((0,), (0,)), ((), ()))
    nt = (((1,), (1,)), ((), ()))

    @pl.when(ph == 0)
    def _():
        cid = jnp.where(i < n_ctx_chunks, n_ctx_chunks - 1 - i, n_chunks + n_ctx_chunks - 1 - i)
        for h in range(RET_HEADS):
            sl = slice(h * hd, (h + 1) * hd)
            s_old = s_run[h]
            s_bwd[cid, h] = s_old
            kw = (k_ref[:, sl] * kwb_ref[h]).astype(BF16)
            kv = lax.dot_general(kw, v_ref[:, sl], tn, preferred_element_type=F32)
            s_run[h] = dec_ref[1, h] * s_old + kv

    @pl.when(ph == 1)
    def _():
        for h in range(RET_HEADS):
            sl = slice(h * hd, (h + 1) * hd)
            q = q_ref[:, sl]
            k = k_ref[:, sl]
            v = v_ref[:, sl]
            s_old = s_run[h]
            a = lax.dot_general(q, k, nt, preferred_element_type=F32)
            o = jnp.dot((a * dmask_ref[h]).astype(BF16), v, preferred_element_type=F32)
            o += qwf_ref[h] * jnp.dot(q, s_old.astype(BF16), preferred_element_type=F32)
            o += qwb_ref[h] * jnp.dot(q, s_bwd[i, h].astype(BF16), preferred_element_type=F32)
            kv = lax.dot_general((k * kwf_ref[h]).astype(BF16), v, tn, preferred_element_type=F32)
            s_run[h] = dec_ref[0, h] * s_old + kv
            o = o * lax.rsqrt(jnp.mean(o * o, axis=-1, keepdims=True) + EPS)
            g = g_ref[:, sl]
            o_ref[:, sl] = o * (g * jax.nn.sigmoid(g))


def _retention(pq, pg, decay_logit, *, n_batch, seq, ctx_len):
    c = RET_CHUNK
    width = RET_HEADS * RET_HEAD_DIM
    n_lat = seq // c
    n_ctx = ctx_len // c
    n_ch = n_lat + n_ctx
    n_rows = pq.shape[0]
    lat_blocks = n_batch * n_lat

    log_gamma = jax.nn.log_sigmoid(decay_logit.astype(F32))
    pos = jnp.arange(c, dtype=F32)
    diff = pos[:, None] - pos[None, :]
    lf = log_gamma[0][:, None, None]
    lb = log_gamma[1][:, None, None]
    dmask = (jnp.where(diff >= 0, jnp.exp(lf * jnp.maximum(diff, 0.0)), 0.0)
             + jnp.where(diff <= 0, jnp.exp(lb * jnp.maximum(-diff, 0.0)), 0.0))
    ones = jnp.ones((1, 1, RET_HEAD_DIM), F32)
    col = pos[None, :, None]
    qwf = jnp.exp(lf * (col + 1.0)) * ones
    qwb = jnp.exp(lb * (c - col)) * ones
    kwf = jnp.exp(lf * (c - 1.0 - col)) * ones
    kwb = jnp.exp(lb * col) * ones
    chunk_decay = jnp.exp(log_gamma * c)

    def chunk_id(ph, i):
        bwd = jnp.where(i < n_ctx, n_ctx - 1 - i, n_ch + n_ctx - 1 - i)
        return jnp.where(ph == 0, bwd, i)

    def row_block(b, cid):
        return jnp.where(cid < n_ctx, lat_blocks + b * n_ctx + cid, b * n_lat + cid - n_ctx)

    def col_spec(j, forward_only):
        def idx(b, ph, i):
            cid = chunk_id(ph, i)
            if forward_only:
                cid = jnp.where(ph == 0, 0, cid)
            return (row_block(b, cid), j)
        return pl.BlockSpec((c, width), idx)

    head_tab = _resident((RET_HEADS, c, RET_HEAD_DIM))
    kern = functools.partial(_retention_kernel, n_chunks=n_ch, n_ctx_chunks=n_ctx)
    return pl.pallas_call(
        kern,
        grid=(n_batch, 2, n_ch),
        in_specs=[
            pl.BlockSpec(memory_space=pltpu.SMEM),
            col_spec(0, True), col_spec(1, False), col_spec(2, False), col_spec(0, True),
            _resident((RET_HEADS, c, c)), head_tab, head_tab, head_tab, head_tab,
        ],
        out_specs=col_spec(0, True),
        out_shape=jax.ShapeDtypeStruct((n_rows, width), F32),
        scratch_shapes=[
            pltpu.VMEM((RET_HEADS, RET_HEAD_DIM, RET_HEAD_DIM), F32),
            pltpu.VMEM((n_ch, RET_HEADS, RET_HEAD_DIM, RET_HEAD_DIM), F32),
        ],
        compiler_params=_cparams("parallel", "arbitrary", "arbitrary"),
        name="retention",
    )(chunk_decay, pq, pq, pq, pg, dmask, qwf, qwb, kwf, kwb)


def _cmul(a, b):
    return a[0] * b[0] - a[1] * b[1], a[0] * b[1] + a[1] * b[0]


def _pair_blockdiag(a):
    a0, a1 = a[0::2], a[1::2]
    z = jnp.zeros_like(a0)
    return jnp.concatenate([jnp.concatenate([a0, z], axis=2), jnp.concatenate([z, a1], axis=2)],
                           axis=1)


def _pair_lanes(a):
    return jnp.concatenate([a[0::2], a[1::2]], axis=2)


def _powers(a, n):
    one = (jnp.ones_like(a[0]), jnp.zeros_like(a[0]))
    pw = (jnp.stack([one[0], a[0]], axis=1), jnp.stack([one[1], a[1]], axis=1))
    step = a
    while pw[0].shape[1] < n:
        step = _cmul(step, step)
        nxt = _cmul(pw, (step[0][:, None], step[1][:, None]))
        pw = (jnp.concatenate([pw[0], nxt[0]], axis=1), jnp.concatenate([pw[1], nxt[1]], axis=1))
    top = _cmul(step, step)
    return (jnp.concatenate([pw[0], top[0][:, None]], axis=1),
            jnp.concatenate([pw[1], top[1][:, None]], axis=1))


def _s5_weights(lam_re, lam_im, log_dt, b_re, b_im, c_re, c_im, d_skip):
    q = S5_CHUNK
    n_g = lam_re.shape[1]
    hi = lax.Precision.HIGHEST
    kall = None
    zw = []
    rw = []
    tabs = []
    for d in range(2):
        lr = jnp.minimum(lam_re[d].astype(F32), -1e-4)
        li = lam_im[d].astype(F32)
        dt = jnp.exp(log_dt[d].astype(F32))[:, None]
        mag = jnp.exp(lr * dt)
        a = (mag * jnp.cos(li * dt), mag * jnp.sin(li * dt))
        den = lr * lr + li * li
        am1 = (a[0] - 1.0, a[1])
        coef = ((am1[0] * lr + am1[1] * li) / den, (am1[1] * lr - am1[0] * li) / den)
        bt = (b_re[d].astype(F32).transpose(0, 2, 1), b_im[d].astype(F32).transpose(0, 2, 1))
        bb = _cmul((coef[0][:, None], coef[1][:, None]), bt)
        cm = (c_re[d].astype(F32), c_im[d].astype(F32))
        pw_re, pw_im = _powers(a, q)
        e = _cmul((pw_re[:, :q, None, :], pw_im[:, :q, None, :]),
                  (bb[0][:, None], bb[1][:, None]))
        taps = (jnp.einsum('gkp,gtjp->gjtk', cm[0], e[0], precision=hi)
                - jnp.einsum('gkp,gtjp->gjtk', cm[1], e[1], precision=hi))
        if d == 0:
            kall = jnp.concatenate([jnp.zeros_like(taps[:, :, 1:]), taps], axis=2)
        else:
            kall = kall + jnp.concatenate([taps[:, :, ::-1], jnp.zeros_like(taps[:, :, 1:])], axis=2)
        tsel = np.arange(q)[::-1] if d == 0 else np.arange(q)
        z = _cmul((pw_re[:, tsel][:, :, None, :], pw_im[:, tsel][:, :, None, :]),
                  (bb[0][:, None], bb[1][:, None]))
        zw.append((z[0].reshape(n_g, q * S5_GROUP, S5_STATE),
                   z[1].reshape(n_g, q * S5_GROUP, S5_STATE)))
        rsel = np.arange(1, q + 1) if d == 0 else np.arange(q, 0, -1)
        ct = (cm[0].transpose(0, 2, 1)[:, :, None, :], cm[1].transpose(0, 2, 1)[:, :, None, :])
        pt = (pw_re[:, rsel].transpose(0, 2, 1)[..., None], pw_im[:, rsel].transpose(0, 2, 1)[..., None])
        r = _cmul(ct, pt)
        rw.append((r[0].reshape(n_g, S5_STATE, q * S5_GROUP),
                   (-r[1]).reshape(n_g, S5_STATE, q * S5_GROUP)))
        a16 = (pw_re[:, q], pw_im[:, q])
        ramp_re, ramp_im = _powers(a16, S5_TILE)
        a32 = (ramp_re[:, 2], ramp_im[:, 2])
        a64 = (ramp_re[:, 4], ramp_im[:, 4])
        a128 = (ramp_re[:, 8], ramp_im[:, 8])
        rsl = slice(0, S5_TILE) if d == 0 else slice(S5_TILE - 1, None, -1)
        rows = [a16[0], a16[1], a32[0], a32[1], a64[0], a64[1], a128[0], a128[1]]
        tabs.append(jnp.concatenate([jnp.stack(rows, axis=1), ramp_re[:, rsl], ramp_im[:, rsl]],
                                    axis=1))
    dsk = d_skip.astype(F32).reshape(n_g, S5_GROUP)
    eye = jnp.eye(S5_GROUP, dtype=F32)
    kall = kall.at[:, :, q - 1].add(dsk[:, :, None] * eye[None])
    kall = jnp.concatenate([kall, jnp.zeros_like(kall[:, :, :1])], axis=2)
    kflat = kall.reshape(n_g, S5_GROUP, 2 * q * S5_GROUP)
    wz = jnp.concatenate([_pair_blockdiag(z) for pair in zw for z in pair], axis=2)
    w2 = jnp.concatenate([_pair_blockdiag(r) for pair in rw for r in pair], axis=1)
    tab = jnp.concatenate([_pair_lanes(t) for t in tabs], axis=1)
    return kflat, wz.astype(BF16), w2.astype(BF16), tab


def _tile_shift(x, s, down):
    row = lax.broadcasted_iota(jnp.int32, x.shape, 1)
    if down:
        return jnp.where(row >= s, pltpu.roll(x, s, axis=1), 0.0)
    return jnp.where(row < S5_TILE - s, pltpu.roll(x, S5_TILE - s, axis=1), 0.0)


S5_LANE_GROUPS = 128 // S5_GROUP


def _block_transpose(arrs):
    lane = lax.broadcasted_iota(jnp.int32, arrs[0].shape, 1)
    blk = lane // S5_GROUP
    a = list(arrs)
    for bit in range(3):
        s = 1 << bit
        hi = (blk & s) != 0
        new = list(a)
        for i in range(S5_LANE_GROUPS):
            if i & s:
                continue
            new[i] = jnp.where(hi, pltpu.roll(a[i + s], S5_GROUP * s, axis=1), a[i])
            new[i + s] = jnp.where(hi, a[i + s], pltpu.roll(a[i], 128 - S5_GROUP * s, axis=1))
        a = new
    return a


def _s5_kernel(vl_ref, vc_ref, kf_ref, wz_ref, w2_ref, tab_ref, yl_ref, yc_ref, m_scr, zx, xin,
               *, lat_tiles, ctx_tiles):
    q = S5_CHUNK
    lanes = 2 * S5_STATE
    pw = 2 * q * S5_GROUP
    gw = q * S5_GROUP

    @pl.when(pl.program_id(1) == 0)
    def _():
        m_scr[...] = jnp.zeros_like(m_scr)
        for g in range(S5_LANE_GROUPS):
            taps = kf_ref[g]
            base = (g % 2) * gw
            for ti in range(q):
                lo = S5_GROUP * (q - 1 - ti)
                m_scr[g // 2, base + ti * S5_GROUP:base + (ti + 1) * S5_GROUP, base:base + gw] = (
                    taps[:, lo:lo + gw].astype(BF16))

    n_tiles = ctx_tiles + lat_tiles
    ctx_rows = ctx_tiles * S5_TILE

    for p in range(m_scr.shape[0]):
        cols = slice(p * pw, (p + 1) * pw)
        v = jnp.concatenate([vc_ref[:, cols], vl_ref[:, cols]], axis=0).astype(BF16)
        y_intra = jnp.dot(v, m_scr[p], preferred_element_type=F32)
        y = jnp.dot(v, wz_ref[p], preferred_element_type=F32)

        def tab_row(r):
            return tab_ref[p, r:r + 1, :]

        for d in range(2):
            down = d == 0
            t0 = 24 * d
            lo = 2 * d * lanes
            z = (y[:, lo:lo + lanes].reshape(n_tiles, S5_TILE, lanes),
                 y[:, lo + lanes:lo + 2 * lanes].reshape(n_tiles, S5_TILE, lanes))
            loc = (_tile_shift(z[0], 1, down), _tile_shift(z[1], 1, down))
            for k, s in enumerate((1, 2, 4)):
                mul = (tab_row(t0 + 2 * k)[None], tab_row(t0 + 2 * k + 1)[None])
                inc = _cmul(mul, (_tile_shift(loc[0], s, down), _tile_shift(loc[1], s, down)))
                loc = (loc[0] + inc[0], loc[1] + inc[1])
            zx[2 * d] = loc[0]
            zx[2 * d + 1] = loc[1]
            zx[4 + 2 * d] = z[0]
            zx[4 + 2 * d + 1] = z[1]

        for d in range(2):
            t0 = 24 * d
            edge = S5_TILE - 1 if d == 0 else 0
            a16 = (tab_row(t0), tab_row(t0 + 1))
            a128 = (tab_row(t0 + 6), tab_row(t0 + 7))
            ramp = (tab_ref[p, t0 + 8:t0 + 16, :], tab_ref[p, t0 + 16:t0 + 24, :])

            def tile_step(j, carry, d=d, edge=edge, a16=a16, a128=a128, ramp=ramp):
                loc = (zx[2 * d, j], zx[2 * d + 1, j])
                z = (zx[4 + 2 * d, j], zx[4 + 2 * d + 1, j])
                inc = _cmul(ramp, carry)
                xin[j, :, (2 * d) * lanes:(2 * d + 1) * lanes] = loc[0] + inc[0]
                xin[j, :, (2 * d + 1) * lanes:(2 * d + 2) * lanes] = loc[1] + inc[1]
                e_loc = _cmul(a16, (loc[0][edge:edge + 1], loc[1][edge:edge + 1]))
                nxt = _cmul(a128, carry)
                return (nxt[0] + e_loc[0] + z[0][edge:edge + 1],
                        nxt[1] + e_loc[1] + z[1][edge:edge + 1])

            zero = (jnp.zeros((1, lanes), F32), jnp.zeros((1, lanes), F32))
            if d == 0:
                lax.fori_loop(0, n_tiles, tile_step, zero)
            else:
                mid = lax.fori_loop(0, ctx_tiles, lambda t, c: tile_step(ctx_tiles - 1 - t, c), zero)
                lax.fori_loop(0, lat_tiles, lambda t, c: tile_step(n_tiles - 1 - t, c), mid)

        x = xin[...].reshape(n_tiles * S5_TILE, 4 * lanes)
        y_all = y_intra + jnp.dot(x.astype(BF16), w2_ref[p], preferred_element_type=F32)
        yc_ref[:, cols] = y_all[:ctx_rows]
        yl_ref[:, cols] = y_all[ctx_rows:]


def _s5(v, weights, *, n_batch, seq, ctx_len):
    kflat, wz, w2, tab = weights
    q = S5_CHUNK
    lanes = 2 * S5_STATE
    pw = 2 * q * S5_GROUP
    n_pairs = w2.shape[0]
    ppb = S5_LANE_GROUPS // 2
    n_blocks = n_pairs // ppb
    bw = ppb * pw
    lat_rows = seq // q
    ctx_rows = ctx_len // q
    n_tiles = (lat_rows + ctx_rows) // S5_TILE
    ctx_blk0 = n_batch * lat_rows // ctx_rows
    kern = functools.partial(_s5_kernel, lat_tiles=lat_rows // S5_TILE, ctx_tiles=ctx_rows // S5_TILE)
    return pl.pallas_call(
        kern,
        grid=(n_blocks, n_batch),
        in_specs=[
            pl.BlockSpec((lat_rows, bw), lambda j, b: (b, j)),
            pl.BlockSpec((ctx_rows, bw), lambda j, b: (ctx_blk0 + b, j)),
            pl.BlockSpec((S5_LANE_GROUPS,) + kflat.shape[1:], lambda j, b: (j, 0, 0)),
            pl.BlockSpec((ppb, pw, 4 * lanes), lambda j, b: (j, 0, 0)),
            pl.BlockSpec((ppb, 4 * lanes, pw), lambda j, b: (j, 0, 0)),
            pl.BlockSpec((ppb, tab.shape[1], lanes), lambda j, b: (j, 0, 0)),
        ],
        out_specs=[pl.BlockSpec((lat_rows, bw), lambda j, b: (b, j)),
                   pl.BlockSpec((ctx_rows, bw), lambda j, b: (b, j))],
        out_shape=[jax.ShapeDtypeStruct((n_batch * lat_rows, n_blocks * bw), F32),
                   jax.ShapeDtypeStruct((n_batch * ctx_rows, n_blocks * bw), F32)],
        scratch_shapes=[
            pltpu.VMEM((ppb, pw, pw), BF16),
            pltpu.VMEM((8, n_tiles, S5_TILE, lanes), F32),
            pltpu.VMEM((n_tiles, S5_TILE, 4 * lanes), F32),
        ],
        compiler_params=_cparams("arbitrary", "arbitrary"),
        name="s5",
    )(v, v, kflat, wz, w2, tab)


NA_QROWS = 4
NA_KROWS = NA_QROWS + NA_KH


NA_REL_ROWS = 2 * NA_KH - 1


def _na_tile_index(kind, a, m):
    first, rel0 = ((0, NA_KH - 1 - a), (a, NA_QROWS - 1 - a), (NA_QROWS, -1 - a))[kind]
    return rel0 + m if first <= m < first + NA_KH else NA_REL_ROWS


def _na_bias_tiles(rpb):
    w = GRID_W
    qcol = np.arange(w)
    kcol = np.arange(w)
    wstart = np.clip(qcol - NA_KW // 2, 0, w - NA_KW)
    valid = (kcol[None, :] >= wstart[:, None]) & (kcol[None, :] < wstart[:, None] + NA_KW)
    rel = np.clip(kcol[None, :] - qcol[:, None], -(NA_KW - 1), NA_KW - 1) + NA_KW - 1
    onehot = (rel[None] == np.arange(2 * NA_KW - 1)[:, None, None]).astype(np.float32)
    tiles = jnp.einsum('hrj,jqk->hrqk', rpb.astype(F32), jnp.asarray(onehot),
                       precision=lax.Precision.HIGHEST)
    tiles = jnp.where(jnp.asarray(valid)[None, None], tiles, NEG_INF)
    return jnp.concatenate([tiles, jnp.full((NA_HEADS, 1, w, w), NEG_INF, F32)], axis=1)


def _na_kernel(q_ref, k_ref, v_ref, kc_ref, vc_ref, tiles_ref, o_ref, bias_ref, vx, vcx, s_even, s_odd,
               p_even, p_odd, *, rows):
    w = GRID_W
    dh = NA_HEAD_DIM
    lb = 2 * dh
    nq = NA_QROWS * w
    nk = NA_KROWS * w
    n_blocks = rows // NA_QROWS

    @pl.when(pl.program_id(1) == 0)
    def _():
        for kind in range(3):
            for hh in range(2):
                for a in range(NA_QROWS):
                    for m in range(0, NA_KROWS, 2):
                        pair = [tiles_ref[hh, _na_tile_index(kind, a, m + e)] for e in range(2)]
                        bias_ref[kind, hh * nq + a * w:hh * nq + (a + 1) * w, m * w:(m + 2) * w] = (
                            jnp.concatenate(pair, axis=1))

    nt = (((1,), (1,)), ((), ()))
    scale = dh ** -0.5
    kc = kc_ref[...]
    lane = lax.broadcasted_iota(jnp.int32, (nq, lb), 1)
    first = lane < dh

    for dst, src in ((vx, v_ref), (vcx, vc_ref)):
        n = src.shape[0]
        dst[:, :lb] = src[...]
        dst[:, lb:] = (lax.broadcasted_iota(jnp.int32, (n, lb), 1) == 0).astype(BF16)

    def key_offset(i):
        r0 = jnp.clip(NA_QROWS * i - NA_KH // 2, 0, rows - NA_KROWS)
        return pl.multiple_of(r0 * w, NA_QROWS * w)

    def scores(i, s_ref):
        kind = jnp.where(i == 0, 0, jnp.where(i == n_blocks - 1, 2, 1))
        q = q_ref[pl.ds(pl.multiple_of(i * nq, nq), nq), :] * scale
        zero = jnp.zeros_like(q)
        qs = jnp.concatenate([jnp.where(first, q, zero), jnp.where(first, zero, q)], axis=0)
        kl = k_ref[pl.ds(key_offset(i), nk), :]
        s_ref[:, :nk] = lax.dot_general(qs, kl, nt, preferred_element_type=F32) + bias_ref[kind]
        s_ref[:, nk:] = lax.dot_general(qs, kc, nt, preferred_element_type=F32)

    def softmax(s_ref, p_ref):
        s = s_ref[...]
        p_ref[...] = jnp.exp(s - jnp.max(s, axis=-1, keepdims=True)).astype(BF16)

    def attend(i, p_ref):
        o = jnp.dot(p_ref[:, :nk], vx[pl.ds(key_offset(i), nk), :], preferred_element_type=F32)
        o += jnp.dot(p_ref[:, nk:], vcx[...], preferred_element_type=F32)
        o = o[:, :lb] / o[:, lb:lb + 1]
        o_ref[pl.ds(pl.multiple_of(i * nq, nq), nq), :] = jnp.where(first, o[:nq], o[nq:]).astype(
            o_ref.dtype)

    scores(0, s_even)
    softmax(s_even, p_even)
    scores(1, s_odd)

    def body(j, carry):
        attend(2 * j - 2, p_even)
        softmax(s_odd, p_odd)
        scores(2 * j, s_even)
        attend(2 * j - 1, p_odd)
        softmax(s_even, p_even)
        scores(2 * j + 1, s_odd)
        return carry

    lax.fori_loop(1, n_blocks // 2, body, 0, unroll=True)
    attend(n_blocks - 2, p_even)
    softmax(s_odd, p_odd)
    attend(n_blocks - 1, p_odd)


def _natten(p, tiles, *, n_batch, seq, ctx_len):
    d = NA_HEADS * NA_HEAD_DIM
    lb = 2 * NA_HEAD_DIM
    n_pairs = NA_HEADS // 2
    ctx_blk0 = n_batch * seq // ctx_len
    rows = seq // GRID_W
    assert rows % (2 * NA_QROWS) == 0 and rows >= NA_KROWS + NA_QROWS
    stacked = 2 * NA_QROWS * GRID_W
    n_keys = NA_KROWS * GRID_W + ctx_len
    lat = lambda part: pl.BlockSpec((seq, lb), lambda j, b: (b, part * n_pairs + j))
    ctx = lambda part: pl.BlockSpec((ctx_len, lb), lambda j, b: (ctx_blk0 + b, part * n_pairs + j))
    return pl.pallas_call(
        functools.partial(_na_kernel, rows=rows),
        grid=(n_pairs, n_batch),
        in_specs=[lat(0), lat(1), lat(2), ctx(1), ctx(2),
                  pl.BlockSpec((2,) + tiles.shape[1:], lambda j, b: (j, 0, 0, 0))],
        out_specs=pl.BlockSpec((seq, lb), lambda j, b: (b, j)),
        out_shape=jax.ShapeDtypeStruct((n_batch * seq, d), BF16),
        scratch_shapes=[pltpu.VMEM((3, stacked, NA_KROWS * GRID_W), F32),
                        pltpu.VMEM((seq, 2 * lb), BF16), pltpu.VMEM((ctx_len, 2 * lb), BF16),
                        pltpu.VMEM((stacked, n_keys), F32), pltpu.VMEM((stacked, n_keys), F32),
                        pltpu.VMEM((stacked, n_keys), BF16), pltpu.VMEM((stacked, n_keys), BF16)],
        compiler_params=_cparams("arbitrary", "arbitrary"),
        name="natten",
    )(p, p, p, p, p, tiles)


def kernel(x, c, ctx, c_ctx, w_mod, b_mod, norm_g, ffn_w1, ffn_w2, w_in_ab, w_out_ab, ret_decay_logit, s5_lam_re, s5_lam_im, s5_log_dt, s5_b_re, s5_b_im, s5_c_re, s5_c_im, s5_d, s5_glu_w, s5_glu_b, na_w_qkv, na_w_o, na_rpb, final_g):
    n_batch, seq, d = x.shape
    ctx_len = ctx.shape[1]
    depth = w_mod.shape[0]
    n_lat = n_batch * seq
    n_all = n_lat + n_batch * ctx_len
    lat_tiles = seq // TOKEN_TILE
    assert seq % TOKEN_TILE == 0 and (n_batch * ctx_len) % TOKEN_TILE == 0
    assert n_batch + 1 <= MOD_ROWS and seq % (GRID_W * NA_KH) == 0

    cvec = jnp.concatenate([c, c_ctx[None], jnp.zeros((MOD_ROWS - n_batch - 1, d), F32)], axis=0)
    mod = _modulation(cvec, w_mod, b_mod).reshape(depth, MOD_ROWS, N_MOD, d)
    h_parts = (x.reshape(n_lat, d), ctx.reshape(n_batch * ctx_len, d))
    common = dict(lat_tiles=lat_tiles, n_batch=n_batch)
    dims = dict(n_batch=n_batch, seq=seq, ctx_len=ctx_len)
    gains = norm_g.astype(F32).reshape(depth, 3, 1, d)
    w1 = ffn_w1.astype(BF16)
    w2 = ffn_w2.astype(BF16)

    for layer in range(depth):
        last = layer == depth - 1
        i = layer // 2
        half = functools.partial(_half_layer, mod=mod[layer], gains=gains, w1=w1, w2=w2, layer=layer,
                                 **common)
        if layer % 2 == 0:
            h, pq, pg, pv = half(h_parts, n_rows=n_all, post="ab", post_w=w_in_ab.astype(BF16),
                                 post_wi=i, rope=_rope_tables(seq))
            r = _retention(pq, pg, ret_decay_logit[i], **dims)
            weights = _s5_weights(s5_lam_re[i], s5_lam_im[i], s5_log_dt[i], s5_b_re[i], s5_b_im[i],
                                  s5_c_re[i], s5_c_im[i], s5_d[i])
            ys_parts = _s5(pv, weights, **dims)
            pre = dict(pre="ab", pre_args=(r, ys_parts, s5_glu_w.astype(BF16),
                                           s5_glu_b.astype(F32)[:, None, :], w_out_ab.astype(BF16), i))
        else:
            assert last
            h, p = half(h_parts, n_rows=n_all, post="na", post_w=na_w_qkv.astype(BF16), post_wi=i)
            att = _natten(p, _na_bias_tiles(na_rpb[i]), **dims)
            pre = dict(pre="na", pre_args=(att, na_w_o.astype(BF16), i))
        (h,) = half((h,), n_rows=n_lat if last else n_all, final_g=final_g if last else None, **pre)
        h_parts = (h,)
    return h[:n_lat].reshape(n_batch, seq, d)
```

```python
import functools
import math
from typing import NamedTuple

import numpy as np
import jax
import jax.numpy as jnp
from jax import lax
from jax.experimental import pallas as pl
from jax.experimental.pallas import tpu as pltpu

F32 = jnp.float32
BF16 = jnp.bfloat16

EPS = 1e-6
ROPE_BASE = 10000.0
GRID_W = 64
N_MOD = 9
RET_HEADS = 4
RET_HEAD_DIM = 128
RET_CHUNK = 256
S5_GROUP = 16
S5_STATE = 64
S5_CHUNK = 16
S5_TILE = 8
NA_HEADS = 16
NA_HEAD_DIM = 64
NA_KH = 8
NA_KW = 16
NEG_INF = -1e30

TOKEN_TILE = 512
VMEM_LIMIT = 56 * 1024 * 1024
MOD_ROWS = 8


def _cparams(*sem):
    return pltpu.CompilerParams(dimension_semantics=sem, vmem_limit_bytes=VMEM_LIMIT)


def _resident(shape):
    nd = len(shape)
    return pl.BlockSpec(shape, lambda *_: (0,) * nd, pipeline_mode=pl.Buffered(1))


def _mod_kernel(c_ref, w_ref, b_ref, o_ref):
    c = c_ref[...]
    s = c * jax.nn.sigmoid(c)
    o_ref[0] = jnp.dot(s, w_ref[0], preferred_element_type=F32,
                       precision=lax.Precision.HIGHEST) + b_ref[0]


def _modulation(cvec, w_mod, b_mod):
    depth, d, nd = w_mod.shape
    tn = nd // 4 if nd % 512 == 0 else d
    return pl.pallas_call(
        _mod_kernel,
        grid=(depth, nd // tn),
        in_specs=[
            pl.BlockSpec((MOD_ROWS, d), lambda l, j: (0, 0)),
            pl.BlockSpec((1, d, tn), lambda l, j: (l, 0, j)),
            pl.BlockSpec((1, 1, tn), lambda l, j: (l, 0, j)),
        ],
        out_specs=pl.BlockSpec((1, MOD_ROWS, tn), lambda l, j: (l, 0, j)),
        out_shape=jax.ShapeDtypeStruct((depth, MOD_ROWS, nd), F32),
        compiler_params=_cparams("parallel", "parallel"),
        name="modulation",
    )(cvec, w_mod, b_mod.reshape(depth, 1, nd))


def _rms(x):
    return x * lax.rsqrt(jnp.mean(x * x, axis=-1, keepdims=True) + EPS)


def _modulated(h, m, g, mi):
    return (_rms(h) * g) * (1.0 + m[mi + 1:mi + 2]) + m[mi:mi + 1]


def _tile_specs(n_lat_tiles_per_batch, n_batch, d):
    def mod_idx(i):
        return (jnp.minimum(i // n_lat_tiles_per_batch, n_batch), 0, 0)
    h_spec = pl.BlockSpec((TOKEN_TILE, d), lambda i: (i, 0))
    m_spec = pl.BlockSpec((1, N_MOD, d), mod_idx)
    return h_spec, m_spec


def _pinned(block_shape, index):
    return pl.BlockSpec(block_shape, lambda *_: index, pipeline_mode=pl.Buffered(1))


def _stream_specs(parts, width, rows=TOKEN_TILE):
    if len(parts) == 1:
        return [pl.BlockSpec((rows, width), lambda i: (i, 0))], 0
    n0 = parts[0].shape[0] // rows
    return [pl.BlockSpec((rows, width), lambda i: (jnp.minimum(i, n0 - 1), 0)),
            pl.BlockSpec((rows, width), lambda i: (jnp.maximum(i - n0, 0), 0))], n0


def _stream_tile(refs, n0):
    if len(refs) == 1:
        return refs[0][...]
    return jnp.where(pl.program_id(0) < n0, refs[0][...], refs[1][...])


class _HalfCfg(NamedTuple):
    n_h: int
    n0: int
    pre: str
    n_y: int
    ny0: int
    post: str
    final: bool


def _gelu_tanh(y):
    return 0.5 * y * (1.0 + jnp.tanh(math.sqrt(2.0 / math.pi) * (y + 0.044715 * (y * y * y))))


S5_CHUNKS_PER_TILE = TOKEN_TILE // S5_CHUNK


def _to_chunk_rows(u_scr, out_ref):
    half = 128 // S5_GROUP
    gw = S5_CHUNK * S5_GROUP
    for blk in range(u_scr.shape[0]):
        for th in range(S5_CHUNK // half):
            arrs = [u_scr[blk, pl.ds(th * half + tl, S5_CHUNKS_PER_TILE, stride=S5_CHUNK), :]
                    for tl in range(half)]
            for g, x in enumerate(_block_transpose(arrs)):
                lo = (blk * half + g) * gw + th * 128
                out_ref[:, lo:lo + 128] = x


def _from_chunk_rows(y, y_scr):
    half = 128 // S5_GROUP
    gw = S5_CHUNK * S5_GROUP
    for blk in range(y_scr.shape[0]):
        for th in range(S5_CHUNK // half):
            arrs = [y[:, (blk * half + g) * gw + th * 128:(blk * half + g) * gw + (th + 1) * 128]
                    for g in range(half)]
            for tl, x in enumerate(_block_transpose(arrs)):
                y_scr[blk, pl.ds(th * half + tl, S5_CHUNKS_PER_TILE, stride=S5_CHUNK), :] = x


def _half_kernel(*refs, cfg):
    it = iter(refs)
    take = lambda n: [next(it) for _ in range(n)]
    h_refs = take(cfg.n_h)
    m_ref, g_ref = take(2)
    h = _stream_tile(h_refs, cfg.n0)
    m = m_ref[0]
    gate_mix = m[5:6]
    relayout_scr = refs[-1]
    if cfg.pre == "ab":
        (r_ref,) = take(1)
        y_refs = take(cfg.n_y)
        gw_ref, gb_ref, wr_ref, ws_ref = take(4)
        mix = jnp.dot(r_ref[...].astype(BF16), wr_ref[...], preferred_element_type=F32)
        _from_chunk_rows(_stream_tile(y_refs, cfg.ny0), relayout_scr)
        g = _gelu_tanh(jnp.concatenate([relayout_scr[j] for j in range(relayout_scr.shape[0])], axis=1))
        s = g * jax.nn.sigmoid(jnp.dot(g.astype(BF16), gw_ref[...], preferred_element_type=F32)
                               + gb_ref[...])
        mix += jnp.dot(s.astype(BF16), ws_ref[...], preferred_element_type=F32)
        h = h + gate_mix * mix
    elif cfg.pre == "na":
        a_ref, wo_ref = take(2)
        h = h + gate_mix * jnp.dot(a_ref[...], wo_ref[...], preferred_element_type=F32)
    w1a_ref, w1b_ref, w2_ref = take(3)
    k = 1 if cfg.pre else 0
    mi = 6 * k
    xm = _modulated(h, m, g_ref[2 * k], mi).astype(BF16)
    a = jnp.dot(xm, w1a_ref[...], preferred_element_type=F32)
    b = jnp.dot(xm, w1b_ref[...], preferred_element_type=F32)
    hid = (a * jax.nn.sigmoid(a) * b).astype(BF16)
    h = h + (0.5 * m[mi + 2:mi + 3]) * jnp.dot(hid, w2_ref[...], preferred_element_type=F32)
    if cfg.post:
        (wp_ref,) = take(1)
    if cfg.post == "ab":
        cos_ref, sa_ref, sb_ref = take(3)
    if cfg.final:
        (fg_ref,) = take(1)
    outs = list(it)
    if "ab" in (cfg.pre, cfg.post):
        outs.pop()
    outs[0][...] = _rms(h) * fg_ref[...] if cfg.final else h
    if not cfg.post:
        return
    xm = _modulated(h, m, g_ref[1], 3).astype(BF16)
    if cfg.post == "na":
        outs[1][...] = jnp.dot(xm, wp_ref[...], preferred_element_type=F32).astype(BF16)
        return
    width = RET_HEADS * RET_HEAD_DIM
    pr_u = jnp.dot(xm, wp_ref[:, 4 * width:], preferred_element_type=F32)
    for j in range(relayout_scr.shape[0]):
        relayout_scr[j] = pr_u[:, j * 128:(j + 1) * 128]
    _to_chunk_rows(relayout_scr, outs[3])
    pr_qk = jnp.dot(xm, wp_ref[:, :2 * width], preferred_element_type=F32)
    cos, sa, sb = cos_ref[...], sa_ref[...], sb_ref[...]
    for hh in range(RET_HEADS):
        lo = hh * RET_HEAD_DIM
        hi = lo + RET_HEAD_DIM
        outs[1][:, lo:hi] = _rope(pr_qk[:, lo:hi], cos, sa, sb).astype(BF16)
        kr = _rope(pr_qk[:, width + lo:width + hi], cos, sa, sb) * (RET_HEAD_DIM ** -0.5)
        outs[1][:, width + lo:width + hi] = kr.astype(BF16)
    pr_vg = jnp.dot(xm, wp_ref[:, 2 * width:4 * width], preferred_element_type=F32)
    outs[1][:, 2 * width:] = pr_vg[:, :width].astype(BF16)
    outs[2][...] = pr_vg[:, width:]


def _half_layer(h_parts, mod, gains, w1, w2, *, layer, n_rows, lat_tiles, n_batch,
                pre="", pre_args=(), post="", post_w=None, post_wi=0, rope=None, final_g=None):
    d = h_parts[0].shape[1]
    f = w2.shape[2]
    k = 1 if pre else 0
    h_specs, n0 = _stream_specs(h_parts, d)
    _, m_spec = _tile_specs(lat_tiles, n_batch, d)
    in_specs = h_specs + [m_spec, _pinned((None, 3, 1, d), (layer, 0, 0, 0))]
    args = list(h_parts) + [mod, gains]
    n_y = ny0 = 0
    scratch = []
    if pre == "ab":
        r, ys_parts, glu_w, glu_b, w_out, wi = pre_args
        w = r.shape[1]
        y_specs, ny0 = _stream_specs(ys_parts, S5_CHUNK * w, rows=S5_CHUNKS_PER_TILE)
        n_y = len(ys_parts)
        scratch = [pltpu.VMEM((w // 128, TOKEN_TILE, 128), F32)]
        in_specs += [pl.BlockSpec((TOKEN_TILE, w), lambda i: (i, 0))] + y_specs + [
            _pinned((None, w, w), (wi, 0, 0)), _pinned((None, 1, w), (wi, 0, 0)),
            _pinned((None, w, d), (wi, 0, 0)), _pinned((None, w, d), (wi, 1, 0))]
        args += [r, *ys_parts, glu_w, glu_b, w_out, w_out]
    elif pre == "na":
        att, w_o, wi = pre_args
        in_specs += [pl.BlockSpec((TOKEN_TILE, d), lambda i: (i, 0)), _pinned((None, d, d), (wi, 0, 0))]
        args += [att, w_o]
    in_specs += [_pinned((None, None, d, f), (layer, k, 0, 0)),
                 _pinned((None, None, d, f), (layer, k, 0, 1)),
                 _pinned((None, None, f, d), (layer, k, 0, 0))]
    args += [w1, w1, w2]
    out_specs = [pl.BlockSpec((TOKEN_TILE, d), lambda i: (i, 0))]
    out_shape = [jax.ShapeDtypeStruct((n_rows, d), F32)]
    if post:
        n = post_w.shape[2]
        in_specs.append(_pinned((None, d, n), (post_wi, 0, 0)))
        args.append(post_w)
    if post == "ab":
        n_lat_tiles = lat_tiles * n_batch
        tab = pl.BlockSpec((TOKEN_TILE, RET_HEAD_DIM),
                           lambda i: (jnp.where(i < n_lat_tiles, i % lat_tiles, lat_tiles), 0))
        in_specs += [tab, tab, tab]
        args += list(rope)
        width = RET_HEADS * RET_HEAD_DIM
        n_u = n - 4 * width
        out_specs += [pl.BlockSpec((TOKEN_TILE, 3 * width), lambda i: (i, 0)),
                      pl.BlockSpec((TOKEN_TILE, width), lambda i: (i, 0)),
                      pl.BlockSpec((S5_CHUNKS_PER_TILE, S5_CHUNK * n_u), lambda i: (i, 0))]
        out_shape += [jax.ShapeDtypeStruct((n_rows, 3 * width), BF16),
                      jax.ShapeDtypeStruct((n_rows, width), F32),
                      jax.ShapeDtypeStruct((n_rows // S5_CHUNK, S5_CHUNK * n_u), F32)]
        scratch = [pltpu.VMEM((n_u // 128, TOKEN_TILE, 128), F32)]
    elif post == "na":
        out_specs.append(pl.BlockSpec((TOKEN_TILE, n), lambda i: (i, 0)))
        out_shape.append(jax.ShapeDtypeStruct((n_rows, n), BF16))
    if final_g is not None:
        in_specs.append(_resident((1, d)))
        args.append(final_g.reshape(1, d))
    cfg = _HalfCfg(n_h=len(h_parts), n0=n0, pre=pre, n_y=n_y, ny0=ny0, post=post,
                   final=final_g is not None)
    return pl.pallas_call(
        functools.partial(_half_kernel, cfg=cfg),
        grid=(n_rows // TOKEN_TILE,),
        in_specs=in_specs,
        out_specs=out_specs,
        out_shape=out_shape,
        scratch_shapes=scratch,
        compiler_params=_cparams("parallel"),
        name="half_layer",
    )(*args)


def _rope_tables(seq):
    half = RET_HEAD_DIM // 2
    quarter = half // 2
    inv = ROPE_BASE ** (-np.arange(0, half, 2, dtype=np.float64) / half)
    t = np.arange(seq)
    lane = np.arange(RET_HEAD_DIM)
    pos = np.where(lane[None, :] < half, (t // GRID_W)[:, None], (t % GRID_W)[:, None])
    ang = pos.astype(np.float32).astype(np.float64) * inv.astype(np.float32)[lane % quarter][None, :]
    first = (lane % half) < quarter
    cos = np.cos(ang)
    sin = np.sin(ang)
    sa = np.where(first[None, :], -sin, 0.0)
    sb = np.where(first[None, :], 0.0, sin)
    pad1 = np.ones((TOKEN_TILE, RET_HEAD_DIM))
    pad0 = np.zeros((TOKEN_TILE, RET_HEAD_DIM))
    tabs = [np.concatenate([cos, pad1]), np.concatenate([sa, pad0]), np.concatenate([sb, pad0])]
    return [jnp.asarray(x, F32) for x in tabs]


def _rope(x, cos, sa, sb):
    quarter = RET_HEAD_DIM // 4
    up = pltpu.roll(x, RET_HEAD_DIM - quarter, axis=1)
    dn = pltpu.roll(x, quarter, axis=1)
    return x * cos + up * sa + dn * sb


def _retention_kernel(dec_ref, ql_ref, kl_ref, vl_ref, qc_ref, kc_ref, vc_ref, g_ref,
                      dmask_ref, qwf_ref, qwb_ref, kwf_ref, kwb_ref, o_ref,
                      s_run, s_bwd, *, n_lat, n_ctx):
    i = pl.program_id(1)
    hd = RET_HEAD_DIM
    c = RET_CHUNK
    tn = (((0,), (0,)), ((), ()))
    nt = (((1,), (1,)), ((), ()))

    def bwd_chunk(k_ref, v_ref, rows, cid):
        for h in range(RET_HEADS):
            sl = slice(h * hd, (h + 1) * hd)
            s_old = s_run[h]
            s_bwd[cid, h] = s_old
            kw = (k_ref[rows, sl] * kwb_ref[h]).astype(BF16)
            kv = lax.dot_general(kw, v_ref[rows, sl], tn, preferred_element_type=F32)
            s_run[h] = dec_ref[1, h] * s_old + kv

    def fwd_chunk(q_ref, k_ref, v_ref, rows, cid):
        for h in range(RET_HEADS):
            sl = slice(h * hd, (h + 1) * hd)
            q = q_ref[rows, sl]
            k = k_ref[rows, sl]
            v = v_ref[rows, sl]
            s_old = s_run[h]
            a = lax.dot_general(q, k, nt, preferred_element_type=F32)
            o = jnp.dot((a * dmask_ref[h]).astype(BF16), v, preferred_element_type=F32)
            o += qwf_ref[h] * jnp.dot(q, s_old.astype(BF16), preferred_element_type=F32)
            o += qwb_ref[h] * jnp.dot(q, s_bwd[cid, h].astype(BF16), preferred_element_type=F32)
            kv = lax.dot_general((k * kwf_ref[h]).astype(BF16), v, tn, preferred_element_type=F32)
            s_run[h] = dec_ref[0, h] * s_old + kv
            o = o * lax.rsqrt(jnp.mean(o * o, axis=-1, keepdims=True) + EPS)
            g = g_ref[:, sl]
            o_ref[:, sl] = o * (g * jax.nn.sigmoid(g))

    def lat_rows(j):
        return pl.ds(pl.multiple_of(j * c, c), c)

    @pl.when(i == 0)
    def _():
        s_run[...] = jnp.zeros_like(s_run)
        for j in reversed(range(n_ctx)):
            bwd_chunk(kc_ref, vc_ref, slice(j * c, (j + 1) * c), j)

        def body(t, carry):
            j = n_lat - 1 - t
            bwd_chunk(kl_ref, vl_ref, lat_rows(j), n_ctx + j)
            return carry

        lax.fori_loop(0, n_lat, body, 0)
        s_run[...] = jnp.zeros_like(s_run)

    for j in range(n_ctx):
        @pl.when(i == 1 + j)
        def _(j=j):
            fwd_chunk(qc_ref, kc_ref, vc_ref, slice(j * c, (j + 1) * c), j)

    @pl.when(i > n_ctx)
    def _():
        fwd_chunk(ql_ref, kl_ref, vl_ref, lat_rows(i - 1 - n_ctx), i - 1)


def _retention(pq, pg, decay_logit, *, n_batch, seq, ctx_len):
    c = RET_CHUNK
    width = RET_HEADS * RET_HEAD_DIM
    n_lat = seq // c
    n_ctx = ctx_len // c
    n_ch = n_lat + n_ctx
    n_rows = pq.shape[0]
    lat_blocks = n_batch * n_lat

    log_gamma = jax.nn.log_sigmoid(decay_logit.astype(F32))
    pos = jnp.arange(c, dtype=F32)
    diff = pos[:, None] - pos[None, :]
    lf = log_gamma[0][:, None, None]
    lb = log_gamma[1][:, None, None]
    dmask = (jnp.where(diff >= 0, jnp.exp(lf * jnp.maximum(diff, 0.0)), 0.0)
             + jnp.where(diff <= 0, jnp.exp(lb * jnp.maximum(-diff, 0.0)), 0.0))
    ones = jnp.ones((1, 1, RET_HEAD_DIM), F32)
    col = pos[None, :, None]
    qwf = jnp.exp(lf * (col + 1.0)) * ones
    qwb = jnp.exp(lb * (c - col)) * ones
    kwf = jnp.exp(lf * (c - 1.0 - col)) * ones
    kwb = jnp.exp(lb * col) * ones
    chunk_decay = jnp.exp(log_gamma * c)

    ctx_blk0 = n_batch * seq // ctx_len

    def chunk_block(b, i):
        cid = jnp.maximum(i - 1, 0)
        return (jnp.where(cid < n_ctx, lat_blocks + b * n_ctx + cid, b * n_lat + cid - n_ctx), 0)

    chunk_spec = pl.BlockSpec((c, width), chunk_block)
    lat = lambda part: pl.BlockSpec((seq, width), lambda b, i: (b, part))
    ctx = lambda part: pl.BlockSpec((ctx_len, width), lambda b, i: (ctx_blk0 + b, part))
    head_tab = _resident((RET_HEADS, c, RET_HEAD_DIM))
    kern = functools.partial(_retention_kernel, n_lat=n_lat, n_ctx=n_ctx)
    return pl.pallas_call(
        kern,
        grid=(n_batch, 1 + n_ch),
        in_specs=[
            pl.BlockSpec(memory_space=pltpu.SMEM),
            lat(0), lat(1), lat(2), ctx(0), ctx(1), ctx(2), chunk_spec,
            _resident((RET_HEADS, c, c)), head_tab, head_tab, head_tab, head_tab,
        ],
        out_specs=chunk_spec,
        out_shape=jax.ShapeDtypeStruct((n_rows, width), F32),
        scratch_shapes=[
            pltpu.VMEM((RET_HEADS, RET_HEAD_DIM, RET_HEAD_DIM), F32),
            pltpu.VMEM((n_ch, RET_HEADS, RET_HEAD_DIM, RET_HEAD_DIM), F32),
        ],
        compiler_params=_cparams("parallel", "arbitrary"),
        name="retention",
    )(chunk_decay, pq, pq, pq, pq, pq, pq, pg, dmask, qwf, qwb, kwf, kwb)


def _cmul(a, b):
    return a[0] * b[0] - a[1] * b[1], a[0] * b[1] + a[1] * b[0]


def _pair_blockdiag(a):
    a0, a1 = a[0::2], a[1::2]
    z = jnp.zeros_like(a0)
    return jnp.concatenate([jnp.concatenate([a0, z], axis=2), jnp.concatenate([z, a1], axis=2)],
                           axis=1)


def _pair_lanes(a):
    return jnp.concatenate([a[0::2], a[1::2]], axis=2)


def _powers(a, n):
    one = (jnp.ones_like(a[0]), jnp.zeros_like(a[0]))
    pw = (jnp.stack([one[0], a[0]], axis=1), jnp.stack([one[1], a[1]], axis=1))
    step = a
    while pw[0].shape[1] < n:
        step = _cmul(step, step)
        nxt = _cmul(pw, (step[0][:, None], step[1][:, None]))
        pw = (jnp.concatenate([pw[0], nxt[0]], axis=1), jnp.concatenate([pw[1], nxt[1]], axis=1))
    top = _cmul(step, step)
    return (jnp.concatenate([pw[0], top[0][:, None]], axis=1),
            jnp.concatenate([pw[1], top[1][:, None]], axis=1))


def _s5_weights(lam_re, lam_im, log_dt, b_re, b_im, c_re, c_im, d_skip):
    q = S5_CHUNK
    n_g = lam_re.shape[1]
    hi = lax.Precision.HIGHEST
    kall = None
    zw = []
    rw = []
    tabs = []
    for d in range(2):
        lr = jnp.minimum(lam_re[d].astype(F32), -1e-4)
        li = lam_im[d].astype(F32)
        dt = jnp.exp(log_dt[d].astype(F32))[:, None]
        mag = jnp.exp(lr * dt)
        a = (mag * jnp.cos(li * dt), mag * jnp.sin(li * dt))
        den = lr * lr + li * li
        am1 = (a[0] - 1.0, a[1])
        coef = ((am1[0] * lr + am1[1] * li) / den, (am1[1] * lr - am1[0] * li) / den)
        bt = (b_re[d].astype(F32).transpose(0, 2, 1), b_im[d].astype(F32).transpose(0, 2, 1))
        bb = _cmul((coef[0][:, None], coef[1][:, None]), bt)
        cm = (c_re[d].astype(F32), c_im[d].astype(F32))
        pw_re, pw_im = _powers(a, q)
        e = _cmul((pw_re[:, :q, None, :], pw_im[:, :q, None, :]),
                  (bb[0][:, None], bb[1][:, None]))
        taps = (jnp.einsum('gkp,gtjp->gjtk', cm[0], e[0], precision=hi)
                - jnp.einsum('gkp,gtjp->gjtk', cm[1], e[1], precision=hi))
        if d == 0:
            kall = jnp.concatenate([jnp.zeros_like(taps[:, :, 1:]), taps], axis=2)
        else:
            kall = kall + jnp.concatenate([taps[:, :, ::-1], jnp.zeros_like(taps[:, :, 1:])], axis=2)
        tsel = np.arange(q)[::-1] if d == 0 else np.arange(q)
        z = _cmul((pw_re[:, tsel][:, :, None, :], pw_im[:, tsel][:, :, None, :]),
                  (bb[0][:, None], bb[1][:, None]))
        zw.append((z[0].reshape(n_g, q * S5_GROUP, S5_STATE),
                   z[1].reshape(n_g, q * S5_GROUP, S5_STATE)))
        rsel = np.arange(1, q + 1) if d == 0 else np.arange(q, 0, -1)
        ct = (cm[0].transpose(0, 2, 1)[:, :, None, :], cm[1].transpose(0, 2, 1)[:, :, None, :])
        pt = (pw_re[:, rsel].transpose(0, 2, 1)[..., None], pw_im[:, rsel].transpose(0, 2, 1)[..., None])
        r = _cmul(ct, pt)
        rw.append((r[0].reshape(n_g, S5_STATE, q * S5_GROUP),
                   (-r[1]).reshape(n_g, S5_STATE, q * S5_GROUP)))
        a16 = (pw_re[:, q], pw_im[:, q])
        ramp_re, ramp_im = _powers(a16, S5_TILE)
        a32 = (ramp_re[:, 2], ramp_im[:, 2])
        a64 = (ramp_re[:, 4], ramp_im[:, 4])
        a128 = (ramp_re[:, 8], ramp_im[:, 8])
        rsl = slice(0, S5_TILE) if d == 0 else slice(S5_TILE - 1, None, -1)
        rows = [a16[0], a16[1], a32[0], a32[1], a64[0], a64[1], a128[0], a128[1]]
        tabs.append(jnp.concatenate([jnp.stack(rows, axis=1), ramp_re[:, rsl], ramp_im[:, rsl]],
                                    axis=1))
    dsk = d_skip.astype(F32).reshape(n_g, S5_GROUP)
    eye = jnp.eye(S5_GROUP, dtype=F32)
    kall = kall.at[:, :, q - 1].add(dsk[:, :, None] * eye[None])
    kall = jnp.concatenate([kall, jnp.zeros_like(kall[:, :, :1])], axis=2)
    kflat = kall.reshape(n_g, S5_GROUP, 2 * q * S5_GROUP)
    wz = jnp.concatenate([_pair_blockdiag(z) for pair in zw for z in pair], axis=2)
    w2 = jnp.concatenate([_pair_blockdiag(r) for pair in rw for r in pair], axis=1)
    tab = jnp.concatenate([_pair_lanes(t) for t in tabs], axis=1)
    return kflat, wz.astype(BF16), w2.astype(BF16), tab


def _tile_shift(x, s, down):
    row = lax.broadcasted_iota(jnp.int32, x.shape, 1)
    if down:
        return jnp.where(row >= s, pltpu.roll(x, s, axis=1), 0.0)
    return jnp.where(row < S5_TILE - s, pltpu.roll(x, S5_TILE - s, axis=1), 0.0)


S5_LANE_GROUPS = 128 // S5_GROUP


def _block_transpose(arrs):
    lane = lax.broadcasted_iota(jnp.int32, arrs[0].shape, 1)
    blk = lane // S5_GROUP
    a = list(arrs)
    for bit in range(3):
        s = 1 << bit
        hi = (blk & s) != 0
        new = list(a)
        for i in range(S5_LANE_GROUPS):
            if i & s:
                continue
            new[i] = jnp.where(hi, pltpu.roll(a[i + s], S5_GROUP * s, axis=1), a[i])
            new[i + s] = jnp.where(hi, a[i + s], pltpu.roll(a[i], 128 - S5_GROUP * s, axis=1))
        a = new
    return a


def _s5_kernel(vl_ref, vc_ref, kf_ref, wz_ref, w2_ref, tab_ref, yl_ref, yc_ref, m_scr, zx, xin,
               *, lat_tiles, ctx_tiles):
    q = S5_CHUNK
    lanes = 2 * S5_STATE
    pw = 2 * q * S5_GROUP
    gw = q * S5_GROUP

    @pl.when(pl.program_id(1) == 0)
    def _():
        m_scr[...] = jnp.zeros_like(m_scr)
        for g in range(S5_LANE_GROUPS):
            taps = kf_ref[g]
            base = (g % 2) * gw
            for ti in range(q):
                lo = S5_GROUP * (q - 1 - ti)
                m_scr[g // 2, base + ti * S5_GROUP:base + (ti + 1) * S5_GROUP, base:base + gw] = (
                    taps[:, lo:lo + gw].astype(BF16))

    n_tiles = ctx_tiles + lat_tiles
    ctx_rows = ctx_tiles * S5_TILE

    for p in range(m_scr.shape[0]):
        cols = slice(p * pw, (p + 1) * pw)
        v = jnp.concatenate([vc_ref[:, cols], vl_ref[:, cols]], axis=0).astype(BF16)
        y_intra = jnp.dot(v, m_scr[p], preferred_element_type=F32)
        y = jnp.dot(v, wz_ref[p], preferred_element_type=F32)

        def tab_row(r):
            return tab_ref[p, r:r + 1, :]

        for d in range(2):
            down = d == 0
            t0 = 24 * d
            lo = 2 * d * lanes
            z = (y[:, lo:lo + lanes].reshape(n_tiles, S5_TILE, lanes),
                 y[:, lo + lanes:lo + 2 * lanes].reshape(n_tiles, S5_TILE, lanes))
            loc = (_tile_shift(z[0], 1, down), _tile_shift(z[1], 1, down))
            for k, s in enumerate((1, 2, 4)):
                mul = (tab_row(t0 + 2 * k)[None], tab_row(t0 + 2 * k + 1)[None])
                inc = _cmul(mul, (_tile_shift(loc[0], s, down), _tile_shift(loc[1], s, down)))
                loc = (loc[0] + inc[0], loc[1] + inc[1])
            zx[2 * d] = loc[0]
            zx[2 * d + 1] = loc[1]
            zx[4 + 2 * d] = z[0]
            zx[4 + 2 * d + 1] = z[1]

        for d in range(2):
            t0 = 24 * d
            edge = S5_TILE - 1 if d == 0 else 0
            a16 = (tab_row(t0), tab_row(t0 + 1))
            a128 = (tab_row(t0 + 6), tab_row(t0 + 7))
            ramp = (tab_ref[p, t0 + 8:t0 + 16, :], tab_ref[p, t0 + 16:t0 + 24, :])

            def tile_step(j, carry, d=d, edge=edge, a16=a16, a128=a128, ramp=ramp):
                loc = (zx[2 * d, j], zx[2 * d + 1, j])
                z = (zx[4 + 2 * d, j], zx[4 + 2 * d + 1, j])
                inc = _cmul(ramp, carry)
                xin[j, :, (2 * d) * lanes:(2 * d + 1) * lanes] = loc[0] + inc[0]
                xin[j, :, (2 * d + 1) * lanes:(2 * d + 2) * lanes] = loc[1] + inc[1]
                e_loc = _cmul(a16, (loc[0][edge:edge + 1], loc[1][edge:edge + 1]))
                nxt = _cmul(a128, carry)
                return (nxt[0] + e_loc[0] + z[0][edge:edge + 1],
                        nxt[1] + e_loc[1] + z[1][edge:edge + 1])

            zero = (jnp.zeros((1, lanes), F32), jnp.zeros((1, lanes), F32))
            if d == 0:
                lax.fori_loop(0, n_tiles, tile_step, zero)
            else:
                mid = lax.fori_loop(0, ctx_tiles, lambda t, c: tile_step(ctx_tiles - 1 - t, c), zero)
                lax.fori_loop(0, lat_tiles, lambda t, c: tile_step(n_tiles - 1 - t, c), mid)

        x = xin[...].reshape(n_tiles * S5_TILE, 4 * lanes)
        y_all = y_intra + jnp.dot(x.astype(BF16), w2_ref[p], preferred_element_type=F32)
        yc_ref[:, cols] = y_all[:ctx_rows]
        yl_ref[:, cols] = y_all[ctx_rows:]


def _s5(v, weights, *, n_batch, seq, ctx_len):
    kflat, wz, w2, tab = weights
    q = S5_CHUNK
    lanes = 2 * S5_STATE
    pw = 2 * q * S5_GROUP
    n_pairs = w2.shape[0]
    ppb = S5_LANE_GROUPS // 2
    n_blocks = n_pairs // ppb
    bw = ppb * pw
    lat_rows = seq // q
    ctx_rows = ctx_len // q
    n_tiles = (lat_rows + ctx_rows) // S5_TILE
    ctx_blk0 = n_batch * lat_rows // ctx_rows
    kern = functools.partial(_s5_kernel, lat_tiles=lat_rows // S5_TILE, ctx_tiles=ctx_rows // S5_TILE)
    return pl.pallas_call(
        kern,
        grid=(n_blocks, n_batch),
        in_specs=[
            pl.BlockSpec((lat_rows, bw), lambda j, b: (b, j)),
            pl.BlockSpec((ctx_rows, bw), lambda j, b: (ctx_blk0 + b, j)),
            pl.BlockSpec((S5_LANE_GROUPS,) + kflat.shape[1:], lambda j, b: (j, 0, 0)),
            pl.BlockSpec((ppb, pw, 4 * lanes), lambda j, b: (j, 0, 0)),
            pl.BlockSpec((ppb, 4 * lanes, pw), lambda j, b: (j, 0, 0)),
            pl.BlockSpec((ppb, tab.shape[1], lanes), lambda j, b: (j, 0, 0)),
        ],
        out_specs=[pl.BlockSpec((lat_rows, bw), lambda j, b: (b, j)),
                   pl.BlockSpec((ctx_rows, bw), lambda j, b: (b, j))],
        out_shape=[jax.ShapeDtypeStruct((n_batch * lat_rows, n_blocks * bw), F32),
                   jax.ShapeDtypeStruct((n_batch * ctx_rows, n_blocks * bw), F32)],
        scratch_shapes=[
            pltpu.VMEM((ppb, pw, pw), BF16),
            pltpu.VMEM((8, n_tiles, S5_TILE, lanes), F32),
            pltpu.VMEM((n_tiles, S5_TILE, 4 * lanes), F32),
        ],
        compiler_params=_cparams("arbitrary", "arbitrary"),
        name="s5",
    )(v, v, kflat, wz, w2, tab)


NA_QROWS = 4
NA_KROWS = NA_QROWS + NA_KH


NA_REL_ROWS = 2 * NA_KH - 1


def _na_tile_index(kind, a, m):
    first, rel0 = ((0, NA_KH - 1 - a), (a, NA_QROWS - 1 - a), (NA_QROWS, -1 - a))[kind]
    return rel0 + m if first <= m < first + NA_KH else NA_REL_ROWS


def _na_bias_tiles(rpb):
    w = GRID_W
    qcol = np.arange(w)
    kcol = np.arange(w)
    wstart = np.clip(qcol - NA_KW // 2, 0, w - NA_KW)
    valid = (kcol[None, :] >= wstart[:, None]) & (kcol[None, :] < wstart[:, None] + NA_KW)
    rel = np.clip(kcol[None, :] - qcol[:, None], -(NA_KW - 1), NA_KW - 1) + NA_KW - 1
    onehot = (rel[None] == np.arange(2 * NA_KW - 1)[:, None, None]).astype(np.float32)
    tiles = jnp.einsum('hrj,jqk->hrqk', rpb.astype(F32), jnp.asarray(onehot),
                       precision=lax.Precision.HIGHEST)
    tiles = jnp.where(jnp.asarray(valid)[None, None], tiles, NEG_INF)
    return jnp.concatenate([tiles, jnp.full((NA_HEADS, 1, w, w), NEG_INF, F32)], axis=1)


def _na_kernel(q_ref, k_ref, v_ref, kc_ref, vc_ref, tiles_ref, o_ref, bias_ref, vx, vcx, s_even, s_odd,
               p_even, p_odd, *, rows):
    w = GRID_W
    dh = NA_HEAD_DIM
    lb = 2 * dh
    nq = NA_QROWS * w
    nk = NA_KROWS * w
    n_blocks = rows // NA_QROWS

    @pl.when(pl.program_id(1) == 0)
    def _():
        for kind in range(3):
            for hh in range(2):
                for a in range(NA_QROWS):
                    for m in range(0, NA_KROWS, 2):
                        pair = [tiles_ref[hh, _na_tile_index(kind, a, m + e)] for e in range(2)]
                        bias_ref[kind, hh * nq + a * w:hh * nq + (a + 1) * w, m * w:(m + 2) * w] = (
                            jnp.concatenate(pair, axis=1))

    nt = (((1,), (1,)), ((), ()))
    scale = dh ** -0.5
    kc = kc_ref[...]
    lane = lax.broadcasted_iota(jnp.int32, (nq, lb), 1)
    first = lane < dh

    for dst, src in ((vx, v_ref), (vcx, vc_ref)):
        n = src.shape[0]
        dst[:, :lb] = src[...]
        dst[:, lb:] = (lax.broadcasted_iota(jnp.int32, (n, lb), 1) == 0).astype(BF16)

    def key_offset(i):
        r0 = jnp.clip(NA_QROWS * i - NA_KH // 2, 0, rows - NA_KROWS)
        return pl.multiple_of(r0 * w, NA_QROWS * w)

    def scores(i, s_ref):
        kind = jnp.where(i == 0, 0, jnp.where(i == n_blocks - 1, 2, 1))
        q = q_ref[pl.ds(pl.multiple_of(i * nq, nq), nq), :] * scale
        zero = jnp.zeros_like(q)
        qs = jnp.concatenate([jnp.where(first, q, zero), jnp.where(first, zero, q)], axis=0)
        kl = k_ref[pl.ds(key_offset(i), nk), :]
        s_ref[:, :nk] = lax.dot_general(qs, kl, nt, preferred_element_type=F32) + bias_ref[kind]
        s_ref[:, nk:] = lax.dot_general(qs, kc, nt, preferred_element_type=F32)

    def softmax(s_ref, p_ref):
        s = s_ref[...]
        p_ref[...] = jnp.exp(s - jnp.max(s, axis=-1, keepdims=True)).astype(BF16)

    def attend(i, p_ref):
        o = jnp.dot(p_ref[:, :nk], vx[pl.ds(key_offset(i), nk), :], preferred_element_type=F32)
        o += jnp.dot(p_ref[:, nk:], vcx[...], preferred_element_type=F32)
        o = o[:, :lb] / o[:, lb:lb + 1]
        o_ref[pl.ds(pl.multiple_of(i * nq, nq), nq), :] = jnp.where(first, o[:nq], o[nq:]).astype(
            o_ref.dtype)

    scores(0, s_even)
    softmax(s_even, p_even)
    scores(1, s_odd)

    def body(j, carry):
        attend(2 * j - 2, p_even)
        softmax(s_odd, p_odd)
        scores(2 * j, s_even)
        attend(2 * j - 1, p_odd)
        softmax(s_even, p_even)
        scores(2 * j + 1, s_odd)
        return carry

    lax.fori_loop(1, n_blocks // 2, body, 0, unroll=True)
    attend(n_blocks - 2, p_even)
    softmax(s_odd, p_odd)
    attend(n_blocks - 1, p_odd)


def _natten(p, tiles, *, n_batch, seq, ctx_len):
    d = NA_HEADS * NA_HEAD_DIM
    lb = 2 * NA_HEAD_DIM
    n_pairs = NA_HEADS // 2
    ctx_blk0 = n_batch * seq // ctx_len
    rows = seq // GRID_W
    assert rows % (2 * NA_QROWS) == 0 and rows >= NA_KROWS + NA_QROWS
    stacked = 2 * NA_QROWS * GRID_W
    n_keys = NA_KROWS * GRID_W + ctx_len
    lat = lambda part: pl.BlockSpec((seq, lb), lambda j, b: (b, part * n_pairs + j))
    ctx = lambda part: pl.BlockSpec((ctx_len, lb), lambda j, b: (ctx_blk0 + b, part * n_pairs + j))
    return pl.pallas_call(
        functools.partial(_na_kernel, rows=rows),
        grid=(n_pairs, n_batch),
        in_specs=[lat(0), lat(1), lat(2), ctx(1), ctx(2),
                  pl.BlockSpec((2,) + tiles.shape[1:], lambda j, b: (j, 0, 0, 0))],
        out_specs=pl.BlockSpec((seq, lb), lambda j, b: (b, j)),
        out_shape=jax.ShapeDtypeStruct((n_batch * seq, d), BF16),
        scratch_shapes=[pltpu.VMEM((3, stacked, NA_KROWS * GRID_W), F32),
                        pltpu.VMEM((seq, 2 * lb), BF16), pltpu.VMEM((ctx_len, 2 * lb), BF16),
                        pltpu.VMEM((stacked, n_keys), F32), pltpu.VMEM((stacked, n_keys), F32),
                        pltpu.VMEM((stacked, n_keys), BF16), pltpu.VMEM((stacked, n_keys), BF16)],
        compiler_params=_cparams("arbitrary", "arbitrary"),
        name="natten",
    )(p, p, p, p, p, tiles)


def kernel(x, c, ctx, c_ctx, w_mod, b_mod, norm_g, ffn_w1, ffn_w2, w_in_ab, w_out_ab, ret_decay_logit, s5_lam_re, s5_lam_im, s5_log_dt, s5_b_re, s5_b_im, s5_c_re, s5_c_im, s5_d, s5_glu_w, s5_glu_b, na_w_qkv, na_w_o, na_rpb, final_g):
    n_batch, seq, d = x.shape
    ctx_len = ctx.shape[1]
    depth = w_mod.shape[0]
    n_lat = n_batch * seq
    n_all = n_lat + n_batch * ctx_len
    lat_tiles = seq // TOKEN_TILE
    assert seq % TOKEN_TILE == 0 and (n_batch * ctx_len) % TOKEN_TILE == 0
    assert n_batch + 1 <= MOD_ROWS and seq % (GRID_W * NA_KH) == 0

    cvec = jnp.concatenate([c, c_ctx[None], jnp.zeros((MOD_ROWS - n_batch - 1, d), F32)], axis=0)
    mod = _modulation(cvec, w_mod, b_mod).reshape(depth, MOD_ROWS, N_MOD, d)
    h_parts = (x.reshape(n_lat, d), ctx.reshape(n_batch * ctx_len, d))
    common = dict(lat_tiles=lat_tiles, n_batch=n_batch)
    dims = dict(n_batch=n_batch, seq=seq, ctx_len=ctx_len)
    gains = norm_g.astype(F32).reshape(depth, 3, 1, d)
    w1 = ffn_w1.astype(BF16)
    w2 = ffn_w2.astype(BF16)

    for layer in range(depth):
        last = layer == depth - 1
        i = layer // 2
        half = functools.partial(_half_layer, mod=mod[layer], gains=gains, w1=w1, w2=w2, layer=layer,
                                 **common)
        if layer % 2 == 0:
            h, pq, pg, pv = half(h_parts, n_rows=n_all, post="ab", post_w=w_in_ab.astype(BF16),
                                 post_wi=i, rope=_rope_tables(seq))
            r = _retention(pq, pg, ret_decay_logit[i], **dims)
            weights = _s5_weights(s5_lam_re[i], s5_lam_im[i], s5_log_dt[i], s5_b_re[i], s5_b_im[i],
                                  s5_c_re[i], s5_c_im[i], s5_d[i])
            ys_parts = _s5(pv, weights, **dims)
            pre = dict(pre="ab", pre_args=(r, ys_parts, s5_glu_w.astype(BF16),
                                           s5_glu_b.astype(F32)[:, None, :], w_out_ab.astype(BF16), i))
        else:
            assert last
            h, p = half(h_parts, n_rows=n_all, post="na", post_w=na_w_qkv.astype(BF16), post_wi=i)
            att = _natten(p, _na_bias_tiles(na_rpb[i]), **dims)
            pre = dict(pre="na", pre_args=(att, na_w_o.astype(BF16), i))
        (h,) = half((h,), n_rows=n_lat if last else n_all, final_g=final_g if last else None, **pre)
        h_parts = (h,)
    return h[:n_lat].reshape(n_batch, seq, d)
```

```python
import functools
import math
from typing import NamedTuple

import numpy as np
import jax
import jax.numpy as jnp
from jax import lax
from jax.experimental import pallas as pl
from jax.experimental.pallas import tpu as pltpu

F32 = jnp.float32
BF16 = jnp.bfloat16

EPS = 1e-6
ROPE_BASE = 10000.0
GRID_W = 64
N_MOD = 9
RET_HEADS = 4
RET_HEAD_DIM = 128
RET_CHUNK = 256
S5_GROUP = 16
S5_STATE = 64
S5_CHUNK = 16
S5_TILE = 8
NA_HEADS = 16
NA_HEAD_DIM = 64
NA_KH = 8
NA_KW = 16
NEG_INF = -1e30

TOKEN_TILE = 512
VMEM_LIMIT = 56 * 1024 * 1024
MOD_ROWS = 8


def _cparams(*sem):
    return pltpu.CompilerParams(dimension_semantics=sem, vmem_limit_bytes=VMEM_LIMIT)


def _resident(shape):
    nd = len(shape)
    return pl.BlockSpec(shape, lambda *_: (0,) * nd, pipeline_mode=pl.Buffered(1))


def _mod_kernel(c_ref, w_ref, b_ref, o_ref):
    c = c_ref[...]
    s = c * jax.nn.sigmoid(c)
    o_ref[0] = jnp.dot(s, w_ref[0], preferred_element_type=F32,
                       precision=lax.Precision.HIGHEST) + b_ref[0]


def _modulation(cvec, w_mod, b_mod):
    depth, d, nd = w_mod.shape
    tn = nd // 4 if nd % 512 == 0 else d
    return pl.pallas_call(
        _mod_kernel,
        grid=(depth, nd // tn),
        in_specs=[
            pl.BlockSpec((MOD_ROWS, d), lambda l, j: (0, 0)),
            pl.BlockSpec((1, d, tn), lambda l, j: (l, 0, j)),
            pl.BlockSpec((1, 1, tn), lambda l, j: (l, 0, j)),
        ],
        out_specs=pl.BlockSpec((1, MOD_ROWS, tn), lambda l, j: (l, 0, j)),
        out_shape=jax.ShapeDtypeStruct((depth, MOD_ROWS, nd), F32),
        compiler_params=_cparams("parallel", "parallel"),
        name="modulation",
    )(cvec, w_mod, b_mod.reshape(depth, 1, nd))


def _rms(x):
    return x * lax.rsqrt(jnp.mean(x * x, axis=-1, keepdims=True) + EPS)


def _modulated(h, m, g, mi):
    return (_rms(h) * g) * (1.0 + m[mi + 1:mi + 2]) + m[mi:mi + 1]


def _tile_specs(n_lat_tiles_per_batch, n_batch, d):
    def mod_idx(i):
        return (jnp.minimum(i // n_lat_tiles_per_batch, n_batch), 0, 0)
    h_spec = pl.BlockSpec((TOKEN_TILE, d), lambda i: (i, 0))
    m_spec = pl.BlockSpec((1, N_MOD, d), mod_idx)
    return h_spec, m_spec


def _pinned(block_shape, index):
    return pl.BlockSpec(block_shape, lambda *_: index, pipeline_mode=pl.Buffered(1))


def _stream_specs(parts, width, rows=TOKEN_TILE):
    if len(parts) == 1:
        return [pl.BlockSpec((rows, width), lambda i: (i, 0))], 0
    n0 = parts[0].shape[0] // rows
    return [pl.BlockSpec((rows, width), lambda i: (jnp.minimum(i, n0 - 1), 0)),
            pl.BlockSpec((rows, width), lambda i: (jnp.maximum(i - n0, 0), 0))], n0


def _stream_tile(refs, n0):
    if len(refs) == 1:
        return refs[0][...]
    return jnp.where(pl.program_id(0) < n0, refs[0][...], refs[1][...])


class _HalfCfg(NamedTuple):
    n_h: int
    n0: int
    pre: str
    n_y: int
    ny0: int
    post: str
    final: bool


def _gelu_tanh(y):
    return 0.5 * y * (1.0 + jnp.tanh(math.sqrt(2.0 / math.pi) * (y + 0.044715 * (y * y * y))))


S5_CHUNKS_PER_TILE = TOKEN_TILE // S5_CHUNK


def _to_chunk_rows(u_scr, out_ref):
    half = 128 // S5_GROUP
    gw = S5_CHUNK * S5_GROUP
    for blk in range(u_scr.shape[0]):
        for th in range(S5_CHUNK // half):
            arrs = [u_scr[blk, pl.ds(th * half + tl, S5_CHUNKS_PER_TILE, stride=S5_CHUNK), :]
                    for tl in range(half)]
            for g, x in enumerate(_block_transpose(arrs)):
                lo = (blk * half + g) * gw + th * 128
                out_ref[:, lo:lo + 128] = x


def _from_chunk_rows(y, y_scr):
    half = 128 // S5_GROUP
    gw = S5_CHUNK * S5_GROUP
    for blk in range(y_scr.shape[0]):
        for th in range(S5_CHUNK // half):
            arrs = [y[:, (blk * half + g) * gw + th * 128:(blk * half + g) * gw + (th + 1) * 128]
                    for g in range(half)]
            for tl, x in enumerate(_block_transpose(arrs)):
                y_scr[blk, pl.ds(th * half + tl, S5_CHUNKS_PER_TILE, stride=S5_CHUNK), :] = x


def _half_kernel(*refs, cfg):
    it = iter(refs)
    take = lambda n: [next(it) for _ in range(n)]
    h_refs = take(cfg.n_h)
    m_ref, g_ref = take(2)
    h = _stream_tile(h_refs, cfg.n0)
    m = m_ref[0]
    gate_mix = m[5:6]
    relayout_scr = refs[-1]
    if cfg.pre == "ab":
        (r_ref,) = take(1)
        y_refs = take(cfg.n_y)
        gw_ref, gb_ref, wr_ref, ws_ref = take(4)
        h = h + gate_mix * jnp.dot(r_ref[...], wr_ref[...], preferred_element_type=F32)
        _from_chunk_rows(_stream_tile(y_refs, cfg.ny0), relayout_scr)
        g = _gelu_tanh(jnp.concatenate([relayout_scr[j] for j in range(relayout_scr.shape[0])], axis=1))
        s = g * jax.nn.sigmoid(jnp.dot(g.astype(BF16), gw_ref[...], preferred_element_type=F32)
                               + gb_ref[...])
        h = h + gate_mix * jnp.dot(s.astype(BF16), ws_ref[...], preferred_element_type=F32)
    elif cfg.pre == "na":
        a_ref, wo_ref = take(2)
        h = h + gate_mix * jnp.dot(a_ref[...], wo_ref[...], preferred_element_type=F32)
    w1a_ref, w1b_ref, w2_ref = take(3)
    k = 1 if cfg.pre else 0
    mi = 6 * k
    xm = _modulated(h, m, g_ref[2 * k], mi).astype(BF16)
    a = jnp.dot(xm, w1a_ref[...], preferred_element_type=F32)
    b = jnp.dot(xm, w1b_ref[...], preferred_element_type=F32)
    hid = (a * jax.nn.sigmoid(a) * b).astype(BF16)
    h = h + (0.5 * m[mi + 2:mi + 3]) * jnp.dot(hid, w2_ref[...], preferred_element_type=F32)
    if cfg.post:
        (wp_ref,) = take(1)
    if cfg.post == "ab":
        cos_ref, sa_ref, sb_ref = take(3)
    if cfg.final:
        (fg_ref,) = take(1)
    outs = list(it)
    if "ab" in (cfg.pre, cfg.post):
        outs.pop()
    outs[0][...] = _rms(h) * fg_ref[...] if cfg.final else h
    if not cfg.post:
        return
    xm = _modulated(h, m, g_ref[1], 3).astype(BF16)
    if cfg.post == "na":
        outs[1][...] = jnp.dot(xm, wp_ref[...], preferred_element_type=F32).astype(BF16)
        return
    width = RET_HEADS * RET_HEAD_DIM
    pr_u = jnp.dot(xm, wp_ref[:, 4 * width:], preferred_element_type=F32)
    for j in range(relayout_scr.shape[0]):
        relayout_scr[j] = pr_u[:, j * 128:(j + 1) * 128]
    _to_chunk_rows(relayout_scr, outs[3])
    pr_qk = jnp.dot(xm, wp_ref[:, :2 * width], preferred_element_type=F32)
    cos, sa, sb = cos_ref[...], sa_ref[...], sb_ref[...]
    for hh in range(RET_HEADS):
        lo = hh * RET_HEAD_DIM
        hi = lo + RET_HEAD_DIM
        outs[1][:, lo:hi] = _rope(pr_qk[:, lo:hi], cos, sa, sb).astype(BF16)
        kr = _rope(pr_qk[:, width + lo:width + hi], cos, sa, sb) * (RET_HEAD_DIM ** -0.5)
        outs[1][:, width + lo:width + hi] = kr.astype(BF16)
    pr_vg = jnp.dot(xm, wp_ref[:, 2 * width:4 * width], preferred_element_type=F32)
    outs[1][:, 2 * width:] = pr_vg[:, :width].astype(BF16)
    outs[2][...] = pr_vg[:, width:]


def _half_layer(h_parts, mod, gains, w1, w2, *, layer, n_rows, lat_tiles, n_batch,
                pre="", pre_args=(), post="", post_w=None, post_wi=0, rope=None, final_g=None):
    d = h_parts[0].shape[1]
    f = w2.shape[2]
    k = 1 if pre else 0
    h_specs, n0 = _stream_specs(h_parts, d)
    _, m_spec = _tile_specs(lat_tiles, n_batch, d)
    in_specs = h_specs + [m_spec, _pinned((None, 3, 1, d), (layer, 0, 0, 0))]
    args = list(h_parts) + [mod, gains]
    n_y = ny0 = 0
    scratch = []
    if pre == "ab":
        r, ys_parts, glu_w, glu_b, w_out, wi = pre_args
        w = r.shape[1]
        y_specs, ny0 = _stream_specs(ys_parts, S5_CHUNK * w, rows=S5_CHUNKS_PER_TILE)
        n_y = len(ys_parts)
        scratch = [pltpu.VMEM((w // 128, TOKEN_TILE, 128), F32)]
        in_specs += [pl.BlockSpec((TOKEN_TILE, w), lambda i: (i, 0))] + y_specs + [
            _pinned((None, w, w), (wi, 0, 0)), _pinned((None, 1, w), (wi, 0, 0)),
            _pinned((None, w, d), (wi, 0, 0)), _pinned((None, w, d), (wi, 1, 0))]
        args += [r, *ys_parts, glu_w, glu_b, w_out, w_out]
    elif pre == "na":
        att, w_o, wi = pre_args
        in_specs += [pl.BlockSpec((TOKEN_TILE, d), lambda i: (i, 0)), _pinned((None, d, d), (wi, 0, 0))]
        args += [att, w_o]
    in_specs += [_pinned((None, None, d, f), (layer, k, 0, 0)),
                 _pinned((None, None, d, f), (layer, k, 0, 1)),
                 _pinned((None, None, f, d), (layer, k, 0, 0))]
    args += [w1, w1, w2]
    out_specs = [pl.BlockSpec((TOKEN_TILE, d), lambda i: (i, 0))]
    out_shape = [jax.ShapeDtypeStruct((n_rows, d), F32)]
    if post:
        n = post_w.shape[2]
        in_specs.append(_pinned((None, d, n), (post_wi, 0, 0)))
        args.append(post_w)
    if post == "ab":
        n_lat_tiles = lat_tiles * n_batch
        tab = pl.BlockSpec((TOKEN_TILE, RET_HEAD_DIM),
                           lambda i: (jnp.where(i < n_lat_tiles, i % lat_tiles, lat_tiles), 0))
        in_specs += [tab, tab, tab]
        args += list(rope)
        width = RET_HEADS * RET_HEAD_DIM
        n_u = n - 4 * width
        out_specs += [pl.BlockSpec((TOKEN_TILE, 3 * width), lambda i: (i, 0)),
                      pl.BlockSpec((TOKEN_TILE, width), lambda i: (i, 0)),
                      pl.BlockSpec((S5_CHUNKS_PER_TILE, S5_CHUNK * n_u), lambda i: (i, 0))]
        out_shape += [jax.ShapeDtypeStruct((n_rows, 3 * width), BF16),
                      jax.ShapeDtypeStruct((n_rows, width), F32),
                      jax.ShapeDtypeStruct((n_rows // S5_CHUNK, S5_CHUNK * n_u), F32)]
        scratch = [pltpu.VMEM((n_u // 128, TOKEN_TILE, 128), F32)]
    elif post == "na":
        out_specs.append(pl.BlockSpec((TOKEN_TILE, n), lambda i: (i, 0)))
        out_shape.append(jax.ShapeDtypeStruct((n_rows, n), BF16))
    if final_g is not None:
        in_specs.append(_resident((1, d)))
        args.append(final_g.reshape(1, d))
    cfg = _HalfCfg(n_h=len(h_parts), n0=n0, pre=pre, n_y=n_y, ny0=ny0, post=post,
                   final=final_g is not None)
    return pl.pallas_call(
        functools.partial(_half_kernel, cfg=cfg),
        grid=(n_rows // TOKEN_TILE,),
        in_specs=in_specs,
        out_specs=out_specs,
        out_shape=out_shape,
        scratch_shapes=scratch,
        compiler_params=_cparams("parallel"),
        name="half_layer",
    )(*args)


def _rope_tables(seq):
    half = RET_HEAD_DIM // 2
    quarter = half // 2
    inv = ROPE_BASE ** (-np.arange(0, half, 2, dtype=np.float64) / half)
    t = np.arange(seq)
    lane = np.arange(RET_HEAD_DIM)
    pos = np.where(lane[None, :] < half, (t // GRID_W)[:, None], (t % GRID_W)[:, None])
    ang = pos.astype(np.float32).astype(np.float64) * inv.astype(np.float32)[lane % quarter][None, :]
    first = (lane % half) < quarter
    cos = np.cos(ang)
    sin = np.sin(ang)
    sa = np.where(first[None, :], -sin, 0.0)
    sb = np.where(first[None, :], 0.0, sin)
    pad1 = np.ones((TOKEN_TILE, RET_HEAD_DIM))
    pad0 = np.zeros((TOKEN_TILE, RET_HEAD_DIM))
    tabs = [np.concatenate([cos, pad1]), np.concatenate([sa, pad0]), np.concatenate([sb, pad0])]
    return [jnp.asarray(x, F32) for x in tabs]


def _rope(x, cos, sa, sb):
    quarter = RET_HEAD_DIM // 4
    up = pltpu.roll(x, RET_HEAD_DIM - quarter, axis=1)
    dn = pltpu.roll(x, quarter, axis=1)
    return x * cos + up * sa + dn * sb


def _retention_kernel(dec_ref, ql_ref, kl_ref, vl_ref, qc_ref, kc_ref, vc_ref, g_ref,
                      dmask_ref, qwf_ref, qwb_ref, kwf_ref, kwb_ref, o_ref,
                      s_run, s_bwd, *, n_lat, n_ctx):
    i = pl.program_id(1)
    hd = RET_HEAD_DIM
    c = RET_CHUNK
    tn = (((0,), (0,)), ((), ()))
    nt = (((1,), (1,)), ((), ()))

    def bwd_chunk(k_ref, v_ref, rows, cid):
        for h in range(RET_HEADS):
            sl = slice(h * hd, (h + 1) * hd)
            s_old = s_run[h]
            s_bwd[cid, h] = s_old
            kw = (k_ref[rows, sl] * kwb_ref[h]).astype(BF16)
            kv = lax.dot_general(kw, v_ref[rows, sl], tn, preferred_element_type=F32)
            s_run[h] = dec_ref[1, h] * s_old + kv

    def fwd_chunk(q_ref, k_ref, v_ref, rows, cid):
        for h in range(RET_HEADS):
            sl = slice(h * hd, (h + 1) * hd)
            q = q_ref[rows, sl]
            k = k_ref[rows, sl]
            v = v_ref[rows, sl]
            s_old = s_run[h]
            a = lax.dot_general(q, k, nt, preferred_element_type=F32)
            o = jnp.dot((a * dmask_ref[h]).astype(BF16), v, preferred_element_type=F32)
            o += qwf_ref[h] * jnp.dot(q, s_old.astype(BF16), preferred_element_type=F32)
            o += qwb_ref[h] * jnp.dot(q, s_bwd[cid, h].astype(BF16), preferred_element_type=F32)
            kv = lax.dot_general((k * kwf_ref[h]).astype(BF16), v, tn, preferred_element_type=F32)
            s_run[h] = dec_ref[0, h] * s_old + kv
            o = o * lax.rsqrt(jnp.mean(o * o, axis=-1, keepdims=True) + EPS)
            g = g_ref[:, sl]
            o_ref[:, sl] = (o * (g * jax.nn.sigmoid(g))).astype(o_ref.dtype)

    def lat_rows(j):
        return pl.ds(pl.multiple_of(j * c, c), c)

    @pl.when(i == 0)
    def _():
        s_run[...] = jnp.zeros_like(s_run)
        for j in reversed(range(n_ctx)):
            bwd_chunk(kc_ref, vc_ref, slice(j * c, (j + 1) * c), j)

        def body(t, carry):
            j = n_lat - 1 - t
            bwd_chunk(kl_ref, vl_ref, lat_rows(j), n_ctx + j)
            return carry

        lax.fori_loop(0, n_lat, body, 0)
        s_run[...] = jnp.zeros_like(s_run)

    for j in range(n_ctx):
        @pl.when(i == 1 + j)
        def _(j=j):
            fwd_chunk(qc_ref, kc_ref, vc_ref, slice(j * c, (j + 1) * c), j)

    @pl.when(i > n_ctx)
    def _():
        fwd_chunk(ql_ref, kl_ref, vl_ref, lat_rows(i - 1 - n_ctx), i - 1)


def _retention(pq, pg, decay_logit, *, n_batch, seq, ctx_len):
    c = RET_CHUNK
    width = RET_HEADS * RET_HEAD_DIM
    n_lat = seq // c
    n_ctx = ctx_len // c
    n_ch = n_lat + n_ctx
    n_rows = pq.shape[0]
    lat_blocks = n_batch * n_lat

    log_gamma = jax.nn.log_sigmoid(decay_logit.astype(F32))
    pos = jnp.arange(c, dtype=F32)
    diff = pos[:, None] - pos[None, :]
    lf = log_gamma[0][:, None, None]
    lb = log_gamma[1][:, None, None]
    dmask = (jnp.where(diff >= 0, jnp.exp(lf * jnp.maximum(diff, 0.0)), 0.0)
             + jnp.where(diff <= 0, jnp.exp(lb * jnp.maximum(-diff, 0.0)), 0.0))
    ones = jnp.ones((1, 1, RET_HEAD_DIM), F32)
    col = pos[None, :, None]
    qwf = jnp.exp(lf * (col + 1.0)) * ones
    qwb = jnp.exp(lb * (c - col)) * ones
    kwf = jnp.exp(lf * (c - 1.0 - col)) * ones
    kwb = jnp.exp(lb * col) * ones
    chunk_decay = jnp.exp(log_gamma * c)

    ctx_blk0 = n_batch * seq // ctx_len

    def chunk_block(b, i):
        cid = jnp.maximum(i - 1, 0)
        return (jnp.where(cid < n_ctx, lat_blocks + b * n_ctx + cid, b * n_lat + cid - n_ctx), 0)

    chunk_spec = pl.BlockSpec((c, width), chunk_block)
    lat = lambda part: pl.BlockSpec((seq, width), lambda b, i: (b, part))
    ctx = lambda part: pl.BlockSpec((ctx_len, width), lambda b, i: (ctx_blk0 + b, part))
    head_tab = _resident((RET_HEADS, c, RET_HEAD_DIM))
    kern = functools.partial(_retention_kernel, n_lat=n_lat, n_ctx=n_ctx)
    return pl.pallas_call(
        kern,
        grid=(n_batch, 1 + n_ch),
        in_specs=[
            pl.BlockSpec(memory_space=pltpu.SMEM),
            lat(0), lat(1), lat(2), ctx(0), ctx(1), ctx(2), chunk_spec,
            _resident((RET_HEADS, c, c)), head_tab, head_tab, head_tab, head_tab,
        ],
        out_specs=chunk_spec,
        out_shape=jax.ShapeDtypeStruct((n_rows, width), BF16),
        scratch_shapes=[
            pltpu.VMEM((RET_HEADS, RET_HEAD_DIM, RET_HEAD_DIM), F32),
            pltpu.VMEM((n_ch, RET_HEADS, RET_HEAD_DIM, RET_HEAD_DIM), F32),
        ],
        compiler_params=_cparams("parallel", "arbitrary"),
        name="retention",
    )(chunk_decay, pq, pq, pq, pq, pq, pq, pg, dmask, qwf, qwb, kwf, kwb)


def _cmul(a, b):
    return a[0] * b[0] - a[1] * b[1], a[0] * b[1] + a[1] * b[0]


def _pair_blockdiag(a):
    a0, a1 = a[0::2], a[1::2]
    z = jnp.zeros_like(a0)
    return jnp.concatenate([jnp.concatenate([a0, z], axis=2), jnp.concatenate([z, a1], axis=2)],
                           axis=1)


def _pair_lanes(a):
    return jnp.concatenate([a[0::2], a[1::2]], axis=2)


def _powers(a, n):
    one = (jnp.ones_like(a[0]), jnp.zeros_like(a[0]))
    pw = (jnp.stack([one[0], a[0]], axis=1), jnp.stack([one[1], a[1]], axis=1))
    step = a
    while pw[0].shape[1] < n:
        step = _cmul(step, step)
        nxt = _cmul(pw, (step[0][:, None], step[1][:, None]))
        pw = (jnp.concatenate([pw[0], nxt[0]], axis=1), jnp.concatenate([pw[1], nxt[1]], axis=1))
    top = _cmul(step, step)
    return (jnp.concatenate([pw[0], top[0][:, None]], axis=1),
            jnp.concatenate([pw[1], top[1][:, None]], axis=1))


def _s5_weights(lam_re, lam_im, log_dt, b_re, b_im, c_re, c_im, d_skip):
    q = S5_CHUNK
    n_g = lam_re.shape[1]
    hi = lax.Precision.HIGHEST
    kall = None
    zw = []
    rw = []
    tabs = []
    for d in range(2):
        lr = jnp.minimum(lam_re[d].astype(F32), -1e-4)
        li = lam_im[d].astype(F32)
        dt = jnp.exp(log_dt[d].astype(F32))[:, None]
        mag = jnp.exp(lr * dt)
        a = (mag * jnp.cos(li * dt), mag * jnp.sin(li * dt))
        den = lr * lr + li * li
        am1 = (a[0] - 1.0, a[1])
        coef = ((am1[0] * lr + am1[1] * li) / den, (am1[1] * lr - am1[0] * li) / den)
        bt = (b_re[d].astype(F32).transpose(0, 2, 1), b_im[d].astype(F32).transpose(0, 2, 1))
        bb = _cmul((coef[0][:, None], coef[1][:, None]), bt)
        cm = (c_re[d].astype(F32), c_im[d].astype(F32))
        pw_re, pw_im = _powers(a, q)
        e = _cmul((pw_re[:, :q, None, :], pw_im[:, :q, None, :]),
                  (bb[0][:, None], bb[1][:, None]))
        taps = (jnp.einsum('gkp,gtjp->gjtk', cm[0], e[0], precision=hi)
                - jnp.einsum('gkp,gtjp->gjtk', cm[1], e[1], precision=hi))
        if d == 0:
            kall = jnp.concatenate([jnp.zeros_like(taps[:, :, 1:]), taps], axis=2)
        else:
            kall = kall + jnp.concatenate([taps[:, :, ::-1], jnp.zeros_like(taps[:, :, 1:])], axis=2)
        tsel = np.arange(q)[::-1] if d == 0 else np.arange(q)
        z = _cmul((pw_re[:, tsel][:, :, None, :], pw_im[:, tsel][:, :, None, :]),
                  (bb[0][:, None], bb[1][:, None]))
        zw.append((z[0].reshape(n_g, q * S5_GROUP, S5_STATE),
                   z[1].reshape(n_g, q * S5_GROUP, S5_STATE)))
        rsel = np.arange(1, q + 1) if d == 0 else np.arange(q, 0, -1)
        ct = (cm[0].transpose(0, 2, 1)[:, :, None, :], cm[1].transpose(0, 2, 1)[:, :, None, :])
        pt = (pw_re[:, rsel].transpose(0, 2, 1)[..., None], pw_im[:, rsel].transpose(0, 2, 1)[..., None])
        r = _cmul(ct, pt)
        rw.append((r[0].reshape(n_g, S5_STATE, q * S5_GROUP),
                   (-r[1]).reshape(n_g, S5_STATE, q * S5_GROUP)))
        a16 = (pw_re[:, q], pw_im[:, q])
        ramp_re, ramp_im = _powers(a16, S5_TILE)
        a32 = (ramp_re[:, 2], ramp_im[:, 2])
        a64 = (ramp_re[:, 4], ramp_im[:, 4])
        a128 = (ramp_re[:, 8], ramp_im[:, 8])
        rsl = slice(0, S5_TILE) if d == 0 else slice(S5_TILE - 1, None, -1)
        rows = [a16[0], a16[1], a32[0], a32[1], a64[0], a64[1], a128[0], a128[1]]
        tabs.append(jnp.concatenate([jnp.stack(rows, axis=1), ramp_re[:, rsl], ramp_im[:, rsl]],
                                    axis=1))
    dsk = d_skip.astype(F32).reshape(n_g, S5_GROUP)
    eye = jnp.eye(S5_GROUP, dtype=F32)
    kall = kall.at[:, :, q - 1].add(dsk[:, :, None] * eye[None])
    kall = jnp.concatenate([kall, jnp.zeros_like(kall[:, :, :1])], axis=2)
    kflat = kall.reshape(n_g, S5_GROUP, 2 * q * S5_GROUP)
    wz = jnp.concatenate([_pair_blockdiag(z) for pair in zw for z in pair], axis=2)
    w2 = jnp.concatenate([_pair_blockdiag(r) for pair in rw for r in pair], axis=1)
    tab = jnp.concatenate([_pair_lanes(t) for t in tabs], axis=1)
    return kflat, wz.astype(BF16), w2.astype(BF16), tab


def _tile_shift(x, s, down):
    row = lax.broadcasted_iota(jnp.int32, x.shape, 1)
    if down:
        return jnp.where(row >= s, pltpu.roll(x, s, axis=1), 0.0)
    return jnp.where(row < S5_TILE - s, pltpu.roll(x, S5_TILE - s, axis=1), 0.0)


S5_LANE_GROUPS = 128 // S5_GROUP


def _block_transpose(arrs):
    lane = lax.broadcasted_iota(jnp.int32, arrs[0].shape, 1)
    blk = lane // S5_GROUP
    a = list(arrs)
    for bit in range(3):
        s = 1 << bit
        hi = (blk & s) != 0
        new = list(a)
        for i in range(S5_LANE_GROUPS):
            if i & s:
                continue
            new[i] = jnp.where(hi, pltpu.roll(a[i + s], S5_GROUP * s, axis=1), a[i])
            new[i + s] = jnp.where(hi, a[i + s], pltpu.roll(a[i], 128 - S5_GROUP * s, axis=1))
        a = new
    return a


def _s5_kernel(vl_ref, vc_ref, kf_ref, wz_ref, w2_ref, tab_ref, yl_ref, yc_ref, m_scr, zx, xin,
               *, lat_tiles, ctx_tiles):
    q = S5_CHUNK
    lanes = 2 * S5_STATE
    pw = 2 * q * S5_GROUP
    gw = q * S5_GROUP

    @pl.when(pl.program_id(1) == 0)
    def _():
        m_scr[...] = jnp.zeros_like(m_scr)
        for g in range(S5_LANE_GROUPS):
            taps = kf_ref[g]
            base = (g % 2) * gw
            for ti in range(q):
                lo = S5_GROUP * (q - 1 - ti)
                m_scr[g // 2, base + ti * S5_GROUP:base + (ti + 1) * S5_GROUP, base:base + gw] = (
                    taps[:, lo:lo + gw].astype(BF16))

    n_tiles = ctx_tiles + lat_tiles
    ctx_rows = ctx_tiles * S5_TILE

    for p in range(m_scr.shape[0]):
        cols = slice(p * pw, (p + 1) * pw)
        v = jnp.concatenate([vc_ref[:, cols], vl_ref[:, cols]], axis=0).astype(BF16)
        y_intra = jnp.dot(v, m_scr[p], preferred_element_type=F32)
        y = jnp.dot(v, wz_ref[p], preferred_element_type=F32)

        def tab_row(r):
            return tab_ref[p, r:r + 1, :]

        for d in range(2):
            down = d == 0
            t0 = 24 * d
            lo = 2 * d * lanes
            z = (y[:, lo:lo + lanes].reshape(n_tiles, S5_TILE, lanes),
                 y[:, lo + lanes:lo + 2 * lanes].reshape(n_tiles, S5_TILE, lanes))
            loc = (_tile_shift(z[0], 1, down), _tile_shift(z[1], 1, down))
            for k, s in enumerate((1, 2, 4)):
                mul = (tab_row(t0 + 2 * k)[None], tab_row(t0 + 2 * k + 1)[None])
                inc = _cmul(mul, (_tile_shift(loc[0], s, down), _tile_shift(loc[1], s, down)))
                loc = (loc[0] + inc[0], loc[1] + inc[1])
            zx[2 * d] = loc[0]
            zx[2 * d + 1] = loc[1]
            zx[4 + 2 * d] = z[0]
            zx[4 + 2 * d + 1] = z[1]

        for d in range(2):
            t0 = 24 * d
            edge = S5_TILE - 1 if d == 0 else 0
            a16 = (tab_row(t0), tab_row(t0 + 1))
            a128 = (tab_row(t0 + 6), tab_row(t0 + 7))
            ramp = (tab_ref[p, t0 + 8:t0 + 16, :], tab_ref[p, t0 + 16:t0 + 24, :])

            def tile_step(j, carry, d=d, edge=edge, a16=a16, a128=a128, ramp=ramp):
                loc = (zx[2 * d, j], zx[2 * d + 1, j])
                z = (zx[4 + 2 * d, j], zx[4 + 2 * d + 1, j])
                inc = _cmul(ramp, carry)
                xin[j, :, (2 * d) * lanes:(2 * d + 1) * lanes] = loc[0] + inc[0]
                xin[j, :, (2 * d + 1) * lanes:(2 * d + 2) * lanes] = loc[1] + inc[1]
                e_loc = _cmul(a16, (loc[0][edge:edge + 1], loc[1][edge:edge + 1]))
                nxt = _cmul(a128, carry)
                return (nxt[0] + e_loc[0] + z[0][edge:edge + 1],
                        nxt[1] + e_loc[1] + z[1][edge:edge + 1])

            zero = (jnp.zeros((1, lanes), F32), jnp.zeros((1, lanes), F32))
            if d == 0:
                lax.fori_loop(0, n_tiles, tile_step, zero)
            else:
                mid = lax.fori_loop(0, ctx_tiles, lambda t, c: tile_step(ctx_tiles - 1 - t, c), zero)
                lax.fori_loop(0, lat_tiles, lambda t, c: tile_step(n_tiles - 1 - t, c), mid)

        x = xin[...].reshape(n_tiles * S5_TILE, 4 * lanes)
        y_all = y_intra + jnp.dot(x.astype(BF16), w2_ref[p], preferred_element_type=F32)
        yc_ref[:, cols] = y_all[:ctx_rows]
        yl_ref[:, cols] = y_all[ctx_rows:]


def _s5(v, weights, *, n_batch, seq, ctx_len):
    kflat, wz, w2, tab = weights
    q = S5_CHUNK
    lanes = 2 * S5_STATE
    pw = 2 * q * S5_GROUP
    n_pairs = w2.shape[0]
    ppb = S5_LANE_GROUPS // 2
    n_blocks = n_pairs // ppb
    bw = ppb * pw
    lat_rows = seq // q
    ctx_rows = ctx_len // q
    n_tiles = (lat_rows + ctx_rows) // S5_TILE
    ctx_blk0 = n_batch * lat_rows // ctx_rows
    kern = functools.partial(_s5_kernel, lat_tiles=lat_rows // S5_TILE, ctx_tiles=ctx_rows // S5_TILE)
    return pl.pallas_call(
        kern,
        grid=(n_blocks, n_batch),
        in_specs=[
            pl.BlockSpec((lat_rows, bw), lambda j, b: (b, j)),
            pl.BlockSpec((ctx_rows, bw), lambda j, b: (ctx_blk0 + b, j)),
            pl.BlockSpec((S5_LANE_GROUPS,) + kflat.shape[1:], lambda j, b: (j, 0, 0)),
            pl.BlockSpec((ppb, pw, 4 * lanes), lambda j, b: (j, 0, 0)),
            pl.BlockSpec((ppb, 4 * lanes, pw), lambda j, b: (j, 0, 0)),
            pl.BlockSpec((ppb, tab.shape[1], lanes), lambda j, b: (j, 0, 0)),
        ],
        out_specs=[pl.BlockSpec((lat_rows, bw), lambda j, b: (b, j)),
                   pl.BlockSpec((ctx_rows, bw), lambda j, b: (b, j))],
        out_shape=[jax.ShapeDtypeStruct((n_batch * lat_rows, n_blocks * bw), F32),
                   jax.ShapeDtypeStruct((n_batch * ctx_rows, n_blocks * bw), F32)],
        scratch_shapes=[
            pltpu.VMEM((ppb, pw, pw), BF16),
            pltpu.VMEM((8, n_tiles, S5_TILE, lanes), F32),
            pltpu.VMEM((n_tiles, S5_TILE, 4 * lanes), F32),
        ],
        compiler_params=_cparams("arbitrary", "arbitrary"),
        name="s5",
    )(v, v, kflat, wz, w2, tab)


NA_QROWS = 4
NA_KROWS = NA_QROWS + NA_KH


NA_REL_ROWS = 2 * NA_KH - 1


def _na_tile_index(kind, a, m):
    first, rel0 = ((0, NA_KH - 1 - a), (a, NA_QROWS - 1 - a), (NA_QROWS, -1 - a))[kind]
    return rel0 + m if first <= m < first + NA_KH else NA_REL_ROWS


def _na_bias_tiles(rpb):
    w = GRID_W
    qcol = np.arange(w)
    kcol = np.arange(w)
    wstart = np.clip(qcol - NA_KW // 2, 0, w - NA_KW)
    valid = (kcol[None, :] >= wstart[:, None]) & (kcol[None, :] < wstart[:, None] + NA_KW)
    rel = np.clip(kcol[None, :] - qcol[:, None], -(NA_KW - 1), NA_KW - 1) + NA_KW - 1
    onehot = (rel[None] == np.arange(2 * NA_KW - 1)[:, None, None]).astype(np.float32)
    tiles = jnp.einsum('hrj,jqk->hrqk', rpb.astype(F32), jnp.asarray(onehot),
                       precision=lax.Precision.HIGHEST)
    tiles = jnp.where(jnp.asarray(valid)[None, None], tiles, NEG_INF)
    return jnp.concatenate([tiles, jnp.full((NA_HEADS, 1, w, w), NEG_INF, F32)], axis=1)


def _na_kernel(q_ref, k_ref, v_ref, kc_ref, vc_ref, tiles_ref, o_ref, bias_ref, vx, vcx, s_even, s_odd,
               p_even, p_odd, *, rows):
    w = GRID_W
    dh = NA_HEAD_DIM
    lb = 2 * dh
    nq = NA_QROWS * w
    nk = NA_KROWS * w
    n_blocks = rows // NA_QROWS

    @pl.when(pl.program_id(1) == 0)
    def _():
        for kind in range(3):
            for hh in range(2):
                for a in range(NA_QROWS):
                    for m in range(0, NA_KROWS, 2):
                        pair = [tiles_ref[hh, _na_tile_index(kind, a, m + e)] for e in range(2)]
                        bias_ref[kind, hh * nq + a * w:hh * nq + (a + 1) * w, m * w:(m + 2) * w] = (
                            jnp.concatenate(pair, axis=1))

    nt = (((1,), (1,)), ((), ()))
    scale = dh ** -0.5
    kc = kc_ref[...]
    lane = lax.broadcasted_iota(jnp.int32, (nq, lb), 1)
    first = lane < dh

    for dst, src in ((vx, v_ref), (vcx, vc_ref)):
        n = src.shape[0]
        dst[:, :lb] = src[...]
        dst[:, lb:] = (lax.broadcasted_iota(jnp.int32, (n, lb), 1) == 0).astype(BF16)

    def key_offset(i):
        r0 = jnp.clip(NA_QROWS * i - NA_KH // 2, 0, rows - NA_KROWS)
        return pl.multiple_of(r0 * w, NA_QROWS * w)

    def scores(i, s_ref):
        kind = jnp.where(i == 0, 0, jnp.where(i == n_blocks - 1, 2, 1))
        q = q_ref[pl.ds(pl.multiple_of(i * nq, nq), nq), :] * scale
        zero = jnp.zeros_like(q)
        qs = jnp.concatenate([jnp.where(first, q, zero), jnp.where(first, zero, q)], axis=0)
        kl = k_ref[pl.ds(key_offset(i), nk), :]
        s_ref[:, :nk] = lax.dot_general(qs, kl, nt, preferred_element_type=F32) + bias_ref[kind]
        s_ref[:, nk:] = lax.dot_general(qs, kc, nt, preferred_element_type=F32)

    def softmax(s_ref, p_ref):
        s = s_ref[...]
        p_ref[...] = jnp.exp(s - jnp.max(s, axis=-1, keepdims=True)).astype(BF16)

    def attend(i, p_ref):
        o = jnp.dot(p_ref[:, :nk], vx[pl.ds(key_offset(i), nk), :], preferred_element_type=F32)
        o += jnp.dot(p_ref[:, nk:], vcx[...], preferred_element_type=F32)
        o = o[:, :lb] / o[:, lb:lb + 1]
        o_ref[pl.ds(pl.multiple_of(i * nq, nq), nq), :] = jnp.where(first, o[:nq], o[nq:]).astype(
            o_ref.dtype)

    scores(0, s_even)
    softmax(s_even, p_even)
    scores(1, s_odd)

    def body(j, carry):
        attend(2 * j - 2, p_even)
        softmax(s_odd, p_odd)
        scores(2 * j, s_even)
        attend(2 * j - 1, p_odd)
        softmax(s_even, p_even)
        scores(2 * j + 1, s_odd)
        return carry

    lax.fori_loop(1, n_blocks // 2, body, 0, unroll=True)
    attend(n_blocks - 2, p_even)
    softmax(s_odd, p_odd)
    attend(n_blocks - 1, p_odd)


def _natten(p, tiles, *, n_batch, seq, ctx_len):
    d = NA_HEADS * NA_HEAD_DIM
    lb = 2 * NA_HEAD_DIM
    n_pairs = NA_HEADS // 2
    ctx_blk0 = n_batch * seq // ctx_len
    rows = seq // GRID_W
    assert rows % (2 * NA_QROWS) == 0 and rows >= NA_KROWS + NA_QROWS
    stacked = 2 * NA_QROWS * GRID_W
    n_keys = NA_KROWS * GRID_W + ctx_len
    lat = lambda part: pl.BlockSpec((seq, lb), lambda j, b: (b, part * n_pairs + j))
    ctx = lambda part: pl.BlockSpec((ctx_len, lb), lambda j, b: (ctx_blk0 + b, part * n_pairs + j))
    return pl.pallas_call(
        functools.partial(_na_kernel, rows=rows),
        grid=(n_pairs, n_batch),
        in_specs=[lat(0), lat(1), lat(2), ctx(1), ctx(2),
                  pl.BlockSpec((2,) + tiles.shape[1:], lambda j, b: (j, 0, 0, 0))],
        out_specs=pl.BlockSpec((seq, lb), lambda j, b: (b, j)),
        out_shape=jax.ShapeDtypeStruct((n_batch * seq, d), BF16),
        scratch_shapes=[pltpu.VMEM((3, stacked, NA_KROWS * GRID_W), F32),
                        pltpu.VMEM((seq, 2 * lb), BF16), pltpu.VMEM((ctx_len, 2 * lb), BF16),
                        pltpu.VMEM((stacked, n_keys), F32), pltpu.VMEM((stacked, n_keys), F32),
                        pltpu.VMEM((stacked, n_keys), BF16), pltpu.VMEM((stacked, n_keys), BF16)],
        compiler_params=_cparams("arbitrary", "arbitrary"),
        name="natten",
    )(p, p, p, p, p, tiles)


def kernel(x, c, ctx, c_ctx, w_mod, b_mod, norm_g, ffn_w1, ffn_w2, w_in_ab, w_out_ab, ret_decay_logit, s5_lam_re, s5_lam_im, s5_log_dt, s5_b_re, s5_b_im, s5_c_re, s5_c_im, s5_d, s5_glu_w, s5_glu_b, na_w_qkv, na_w_o, na_rpb, final_g):
    n_batch, seq, d = x.shape
    ctx_len = ctx.shape[1]
    depth = w_mod.shape[0]
    n_lat = n_batch * seq
    n_all = n_lat + n_batch * ctx_len
    lat_tiles = seq // TOKEN_TILE
    assert seq % TOKEN_TILE == 0 and (n_batch * ctx_len) % TOKEN_TILE == 0
    assert n_batch + 1 <= MOD_ROWS and seq % (GRID_W * NA_KH) == 0

    cvec = jnp.concatenate([c, c_ctx[None], jnp.zeros((MOD_ROWS - n_batch - 1, d), F32)], axis=0)
    mod = _modulation(cvec, w_mod, b_mod).reshape(depth, MOD_ROWS, N_MOD, d)
    h_parts = (x.reshape(n_lat, d), ctx.reshape(n_batch * ctx_len, d))
    common = dict(lat_tiles=lat_tiles, n_batch=n_batch)
    dims = dict(n_batch=n_batch, seq=seq, ctx_len=ctx_len)
    gains = norm_g.astype(F32).reshape(depth, 3, 1, d)
    w1 = ffn_w1.astype(BF16)
    w2 = ffn_w2.astype(BF16)

    for layer in range(depth):
        last = layer == depth - 1
        i = layer // 2
        half = functools.partial(_half_layer, mod=mod[layer], gains=gains, w1=w1, w2=w2, layer=layer,
                                 **common)
        if layer % 2 == 0:
            h, pq, pg, pv = half(h_parts, n_rows=n_all, post="ab", post_w=w_in_ab.astype(BF16),
                                 post_wi=i, rope=_rope_tables(seq))
            r = _retention(pq, pg, ret_decay_logit[i], **dims)
            weights = _s5_weights(s5_lam_re[i], s5_lam_im[i], s5_log_dt[i], s5_b_re[i], s5_b_im[i],
                                  s5_c_re[i], s5_c_im[i], s5_d[i])
            ys_parts = _s5(pv, weights, **dims)
            pre = dict(pre="ab", pre_args=(r, ys_parts, s5_glu_w.astype(BF16),
                                           s5_glu_b.astype(F32)[:, None, :], w_out_ab.astype(BF16), i))
        else:
            assert last
            h, p = half(h_parts, n_rows=n_all, post="na", post_w=na_w_qkv.astype(BF16), post_wi=i)
            att = _natten(p, _na_bias_tiles(na_rpb[i]), **dims)
            pre = dict(pre="na", pre_args=(att, na_w_o.astype(BF16), i))
        (h,) = half((h,), n_rows=n_lat if last else n_all, final_g=final_g if last else None, **pre)
        h_parts = (h,)
    return h[:n_lat].reshape(n_batch, seq, d)
```

```python
import functools
import math
from typing import NamedTuple

import numpy as np
import jax
import jax.numpy as jnp
from jax import lax
from jax.experimental import pallas as pl
from jax.experimental.pallas import tpu as pltpu

F32 = jnp.float32
BF16 = jnp.bfloat16

EPS = 1e-6
ROPE_BASE = 10000.0
GRID_W = 64
N_MOD = 9
RET_HEADS = 4
RET_HEAD_DIM = 128
RET_CHUNK = 256
S5_GROUP = 16
S5_STATE = 64
S5_CHUNK = 16
S5_TILE = 8
NA_HEADS = 16
NA_HEAD_DIM = 64
NA_KH = 8
NA_KW = 16
NEG_INF = -1e30

TOKEN_TILE = 512
VMEM_LIMIT = 56 * 1024 * 1024
MOD_ROWS = 8


def _cparams(*sem):
    return pltpu.CompilerParams(dimension_semantics=sem, vmem_limit_bytes=VMEM_LIMIT)


def _resident(shape):
    nd = len(shape)
    return pl.BlockSpec(shape, lambda *_: (0,) * nd, pipeline_mode=pl.Buffered(1))


def _mod_kernel(c_ref, w_ref, b_ref, o_ref):
    c = c_ref[...]
    s = c * jax.nn.sigmoid(c)
    o_ref[0] = jnp.dot(s, w_ref[0], preferred_element_type=F32,
                       precision=lax.Precision.HIGHEST) + b_ref[0]


def _modulation(cvec, w_mod, b_mod):
    depth, d, nd = w_mod.shape
    tn = nd // 4 if nd % 512 == 0 else d
    return pl.pallas_call(
        _mod_kernel,
        grid=(depth, nd // tn),
        in_specs=[
            pl.BlockSpec((MOD_ROWS, d), lambda l, j: (0, 0)),
            pl.BlockSpec((1, d, tn), lambda l, j: (l, 0, j)),
            pl.BlockSpec((1, 1, tn), lambda l, j: (l, 0, j)),
        ],
        out_specs=pl.BlockSpec((1, MOD_ROWS, tn), lambda l, j: (l, 0, j)),
        out_shape=jax.ShapeDtypeStruct((depth, MOD_ROWS, nd), F32),
        compiler_params=_cparams("parallel", "parallel"),
        name="modulation",
    )(cvec, w_mod, b_mod.reshape(depth, 1, nd))


def _rms(x):
    return x * lax.rsqrt(jnp.mean(x * x, axis=-1, keepdims=True) + EPS)


def _modulated(h, m, g, mi):
    return (_rms(h) * g) * (1.0 + m[mi + 1:mi + 2]) + m[mi:mi + 1]


def _tile_specs(n_lat_tiles_per_batch, n_batch, d):
    def mod_idx(i):
        return (jnp.minimum(i // n_lat_tiles_per_batch, n_batch), 0, 0)
    h_spec = pl.BlockSpec((TOKEN_TILE, d), lambda i: (i, 0))
    m_spec = pl.BlockSpec((1, N_MOD, d), mod_idx)
    return h_spec, m_spec


def _pinned(block_shape, index):
    return pl.BlockSpec(block_shape, lambda *_: index, pipeline_mode=pl.Buffered(1))


def _stream_specs(parts, width, rows=TOKEN_TILE):
    if len(parts) == 1:
        return [pl.BlockSpec((rows, width), lambda i: (i, 0))], 0
    n0 = parts[0].shape[0] // rows
    return [pl.BlockSpec((rows, width), lambda i: (jnp.minimum(i, n0 - 1), 0)),
            pl.BlockSpec((rows, width), lambda i: (jnp.maximum(i - n0, 0), 0))], n0


def _stream_tile(refs, n0):
    if len(refs) == 1:
        return refs[0][...]
    return jnp.where(pl.program_id(0) < n0, refs[0][...], refs[1][...])


class _HalfCfg(NamedTuple):
    n_h: int
    n0: int
    pre: str
    n_y: int
    ny0: int
    post: str
    final: bool


def _gelu_tanh(y):
    return 0.5 * y * (1.0 + jnp.tanh(math.sqrt(2.0 / math.pi) * (y + 0.044715 * (y * y * y))))


S5_CHUNKS_PER_TILE = TOKEN_TILE // S5_CHUNK


def _to_chunk_rows(u_scr, out_ref):
    half = 128 // S5_GROUP
    gw = S5_CHUNK * S5_GROUP
    for blk in range(u_scr.shape[0]):
        for th in range(S5_CHUNK // half):
            arrs = [u_scr[blk, pl.ds(th * half + tl, S5_CHUNKS_PER_TILE, stride=S5_CHUNK), :]
                    for tl in range(half)]
            for g, x in enumerate(_block_transpose(arrs)):
                lo = (blk * half + g) * gw + th * 128
                out_ref[:, lo:lo + 128] = x


def _from_chunk_rows(y, y_scr):
    half = 128 // S5_GROUP
    gw = S5_CHUNK * S5_GROUP
    for blk in range(y_scr.shape[0]):
        for th in range(S5_CHUNK // half):
            arrs = [y[:, (blk * half + g) * gw + th * 128:(blk * half + g) * gw + (th + 1) * 128]
                    for g in range(half)]
            for tl, x in enumerate(_block_transpose(arrs)):
                y_scr[blk, pl.ds(th * half + tl, S5_CHUNKS_PER_TILE, stride=S5_CHUNK), :] = x


def _half_kernel(*refs, cfg):
    it = iter(refs)
    take = lambda n: [next(it) for _ in range(n)]
    h_refs = take(cfg.n_h)
    m_ref, g_ref = take(2)
    h = _stream_tile(h_refs, cfg.n0)
    m = m_ref[0]
    gate_mix = m[5:6]
    relayout_scr = refs[-1]
    if cfg.pre == "ab":
        (r_ref,) = take(1)
        y_refs = take(cfg.n_y)
        gw_ref, gb_ref, wr_ref, ws_ref = take(4)
        h = h + gate_mix * jnp.dot(r_ref[...], wr_ref[...], preferred_element_type=F32)
        _from_chunk_rows(_stream_tile(y_refs, cfg.ny0), relayout_scr)
        g = _gelu_tanh(jnp.concatenate([relayout_scr[j] for j in range(relayout_scr.shape[0])], axis=1))
        s = g * jax.nn.sigmoid(jnp.dot(g.astype(BF16), gw_ref[...], preferred_element_type=F32)
                               + gb_ref[...])
        h = h + gate_mix * jnp.dot(s.astype(BF16), ws_ref[...], preferred_element_type=F32)
    elif cfg.pre == "na":
        a_ref, wo_ref = take(2)
        h = h + gate_mix * jnp.dot(a_ref[...], wo_ref[...], preferred_element_type=F32)
    w1a_ref, w1b_ref, w2_ref = take(3)
    k = 1 if cfg.pre else 0
    mi = 6 * k
    xm = _modulated(h, m, g_ref[2 * k], mi).astype(BF16)
    a = jnp.dot(xm, w1a_ref[...], preferred_element_type=F32)
    b = jnp.dot(xm, w1b_ref[...], preferred_element_type=F32)
    hid = (a * jax.nn.sigmoid(a) * b).astype(BF16)
    h = h + (0.5 * m[mi + 2:mi + 3]) * jnp.dot(hid, w2_ref[...], preferred_element_type=F32)
    if cfg.post:
        (wp_ref,) = take(1)
    if cfg.post == "ab":
        cos_ref, sa_ref, sb_ref = take(3)
    if cfg.final:
        (fg_ref,) = take(1)
    outs = list(it)
    if "ab" in (cfg.pre, cfg.post):
        outs.pop()
    outs[0][...] = _rms(h) * fg_ref[...] if cfg.final else h
    if not cfg.post:
        return
    xm = _modulated(h, m, g_ref[1], 3).astype(BF16)
    if cfg.post == "na":
        outs[1][...] = jnp.dot(xm, wp_ref[...], preferred_element_type=F32).astype(BF16)
        return
    width = RET_HEADS * RET_HEAD_DIM
    pr_u = jnp.dot(xm, wp_ref[:, 4 * width:], preferred_element_type=F32)
    for j in range(relayout_scr.shape[0]):
        relayout_scr[j] = pr_u[:, j * 128:(j + 1) * 128]
    _to_chunk_rows(relayout_scr, outs[3])
    pr_qk = jnp.dot(xm, wp_ref[:, :2 * width], preferred_element_type=F32)
    cos, sa, sb = cos_ref[...], sa_ref[...], sb_ref[...]
    for hh in range(RET_HEADS):
        lo = hh * RET_HEAD_DIM
        hi = lo + RET_HEAD_DIM
        outs[1][:, lo:hi] = _rope(pr_qk[:, lo:hi], cos, sa, sb).astype(BF16)
        kr = _rope(pr_qk[:, width + lo:width + hi], cos, sa, sb) * (RET_HEAD_DIM ** -0.5)
        outs[1][:, width + lo:width + hi] = kr.astype(BF16)
    pr_vg = jnp.dot(xm, wp_ref[:, 2 * width:4 * width], preferred_element_type=F32)
    outs[1][:, 2 * width:] = pr_vg[:, :width].astype(BF16)
    outs[2][...] = pr_vg[:, width:]


def _half_layer(h_parts, mod, gains, w1, w2, *, layer, n_rows, lat_tiles, n_batch,
                pre="", pre_args=(), post="", post_w=None, post_wi=0, rope=None, final_g=None):
    d = h_parts[0].shape[1]
    f = w2.shape[2]
    k = 1 if pre else 0
    h_specs, n0 = _stream_specs(h_parts, d)
    _, m_spec = _tile_specs(lat_tiles, n_batch, d)
    in_specs = h_specs + [m_spec, _pinned((None, 3, 1, d), (layer, 0, 0, 0))]
    args = list(h_parts) + [mod, gains]
    n_y = ny0 = 0
    scratch = []
    if pre == "ab":
        r, ys_parts, glu_w, glu_b, w_out, wi = pre_args
        w = r.shape[1]
        y_specs, ny0 = _stream_specs(ys_parts, S5_CHUNK * w, rows=S5_CHUNKS_PER_TILE)
        n_y = len(ys_parts)
        scratch = [pltpu.VMEM((w // 128, TOKEN_TILE, 128), F32)]
        in_specs += [pl.BlockSpec((TOKEN_TILE, w), lambda i: (i, 0))] + y_specs + [
            _pinned((None, w, w), (wi, 0, 0)), _pinned((None, 1, w), (wi, 0, 0)),
            _pinned((None, w, d), (wi, 0, 0)), _pinned((None, w, d), (wi, 1, 0))]
        args += [r, *ys_parts, glu_w, glu_b, w_out, w_out]
    elif pre == "na":
        att, w_o, wi = pre_args
        in_specs += [pl.BlockSpec((TOKEN_TILE, d), lambda i: (i, 0)), _pinned((None, d, d), (wi, 0, 0))]
        args += [att, w_o]
    in_specs += [_pinned((None, None, d, f), (layer, k, 0, 0)),
                 _pinned((None, None, d, f), (layer, k, 0, 1)),
                 _pinned((None, None, f, d), (layer, k, 0, 0))]
    args += [w1, w1, w2]
    out_specs = [pl.BlockSpec((TOKEN_TILE, d), lambda i: (i, 0))]
    out_shape = [jax.ShapeDtypeStruct((n_rows, d), F32)]
    if post:
        n = post_w.shape[2]
        in_specs.append(_pinned((None, d, n), (post_wi, 0, 0)))
        args.append(post_w)
    if post == "ab":
        n_lat_tiles = lat_tiles * n_batch
        tab = pl.BlockSpec((TOKEN_TILE, RET_HEAD_DIM),
                           lambda i: (jnp.where(i < n_lat_tiles, i % lat_tiles, lat_tiles), 0))
        in_specs += [tab, tab, tab]
        args += list(rope)
        width = RET_HEADS * RET_HEAD_DIM
        n_u = n - 4 * width
        out_specs += [pl.BlockSpec((TOKEN_TILE, 3 * width), lambda i: (i, 0)),
                      pl.BlockSpec((TOKEN_TILE, width), lambda i: (i, 0)),
                      pl.BlockSpec((S5_CHUNKS_PER_TILE, S5_CHUNK * n_u), lambda i: (i, 0))]
        out_shape += [jax.ShapeDtypeStruct((n_rows, 3 * width), BF16),
                      jax.ShapeDtypeStruct((n_rows, width), F32),
                      jax.ShapeDtypeStruct((n_rows // S5_CHUNK, S5_CHUNK * n_u), F32)]
        scratch = [pltpu.VMEM((n_u // 128, TOKEN_TILE, 128), F32)]
    elif post == "na":
        out_specs.append(pl.BlockSpec((TOKEN_TILE, n), lambda i: (i, 0)))
        out_shape.append(jax.ShapeDtypeStruct((n_rows, n), BF16))
    if final_g is not None:
        in_specs.append(_resident((1, d)))
        args.append(final_g.reshape(1, d))
    cfg = _HalfCfg(n_h=len(h_parts), n0=n0, pre=pre, n_y=n_y, ny0=ny0, post=post,
                   final=final_g is not None)
    return pl.pallas_call(
        functools.partial(_half_kernel, cfg=cfg),
        grid=(n_rows // TOKEN_TILE,),
        in_specs=in_specs,
        out_specs=out_specs,
        out_shape=out_shape,
        scratch_shapes=scratch,
        compiler_params=_cparams("parallel"),
        name="half_layer",
    )(*args)


def _rope_tables(seq):
    half = RET_HEAD_DIM // 2
    quarter = half // 2
    inv = ROPE_BASE ** (-np.arange(0, half, 2, dtype=np.float64) / half)
    t = np.arange(seq)
    lane = np.arange(RET_HEAD_DIM)
    pos = np.where(lane[None, :] < half, (t // GRID_W)[:, None], (t % GRID_W)[:, None])
    ang = pos.astype(np.float32).astype(np.float64) * inv.astype(np.float32)[lane % quarter][None, :]
    first = (lane % half) < quarter
    cos = np.cos(ang)
    sin = np.sin(ang)
    sa = np.where(first[None, :], -sin, 0.0)
    sb = np.where(first[None, :], 0.0, sin)
    pad1 = np.ones((TOKEN_TILE, RET_HEAD_DIM))
    pad0 = np.zeros((TOKEN_TILE, RET_HEAD_DIM))
    tabs = [np.concatenate([cos, pad1]), np.concatenate([sa, pad0]), np.concatenate([sb, pad0])]
    return [jnp.asarray(x, F32) for x in tabs]


def _rope(x, cos, sa, sb):
    quarter = RET_HEAD_DIM // 4
    up = pltpu.roll(x, RET_HEAD_DIM - quarter, axis=1)
    dn = pltpu.roll(x, quarter, axis=1)
    return x * cos + up * sa + dn * sb


def _retention_kernel(dec_ref, ql_ref, kl_ref, vl_ref, qc_ref, kc_ref, vc_ref, g_ref,
                      dmask_ref, qwf_ref, qwb_ref, kwf_ref, kwb_ref, o_ref,
                      s_run, s_bwd, *, n_lat, n_ctx):
    i = pl.program_id(1)
    hd = RET_HEAD_DIM
    c = RET_CHUNK
    tn = (((0,), (0,)), ((), ()))
    nt = (((1,), (1,)), ((), ()))

    def bwd_chunk(k_ref, v_ref, rows, cid):
        for h in range(RET_HEADS):
            sl = slice(h * hd, (h + 1) * hd)
            s_old = s_run[h]
            s_bwd[cid, h] = s_old
            kw = (k_ref[rows, sl] * kwb_ref[h]).astype(BF16)
            kv = lax.dot_general(kw, v_ref[rows, sl], tn, preferred_element_type=F32)
            s_run[h] = dec_ref[1, h] * s_old + kv

    def fwd_chunk(q_ref, k_ref, v_ref, rows, cid):
        for h in range(RET_HEADS):
            sl = slice(h * hd, (h + 1) * hd)
            q = q_ref[rows, sl]
            k = k_ref[rows, sl]
            v = v_ref[rows, sl]
            s_old = s_run[h]
            a = lax.dot_general(q, k, nt, preferred_element_type=F32)
            o = jnp.dot((a * dmask_ref[h]).astype(BF16), v, preferred_element_type=F32)
            o += qwf_ref[h] * jnp.dot(q, s_old.astype(BF16), preferred_element_type=F32)
            o += qwb_ref[h] * jnp.dot(q, s_bwd[cid, h].astype(BF16), preferred_element_type=F32)
            kv = lax.dot_general((k * kwf_ref[h]).astype(BF16), v, tn, preferred_element_type=F32)
            s_run[h] = dec_ref[0, h] * s_old + kv
            o = o * lax.rsqrt(jnp.mean(o * o, axis=-1, keepdims=True) + EPS)
            g = g_ref[:, sl]
            o_ref[:, sl] = (o * (g * jax.nn.sigmoid(g))).astype(o_ref.dtype)

    def lat_rows(j):
        return pl.ds(pl.multiple_of(j * c, c), c)

    @pl.when(i == 0)
    def _():
        s_run[...] = jnp.zeros_like(s_run)
        for j in reversed(range(n_ctx)):
            bwd_chunk(kc_ref, vc_ref, slice(j * c, (j + 1) * c), j)

        def body(t, carry):
            j = n_lat - 1 - t
            bwd_chunk(kl_ref, vl_ref, lat_rows(j), n_ctx + j)
            return carry

        lax.fori_loop(0, n_lat, body, 0)
        s_run[...] = jnp.zeros_like(s_run)

    for j in range(n_ctx):
        @pl.when(i == 1 + j)
        def _(j=j):
            fwd_chunk(qc_ref, kc_ref, vc_ref, slice(j * c, (j + 1) * c), j)

    @pl.when(i > n_ctx)
    def _():
        fwd_chunk(ql_ref, kl_ref, vl_ref, lat_rows(i - 1 - n_ctx), i - 1)


def _retention(pq, pg, decay_logit, *, n_batch, seq, ctx_len):
    c = RET_CHUNK
    width = RET_HEADS * RET_HEAD_DIM
    n_lat = seq // c
    n_ctx = ctx_len // c
    n_ch = n_lat + n_ctx
    n_rows = pq.shape[0]
    lat_blocks = n_batch * n_lat

    log_gamma = jax.nn.log_sigmoid(decay_logit.astype(F32))
    pos = jnp.arange(c, dtype=F32)
    diff = pos[:, None] - pos[None, :]
    lf = log_gamma[0][:, None, None]
    lb = log_gamma[1][:, None, None]
    dmask = (jnp.where(diff >= 0, jnp.exp(lf * jnp.maximum(diff, 0.0)), 0.0)
             + jnp.where(diff <= 0, jnp.exp(lb * jnp.maximum(-diff, 0.0)), 0.0))
    ones = jnp.ones((1, 1, RET_HEAD_DIM), F32)
    col = pos[None, :, None]
    qwf = jnp.exp(lf * (col + 1.0)) * ones
    qwb = jnp.exp(lb * (c - col)) * ones
    kwf = jnp.exp(lf * (c - 1.0 - col)) * ones
    kwb = jnp.exp(lb * col) * ones
    chunk_decay = jnp.exp(log_gamma * c)

    ctx_blk0 = n_batch * seq // ctx_len

    def chunk_block(b, i):
        cid = jnp.maximum(i - 1, 0)
        return (jnp.where(cid < n_ctx, lat_blocks + b * n_ctx + cid, b * n_lat + cid - n_ctx), 0)

    chunk_spec = pl.BlockSpec((c, width), chunk_block)
    lat = lambda part: pl.BlockSpec((seq, width), lambda b, i: (b, part))
    ctx = lambda part: pl.BlockSpec((ctx_len, width), lambda b, i: (ctx_blk0 + b, part))
    head_tab = _resident((RET_HEADS, c, RET_HEAD_DIM))
    kern = functools.partial(_retention_kernel, n_lat=n_lat, n_ctx=n_ctx)
    return pl.pallas_call(
        kern,
        grid=(n_batch, 1 + n_ch),
        in_specs=[
            pl.BlockSpec(memory_space=pltpu.SMEM),
            lat(0), lat(1), lat(2), ctx(0), ctx(1), ctx(2), chunk_spec,
            _resident((RET_HEADS, c, c)), head_tab, head_tab, head_tab, head_tab,
        ],
        out_specs=chunk_spec,
        out_shape=jax.ShapeDtypeStruct((n_rows, width), BF16),
        scratch_shapes=[
            pltpu.VMEM((RET_HEADS, RET_HEAD_DIM, RET_HEAD_DIM), F32),
            pltpu.VMEM((n_ch, RET_HEADS, RET_HEAD_DIM, RET_HEAD_DIM), F32),
        ],
        compiler_params=_cparams("parallel", "arbitrary"),
        name="retention",
    )(chunk_decay, pq, pq, pq, pq, pq, pq, pg, dmask, qwf, qwb, kwf, kwb)


def _cmul(a, b):
    return a[0] * b[0] - a[1] * b[1], a[0] * b[1] + a[1] * b[0]


def _pair_blockdiag(a):
    a0, a1 = a[0::2], a[1::2]
    z = jnp.zeros_like(a0)
    return jnp.concatenate([jnp.concatenate([a0, z], axis=2), jnp.concatenate([z, a1], axis=2)],
                           axis=1)


def _pair_lanes(a):
    return jnp.concatenate([a[0::2], a[1::2]], axis=2)


def _powers(a, n):
    one = (jnp.ones_like(a[0]), jnp.zeros_like(a[0]))
    pw = (jnp.stack([one[0], a[0]], axis=1), jnp.stack([one[1], a[1]], axis=1))
    step = a
    while pw[0].shape[1] < n:
        step = _cmul(step, step)
        nxt = _cmul(pw, (step[0][:, None], step[1][:, None]))
        pw = (jnp.concatenate([pw[0], nxt[0]], axis=1), jnp.concatenate([pw[1], nxt[1]], axis=1))
    top = _cmul(step, step)
    return (jnp.concatenate([pw[0], top[0][:, None]], axis=1),
            jnp.concatenate([pw[1], top[1][:, None]], axis=1))


def _s5_weights(lam_re, lam_im, log_dt, b_re, b_im, c_re, c_im, d_skip):
    q = S5_CHUNK
    n_g = lam_re.shape[1]
    hi = lax.Precision.HIGHEST
    kall = None
    zw = []
    rw = []
    tabs = []
    for d in range(2):
        lr = jnp.minimum(lam_re[d].astype(F32), -1e-4)
        li = lam_im[d].astype(F32)
        dt = jnp.exp(log_dt[d].astype(F32))[:, None]
        mag = jnp.exp(lr * dt)
        a = (mag * jnp.cos(li * dt), mag * jnp.sin(li * dt))
        den = lr * lr + li * li
        am1 = (a[0] - 1.0, a[1])
        coef = ((am1[0] * lr + am1[1] * li) / den, (am1[1] * lr - am1[0] * li) / den)
        bt = (b_re[d].astype(F32).transpose(0, 2, 1), b_im[d].astype(F32).transpose(0, 2, 1))
        bb = _cmul((coef[0][:, None], coef[1][:, None]), bt)
        cm = (c_re[d].astype(F32), c_im[d].astype(F32))
        pw_re, pw_im = _powers(a, q)
        e = _cmul((pw_re[:, :q, None, :], pw_im[:, :q, None, :]),
                  (bb[0][:, None], bb[1][:, None]))
        taps = (jnp.einsum('gkp,gtjp->gjtk', cm[0], e[0], precision=hi)
                - jnp.einsum('gkp,gtjp->gjtk', cm[1], e[1], precision=hi))
        if d == 0:
            kall = jnp.concatenate([jnp.zeros_like(taps[:, :, 1:]), taps], axis=2)
        else:
            kall = kall + jnp.concatenate([taps[:, :, ::-1], jnp.zeros_like(taps[:, :, 1:])], axis=2)
        tsel = np.arange(q)[::-1] if d == 0 else np.arange(q)
        z = _cmul((pw_re[:, tsel][:, :, None, :], pw_im[:, tsel][:, :, None, :]),
                  (bb[0][:, None], bb[1][:, None]))
        zw.append((z[0].reshape(n_g, q * S5_GROUP, S5_STATE),
                   z[1].reshape(n_g, q * S5_GROUP, S5_STATE)))
        rsel = np.arange(1, q + 1) if d == 0 else np.arange(q, 0, -1)
        ct = (cm[0].transpose(0, 2, 1)[:, :, None, :], cm[1].transpose(0, 2, 1)[:, :, None, :])
        pt = (pw_re[:, rsel].transpose(0, 2, 1)[..., None], pw_im[:, rsel].transpose(0, 2, 1)[..., None])
        r = _cmul(ct, pt)
        rw.append((r[0].reshape(n_g, S5_STATE, q * S5_GROUP),
                   (-r[1]).reshape(n_g, S5_STATE, q * S5_GROUP)))
        a16 = (pw_re[:, q], pw_im[:, q])
        ramp_re, ramp_im = _powers(a16, S5_TILE)
        a32 = (ramp_re[:, 2], ramp_im[:, 2])
        a64 = (ramp_re[:, 4], ramp_im[:, 4])
        a128 = (ramp_re[:, 8], ramp_im[:, 8])
        rsl = slice(0, S5_TILE) if d == 0 else slice(S5_TILE - 1, None, -1)
        rows = [a16[0], a16[1], a32[0], a32[1], a64[0], a64[1], a128[0], a128[1]]
        tabs.append(jnp.concatenate([jnp.stack(rows, axis=1), ramp_re[:, rsl], ramp_im[:, rsl]],
                                    axis=1))
    dsk = d_skip.astype(F32).reshape(n_g, S5_GROUP)
    eye = jnp.eye(S5_GROUP, dtype=F32)
    kall = kall.at[:, :, q - 1].add(dsk[:, :, None] * eye[None])
    kall = jnp.concatenate([kall, jnp.zeros_like(kall[:, :, :1])], axis=2)
    kflat = kall.reshape(n_g, S5_GROUP, 2 * q * S5_GROUP)
    wz = jnp.concatenate([_pair_blockdiag(z) for pair in zw for z in pair], axis=2)
    w2 = jnp.concatenate([_pair_blockdiag(r) for pair in rw for r in pair], axis=1)
    tab = jnp.concatenate([_pair_lanes(t) for t in tabs], axis=1)
    return kflat, wz.astype(BF16), w2.astype(BF16), tab


def _tile_shift(x, s, down):
    row = lax.broadcasted_iota(jnp.int32, x.shape, 1)
    if down:
        return jnp.where(row >= s, pltpu.roll(x, s, axis=1), 0.0)
    return jnp.where(row < S5_TILE - s, pltpu.roll(x, S5_TILE - s, axis=1), 0.0)


S5_LANE_GROUPS = 128 // S5_GROUP


def _block_transpose(arrs):
    lane = lax.broadcasted_iota(jnp.int32, arrs[0].shape, 1)
    blk = lane // S5_GROUP
    a = list(arrs)
    for bit in range(3):
        s = 1 << bit
        hi = (blk & s) != 0
        new = list(a)
        for i in range(S5_LANE_GROUPS):
            if i & s:
                continue
            new[i] = jnp.where(hi, pltpu.roll(a[i + s], S5_GROUP * s, axis=1), a[i])
            new[i + s] = jnp.where(hi, a[i + s], pltpu.roll(a[i], 128 - S5_GROUP * s, axis=1))
        a = new
    return a


def _s5_kernel(vl_ref, vc_ref, kf_ref, wz_ref, w2_ref, tab_ref, yl_ref, yc_ref, m_scr, zx, xin,
               *, lat_tiles, ctx_tiles):
    q = S5_CHUNK
    lanes = 2 * S5_STATE
    pw = 2 * q * S5_GROUP
    gw = q * S5_GROUP

    @pl.when(pl.program_id(1) == 0)
    def _():
        m_scr[...] = jnp.zeros_like(m_scr)
        for g in range(S5_LANE_GROUPS):
            taps = kf_ref[g]
            base = (g % 2) * gw
            for ti in range(q):
                lo = S5_GROUP * (q - 1 - ti)
                m_scr[g // 2, base + ti * S5_GROUP:base + (ti + 1) * S5_GROUP, base:base + gw] = (
                    taps[:, lo:lo + gw].astype(BF16))

    n_tiles = ctx_tiles + lat_tiles
    ctx_rows = ctx_tiles * S5_TILE

    for p in range(m_scr.shape[0]):
        cols = slice(p * pw, (p + 1) * pw)
        v = jnp.concatenate([vc_ref[:, cols], vl_ref[:, cols]], axis=0).astype(BF16)
        y_intra = jnp.dot(v, m_scr[p], preferred_element_type=F32)
        y = jnp.dot(v, wz_ref[p], preferred_element_type=F32)

        def tab_row(r):
            return tab_ref[p, r:r + 1, :]

        for d in range(2):
            down = d == 0
            t0 = 24 * d
            lo = 2 * d * lanes
            z = (y[:, lo:lo + lanes].reshape(n_tiles, S5_TILE, lanes),
                 y[:, lo + lanes:lo + 2 * lanes].reshape(n_tiles, S5_TILE, lanes))
            loc = (_tile_shift(z[0], 1, down), _tile_shift(z[1], 1, down))
            for k, s in enumerate((1, 2, 4)):
                mul = (tab_row(t0 + 2 * k)[None], tab_row(t0 + 2 * k + 1)[None])
                inc = _cmul(mul, (_tile_shift(loc[0], s, down), _tile_shift(loc[1], s, down)))
                loc = (loc[0] + inc[0], loc[1] + inc[1])
            zx[2 * d] = loc[0]
            zx[2 * d + 1] = loc[1]
            zx[4 + 2 * d] = z[0]
            zx[4 + 2 * d + 1] = z[1]

        for d in range(2):
            t0 = 24 * d
            edge = S5_TILE - 1 if d == 0 else 0
            a16 = (tab_row(t0), tab_row(t0 + 1))
            a128 = (tab_row(t0 + 6), tab_row(t0 + 7))
            ramp = (tab_ref[p, t0 + 8:t0 + 16, :], tab_ref[p, t0 + 16:t0 + 24, :])

            def tile_step(j, carry, d=d, edge=edge, a16=a16, a128=a128, ramp=ramp):
                loc = (zx[2 * d, j], zx[2 * d + 1, j])
                z = (zx[4 + 2 * d, j], zx[4 + 2 * d + 1, j])
                inc = _cmul(ramp, carry)
                xin[j, :, (2 * d) * lanes:(2 * d + 1) * lanes] = loc[0] + inc[0]
                xin[j, :, (2 * d + 1) * lanes:(2 * d + 2) * lanes] = loc[1] + inc[1]
                e_loc = _cmul(a16, (loc[0][edge:edge + 1], loc[1][edge:edge + 1]))
                nxt = _cmul(a128, carry)
                return (nxt[0] + e_loc[0] + z[0][edge:edge + 1],
                        nxt[1] + e_loc[1] + z[1][edge:edge + 1])

            zero = (jnp.zeros((1, lanes), F32), jnp.zeros((1, lanes), F32))
            if d == 0:
                lax.fori_loop(0, n_tiles, tile_step, zero)
            else:
                mid = lax.fori_loop(0, ctx_tiles, lambda t, c: tile_step(ctx_tiles - 1 - t, c), zero)
                lax.fori_loop(0, lat_tiles, lambda t, c: tile_step(n_tiles - 1 - t, c), mid)

        x = xin[...].reshape(n_tiles * S5_TILE, 4 * lanes)
        y_all = y_intra + jnp.dot(x.astype(BF16), w2_ref[p], preferred_element_type=F32)
        yc_ref[:, cols] = y_all[:ctx_rows]
        yl_ref[:, cols] = y_all[ctx_rows:]


def _s5(v, weights, *, n_batch, seq, ctx_len):
    kflat, wz, w2, tab = weights
    q = S5_CHUNK
    lanes = 2 * S5_STATE
    pw = 2 * q * S5_GROUP
    n_pairs = w2.shape[0]
    ppb = S5_LANE_GROUPS // 2
    n_blocks = n_pairs // ppb
    bw = ppb * pw
    lat_rows = seq // q
    ctx_rows = ctx_len // q
    n_tiles = (lat_rows + ctx_rows) // S5_TILE
    ctx_blk0 = n_batch * lat_rows // ctx_rows
    kern = functools.partial(_s5_kernel, lat_tiles=lat_rows // S5_TILE, ctx_tiles=ctx_rows // S5_TILE)
    return pl.pallas_call(
        kern,
        grid=(n_blocks, n_batch),
        in_specs=[
            pl.BlockSpec((lat_rows, bw), lambda j, b: (b, j)),
            pl.BlockSpec((ctx_rows, bw), lambda j, b: (ctx_blk0 + b, j)),
            pl.BlockSpec((S5_LANE_GROUPS,) + kflat.shape[1:], lambda j, b: (j, 0, 0)),
            pl.BlockSpec((ppb, pw, 4 * lanes), lambda j, b: (j, 0, 0)),
            pl.BlockSpec((ppb, 4 * lanes, pw), lambda j, b: (j, 0, 0)),
            pl.BlockSpec((ppb, tab.shape[1], lanes), lambda j, b: (j, 0, 0)),
        ],
        out_specs=[pl.BlockSpec((lat_rows, bw), lambda j, b: (b, j)),
                   pl.BlockSpec((ctx_rows, bw), lambda j, b: (b, j))],
        out_shape=[jax.ShapeDtypeStruct((n_batch * lat_rows, n_blocks * bw), F32),
                   jax.ShapeDtypeStruct((n_batch * ctx_rows, n_blocks * bw), F32)],
        scratch_shapes=[
            pltpu.VMEM((ppb, pw, pw), BF16),
            pltpu.VMEM((8, n_tiles, S5_TILE, lanes), F32),
            pltpu.VMEM((n_tiles, S5_TILE, 4 * lanes), F32),
        ],
        compiler_params=_cparams("arbitrary", "arbitrary"),
        name="s5",
    )(v, v, kflat, wz, w2, tab)


NA_QROWS = 4
NA_KROWS = NA_QROWS + NA_KH


NA_REL_ROWS = 2 * NA_KH - 1


def _na_tile_index(kind, a, m):
    first, rel0 = ((0, NA_KH - 1 - a), (a, NA_QROWS - 1 - a), (NA_QROWS, -1 - a))[kind]
    return rel0 + m if first <= m < first + NA_KH else NA_REL_ROWS


def _na_bias_tiles(rpb):
    w = GRID_W
    qcol = np.arange(w)
    kcol = np.arange(w)
    wstart = np.clip(qcol - NA_KW // 2, 0, w - NA_KW)
    valid = (kcol[None, :] >= wstart[:, None]) & (kcol[None, :] < wstart[:, None] + NA_KW)
    rel = np.clip(kcol[None, :] - qcol[:, None], -(NA_KW - 1), NA_KW - 1) + NA_KW - 1
    onehot = (rel[None] == np.arange(2 * NA_KW - 1)[:, None, None]).astype(np.float32)
    tiles = jnp.einsum('hrj,jqk->hrqk', rpb.astype(F32), jnp.asarray(onehot),
                       precision=lax.Precision.HIGHEST)
    tiles = jnp.where(jnp.asarray(valid)[None, None], tiles, NEG_INF)
    return jnp.concatenate([tiles, jnp.full((NA_HEADS, 1, w, w), NEG_INF, F32)], axis=1)


def _na_kernel(q_ref, k_ref, v_ref, kc_ref, vc_ref, tiles_ref, o_ref, bias_ref, kt, kct, vx, vcx,
               s_even, s_odd, p_even, p_odd, *, rows):
    w = GRID_W
    dh = NA_HEAD_DIM
    lb = 2 * dh
    nq = NA_QROWS * w
    nk = NA_KROWS * w
    n_blocks = rows // NA_QROWS

    @pl.when(pl.program_id(1) == 0)
    def _():
        for kind in range(3):
            for hh in range(2):
                for a in range(NA_QROWS):
                    for m in range(0, NA_KROWS, 2):
                        pair = [tiles_ref[hh, _na_tile_index(kind, a, m + e)] for e in range(2)]
                        bias_ref[kind, hh * nq + a * w:hh * nq + (a + 1) * w, m * w:(m + 2) * w] = (
                            jnp.concatenate(pair, axis=1))

    scale = dh ** -0.5
    lane = lax.broadcasted_iota(jnp.int32, (nq, lb), 1)
    first = lane < dh

    kt[...] = k_ref[...].T
    kct[...] = kc_ref[...].T

    for dst, src in ((vx, v_ref), (vcx, vc_ref)):
        n = src.shape[0]
        dst[:, :lb] = src[...]
        dst[:, lb:] = (lax.broadcasted_iota(jnp.int32, (n, lb), 1) == 0).astype(BF16)

    def key_rows(i):
        r0 = min(max(NA_QROWS * i - NA_KH // 2, 0), rows - NA_KROWS)
        return slice(r0 * w, r0 * w + nk)

    def scores(i, s_ref):
        kind = 0 if i == 0 else (2 if i == n_blocks - 1 else 1)
        q = q_ref[i * nq:(i + 1) * nq, :] * scale
        zero = jnp.zeros_like(q)
        qs = jnp.concatenate([jnp.where(first, q, zero), jnp.where(first, zero, q)], axis=0)
        s_ref[:, :nk] = jnp.dot(qs, kt[:, key_rows(i)], preferred_element_type=F32) + bias_ref[kind]
        s_ref[:, nk:] = jnp.dot(qs, kct[...], preferred_element_type=F32)

    def softmax(s_ref, p_ref):
        s = s_ref[...]
        p_ref[...] = jnp.exp(s - jnp.max(s, axis=-1, keepdims=True)).astype(BF16)

    def attend(i, p_ref):
        o = jnp.dot(p_ref[:, :nk], vx[key_rows(i), :], preferred_element_type=F32)
        o += jnp.dot(p_ref[:, nk:], vcx[...], preferred_element_type=F32)
        o = o[:, :lb] / o[:, lb:lb + 1]
        o_ref[i * nq:(i + 1) * nq, :] = jnp.where(first, o[:nq], o[nq:]).astype(o_ref.dtype)

    scores(0, s_even)
    softmax(s_even, p_even)
    scores(1, s_odd)
    for j in range(1, n_blocks // 2):
        attend(2 * j - 2, p_even)
        softmax(s_odd, p_odd)
        scores(2 * j, s_even)
        attend(2 * j - 1, p_odd)
        softmax(s_even, p_even)
        scores(2 * j + 1, s_odd)
    attend(n_blocks - 2, p_even)
    softmax(s_odd, p_odd)
    attend(n_blocks - 1, p_odd)


def _natten(p, tiles, *, n_batch, seq, ctx_len):
    d = NA_HEADS * NA_HEAD_DIM
    lb = 2 * NA_HEAD_DIM
    n_pairs = NA_HEADS // 2
    ctx_blk0 = n_batch * seq // ctx_len
    rows = seq // GRID_W
    assert rows % (2 * NA_QROWS) == 0 and rows >= NA_KROWS + NA_QROWS
    stacked = 2 * NA_QROWS * GRID_W
    n_keys = NA_KROWS * GRID_W + ctx_len
    lat = lambda part: pl.BlockSpec((seq, lb), lambda j, b: (b, part * n_pairs + j))
    ctx = lambda part: pl.BlockSpec((ctx_len, lb), lambda j, b: (ctx_blk0 + b, part * n_pairs + j))
    return pl.pallas_call(
        functools.partial(_na_kernel, rows=rows),
        grid=(n_pairs, n_batch),
        in_specs=[lat(0), lat(1), lat(2), ctx(1), ctx(2),
                  pl.BlockSpec((2,) + tiles.shape[1:], lambda j, b: (j, 0, 0, 0))],
        out_specs=pl.BlockSpec((seq, lb), lambda j, b: (b, j)),
        out_shape=jax.ShapeDtypeStruct((n_batch * seq, d), BF16),
        scratch_shapes=[pltpu.VMEM((3, stacked, NA_KROWS * GRID_W), F32),
                        pltpu.VMEM((lb, seq), BF16), pltpu.VMEM((lb, ctx_len), BF16),
                        pltpu.VMEM((seq, 2 * lb), BF16), pltpu.VMEM((ctx_len, 2 * lb), BF16),
                        pltpu.VMEM((stacked, n_keys), F32), pltpu.VMEM((stacked, n_keys), F32),
                        pltpu.VMEM((stacked, n_keys), BF16), pltpu.VMEM((stacked, n_keys), BF16)],
        compiler_params=_cparams("arbitrary", "arbitrary"),
        name="natten",
    )(p, p, p, p, p, tiles)


def kernel(x, c, ctx, c_ctx, w_mod, b_mod, norm_g, ffn_w1, ffn_w2, w_in_ab, w_out_ab, ret_decay_logit, s5_lam_re, s5_lam_im, s5_log_dt, s5_b_re, s5_b_im, s5_c_re, s5_c_im, s5_d, s5_glu_w, s5_glu_b, na_w_qkv, na_w_o, na_rpb, final_g):
    n_batch, seq, d = x.shape
    ctx_len = ctx.shape[1]
    depth = w_mod.shape[0]
    n_lat = n_batch * seq
    n_all = n_lat + n_batch * ctx_len
    lat_tiles = seq // TOKEN_TILE
    assert seq % TOKEN_TILE == 0 and (n_batch * ctx_len) % TOKEN_TILE == 0
    assert n_batch + 1 <= MOD_ROWS and seq % (GRID_W * NA_KH) == 0

    cvec = jnp.concatenate([c, c_ctx[None], jnp.zeros((MOD_ROWS - n_batch - 1, d), F32)], axis=0)
    mod = _modulation(cvec, w_mod, b_mod).reshape(depth, MOD_ROWS, N_MOD, d)
    h_parts = (x.reshape(n_lat, d), ctx.reshape(n_batch * ctx_len, d))
    common = dict(lat_tiles=lat_tiles, n_batch=n_batch)
    dims = dict(n_batch=n_batch, seq=seq, ctx_len=ctx_len)
    gains = norm_g.astype(F32).reshape(depth, 3, 1, d)
    w1 = ffn_w1.astype(BF16)
    w2 = ffn_w2.astype(BF16)

    for layer in range(depth):
        last = layer == depth - 1
        i = layer // 2
        half = functools.partial(_half_layer, mod=mod[layer], gains=gains, w1=w1, w2=w2, layer=layer,
                                 **common)
        if layer % 2 == 0:
            h, pq, pg, pv = half(h_parts, n_rows=n_all, post="ab", post_w=w_in_ab.astype(BF16),
                                 post_wi=i, rope=_rope_tables(seq))
            r = _retention(pq, pg, ret_decay_logit[i], **dims)
            weights = _s5_weights(s5_lam_re[i], s5_lam_im[i], s5_log_dt[i], s5_b_re[i], s5_b_im[i],
                                  s5_c_re[i], s5_c_im[i], s5_d[i])
            ys_parts = _s5(pv, weights, **dims)
            pre = dict(pre="ab", pre_args=(r, ys_parts, s5_glu_w.astype(BF16),
                                           s5_glu_b.astype(F32)[:, None, :], w_out_ab.astype(BF16), i))
        else:
            assert last
            h, p = half(h_parts, n_rows=n_all, post="na", post_w=na_w_qkv.astype(BF16), post_wi=i)
            att = _natten(p, _na_bias_tiles(na_rpb[i]), **dims)
            pre = dict(pre="na", pre_args=(att, na_w_o.astype(BF16), i))
        (h,) = half((h,), n_rows=n_lat if last else n_all, final_g=final_g if last else None, **pre)
        h_parts = (h,)
    return h[:n_lat].reshape(n_batch, seq, d)
```

```python
import functools
import math
from typing import NamedTuple

import numpy as np
import jax
import jax.numpy as jnp
from jax import lax
from jax.experimental import pallas as pl
from jax.experimental.pallas import tpu as pltpu

F32 = jnp.float32
BF16 = jnp.bfloat16

EPS = 1e-6
ROPE_BASE = 10000.0
GRID_W = 64
N_MOD = 9
RET_HEADS = 4
RET_HEAD_DIM = 128
RET_CHUNK = 256
S5_GROUP = 16
S5_STATE = 64
S5_CHUNK = 16
S5_TILE = 8
NA_HEADS = 16
NA_HEAD_DIM = 64
NA_KH = 8
NA_KW = 16
NEG_INF = -1e30

TOKEN_TILE = 512
VMEM_LIMIT = 56 * 1024 * 1024
MOD_ROWS = 8


def _cparams(*sem):
    return pltpu.CompilerParams(dimension_semantics=sem, vmem_limit_bytes=VMEM_LIMIT)


def _resident(shape):
    nd = len(shape)
    return pl.BlockSpec(shape, lambda *_: (0,) * nd, pipeline_mode=pl.Buffered(1))


def _mod_kernel(c_ref, w_ref, b_ref, o_ref):
    c = c_ref[...]
    s = c * jax.nn.sigmoid(c)
    o_ref[0] = jnp.dot(s, w_ref[0], preferred_element_type=F32,
                       precision=lax.Precision.HIGHEST) + b_ref[0]


def _modulation(cvec, w_mod, b_mod):
    depth, d, nd = w_mod.shape
    tn = nd // 4 if nd % 512 == 0 else d
    return pl.pallas_call(
        _mod_kernel,
        grid=(depth, nd // tn),
        in_specs=[
            pl.BlockSpec((MOD_ROWS, d), lambda l, j: (0, 0)),
            pl.BlockSpec((1, d, tn), lambda l, j: (l, 0, j)),
            pl.BlockSpec((1, 1, tn), lambda l, j: (l, 0, j)),
        ],
        out_specs=pl.BlockSpec((1, MOD_ROWS, tn), lambda l, j: (l, 0, j)),
        out_shape=jax.ShapeDtypeStruct((depth, MOD_ROWS, nd), F32),
        compiler_params=_cparams("parallel", "parallel"),
        name="modulation",
    )(cvec, w_mod, b_mod.reshape(depth, 1, nd))


def _rms(x):
    return x * lax.rsqrt(jnp.mean(x * x, axis=-1, keepdims=True) + EPS)


def _modulated(h, m, g, mi):
    return (_rms(h) * g) * (1.0 + m[mi + 1:mi + 2]) + m[mi:mi + 1]


def _tile_specs(n_lat_tiles_per_batch, n_batch, d):
    def mod_idx(i):
        return (jnp.minimum(i // n_lat_tiles_per_batch, n_batch), 0, 0)
    h_spec = pl.BlockSpec((TOKEN_TILE, d), lambda i: (i, 0))
    m_spec = pl.BlockSpec((1, N_MOD, d), mod_idx)
    return h_spec, m_spec


def _pinned(block_shape, index):
    return pl.BlockSpec(block_shape, lambda *_: index, pipeline_mode=pl.Buffered(1))


def _stream_specs(parts, width, rows=TOKEN_TILE):
    if len(parts) == 1:
        return [pl.BlockSpec((rows, width), lambda i: (i, 0))], 0
    n0 = parts[0].shape[0] // rows
    return [pl.BlockSpec((rows, width), lambda i: (jnp.minimum(i, n0 - 1), 0)),
            pl.BlockSpec((rows, width), lambda i: (jnp.maximum(i - n0, 0), 0))], n0


def _stream_tile(refs, n0):
    if len(refs) == 1:
        return refs[0][...]
    return jnp.where(pl.program_id(0) < n0, refs[0][...], refs[1][...])


class _HalfCfg(NamedTuple):
    n_h: int
    n0: int
    pre: str
    n_y: int
    ny0: int
    post: str
    final: bool


def _gelu_tanh(y):
    return 0.5 * y * (1.0 + jnp.tanh(math.sqrt(2.0 / math.pi) * (y + 0.044715 * (y * y * y))))


S5_CHUNKS_PER_TILE = TOKEN_TILE // S5_CHUNK


def _to_chunk_rows(u_scr, out_ref):
    half = 128 // S5_GROUP
    gw = S5_CHUNK * S5_GROUP
    for blk in range(u_scr.shape[0]):
        for th in range(S5_CHUNK // half):
            arrs = [u_scr[blk, pl.ds(th * half + tl, S5_CHUNKS_PER_TILE, stride=S5_CHUNK), :]
                    for tl in range(half)]
            for g, x in enumerate(_block_transpose(arrs)):
                lo = (blk * half + g) * gw + th * 128
                out_ref[:, lo:lo + 128] = x


def _from_chunk_rows(y, y_scr):
    half = 128 // S5_GROUP
    gw = S5_CHUNK * S5_GROUP
    for blk in range(y_scr.shape[0]):
        for th in range(S5_CHUNK // half):
            arrs = [y[:, (blk * half + g) * gw + th * 128:(blk * half + g) * gw + (th + 1) * 128]
                    for g in range(half)]
            for tl, x in enumerate(_block_transpose(arrs)):
                y_scr[blk, pl.ds(th * half + tl, S5_CHUNKS_PER_TILE, stride=S5_CHUNK), :] = x


def _half_kernel(*refs, cfg):
    it = iter(refs)
    take = lambda n: [next(it) for _ in range(n)]
    h_refs = take(cfg.n_h)
    m_ref, g_ref = take(2)
    h = _stream_tile(h_refs, cfg.n0)
    m = m_ref[0]
    gate_mix = m[5:6]
    relayout_scr = refs[-1]
    if cfg.pre == "ab":
        (r_ref,) = take(1)
        y_refs = take(cfg.n_y)
        gw_ref, gb_ref, wr_ref, ws_ref = take(4)
        h = h + gate_mix * jnp.dot(r_ref[...], wr_ref[...], preferred_element_type=F32)
        _from_chunk_rows(_stream_tile(y_refs, cfg.ny0), relayout_scr)
        g = _gelu_tanh(jnp.concatenate([relayout_scr[j] for j in range(relayout_scr.shape[0])], axis=1))
        s = g * jax.nn.sigmoid(jnp.dot(g.astype(BF16), gw_ref[...], preferred_element_type=F32)
                               + gb_ref[...])
        h = h + gate_mix * jnp.dot(s.astype(BF16), ws_ref[...], preferred_element_type=F32)
    elif cfg.pre == "na":
        a_ref, wo_ref = take(2)
        h = h + gate_mix * jnp.dot(a_ref[...], wo_ref[...], preferred_element_type=F32)
    w1a_ref, w1b_ref, w2_ref = take(3)
    k = 1 if cfg.pre else 0
    mi = 6 * k
    xm = _modulated(h, m, g_ref[2 * k], mi).astype(BF16)
    a = jnp.dot(xm, w1a_ref[...], preferred_element_type=F32)
    b = jnp.dot(xm, w1b_ref[...], preferred_element_type=F32)
    hid = (a * jax.nn.sigmoid(a) * b).astype(BF16)
    h = h + (0.5 * m[mi + 2:mi + 3]) * jnp.dot(hid, w2_ref[...], preferred_element_type=F32)
    if cfg.post:
        (wp_ref,) = take(1)
    if cfg.post == "ab":
        cos_ref, sa_ref, sb_ref = take(3)
    if cfg.final:
        (fg_ref,) = take(1)
    outs = list(it)
    if "ab" in (cfg.pre, cfg.post):
        outs.pop()
    outs[0][...] = _rms(h) * fg_ref[...] if cfg.final else h
    if not cfg.post:
        return
    xm = _modulated(h, m, g_ref[1], 3).astype(BF16)
    if cfg.post == "na":
        outs[1][...] = jnp.dot(xm, wp_ref[...], preferred_element_type=F32).astype(BF16)
        return
    width = RET_HEADS * RET_HEAD_DIM
    pr_u = jnp.dot(xm, wp_ref[:, 4 * width:], preferred_element_type=F32)
    for j in range(relayout_scr.shape[0]):
        relayout_scr[j] = pr_u[:, j * 128:(j + 1) * 128]
    _to_chunk_rows(relayout_scr, outs[3])
    pr_qk = jnp.dot(xm, wp_ref[:, :2 * width], preferred_element_type=F32)
    cos, sa, sb = cos_ref[...], sa_ref[...], sb_ref[...]
    for hh in range(RET_HEADS):
        lo = hh * RET_HEAD_DIM
        hi = lo + RET_HEAD_DIM
        outs[1][:, lo:hi] = _rope(pr_qk[:, lo:hi], cos, sa, sb).astype(BF16)
        kr = _rope(pr_qk[:, width + lo:width + hi], cos, sa, sb) * (RET_HEAD_DIM ** -0.5)
        outs[1][:, width + lo:width + hi] = kr.astype(BF16)
    pr_vg = jnp.dot(xm, wp_ref[:, 2 * width:4 * width], preferred_element_type=F32)
    outs[1][:, 2 * width:] = pr_vg[:, :width].astype(BF16)
    outs[2][...] = pr_vg[:, width:]


def _half_layer(h_parts, mod, gains, w1, w2, *, layer, n_rows, lat_tiles, n_batch,
                pre="", pre_args=(), post="", post_w=None, post_wi=0, rope=None, final_g=None):
    d = h_parts[0].shape[1]
    f = w2.shape[2]
    k = 1 if pre else 0
    h_specs, n0 = _stream_specs(h_parts, d)
    _, m_spec = _tile_specs(lat_tiles, n_batch, d)
    in_specs = h_specs + [m_spec, _pinned((None, 3, 1, d), (layer, 0, 0, 0))]
    args = list(h_parts) + [mod, gains]
    n_y = ny0 = 0
    scratch = []
    if pre == "ab":
        r, ys_parts, glu_w, glu_b, w_out, wi = pre_args
        w = r.shape[1]
        y_specs, ny0 = _stream_specs(ys_parts, S5_CHUNK * w, rows=S5_CHUNKS_PER_TILE)
        n_y = len(ys_parts)
        scratch = [pltpu.VMEM((w // 128, TOKEN_TILE, 128), F32)]
        in_specs += [pl.BlockSpec((TOKEN_TILE, w), lambda i: (i, 0))] + y_specs + [
            _pinned((None, w, w), (wi, 0, 0)), _pinned((None, 1, w), (wi, 0, 0)),
            _pinned((None, w, d), (wi, 0, 0)), _pinned((None, w, d), (wi, 1, 0))]
        args += [r, *ys_parts, glu_w, glu_b, w_out, w_out]
    elif pre == "na":
        att, w_o, wi = pre_args
        in_specs += [pl.BlockSpec((TOKEN_TILE, d), lambda i: (i, 0)), _pinned((None, d, d), (wi, 0, 0))]
        args += [att, w_o]
    in_specs += [_pinned((None, None, d, f), (layer, k, 0, 0)),
                 _pinned((None, None, d, f), (layer, k, 0, 1)),
                 _pinned((None, None, f, d), (layer, k, 0, 0))]
    args += [w1, w1, w2]
    out_specs = [pl.BlockSpec((TOKEN_TILE, d), lambda i: (i, 0))]
    out_shape = [jax.ShapeDtypeStruct((n_rows, d), F32)]
    if post:
        n = post_w.shape[2]
        in_specs.append(_pinned((None, d, n), (post_wi, 0, 0)))
        args.append(post_w)
    if post == "ab":
        n_lat_tiles = lat_tiles * n_batch
        tab = pl.BlockSpec((TOKEN_TILE, RET_HEAD_DIM),
                           lambda i: (jnp.where(i < n_lat_tiles, i % lat_tiles, lat_tiles), 0))
        in_specs += [tab, tab, tab]
        args += list(rope)
        width = RET_HEADS * RET_HEAD_DIM
        n_u = n - 4 * width
        out_specs += [pl.BlockSpec((TOKEN_TILE, 3 * width), lambda i: (i, 0)),
                      pl.BlockSpec((TOKEN_TILE, width), lambda i: (i, 0)),
                      pl.BlockSpec((S5_CHUNKS_PER_TILE, S5_CHUNK * n_u), lambda i: (i, 0))]
        out_shape += [jax.ShapeDtypeStruct((n_rows, 3 * width), BF16),
                      jax.ShapeDtypeStruct((n_rows, width), F32),
                      jax.ShapeDtypeStruct((n_rows // S5_CHUNK, S5_CHUNK * n_u), F32)]
        scratch = [pltpu.VMEM((n_u // 128, TOKEN_TILE, 128), F32)]
    elif post == "na":
        out_specs.append(pl.BlockSpec((TOKEN_TILE, n), lambda i: (i, 0)))
        out_shape.append(jax.ShapeDtypeStruct((n_rows, n), BF16))
    if final_g is not None:
        in_specs.append(_resident((1, d)))
        args.append(final_g.reshape(1, d))
    cfg = _HalfCfg(n_h=len(h_parts), n0=n0, pre=pre, n_y=n_y, ny0=ny0, post=post,
                   final=final_g is not None)
    return pl.pallas_call(
        functools.partial(_half_kernel, cfg=cfg),
        grid=(n_rows // TOKEN_TILE,),
        in_specs=in_specs,
        out_specs=out_specs,
        out_shape=out_shape,
        scratch_shapes=scratch,
        compiler_params=_cparams("parallel"),
        name="half_layer",
    )(*args)


def _rope_tables(seq):
    half = RET_HEAD_DIM // 2
    quarter = half // 2
    inv = ROPE_BASE ** (-np.arange(0, half, 2, dtype=np.float64) / half)
    t = np.arange(seq)
    lane = np.arange(RET_HEAD_DIM)
    pos = np.where(lane[None, :] < half, (t // GRID_W)[:, None], (t % GRID_W)[:, None])
    ang = pos.astype(np.float32).astype(np.float64) * inv.astype(np.float32)[lane % quarter][None, :]
    first = (lane % half) < quarter
    cos = np.cos(ang)
    sin = np.sin(ang)
    sa = np.where(first[None, :], -sin, 0.0)
    sb = np.where(first[None, :], 0.0, sin)
    pad1 = np.ones((TOKEN_TILE, RET_HEAD_DIM))
    pad0 = np.zeros((TOKEN_TILE, RET_HEAD_DIM))
    tabs = [np.concatenate([cos, pad1]), np.concatenate([sa, pad0]), np.concatenate([sb, pad0])]
    return [jnp.asarray(x, F32) for x in tabs]


def _rope(x, cos, sa, sb):
    quarter = RET_HEAD_DIM // 4
    up = pltpu.roll(x, RET_HEAD_DIM - quarter, axis=1)
    dn = pltpu.roll(x, quarter, axis=1)
    return x * cos + up * sa + dn * sb


def _retention_kernel(dec_ref, ql_ref, kl_ref, vl_ref, qc_ref, kc_ref, vc_ref, g_ref,
                      dmask_ref, qwf_ref, qwb_ref, kwf_ref, kwb_ref, o_ref,
                      s_run, s_bwd, *, n_lat, n_ctx):
    i = pl.program_id(1)
    hd = RET_HEAD_DIM
    c = RET_CHUNK
    tn = (((0,), (0,)), ((), ()))
    nt = (((1,), (1,)), ((), ()))

    def bwd_chunk(k_ref, v_ref, rows, cid):
        for h in range(RET_HEADS):
            sl = slice(h * hd, (h + 1) * hd)
            s_old = s_run[h]
            s_bwd[cid, h] = s_old
            kw = (k_ref[rows, sl] * kwb_ref[h]).astype(BF16)
            kv = lax.dot_general(kw, v_ref[rows, sl], tn, preferred_element_type=F32)
            s_run[h] = dec_ref[1, h] * s_old + kv

    def fwd_chunk(q_ref, k_ref, v_ref, rows, cid):
        for h in range(RET_HEADS):
            sl = slice(h * hd, (h + 1) * hd)
            q = q_ref[rows, sl]
            k = k_ref[rows, sl]
            v = v_ref[rows, sl]
            s_old = s_run[h]
            a = lax.dot_general(q, k, nt, preferred_element_type=F32)
            o = jnp.dot((a * dmask_ref[h]).astype(BF16), v, preferred_element_type=F32)
            o += qwf_ref[h] * jnp.dot(q, s_old.astype(BF16), preferred_element_type=F32)
            o += qwb_ref[h] * jnp.dot(q, s_bwd[cid, h].astype(BF16), preferred_element_type=F32)
            kv = lax.dot_general((k * kwf_ref[h]).astype(BF16), v, tn, preferred_element_type=F32)
            s_run[h] = dec_ref[0, h] * s_old + kv
            o = o * lax.rsqrt(jnp.mean(o * o, axis=-1, keepdims=True) + EPS)
            g = g_ref[:, sl]
            o_ref[:, sl] = (o * (g * jax.nn.sigmoid(g))).astype(o_ref.dtype)

    def lat_rows(j):
        return pl.ds(pl.multiple_of(j * c, c), c)

    @pl.when(i == 0)
    def _():
        s_run[...] = jnp.zeros_like(s_run)
        for j in reversed(range(n_ctx)):
            bwd_chunk(kc_ref, vc_ref, slice(j * c, (j + 1) * c), j)

        def body(t, carry):
            j = n_lat - 1 - t
            bwd_chunk(kl_ref, vl_ref, lat_rows(j), n_ctx + j)
            return carry

        lax.fori_loop(0, n_lat, body, 0)
        s_run[...] = jnp.zeros_like(s_run)

    for j in range(n_ctx):
        @pl.when(i == 1 + j)
        def _(j=j):
            fwd_chunk(qc_ref, kc_ref, vc_ref, slice(j * c, (j + 1) * c), j)

    @pl.when(i > n_ctx)
    def _():
        fwd_chunk(ql_ref, kl_ref, vl_ref, lat_rows(i - 1 - n_ctx), i - 1)


def _retention(pq, pg, decay_logit, *, n_batch, seq, ctx_len):
    c = RET_CHUNK
    width = RET_HEADS * RET_HEAD_DIM
    n_lat = seq // c
    n_ctx = ctx_len // c
    n_ch = n_lat + n_ctx
    n_rows = pq.shape[0]
    lat_blocks = n_batch * n_lat

    log_gamma = jax.nn.log_sigmoid(decay_logit.astype(F32))
    pos = jnp.arange(c, dtype=F32)
    diff = pos[:, None] - pos[None, :]
    lf = log_gamma[0][:, None, None]
    lb = log_gamma[1][:, None, None]
    dmask = (jnp.where(diff >= 0, jnp.exp(lf * jnp.maximum(diff, 0.0)), 0.0)
             + jnp.where(diff <= 0, jnp.exp(lb * jnp.maximum(-diff, 0.0)), 0.0))
    ones = jnp.ones((1, 1, RET_HEAD_DIM), F32)
    col = pos[None, :, None]
    qwf = jnp.exp(lf * (col + 1.0)) * ones
    qwb = jnp.exp(lb * (c - col)) * ones
    kwf = jnp.exp(lf * (c - 1.0 - col)) * ones
    kwb = jnp.exp(lb * col) * ones
    chunk_decay = jnp.exp(log_gamma * c)

    ctx_blk0 = n_batch * seq // ctx_len

    def chunk_block(b, i):
        cid = jnp.maximum(i - 1, 0)
        return (jnp.where(cid < n_ctx, lat_blocks + b * n_ctx + cid, b * n_lat + cid - n_ctx), 0)

    chunk_spec = pl.BlockSpec((c, width), chunk_block)
    lat = lambda part: pl.BlockSpec((seq, width), lambda b, i: (b, part))
    ctx = lambda part: pl.BlockSpec((ctx_len, width), lambda b, i: (ctx_blk0 + b, part))
    head_tab = _resident((RET_HEADS, c, RET_HEAD_DIM))
    kern = functools.partial(_retention_kernel, n_lat=n_lat, n_ctx=n_ctx)
    return pl.pallas_call(
        kern,
        grid=(n_batch, 1 + n_ch),
        in_specs=[
            pl.BlockSpec(memory_space=pltpu.SMEM),
            lat(0), lat(1), lat(2), ctx(0), ctx(1), ctx(2), chunk_spec,
            _resident((RET_HEADS, c, c)), head_tab, head_tab, head_tab, head_tab,
        ],
        out_specs=chunk_spec,
        out_shape=jax.ShapeDtypeStruct((n_rows, width), BF16),
        scratch_shapes=[
            pltpu.VMEM((RET_HEADS, RET_HEAD_DIM, RET_HEAD_DIM), F32),
            pltpu.VMEM((n_ch, RET_HEADS, RET_HEAD_DIM, RET_HEAD_DIM), F32),
        ],
        compiler_params=_cparams("parallel", "arbitrary"),
        name="retention",
    )(chunk_decay, pq, pq, pq, pq, pq, pq, pg, dmask, qwf, qwb, kwf, kwb)


def _cmul(a, b):
    return a[0] * b[0] - a[1] * b[1], a[0] * b[1] + a[1] * b[0]


def _pair_blockdiag(a):
    a0, a1 = a[0::2], a[1::2]
    z = jnp.zeros_like(a0)
    return jnp.concatenate([jnp.concatenate([a0, z], axis=2), jnp.concatenate([z, a1], axis=2)],
                           axis=1)


def _pair_lanes(a):
    return jnp.concatenate([a[0::2], a[1::2]], axis=2)


def _powers(a, n):
    one = (jnp.ones_like(a[0]), jnp.zeros_like(a[0]))
    pw = (jnp.stack([one[0], a[0]], axis=-2), jnp.stack([one[1], a[1]], axis=-2))
    step = a
    while pw[0].shape[-2] < n:
        step = _cmul(step, step)
        nxt = _cmul(pw, (step[0][..., None, :], step[1][..., None, :]))
        pw = (jnp.concatenate([pw[0], nxt[0]], axis=-2), jnp.concatenate([pw[1], nxt[1]], axis=-2))
    top = _cmul(step, step)
    return (jnp.concatenate([pw[0], top[0][..., None, :]], axis=-2),
            jnp.concatenate([pw[1], top[1][..., None, :]], axis=-2))


def _s5_weights(lam_re, lam_im, log_dt, b_re, b_im, c_re, c_im, d_skip):
    q = S5_CHUNK
    n_g = lam_re.shape[1]
    lr = jnp.minimum(lam_re.astype(F32), -1e-4)
    li = lam_im.astype(F32)
    dt = jnp.exp(log_dt.astype(F32))[..., None]
    mag = jnp.exp(lr * dt)
    a = (mag * jnp.cos(li * dt), mag * jnp.sin(li * dt))
    den = lr * lr + li * li
    am1 = (a[0] - 1.0, a[1])
    coef = ((am1[0] * lr + am1[1] * li) / den, (am1[1] * lr - am1[0] * li) / den)
    bt = (b_re.astype(F32).transpose(0, 1, 3, 2), b_im.astype(F32).transpose(0, 1, 3, 2))
    bb = _cmul((coef[0][:, :, None], coef[1][:, :, None]), bt)
    cm = (c_re.astype(F32), c_im.astype(F32))
    pw = _powers(a, q)
    e = _cmul((pw[0][:, :, :q, None, :], pw[1][:, :, :q, None, :]),
              (bb[0][:, :, None], bb[1][:, :, None]))
    taps = jnp.einsum('dgkp,dgtjp->dgjtk', jnp.concatenate([cm[0], -cm[1]], axis=-1),
                      jnp.concatenate([e[0], e[1]], axis=-1), precision=lax.Precision.HIGHEST)
    zero = jnp.zeros_like(taps[0][:, :, 1:])
    kall = (jnp.concatenate([zero, taps[0]], axis=2)
            + jnp.concatenate([taps[1][:, :, ::-1], zero], axis=2))
    zw = [(e[0][0][:, ::-1], e[1][0][:, ::-1]), (e[0][1], e[1][1])]
    zw = [tuple(z.reshape(n_g, q * S5_GROUP, S5_STATE) for z in pair) for pair in zw]
    ct = (cm[0].transpose(0, 1, 3, 2)[:, :, :, None, :], cm[1].transpose(0, 1, 3, 2)[:, :, :, None, :])
    pt = (pw[0].transpose(0, 1, 3, 2)[..., None], pw[1].transpose(0, 1, 3, 2)[..., None])
    r = _cmul(ct, pt)
    rsel = (slice(1, q + 1), slice(q, 0, -1))
    rw = [(r[0][d][:, :, rsel[d]].reshape(n_g, S5_STATE, q * S5_GROUP),
           (-r[1][d][:, :, rsel[d]]).reshape(n_g, S5_STATE, q * S5_GROUP)) for d in range(2)]
    a16 = (pw[0][:, :, q], pw[1][:, :, q])
    ramp = _powers(a16, S5_TILE)
    mult = jnp.stack([x[:, :, s] for s in (1, 2, 4, 8) for x in ramp], axis=2)
    rsl = (slice(0, S5_TILE), slice(S5_TILE - 1, None, -1))
    tabs = [jnp.concatenate([mult[d], ramp[0][d][:, rsl[d]], ramp[1][d][:, rsl[d]]], axis=1)
            for d in range(2)]
    dsk = d_skip.astype(F32).reshape(n_g, S5_GROUP)
    eye = jnp.eye(S5_GROUP, dtype=F32)
    kall = kall.at[:, :, q - 1].add(dsk[:, :, None] * eye[None])
    kall = jnp.concatenate([kall, jnp.zeros_like(kall[:, :, :1])], axis=2)
    kflat = kall.reshape(n_g, S5_GROUP, 2 * q * S5_GROUP)
    wz = jnp.concatenate([_pair_blockdiag(z) for pair in zw for z in pair], axis=2)
    w2 = jnp.concatenate([_pair_blockdiag(r) for pair in rw for r in pair], axis=1)
    tab = jnp.concatenate([_pair_lanes(t) for t in tabs], axis=1)
    return kflat, wz.astype(BF16), w2.astype(BF16), tab


def _tile_shift(x, s, down):
    row = lax.broadcasted_iota(jnp.int32, x.shape, 1)
    if down:
        return jnp.where(row >= s, pltpu.roll(x, s, axis=1), 0.0)
    return jnp.where(row < S5_TILE - s, pltpu.roll(x, S5_TILE - s, axis=1), 0.0)


S5_LANE_GROUPS = 128 // S5_GROUP


def _block_transpose(arrs):
    lane = lax.broadcasted_iota(jnp.int32, arrs[0].shape, 1)
    blk = lane // S5_GROUP
    a = list(arrs)
    for bit in range(3):
        s = 1 << bit
        hi = (blk & s) != 0
        new = list(a)
        for i in range(S5_LANE_GROUPS):
            if i & s:
                continue
            new[i] = jnp.where(hi, pltpu.roll(a[i + s], S5_GROUP * s, axis=1), a[i])
            new[i + s] = jnp.where(hi, a[i + s], pltpu.roll(a[i], 128 - S5_GROUP * s, axis=1))
        a = new
    return a


def _s5_kernel(vl_ref, vc_ref, kf_ref, wz_ref, w2_ref, tab_ref, yl_ref, yc_ref, m_scr, zx, xin,
               *, lat_tiles, ctx_tiles):
    q = S5_CHUNK
    lanes = 2 * S5_STATE
    pw = 2 * q * S5_GROUP
    gw = q * S5_GROUP

    @pl.when(pl.program_id(1) == 0)
    def _():
        m_scr[...] = jnp.zeros_like(m_scr)
        for g in range(S5_LANE_GROUPS):
            taps = kf_ref[g]
            base = (g % 2) * gw
            for ti in range(q):
                lo = S5_GROUP * (q - 1 - ti)
                m_scr[g // 2, base + ti * S5_GROUP:base + (ti + 1) * S5_GROUP, base:base + gw] = (
                    taps[:, lo:lo + gw].astype(BF16))

    n_tiles = ctx_tiles + lat_tiles
    ctx_rows = ctx_tiles * S5_TILE

    for p in range(m_scr.shape[0]):
        cols = slice(p * pw, (p + 1) * pw)
        v = jnp.concatenate([vc_ref[:, cols], vl_ref[:, cols]], axis=0).astype(BF16)
        y_intra = jnp.dot(v, m_scr[p], preferred_element_type=F32)
        y = jnp.dot(v, wz_ref[p], preferred_element_type=F32)

        def tab_row(r):
            return tab_ref[p, r:r + 1, :]

        for d in range(2):
            down = d == 0
            t0 = 24 * d
            lo = 2 * d * lanes
            z = (y[:, lo:lo + lanes].reshape(n_tiles, S5_TILE, lanes),
                 y[:, lo + lanes:lo + 2 * lanes].reshape(n_tiles, S5_TILE, lanes))
            loc = (_tile_shift(z[0], 1, down), _tile_shift(z[1], 1, down))
            for k, s in enumerate((1, 2, 4)):
                mul = (tab_row(t0 + 2 * k)[None], tab_row(t0 + 2 * k + 1)[None])
                inc = _cmul(mul, (_tile_shift(loc[0], s, down), _tile_shift(loc[1], s, down)))
                loc = (loc[0] + inc[0], loc[1] + inc[1])
            zx[2 * d] = loc[0]
            zx[2 * d + 1] = loc[1]
            zx[4 + 2 * d] = z[0]
            zx[4 + 2 * d + 1] = z[1]

        for d in range(2):
            t0 = 24 * d
            edge = S5_TILE - 1 if d == 0 else 0
            a16 = (tab_row(t0), tab_row(t0 + 1))
            a128 = (tab_row(t0 + 6), tab_row(t0 + 7))
            ramp = (tab_ref[p, t0 + 8:t0 + 16, :], tab_ref[p, t0 + 16:t0 + 24, :])

            def tile_step(j, carry, d=d, edge=edge, a16=a16, a128=a128, ramp=ramp):
                loc = (zx[2 * d, j], zx[2 * d + 1, j])
                z = (zx[4 + 2 * d, j], zx[4 + 2 * d + 1, j])
                inc = _cmul(ramp, carry)
                xin[j, :, (2 * d) * lanes:(2 * d + 1) * lanes] = loc[0] + inc[0]
                xin[j, :, (2 * d + 1) * lanes:(2 * d + 2) * lanes] = loc[1] + inc[1]
                e_loc = _cmul(a16, (loc[0][edge:edge + 1], loc[1][edge:edge + 1]))
                nxt = _cmul(a128, carry)
                return (nxt[0] + e_loc[0] + z[0][edge:edge + 1],
                        nxt[1] + e_loc[1] + z[1][edge:edge + 1])

            zero = (jnp.zeros((1, lanes), F32), jnp.zeros((1, lanes), F32))
            if d == 0:
                lax.fori_loop(0, n_tiles, tile_step, zero)
            else:
                mid = lax.fori_loop(0, ctx_tiles, lambda t, c: tile_step(ctx_tiles - 1 - t, c), zero)
                lax.fori_loop(0, lat_tiles, lambda t, c: tile_step(n_tiles - 1 - t, c), mid)

        x = xin[...].reshape(n_tiles * S5_TILE, 4 * lanes)
        y_all = y_intra + jnp.dot(x.astype(BF16), w2_ref[p], preferred_element_type=F32)
        yc_ref[:, cols] = y_all[:ctx_rows]
        yl_ref[:, cols] = y_all[ctx_rows:]


def _s5(v, weights, *, n_batch, seq, ctx_len):
    kflat, wz, w2, tab = weights
    q = S5_CHUNK
    lanes = 2 * S5_STATE
    pw = 2 * q * S5_GROUP
    n_pairs = w2.shape[0]
    ppb = S5_LANE_GROUPS // 2
    n_blocks = n_pairs // ppb
    bw = ppb * pw
    lat_rows = seq // q
    ctx_rows = ctx_len // q
    n_tiles = (lat_rows + ctx_rows) // S5_TILE
    ctx_blk0 = n_batch * lat_rows // ctx_rows
    kern = functools.partial(_s5_kernel, lat_tiles=lat_rows // S5_TILE, ctx_tiles=ctx_rows // S5_TILE)
    return pl.pallas_call(
        kern,
        grid=(n_blocks, n_batch),
        in_specs=[
            pl.BlockSpec((lat_rows, bw), lambda j, b: (b, j)),
            pl.BlockSpec((ctx_rows, bw), lambda j, b: (ctx_blk0 + b, j)),
            pl.BlockSpec((S5_LANE_GROUPS,) + kflat.shape[1:], lambda j, b: (j, 0, 0)),
            pl.BlockSpec((ppb, pw, 4 * lanes), lambda j, b: (j, 0, 0)),
            pl.BlockSpec((ppb, 4 * lanes, pw), lambda j, b: (j, 0, 0)),
            pl.BlockSpec((ppb, tab.shape[1], lanes), lambda j, b: (j, 0, 0)),
        ],
        out_specs=[pl.BlockSpec((lat_rows, bw), lambda j, b: (b, j)),
                   pl.BlockSpec((ctx_rows, bw), lambda j, b: (b, j))],
        out_shape=[jax.ShapeDtypeStruct((n_batch * lat_rows, n_blocks * bw), F32),
                   jax.ShapeDtypeStruct((n_batch * ctx_rows, n_blocks * bw), F32)],
        scratch_shapes=[
            pltpu.VMEM((ppb, pw, pw), BF16),
            pltpu.VMEM((8, n_tiles, S5_TILE, lanes), F32),
            pltpu.VMEM((n_tiles, S5_TILE, 4 * lanes), F32),
        ],
        compiler_params=_cparams("arbitrary", "arbitrary"),
        name="s5",
    )(v, v, kflat, wz, w2, tab)


NA_QROWS = 4
NA_KROWS = NA_QROWS + NA_KH


NA_REL_ROWS = 2 * NA_KH - 1


def _na_tile_index(kind, a, m):
    first, rel0 = ((0, NA_KH - 1 - a), (a, NA_QROWS - 1 - a), (NA_QROWS, -1 - a))[kind]
    return rel0 + m if first <= m < first + NA_KH else NA_REL_ROWS


def _na_bias_tiles(rpb):
    w = GRID_W
    qcol = np.arange(w)
    kcol = np.arange(w)
    wstart = np.clip(qcol - NA_KW // 2, 0, w - NA_KW)
    valid = (kcol[None, :] >= wstart[:, None]) & (kcol[None, :] < wstart[:, None] + NA_KW)
    rel = np.clip(kcol[None, :] - qcol[:, None], -(NA_KW - 1), NA_KW - 1) + NA_KW - 1
    onehot = (rel[None] == np.arange(2 * NA_KW - 1)[:, None, None]).astype(np.float32)
    tiles = jnp.einsum('hrj,jqk->hrqk', rpb.astype(F32), jnp.asarray(onehot),
                       precision=lax.Precision.HIGHEST)
    tiles = jnp.where(jnp.asarray(valid)[None, None], tiles, NEG_INF)
    return jnp.concatenate([tiles, jnp.full((NA_HEADS, 1, w, w), NEG_INF, F32)], axis=1)


def _na_kernel(q_ref, k_ref, v_ref, kc_ref, vc_ref, tiles_ref, o_ref, bias_ref, kt, kct, vx, vcx,
               s_even, s_odd, p_even, p_odd, *, rows):
    w = GRID_W
    dh = NA_HEAD_DIM
    lb = 2 * dh
    nq = NA_QROWS * w
    nk = NA_KROWS * w
    n_blocks = rows // NA_QROWS

    @pl.when(pl.program_id(1) == 0)
    def _():
        for kind in range(3):
            for hh in range(2):
                for a in range(NA_QROWS):
                    for m in range(0, NA_KROWS, 2):
                        pair = [tiles_ref[hh, _na_tile_index(kind, a, m + e)] for e in range(2)]
                        bias_ref[kind, hh * nq + a * w:hh * nq + (a + 1) * w, m * w:(m + 2) * w] = (
                            jnp.concatenate(pair, axis=1))

    scale = dh ** -0.5
    lane = lax.broadcasted_iota(jnp.int32, (nq, lb), 1)
    first = lane < dh

    kt[...] = k_ref[...].T
    kct[...] = kc_ref[...].T

    for dst, src in ((vx, v_ref), (vcx, vc_ref)):
        n = src.shape[0]
        dst[:, :lb] = src[...]
        dst[:, lb:] = (lax.broadcasted_iota(jnp.int32, (n, lb), 1) == 0).astype(BF16)

    def key_rows(i):
        r0 = min(max(NA_QROWS * i - NA_KH // 2, 0), rows - NA_KROWS)
        return slice(r0 * w, r0 * w + nk)

    def scores(i, s_ref):
        kind = 0 if i == 0 else (2 if i == n_blocks - 1 else 1)
        q = q_ref[i * nq:(i + 1) * nq, :] * scale
        zero = jnp.zeros_like(q)
        qs = jnp.concatenate([jnp.where(first, q, zero), jnp.where(first, zero, q)], axis=0)
        s_ref[:, :nk] = jnp.dot(qs, kt[:, key_rows(i)], preferred_element_type=F32) + bias_ref[kind]
        s_ref[:, nk:] = jnp.dot(qs, kct[...], preferred_element_type=F32)

    def softmax(s_ref, p_ref):
        s = s_ref[...]
        p_ref[...] = jnp.exp(s - jnp.max(s, axis=-1, keepdims=True)).astype(BF16)

    def attend(i, p_ref):
        o = jnp.dot(p_ref[:, :nk], vx[key_rows(i), :], preferred_element_type=F32)
        o += jnp.dot(p_ref[:, nk:], vcx[...], preferred_element_type=F32)
        o = o[:, :lb] / o[:, lb:lb + 1]
        o_ref[i * nq:(i + 1) * nq, :] = jnp.where(first, o[:nq], o[nq:]).astype(o_ref.dtype)

    scores(0, s_even)
    softmax(s_even, p_even)
    scores(1, s_odd)
    for j in range(1, n_blocks // 2):
        attend(2 * j - 2, p_even)
        softmax(s_odd, p_odd)
        scores(2 * j, s_even)
        attend(2 * j - 1, p_odd)
        softmax(s_even, p_even)
        scores(2 * j + 1, s_odd)
    attend(n_blocks - 2, p_even)
    softmax(s_odd, p_odd)
    attend(n_blocks - 1, p_odd)


def _natten(p, tiles, *, n_batch, seq, ctx_len):
    d = NA_HEADS * NA_HEAD_DIM
    lb = 2 * NA_HEAD_DIM
    n_pairs = NA_HEADS // 2
    ctx_blk0 = n_batch * seq // ctx_len
    rows = seq // GRID_W
    assert rows % (2 * NA_QROWS) == 0 and rows >= NA_KROWS + NA_QROWS
    stacked = 2 * NA_QROWS * GRID_W
    n_keys = NA_KROWS * GRID_W + ctx_len
    lat = lambda part: pl.BlockSpec((seq, lb), lambda j, b: (b, part * n_pairs + j))
    ctx = lambda part: pl.BlockSpec((ctx_len, lb), lambda j, b: (ctx_blk0 + b, part * n_pairs + j))
    return pl.pallas_call(
        functools.partial(_na_kernel, rows=rows),
        grid=(n_pairs, n_batch),
        in_specs=[lat(0), lat(1), lat(2), ctx(1), ctx(2),
                  pl.BlockSpec((2,) + tiles.shape[1:], lambda j, b: (j, 0, 0, 0))],
        out_specs=pl.BlockSpec((seq, lb), lambda j, b: (b, j)),
        out_shape=jax.ShapeDtypeStruct((n_batch * seq, d), BF16),
        scratch_shapes=[pltpu.VMEM((3, stacked, NA_KROWS * GRID_W), F32),
                        pltpu.VMEM((lb, seq), BF16), pltpu.VMEM((lb, ctx_len), BF16),
                        pltpu.VMEM((seq, 2 * lb), BF16), pltpu.VMEM((ctx_len, 2 * lb), BF16),
                        pltpu.VMEM((stacked, n_keys), F32), pltpu.VMEM((stacked, n_keys), F32),
                        pltpu.VMEM((stacked, n_keys), BF16), pltpu.VMEM((stacked, n_keys), BF16)],
        compiler_params=_cparams("arbitrary", "arbitrary"),
        name="natten",
    )(p, p, p, p, p, tiles)


def kernel(x, c, ctx, c_ctx, w_mod, b_mod, norm_g, ffn_w1, ffn_w2, w_in_ab, w_out_ab, ret_decay_logit, s5_lam_re, s5_lam_im, s5_log_dt, s5_b_re, s5_b_im, s5_c_re, s5_c_im, s5_d, s5_glu_w, s5_glu_b, na_w_qkv, na_w_o, na_rpb, final_g):
    n_batch, seq, d = x.shape
    ctx_len = ctx.shape[1]
    depth = w_mod.shape[0]
    n_lat = n_batch * seq
    n_all = n_lat + n_batch * ctx_len
    lat_tiles = seq // TOKEN_TILE
    assert seq % TOKEN_TILE == 0 and (n_batch * ctx_len) % TOKEN_TILE == 0
    assert n_batch + 1 <= MOD_ROWS and seq % (GRID_W * NA_KH) == 0

    cvec = jnp.concatenate([c, c_ctx[None], jnp.zeros((MOD_ROWS - n_batch - 1, d), F32)], axis=0)
    mod = _modulation(cvec, w_mod, b_mod).reshape(depth, MOD_ROWS, N_MOD, d)
    h_parts = (x.reshape(n_lat, d), ctx.reshape(n_batch * ctx_len, d))
    common = dict(lat_tiles=lat_tiles, n_batch=n_batch)
    dims = dict(n_batch=n_batch, seq=seq, ctx_len=ctx_len)
    gains = norm_g.astype(F32).reshape(depth, 3, 1, d)
    w1 = ffn_w1.astype(BF16)
    w2 = ffn_w2.astype(BF16)

    for layer in range(depth):
        last = layer == depth - 1
        i = layer // 2
        half = functools.partial(_half_layer, mod=mod[layer], gains=gains, w1=w1, w2=w2, layer=layer,
                                 **common)
        if layer % 2 == 0:
            h, pq, pg, pv = half(h_parts, n_rows=n_all, post="ab", post_w=w_in_ab.astype(BF16),
                                 post_wi=i, rope=_rope_tables(seq))
            r = _retention(pq, pg, ret_decay_logit[i], **dims)
            weights = _s5_weights(s5_lam_re[i], s5_lam_im[i], s5_log_dt[i], s5_b_re[i], s5_b_im[i],
                                  s5_c_re[i], s5_c_im[i], s5_d[i])
            ys_parts = _s5(pv, weights, **dims)
            pre = dict(pre="ab", pre_args=(r, ys_parts, s5_glu_w.astype(BF16),
                                           s5_glu_b.astype(F32)[:, None, :], w_out_ab.astype(BF16), i))
        else:
            assert last
            h, p = half(h_parts, n_rows=n_all, post="na", post_w=na_w_qkv.astype(BF16), post_wi=i)
            att = _natten(p, _na_bias_tiles(na_rpb[i]), **dims)
            pre = dict(pre="na", pre_args=(att, na_w_o.astype(BF16), i))
        (h,) = half((h,), n_rows=n_lat if last else n_all, final_g=final_g if last else None, **pre)
        h_parts = (h,)
    return h[:n_lat].reshape(n_batch, seq, d)
```

```python
import functools
import math
from typing import NamedTuple

import numpy as np
import jax
import jax.numpy as jnp
from jax import lax
from jax.experimental import pallas as pl
from jax.experimental.pallas import tpu as pltpu

F32 = jnp.float32
BF16 = jnp.bfloat16

EPS = 1e-6
ROPE_BASE = 10000.0
GRID_W = 64
N_MOD = 9
RET_HEADS = 4
RET_HEAD_DIM = 128
RET_CHUNK = 256
S5_GROUP = 16
S5_STATE = 64
S5_CHUNK = 16
S5_TILE = 8
NA_HEADS = 16
NA_HEAD_DIM = 64
NA_KH = 8
NA_KW = 16
NEG_INF = -1e30

TOKEN_TILE = 512
VMEM_LIMIT = 56 * 1024 * 1024
MOD_ROWS = 8


def _cparams(*sem):
    return pltpu.CompilerParams(dimension_semantics=sem, vmem_limit_bytes=VMEM_LIMIT)


def _resident(shape):
    nd = len(shape)
    return pl.BlockSpec(shape, lambda *_: (0,) * nd, pipeline_mode=pl.Buffered(1))


def _split_bf16(x):
    hi = x.astype(BF16)
    return hi, (x - hi.astype(F32)).astype(BF16)


def _mod_kernel(c_ref, w_ref, b_ref, o_ref):
    c = c_ref[...]
    s_hi, s_lo = _split_bf16(c * jax.nn.sigmoid(c))
    w_hi, w_lo = _split_bf16(w_ref[0])
    acc = jnp.dot(s_hi, w_lo, preferred_element_type=F32)
    acc += jnp.dot(s_lo, w_hi, preferred_element_type=F32)
    acc += jnp.dot(s_hi, w_hi, preferred_element_type=F32)
    o_ref[0] = acc + b_ref[0]


def _modulation(cvec, w_mod, b_mod):
    depth, d, nd = w_mod.shape
    tn = nd // 4 if nd % 512 == 0 else d
    return pl.pallas_call(
        _mod_kernel,
        grid=(depth, nd // tn),
        in_specs=[
            pl.BlockSpec((MOD_ROWS, d), lambda l, j: (0, 0)),
            pl.BlockSpec((1, d, tn), lambda l, j: (l, 0, j)),
            pl.BlockSpec((1, 1, tn), lambda l, j: (l, 0, j)),
        ],
        out_specs=pl.BlockSpec((1, MOD_ROWS, tn), lambda l, j: (l, 0, j)),
        out_shape=jax.ShapeDtypeStruct((depth, MOD_ROWS, nd), F32),
        compiler_params=_cparams("parallel", "parallel"),
        name="modulation",
    )(cvec, w_mod, b_mod.reshape(depth, 1, nd))


def _rms(x):
    return x * lax.rsqrt(jnp.mean(x * x, axis=-1, keepdims=True) + EPS)


def _modulated(h, m, g, mi):
    return (_rms(h) * g) * (1.0 + m[mi + 1:mi + 2]) + m[mi:mi + 1]


def _tile_specs(n_lat_tiles_per_batch, n_batch, d):
    def mod_idx(i):
        return (jnp.minimum(i // n_lat_tiles_per_batch, n_batch), 0, 0)
    h_spec = pl.BlockSpec((TOKEN_TILE, d), lambda i: (i, 0))
    m_spec = pl.BlockSpec((1, N_MOD, d), mod_idx)
    return h_spec, m_spec


def _pinned(block_shape, index):
    return pl.BlockSpec(block_shape, lambda *_: index, pipeline_mode=pl.Buffered(1))


def _stream_specs(parts, width, rows=TOKEN_TILE):
    if len(parts) == 1:
        return [pl.BlockSpec((rows, width), lambda i: (i, 0))], 0
    n0 = parts[0].shape[0] // rows
    return [pl.BlockSpec((rows, width), lambda i: (jnp.minimum(i, n0 - 1), 0)),
            pl.BlockSpec((rows, width), lambda i: (jnp.maximum(i - n0, 0), 0))], n0


def _stream_tile(refs, n0):
    if len(refs) == 1:
        return refs[0][...]
    return jnp.where(pl.program_id(0) < n0, refs[0][...], refs[1][...])


class _HalfCfg(NamedTuple):
    n_h: int
    n0: int
    pre: str
    n_y: int
    ny0: int
    post: str
    final: bool


def _gelu_tanh(y):
    return 0.5 * y * (1.0 + jnp.tanh(math.sqrt(2.0 / math.pi) * (y + 0.044715 * (y * y * y))))


S5_CHUNKS_PER_TILE = TOKEN_TILE // S5_CHUNK


def _to_chunk_rows(u_scr, out_ref):
    half = 128 // S5_GROUP
    gw = S5_CHUNK * S5_GROUP
    for blk in range(u_scr.shape[0]):
        for th in range(S5_CHUNK // half):
            arrs = [u_scr[blk, pl.ds(th * half + tl, S5_CHUNKS_PER_TILE, stride=S5_CHUNK), :]
                    for tl in range(half)]
            for g, x in enumerate(_block_transpose(arrs)):
                lo = (blk * half + g) * gw + th * 128
                out_ref[:, lo:lo + 128] = x


def _from_chunk_rows(y, y_scr):
    half = 128 // S5_GROUP
    gw = S5_CHUNK * S5_GROUP
    for blk in range(y_scr.shape[0]):
        for th in range(S5_CHUNK // half):
            arrs = [y[:, (blk * half + g) * gw + th * 128:(blk * half + g) * gw + (th + 1) * 128]
                    for g in range(half)]
            for tl, x in enumerate(_block_transpose(arrs)):
                y_scr[blk, pl.ds(th * half + tl, S5_CHUNKS_PER_TILE, stride=S5_CHUNK), :] = x


def _half_kernel(*refs, cfg):
    it = iter(refs)
    take = lambda n: [next(it) for _ in range(n)]
    h_refs = take(cfg.n_h)
    m_ref, g_ref = take(2)
    h = _stream_tile(h_refs, cfg.n0)
    m = m_ref[0]
    gate_mix = m[5:6]
    relayout_scr = refs[-1]
    if cfg.pre == "ab":
        (r_ref,) = take(1)
        y_refs = take(cfg.n_y)
        gw_ref, gb_ref, wr_ref, ws_ref = take(4)
        h = h + gate_mix * jnp.dot(r_ref[...], wr_ref[...], preferred_element_type=F32)
        _from_chunk_rows(_stream_tile(y_refs, cfg.ny0), relayout_scr)
        g = _gelu_tanh(jnp.concatenate([relayout_scr[j] for j in range(relayout_scr.shape[0])], axis=1))
        s = g * jax.nn.sigmoid(jnp.dot(g.astype(BF16), gw_ref[...], preferred_element_type=F32)
                               + gb_ref[...])
        h = h + gate_mix * jnp.dot(s.astype(BF16), ws_ref[...], preferred_element_type=F32)
    elif cfg.pre == "na":
        a_ref, wo_ref = take(2)
        h = h + gate_mix * jnp.dot(a_ref[...], wo_ref[...], preferred_element_type=F32)
    w1a_ref, w1b_ref, w2_ref = take(3)
    k = 1 if cfg.pre else 0
    mi = 6 * k
    xm = _modulated(h, m, g_ref[2 * k], mi).astype(BF16)
    a = jnp.dot(xm, w1a_ref[...], preferred_element_type=F32)
    b = jnp.dot(xm, w1b_ref[...], preferred_element_type=F32)
    hid = (a * jax.nn.sigmoid(a) * b).astype(BF16)
    h = h + (0.5 * m[mi + 2:mi + 3]) * jnp.dot(hid, w2_ref[...], preferred_element_type=F32)
    if cfg.post:
        (wp_ref,) = take(1)
    if cfg.post == "ab":
        cos_ref, sa_ref, sb_ref = take(3)
    if cfg.final:
        (fg_ref,) = take(1)
    outs = list(it)
    if "ab" in (cfg.pre, cfg.post):
        outs.pop()
    outs[0][...] = _rms(h) * fg_ref[...] if cfg.final else h
    if not cfg.post:
        return
    xm = _modulated(h, m, g_ref[1], 3).astype(BF16)
    if cfg.post == "na":
        outs[1][...] = jnp.dot(xm, wp_ref[...], preferred_element_type=F32).astype(BF16)
        return
    width = RET_HEADS * RET_HEAD_DIM
    pr_u = jnp.dot(xm, wp_ref[:, 4 * width:], preferred_element_type=F32)
    for j in range(relayout_scr.shape[0]):
        relayout_scr[j] = pr_u[:, j * 128:(j + 1) * 128]
    _to_chunk_rows(relayout_scr, outs[3])
    pr_qk = jnp.dot(xm, wp_ref[:, :2 * width], preferred_element_type=F32)
    cos, sa, sb = cos_ref[...], sa_ref[...], sb_ref[...]
    for hh in range(RET_HEADS):
        lo = hh * RET_HEAD_DIM
        hi = lo + RET_HEAD_DIM
        outs[1][:, lo:hi] = _rope(pr_qk[:, lo:hi], cos, sa, sb).astype(BF16)
        kr = _rope(pr_qk[:, width + lo:width + hi], cos, sa, sb) * (RET_HEAD_DIM ** -0.5)
        outs[1][:, width + lo:width + hi] = kr.astype(BF16)
    pr_vg = jnp.dot(xm, wp_ref[:, 2 * width:4 * width], preferred_element_type=F32)
    outs[1][:, 2 * width:] = pr_vg[:, :width].astype(BF16)
    outs[2][...] = pr_vg[:, width:]


def _half_layer(h_parts, mod, gains, w1, w2, *, layer, n_rows, lat_tiles, n_batch,
                pre="", pre_args=(), post="", post_w=None, post_wi=0, rope=None, final_g=None):
    d = h_parts[0].shape[1]
    f = w2.shape[2]
    k = 1 if pre else 0
    h_specs, n0 = _stream_specs(h_parts, d)
    _, m_spec = _tile_specs(lat_tiles, n_batch, d)
    in_specs = h_specs + [m_spec, _pinned((None, 3, 1, d), (layer, 0, 0, 0))]
    args = list(h_parts) + [mod, gains]
    n_y = ny0 = 0
    scratch = []
    if pre == "ab":
        r, ys_parts, glu_w, glu_b, w_out, wi = pre_args
        w = r.shape[1]
        y_specs, ny0 = _stream_specs(ys_parts, S5_CHUNK * w, rows=S5_CHUNKS_PER_TILE)
        n_y = len(ys_parts)
        scratch = [pltpu.VMEM((w // 128, TOKEN_TILE, 128), F32)]
        in_specs += [pl.BlockSpec((TOKEN_TILE, w), lambda i: (i, 0))] + y_specs + [
            _pinned((None, w, w), (wi, 0, 0)), _pinned((None, 1, w), (wi, 0, 0)),
            _pinned((None, w, d), (wi, 0, 0)), _pinned((None, w, d), (wi, 1, 0))]
        args += [r, *ys_parts, glu_w, glu_b, w_out, w_out]
    elif pre == "na":
        att, w_o, wi = pre_args
        in_specs += [pl.BlockSpec((TOKEN_TILE, d), lambda i: (i, 0)), _pinned((None, d, d), (wi, 0, 0))]
        args += [att, w_o]
    in_specs += [_pinned((None, None, d, f), (layer, k, 0, 0)),
                 _pinned((None, None, d, f), (layer, k, 0, 1)),
                 _pinned((None, None, f, d), (layer, k, 0, 0))]
    args += [w1, w1, w2]
    out_specs = [pl.BlockSpec((TOKEN_TILE, d), lambda i: (i, 0))]
    out_shape = [jax.ShapeDtypeStruct((n_rows, d), F32)]
    if post:
        n = post_w.shape[2]
        in_specs.append(_pinned((None, d, n), (post_wi, 0, 0)))
        args.append(post_w)
    if post == "ab":
        n_lat_tiles = lat_tiles * n_batch
        tab = pl.BlockSpec((TOKEN_TILE, RET_HEAD_DIM),
                           lambda i: (jnp.where(i < n_lat_tiles, i % lat_tiles, lat_tiles), 0))
        in_specs += [tab, tab, tab]
        args += list(rope)
        width = RET_HEADS * RET_HEAD_DIM
        n_u = n - 4 * width
        out_specs += [pl.BlockSpec((TOKEN_TILE, 3 * width), lambda i: (i, 0)),
                      pl.BlockSpec((TOKEN_TILE, width), lambda i: (i, 0)),
                      pl.BlockSpec((S5_CHUNKS_PER_TILE, S5_CHUNK * n_u), lambda i: (i, 0))]
        out_shape += [jax.ShapeDtypeStruct((n_rows, 3 * width), BF16),
                      jax.ShapeDtypeStruct((n_rows, width), F32),
                      jax.ShapeDtypeStruct((n_rows // S5_CHUNK, S5_CHUNK * n_u), F32)]
        scratch = [pltpu.VMEM((n_u // 128, TOKEN_TILE, 128), F32)]
    elif post == "na":
        out_specs.append(pl.BlockSpec((TOKEN_TILE, n), lambda i: (i, 0)))
        out_shape.append(jax.ShapeDtypeStruct((n_rows, n), BF16))
    if final_g is not None:
        in_specs.append(_resident((1, d)))
        args.append(final_g.reshape(1, d))
    cfg = _HalfCfg(n_h=len(h_parts), n0=n0, pre=pre, n_y=n_y, ny0=ny0, post=post,
                   final=final_g is not None)
    return pl.pallas_call(
        functools.partial(_half_kernel, cfg=cfg),
        grid=(n_rows // TOKEN_TILE,),
        in_specs=in_specs,
        out_specs=out_specs,
        out_shape=out_shape,
        scratch_shapes=scratch,
        compiler_params=_cparams("parallel"),
        name="half_layer",
    )(*args)


def _rope_tables(seq):
    half = RET_HEAD_DIM // 2
    quarter = half // 2
    inv = ROPE_BASE ** (-np.arange(0, half, 2, dtype=np.float64) / half)
    t = np.arange(seq)
    lane = np.arange(RET_HEAD_DIM)
    pos = np.where(lane[None, :] < half, (t // GRID_W)[:, None], (t % GRID_W)[:, None])
    ang = pos.astype(np.float32).astype(np.float64) * inv.astype(np.float32)[lane % quarter][None, :]
    first = (lane % half) < quarter
    cos = np.cos(ang)
    sin = np.sin(ang)
    sa = np.where(first[None, :], -sin, 0.0)
    sb = np.where(first[None, :], 0.0, sin)
    pad1 = np.ones((TOKEN_TILE, RET_HEAD_DIM))
    pad0 = np.zeros((TOKEN_TILE, RET_HEAD_DIM))
    tabs = [np.concatenate([cos, pad1]), np.concatenate([sa, pad0]), np.concatenate([sb, pad0])]
    return [jnp.asarray(x, F32) for x in tabs]


def _rope(x, cos, sa, sb):
    quarter = RET_HEAD_DIM // 4
    up = pltpu.roll(x, RET_HEAD_DIM - quarter, axis=1)
    dn = pltpu.roll(x, quarter, axis=1)
    return x * cos + up * sa + dn * sb


def _retention_kernel(dec_ref, ql_ref, kl_ref, vl_ref, qc_ref, kc_ref, vc_ref, g_ref,
                      dmask_ref, qwf_ref, qwb_ref, kwf_ref, kwb_ref, o_ref,
                      s_run, s_bwd, *, n_lat, n_ctx):
    i = pl.program_id(1)
    hd = RET_HEAD_DIM
    c = RET_CHUNK
    tn = (((0,), (0,)), ((), ()))
    nt = (((1,), (1,)), ((), ()))

    def bwd_chunk(k_ref, v_ref, rows, cid):
        for h in range(RET_HEADS):
            sl = slice(h * hd, (h + 1) * hd)
            s_old = s_run[h]
            s_bwd[cid, h] = s_old
            kw = (k_ref[rows, sl] * kwb_ref[h]).astype(BF16)
            kv = lax.dot_general(kw, v_ref[rows, sl], tn, preferred_element_type=F32)
            s_run[h] = dec_ref[1, h] * s_old + kv

    def fwd_chunk(q_ref, k_ref, v_ref, rows, cid):
        for h in range(RET_HEADS):
            sl = slice(h * hd, (h + 1) * hd)
            q = q_ref[rows, sl]
            k = k_ref[rows, sl]
            v = v_ref[rows, sl]
            s_old = s_run[h]
            a = lax.dot_general(q, k, nt, preferred_element_type=F32)
            o = jnp.dot((a * dmask_ref[h]).astype(BF16), v, preferred_element_type=F32)
            o += qwf_ref[h] * jnp.dot(q, s_old.astype(BF16), preferred_element_type=F32)
            o += qwb_ref[h] * jnp.dot(q, s_bwd[cid, h].astype(BF16), preferred_element_type=F32)
            kv = lax.dot_general((k * kwf_ref[h]).astype(BF16), v, tn, preferred_element_type=F32)
            s_run[h] = dec_ref[0, h] * s_old + kv
            o = o * lax.rsqrt(jnp.mean(o * o, axis=-1, keepdims=True) + EPS)
            g = g_ref[:, sl]
            o_ref[:, sl] = (o * (g * jax.nn.sigmoid(g))).astype(o_ref.dtype)

    def lat_rows(j):
        return pl.ds(pl.multiple_of(j * c, c), c)

    @pl.when(i == 0)
    def _():
        s_run[...] = jnp.zeros_like(s_run)
        for j in reversed(range(n_ctx)):
            bwd_chunk(kc_ref, vc_ref, slice(j * c, (j + 1) * c), j)

        def body(t, carry):
            j = n_lat - 1 - t
            bwd_chunk(kl_ref, vl_ref, lat_rows(j), n_ctx + j)
            return carry

        lax.fori_loop(0, n_lat, body, 0)
        s_run[...] = jnp.zeros_like(s_run)

    for j in range(n_ctx):
        @pl.when(i == 1 + j)
        def _(j=j):
            fwd_chunk(qc_ref, kc_ref, vc_ref, slice(j * c, (j + 1) * c), j)

    @pl.when(i > n_ctx)
    def _():
        fwd_chunk(ql_ref, kl_ref, vl_ref, lat_rows(i - 1 - n_ctx), i - 1)


def _retention(pq, pg, decay_logit, *, n_batch, seq, ctx_len):
    c = RET_CHUNK
    width = RET_HEADS * RET_HEAD_DIM
    n_lat = seq // c
    n_ctx = ctx_len // c
    n_ch = n_lat + n_ctx
    n_rows = pq.shape[0]
    lat_blocks = n_batch * n_lat

    log_gamma = jax.nn.log_sigmoid(decay_logit.astype(F32))
    pos = jnp.arange(c, dtype=F32)
    diff = pos[:, None] - pos[None, :]
    lf = log_gamma[0][:, None, None]
    lb = log_gamma[1][:, None, None]
    dmask = (jnp.where(diff >= 0, jnp.exp(lf * jnp.maximum(diff, 0.0)), 0.0)
             + jnp.where(diff <= 0, jnp.exp(lb * jnp.maximum(-diff, 0.0)), 0.0))
    ones = jnp.ones((1, 1, RET_HEAD_DIM), F32)
    col = pos[None, :, None]
    qwf = jnp.exp(lf * (col + 1.0)) * ones
    qwb = jnp.exp(lb * (c - col)) * ones
    kwf = jnp.exp(lf * (c - 1.0 - col)) * ones
    kwb = jnp.exp(lb * col) * ones
    chunk_decay = jnp.exp(log_gamma * c)

    ctx_blk0 = n_batch * seq // ctx_len

    def chunk_block(b, i):
        cid = jnp.maximum(i - 1, 0)
        return (jnp.where(cid < n_ctx, lat_blocks + b * n_ctx + cid, b * n_lat + cid - n_ctx), 0)

    chunk_spec = pl.BlockSpec((c, width), chunk_block)
    lat = lambda part: pl.BlockSpec((seq, width), lambda b, i: (b, part))
    ctx = lambda part: pl.BlockSpec((ctx_len, width), lambda b, i: (ctx_blk0 + b, part))
    head_tab = _resident((RET_HEADS, c, RET_HEAD_DIM))
    kern = functools.partial(_retention_kernel, n_lat=n_lat, n_ctx=n_ctx)
    return pl.pallas_call(
        kern,
        grid=(n_batch, 1 + n_ch),
        in_specs=[
            pl.BlockSpec(memory_space=pltpu.SMEM),
            lat(0), lat(1), lat(2), ctx(0), ctx(1), ctx(2), chunk_spec,
            _resident((RET_HEADS, c, c)), head_tab, head_tab, head_tab, head_tab,
        ],
        out_specs=chunk_spec,
        out_shape=jax.ShapeDtypeStruct((n_rows, width), BF16),
        scratch_shapes=[
            pltpu.VMEM((RET_HEADS, RET_HEAD_DIM, RET_HEAD_DIM), F32),
            pltpu.VMEM((n_ch, RET_HEADS, RET_HEAD_DIM, RET_HEAD_DIM), F32),
        ],
        compiler_params=_cparams("parallel", "arbitrary"),
        name="retention",
    )(chunk_decay, pq, pq, pq, pq, pq, pq, pg, dmask, qwf, qwb, kwf, kwb)


def _cmul(a, b):
    return a[0] * b[0] - a[1] * b[1], a[0] * b[1] + a[1] * b[0]


def _pair_blockdiag(a):
    a0, a1 = a[0::2], a[1::2]
    z = jnp.zeros_like(a0)
    return jnp.concatenate([jnp.concatenate([a0, z], axis=2), jnp.concatenate([z, a1], axis=2)],
                           axis=1)


def _pair_lanes(a):
    return jnp.concatenate([a[0::2], a[1::2]], axis=2)


def _powers(a, n):
    one = (jnp.ones_like(a[0]), jnp.zeros_like(a[0]))
    pw = (jnp.stack([one[0], a[0]], axis=1), jnp.stack([one[1], a[1]], axis=1))
    step = a
    while pw[0].shape[1] < n:
        step = _cmul(step, step)
        nxt = _cmul(pw, (step[0][:, None], step[1][:, None]))
        pw = (jnp.concatenate([pw[0], nxt[0]], axis=1), jnp.concatenate([pw[1], nxt[1]], axis=1))
    top = _cmul(step, step)
    return (jnp.concatenate([pw[0], top[0][:, None]], axis=1),
            jnp.concatenate([pw[1], top[1][:, None]], axis=1))


def _s5_weights(lam_re, lam_im, log_dt, b_re, b_im, c_re, c_im, d_skip):
    q = S5_CHUNK
    n_g = lam_re.shape[1]
    hi = lax.Precision.HIGHEST
    kall = None
    zw = []
    rw = []
    tabs = []
    for d in range(2):
        lr = jnp.minimum(lam_re[d].astype(F32), -1e-4)
        li = lam_im[d].astype(F32)
        dt = jnp.exp(log_dt[d].astype(F32))[:, None]
        mag = jnp.exp(lr * dt)
        a = (mag * jnp.cos(li * dt), mag * jnp.sin(li * dt))
        den = lr * lr + li * li
        am1 = (a[0] - 1.0, a[1])
        coef = ((am1[0] * lr + am1[1] * li) / den, (am1[1] * lr - am1[0] * li) / den)
        bt = (b_re[d].astype(F32).transpose(0, 2, 1), b_im[d].astype(F32).transpose(0, 2, 1))
        bb = _cmul((coef[0][:, None], coef[1][:, None]), bt)
        cm = (c_re[d].astype(F32), c_im[d].astype(F32))
        pw_re, pw_im = _powers(a, q)
        e = _cmul((pw_re[:, :q, None, :], pw_im[:, :q, None, :]),
                  (bb[0][:, None], bb[1][:, None]))
        taps = (jnp.einsum('gkp,gtjp->gjtk', cm[0], e[0], precision=hi)
                - jnp.einsum('gkp,gtjp->gjtk', cm[1], e[1], precision=hi))
        if d == 0:
            kall = jnp.concatenate([jnp.zeros_like(taps[:, :, 1:]), taps], axis=2)
        else:
            kall = kall + jnp.concatenate([taps[:, :, ::-1], jnp.zeros_like(taps[:, :, 1:])], axis=2)
        tsel = np.arange(q)[::-1] if d == 0 else np.arange(q)
        z = _cmul((pw_re[:, tsel][:, :, None, :], pw_im[:, tsel][:, :, None, :]),
                  (bb[0][:, None], bb[1][:, None]))
        zw.append((z[0].reshape(n_g, q * S5_GROUP, S5_STATE),
                   z[1].reshape(n_g, q * S5_GROUP, S5_STATE)))
        rsel = np.arange(1, q + 1) if d == 0 else np.arange(q, 0, -1)
        ct = (cm[0].transpose(0, 2, 1)[:, :, None, :], cm[1].transpose(0, 2, 1)[:, :, None, :])
        pt = (pw_re[:, rsel].transpose(0, 2, 1)[..., None], pw_im[:, rsel].transpose(0, 2, 1)[..., None])
        r = _cmul(ct, pt)
        rw.append((r[0].reshape(n_g, S5_STATE, q * S5_GROUP),
                   (-r[1]).reshape(n_g, S5_STATE, q * S5_GROUP)))
        a16 = (pw_re[:, q], pw_im[:, q])
        ramp_re, ramp_im = _powers(a16, S5_TILE)
        a32 = (ramp_re[:, 2], ramp_im[:, 2])
        a64 = (ramp_re[:, 4], ramp_im[:, 4])
        a128 = (ramp_re[:, 8], ramp_im[:, 8])
        rsl = slice(0, S5_TILE) if d == 0 else slice(S5_TILE - 1, None, -1)
        rows = [a16[0], a16[1], a32[0], a32[1], a64[0], a64[1], a128[0], a128[1]]
        tabs.append(jnp.concatenate([jnp.stack(rows, axis=1), ramp_re[:, rsl], ramp_im[:, rsl]],
                                    axis=1))
    dsk = d_skip.astype(F32).reshape(n_g, S5_GROUP)
    eye = jnp.eye(S5_GROUP, dtype=F32)
    kall = kall.at[:, :, q - 1].add(dsk[:, :, None] * eye[None])
    kall = jnp.concatenate([kall, jnp.zeros_like(kall[:, :, :1])], axis=2)
    kflat = kall.reshape(n_g, S5_GROUP, 2 * q * S5_GROUP)
    wz = jnp.concatenate([_pair_blockdiag(z) for pair in zw for z in pair], axis=2)
    w2 = jnp.concatenate([_pair_blockdiag(r) for pair in rw for r in pair], axis=1)
    tab = jnp.concatenate([_pair_lanes(t) for t in tabs], axis=1)
    return kflat, wz.astype(BF16), w2.astype(BF16), tab


def _tile_shift(x, s, down):
    row = lax.broadcasted_iota(jnp.int32, x.shape, 1)
    if down:
        return jnp.where(row >= s, pltpu.roll(x, s, axis=1), 0.0)
    return jnp.where(row < S5_TILE - s, pltpu.roll(x, S5_TILE - s, axis=1), 0.0)


S5_LANE_GROUPS = 128 // S5_GROUP


def _block_transpose(arrs):
    lane = lax.broadcasted_iota(jnp.int32, arrs[0].shape, 1)
    blk = lane // S5_GROUP
    a = list(arrs)
    for bit in range(3):
        s = 1 << bit
        hi = (blk & s) != 0
        new = list(a)
        for i in range(S5_LANE_GROUPS):
            if i & s:
                continue
            new[i] = jnp.where(hi, pltpu.roll(a[i + s], S5_GROUP * s, axis=1), a[i])
            new[i + s] = jnp.where(hi, a[i + s], pltpu.roll(a[i], 128 - S5_GROUP * s, axis=1))
        a = new
    return a


def _s5_kernel(vl_ref, vc_ref, kf_ref, wz_ref, w2_ref, tab_ref, yl_ref, yc_ref, m_scr, zx, xin,
               *, lat_tiles, ctx_tiles):
    q = S5_CHUNK
    lanes = 2 * S5_STATE
    pw = 2 * q * S5_GROUP
    gw = q * S5_GROUP

    @pl.when(pl.program_id(1) == 0)
    def _():
        m_scr[...] = jnp.zeros_like(m_scr)
        for g in range(S5_LANE_GROUPS):
            taps = kf_ref[g]
            base = (g % 2) * gw
            for ti in range(q):
                lo = S5_GROUP * (q - 1 - ti)
                m_scr[g // 2, base + ti * S5_GROUP:base + (ti + 1) * S5_GROUP, base:base + gw] = (
                    taps[:, lo:lo + gw].astype(BF16))

    n_tiles = ctx_tiles + lat_tiles
    ctx_rows = ctx_tiles * S5_TILE

    for p in range(m_scr.shape[0]):
        cols = slice(p * pw, (p + 1) * pw)
        v = jnp.concatenate([vc_ref[:, cols], vl_ref[:, cols]], axis=0).astype(BF16)
        y_intra = jnp.dot(v, m_scr[p], preferred_element_type=F32)
        y = jnp.dot(v, wz_ref[p], preferred_element_type=F32)

        def tab_row(r):
            return tab_ref[p, r:r + 1, :]

        for d in range(2):
            down = d == 0
            t0 = 24 * d
            lo = 2 * d * lanes
            z = (y[:, lo:lo + lanes].reshape(n_tiles, S5_TILE, lanes),
                 y[:, lo + lanes:lo + 2 * lanes].reshape(n_tiles, S5_TILE, lanes))
            loc = (_tile_shift(z[0], 1, down), _tile_shift(z[1], 1, down))
            for k, s in enumerate((1, 2, 4)):
                mul = (tab_row(t0 + 2 * k)[None], tab_row(t0 + 2 * k + 1)[None])
                inc = _cmul(mul, (_tile_shift(loc[0], s, down), _tile_shift(loc[1], s, down)))
                loc = (loc[0] + inc[0], loc[1] + inc[1])
            zx[2 * d] = loc[0]
            zx[2 * d + 1] = loc[1]
            zx[4 + 2 * d] = z[0]
            zx[4 + 2 * d + 1] = z[1]

        for d in range(2):
            t0 = 24 * d
            edge = S5_TILE - 1 if d == 0 else 0
            a16 = (tab_row(t0), tab_row(t0 + 1))
            a128 = (tab_row(t0 + 6), tab_row(t0 + 7))
            ramp = (tab_ref[p, t0 + 8:t0 + 16, :], tab_ref[p, t0 + 16:t0 + 24, :])

            def tile_step(j, carry, d=d, edge=edge, a16=a16, a128=a128, ramp=ramp):
                loc = (zx[2 * d, j], zx[2 * d + 1, j])
                z = (zx[4 + 2 * d, j], zx[4 + 2 * d + 1, j])
                inc = _cmul(ramp, carry)
                xin[j, :, (2 * d) * lanes:(2 * d + 1) * lanes] = loc[0] + inc[0]
                xin[j, :, (2 * d + 1) * lanes:(2 * d + 2) * lanes] = loc[1] + inc[1]
                e_loc = _cmul(a16, (loc[0][edge:edge + 1], loc[1][edge:edge + 1]))
                nxt = _cmul(a128, carry)
                return (nxt[0] + e_loc[0] + z[0][edge:edge + 1],
                        nxt[1] + e_loc[1] + z[1][edge:edge + 1])

            zero = (jnp.zeros((1, lanes), F32), jnp.zeros((1, lanes), F32))
            if d == 0:
                lax.fori_loop(0, n_tiles, tile_step, zero)
            else:
                mid = lax.fori_loop(0, ctx_tiles, lambda t, c: tile_step(ctx_tiles - 1 - t, c), zero)
                lax.fori_loop(0, lat_tiles, lambda t, c: tile_step(n_tiles - 1 - t, c), mid)

        x = xin[...].reshape(n_tiles * S5_TILE, 4 * lanes)
        y_all = y_intra + jnp.dot(x.astype(BF16), w2_ref[p], preferred_element_type=F32)
        yc_ref[:, cols] = y_all[:ctx_rows]
        yl_ref[:, cols] = y_all[ctx_rows:]


def _s5(v, weights, *, n_batch, seq, ctx_len):
    kflat, wz, w2, tab = weights
    q = S5_CHUNK
    lanes = 2 * S5_STATE
    pw = 2 * q * S5_GROUP
    n_pairs = w2.shape[0]
    ppb = S5_LANE_GROUPS // 2
    n_blocks = n_pairs // ppb
    bw = ppb * pw
    lat_rows = seq // q
    ctx_rows = ctx_len // q
    n_tiles = (lat_rows + ctx_rows) // S5_TILE
    ctx_blk0 = n_batch * lat_rows // ctx_rows
    kern = functools.partial(_s5_kernel, lat_tiles=lat_rows // S5_TILE, ctx_tiles=ctx_rows // S5_TILE)
    return pl.pallas_call(
        kern,
        grid=(n_blocks, n_batch),
        in_specs=[
            pl.BlockSpec((lat_rows, bw), lambda j, b: (b, j)),
            pl.BlockSpec((ctx_rows, bw), lambda j, b: (ctx_blk0 + b, j)),
            pl.BlockSpec((S5_LANE_GROUPS,) + kflat.shape[1:], lambda j, b: (j, 0, 0)),
            pl.BlockSpec((ppb, pw, 4 * lanes), lambda j, b: (j, 0, 0)),
            pl.BlockSpec((ppb, 4 * lanes, pw), lambda j, b: (j, 0, 0)),
            pl.BlockSpec((ppb, tab.shape[1], lanes), lambda j, b: (j, 0, 0)),
        ],
        out_specs=[pl.BlockSpec((lat_rows, bw), lambda j, b: (b, j)),
                   pl.BlockSpec((ctx_rows, bw), lambda j, b: (b, j))],
        out_shape=[jax.ShapeDtypeStruct((n_batch * lat_rows, n_blocks * bw), F32),
                   jax.ShapeDtypeStruct((n_batch * ctx_rows, n_blocks * bw), F32)],
        scratch_shapes=[
            pltpu.VMEM((ppb, pw, pw), BF16),
            pltpu.VMEM((8, n_tiles, S5_TILE, lanes), F32),
            pltpu.VMEM((n_tiles, S5_TILE, 4 * lanes), F32),
        ],
        compiler_params=_cparams("arbitrary", "arbitrary"),
        name="s5",
    )(v, v, kflat, wz, w2, tab)


NA_QROWS = 4
NA_KROWS = NA_QROWS + NA_KH


NA_REL_ROWS = 2 * NA_KH - 1


def _na_tile_index(kind, a, m):
    first, rel0 = ((0, NA_KH - 1 - a), (a, NA_QROWS - 1 - a), (NA_QROWS, -1 - a))[kind]
    return rel0 + m if first <= m < first + NA_KH else NA_REL_ROWS


def _na_bias_tiles(rpb):
    w = GRID_W
    qcol = np.arange(w)
    kcol = np.arange(w)
    wstart = np.clip(qcol - NA_KW // 2, 0, w - NA_KW)
    valid = (kcol[None, :] >= wstart[:, None]) & (kcol[None, :] < wstart[:, None] + NA_KW)
    rel = np.clip(kcol[None, :] - qcol[:, None], -(NA_KW - 1), NA_KW - 1) + NA_KW - 1
    onehot = (rel[None] == np.arange(2 * NA_KW - 1)[:, None, None]).astype(np.float32)
    tiles = jnp.einsum('hrj,jqk->hrqk', rpb.astype(F32), jnp.asarray(onehot),
                       precision=lax.Precision.HIGHEST)
    tiles = jnp.where(jnp.asarray(valid)[None, None], tiles, NEG_INF)
    return jnp.concatenate([tiles, jnp.full((NA_HEADS, 1, w, w), NEG_INF, F32)], axis=1)


def _na_kernel(q_ref, k_ref, v_ref, kc_ref, vc_ref, tiles_ref, o_ref, bias_ref, kt, kct, vx, vcx,
               s_even, s_odd, p_even, p_odd, *, rows):
    w = GRID_W
    dh = NA_HEAD_DIM
    lb = 2 * dh
    nq = NA_QROWS * w
    nk = NA_KROWS * w
    n_blocks = rows // NA_QROWS

    @pl.when(pl.program_id(1) == 0)
    def _():
        for kind in range(3):
            for hh in range(2):
                for a in range(NA_QROWS):
                    for m in range(0, NA_KROWS, 2):
                        pair = [tiles_ref[hh, _na_tile_index(kind, a, m + e)] for e in range(2)]
                        bias_ref[kind, hh * nq + a * w:hh * nq + (a + 1) * w, m * w:(m + 2) * w] = (
                            jnp.concatenate(pair, axis=1))

    scale = dh ** -0.5
    lane = lax.broadcasted_iota(jnp.int32, (nq, lb), 1)
    first = lane < dh

    kt[...] = k_ref[...].T
    kct[...] = kc_ref[...].T

    for dst, src in ((vx, v_ref), (vcx, vc_ref)):
        n = src.shape[0]
        dst[:, :lb] = src[...]
        dst[:, lb:] = (lax.broadcasted_iota(jnp.int32, (n, lb), 1) == 0).astype(BF16)

    def key_rows(i):
        r0 = min(max(NA_QROWS * i - NA_KH // 2, 0), rows - NA_KROWS)
        return slice(r0 * w, r0 * w + nk)

    def scores(i, s_ref):
        kind = 0 if i == 0 else (2 if i == n_blocks - 1 else 1)
        q = q_ref[i * nq:(i + 1) * nq, :] * scale
        zero = jnp.zeros_like(q)
        qs = jnp.concatenate([jnp.where(first, q, zero), jnp.where(first, zero, q)], axis=0)
        s_ref[:, :nk] = jnp.dot(qs, kt[:, key_rows(i)], preferred_element_type=F32) + bias_ref[kind]
        s_ref[:, nk:] = jnp.dot(qs, kct[...], preferred_element_type=F32)

    def softmax(s_ref, p_ref):
        s = s_ref[...]
        p_ref[...] = jnp.exp(s - jnp.max(s, axis=-1, keepdims=True)).astype(BF16)

    def attend(i, p_ref):
        o = jnp.dot(p_ref[:, :nk], vx[key_rows(i), :], preferred_element_type=F32)
        o += jnp.dot(p_ref[:, nk:], vcx[...], preferred_element_type=F32)
        o = o[:, :lb] / o[:, lb:lb + 1]
        o_ref[i * nq:(i + 1) * nq, :] = jnp.where(first, o[:nq], o[nq:]).astype(o_ref.dtype)

    scores(0, s_even)
    softmax(s_even, p_even)
    scores(1, s_odd)
    for j in range(1, n_blocks // 2):
        attend(2 * j - 2, p_even)
        softmax(s_odd, p_odd)
        scores(2 * j, s_even)
        attend(2 * j - 1, p_odd)
        softmax(s_even, p_even)
        scores(2 * j + 1, s_odd)
    attend(n_blocks - 2, p_even)
    softmax(s_odd, p_odd)
    attend(n_blocks - 1, p_odd)


def _natten(p, tiles, *, n_batch, seq, ctx_len):
    d = NA_HEADS * NA_HEAD_DIM
    lb = 2 * NA_HEAD_DIM
    n_pairs = NA_HEADS // 2
    ctx_blk0 = n_batch * seq // ctx_len
    rows = seq // GRID_W
    assert rows % (2 * NA_QROWS) == 0 and rows >= NA_KROWS + NA_QROWS
    stacked = 2 * NA_QROWS * GRID_W
    n_keys = NA_KROWS * GRID_W + ctx_len
    lat = lambda part: pl.BlockSpec((seq, lb), lambda j, b: (b, part * n_pairs + j))
    ctx = lambda part: pl.BlockSpec((ctx_len, lb), lambda j, b: (ctx_blk0 + b, part * n_pairs + j))
    return pl.pallas_call(
        functools.partial(_na_kernel, rows=rows),
        grid=(n_pairs, n_batch),
        in_specs=[lat(0), lat(1), lat(2), ctx(1), ctx(2),
                  pl.BlockSpec((2,) + tiles.shape[1:], lambda j, b: (j, 0, 0, 0))],
        out_specs=pl.BlockSpec((seq, lb), lambda j, b: (b, j)),
        out_shape=jax.ShapeDtypeStruct((n_batch * seq, d), BF16),
        scratch_shapes=[pltpu.VMEM((3, stacked, NA_KROWS * GRID_W), F32),
                        pltpu.VMEM((lb, seq), BF16), pltpu.VMEM((lb, ctx_len), BF16),
                        pltpu.VMEM((seq, 2 * lb), BF16), pltpu.VMEM((ctx_len, 2 * lb), BF16),
                        pltpu.VMEM((stacked, n_keys), F32), pltpu.VMEM((stacked, n_keys), F32),
                        pltpu.VMEM((stacked, n_keys), BF16), pltpu.VMEM((stacked, n_keys), BF16)],
        compiler_params=_cparams("arbitrary", "arbitrary"),
        name="natten",
    )(p, p, p, p, p, tiles)


def kernel(x, c, ctx, c_ctx, w_mod, b_mod, norm_g, ffn_w1, ffn_w2, w_in_ab, w_out_ab, ret_decay_logit, s5_lam_re, s5_lam_im, s5_log_dt, s5_b_re, s5_b_im, s5_c_re, s5_c_im, s5_d, s5_glu_w, s5_glu_b, na_w_qkv, na_w_o, na_rpb, final_g):
    n_batch, seq, d = x.shape
    ctx_len = ctx.shape[1]
    depth = w_mod.shape[0]
    n_lat = n_batch * seq
    n_all = n_lat + n_batch * ctx_len
    lat_tiles = seq // TOKEN_TILE
    assert seq % TOKEN_TILE == 0 and (n_batch * ctx_len) % TOKEN_TILE == 0
    assert n_batch + 1 <= MOD_ROWS and seq % (GRID_W * NA_KH) == 0

    cvec = jnp.concatenate([c, c_ctx[None], jnp.zeros((MOD_ROWS - n_batch - 1, d), F32)], axis=0)
    mod = _modulation(cvec, w_mod, b_mod).reshape(depth, MOD_ROWS, N_MOD, d)
    h_parts = (x.reshape(n_lat, d), ctx.reshape(n_batch * ctx_len, d))
    common = dict(lat_tiles=lat_tiles, n_batch=n_batch)
    dims = dict(n_batch=n_batch, seq=seq, ctx_len=ctx_len)
    gains = norm_g.astype(F32).reshape(depth, 3, 1, d)
    w1 = ffn_w1.astype(BF16)
    w2 = ffn_w2.astype(BF16)

    for layer in range(depth):
        last = layer == depth - 1
        i = layer // 2
        half = functools.partial(_half_layer, mod=mod[layer], gains=gains, w1=w1, w2=w2, layer=layer,
                                 **common)
        if layer % 2 == 0:
            h, pq, pg, pv = half(h_parts, n_rows=n_all, post="ab", post_w=w_in_ab.astype(BF16),
                                 post_wi=i, rope=_rope_tables(seq))
            r = _retention(pq, pg, ret_decay_logit[i], **dims)
            weights = _s5_weights(s5_lam_re[i], s5_lam_im[i], s5_log_dt[i], s5_b_re[i], s5_b_im[i],
                                  s5_c_re[i], s5_c_im[i], s5_d[i])
            ys_parts = _s5(pv, weights, **dims)
            pre = dict(pre="ab", pre_args=(r, ys_parts, s5_glu_w.astype(BF16),
                                           s5_glu_b.astype(F32)[:, None, :], w_out_ab.astype(BF16), i))
        else:
            assert last
            h, p = half(h_parts, n_rows=n_all, post="na", post_w=na_w_qkv.astype(BF16), post_wi=i)
            att = _natten(p, _na_bias_tiles(na_rpb[i]), **dims)
            pre = dict(pre="na", pre_args=(att, na_w_o.astype(BF16), i))
        (h,) = half((h,), n_rows=n_lat if last else n_all, final_g=final_g if last else None, **pre)
        h_parts = (h,)
    return h[:n_lat].reshape(n_batch, seq, d)
```

```python
import functools
import math
from typing import NamedTuple

import numpy as np
import jax
import jax.numpy as jnp
from jax import lax
from jax.experimental import pallas as pl
from jax.experimental.pallas import tpu as pltpu

F32 = jnp.float32
BF16 = jnp.bfloat16

EPS = 1e-6
ROPE_BASE = 10000.0
GRID_W = 64
N_MOD = 9
RET_HEADS = 4
RET_HEAD_DIM = 128
RET_CHUNK = 256
S5_GROUP = 16
S5_STATE = 64
S5_CHUNK = 16
S5_TILE = 8
NA_HEADS = 16
NA_HEAD_DIM = 64
NA_KH = 8
NA_KW = 16
NEG_INF = -1e30

TOKEN_TILE = 512
VMEM_LIMIT = 56 * 1024 * 1024
MOD_ROWS = 8


def _cparams(*sem):
    return pltpu.CompilerParams(dimension_semantics=sem, vmem_limit_bytes=VMEM_LIMIT)


def _resident(shape):
    nd = len(shape)
    return pl.BlockSpec(shape, lambda *_: (0,) * nd, pipeline_mode=pl.Buffered(1))


def _split_bf16(x):
    hi = x.astype(BF16)
    return hi, (x - hi.astype(F32)).astype(BF16)


def _mod_kernel(c_ref, w_ref, b_ref, o_ref):
    c = c_ref[...]
    s_hi, s_lo = _split_bf16(c * jax.nn.sigmoid(c))
    w_hi, w_lo = _split_bf16(w_ref[0])
    acc = jnp.dot(s_hi, w_lo, preferred_element_type=F32)
    acc += jnp.dot(s_lo, w_hi, preferred_element_type=F32)
    acc += jnp.dot(s_hi, w_hi, preferred_element_type=F32)
    o_ref[0] = acc + b_ref[0]


def _modulation(cvec, w_mod, b_mod):
    depth, d, nd = w_mod.shape
    tn = nd // 4 if nd % 512 == 0 else d
    return pl.pallas_call(
        _mod_kernel,
        grid=(depth, nd // tn),
        in_specs=[
            pl.BlockSpec((MOD_ROWS, d), lambda l, j: (0, 0)),
            pl.BlockSpec((1, d, tn), lambda l, j: (l, 0, j)),
            pl.BlockSpec((1, 1, tn), lambda l, j: (l, 0, j)),
        ],
        out_specs=pl.BlockSpec((1, MOD_ROWS, tn), lambda l, j: (l, 0, j)),
        out_shape=jax.ShapeDtypeStruct((depth, MOD_ROWS, nd), F32),
        compiler_params=_cparams("parallel", "parallel"),
        name="modulation",
    )(cvec, w_mod, b_mod.reshape(depth, 1, nd))


def _rms(x):
    return x * lax.rsqrt(jnp.mean(x * x, axis=-1, keepdims=True) + EPS)


def _modulated(h, m, g, mi):
    return (_rms(h) * g) * (1.0 + m[mi + 1:mi + 2]) + m[mi:mi + 1]


def _tile_specs(n_lat_tiles_per_batch, n_batch, d):
    def mod_idx(i):
        return (jnp.minimum(i // n_lat_tiles_per_batch, n_batch), 0, 0)
    h_spec = pl.BlockSpec((TOKEN_TILE, d), lambda i: (i, 0))
    m_spec = pl.BlockSpec((1, N_MOD, d), mod_idx)
    return h_spec, m_spec


def _pinned(block_shape, index):
    return pl.BlockSpec(block_shape, lambda *_: index, pipeline_mode=pl.Buffered(1))


def _stream_specs(parts, width, rows=TOKEN_TILE):
    if len(parts) == 1:
        return [pl.BlockSpec((rows, width), lambda i: (i, 0))], 0
    n0 = parts[0].shape[0] // rows
    return [pl.BlockSpec((rows, width), lambda i: (jnp.minimum(i, n0 - 1), 0)),
            pl.BlockSpec((rows, width), lambda i: (jnp.maximum(i - n0, 0), 0))], n0


def _stream_tile(refs, n0):
    if len(refs) == 1:
        return refs[0][...]
    return jnp.where(pl.program_id(0) < n0, refs[0][...], refs[1][...])


class _HalfCfg(NamedTuple):
    n_h: int
    n0: int
    pre: str
    n_r: int
    nr0: int
    n_y: int
    ny0: int
    post: str
    final: bool


def _gelu_tanh(y):
    return 0.5 * y * (1.0 + jnp.tanh(math.sqrt(2.0 / math.pi) * (y + 0.044715 * (y * y * y))))


S5_CHUNKS_PER_TILE = TOKEN_TILE // S5_CHUNK


def _to_chunk_rows(u_scr, out_ref):
    half = 128 // S5_GROUP
    gw = S5_CHUNK * S5_GROUP
    for blk in range(u_scr.shape[0]):
        for th in range(S5_CHUNK // half):
            arrs = [u_scr[blk, pl.ds(th * half + tl, S5_CHUNKS_PER_TILE, stride=S5_CHUNK), :]
                    for tl in range(half)]
            for g, x in enumerate(_block_transpose(arrs)):
                lo = (blk * half + g) * gw + th * 128
                out_ref[:, lo:lo + 128] = x


def _from_chunk_rows(y, y_scr):
    half = 128 // S5_GROUP
    gw = S5_CHUNK * S5_GROUP
    for blk in range(y_scr.shape[0]):
        for th in range(S5_CHUNK // half):
            arrs = [y[:, (blk * half + g) * gw + th * 128:(blk * half + g) * gw + (th + 1) * 128]
                    for g in range(half)]
            for tl, x in enumerate(_block_transpose(arrs)):
                y_scr[blk, pl.ds(th * half + tl, S5_CHUNKS_PER_TILE, stride=S5_CHUNK), :] = x


def _half_kernel(*refs, cfg):
    it = iter(refs)
    take = lambda n: [next(it) for _ in range(n)]
    h_refs = take(cfg.n_h)
    m_ref, g_ref = take(2)
    h = _stream_tile(h_refs, cfg.n0)
    m = m_ref[0]
    gate_mix = m[5:6]
    relayout_scr = refs[-1]
    if cfg.pre == "ab":
        r_refs = take(cfg.n_r)
        y_refs = take(cfg.n_y)
        gw_ref, gb_ref, wr_ref, ws_ref = take(4)
        h = h + gate_mix * jnp.dot(_stream_tile(r_refs, cfg.nr0), wr_ref[...],
                                   preferred_element_type=F32)
        _from_chunk_rows(_stream_tile(y_refs, cfg.ny0), relayout_scr)
        g = _gelu_tanh(jnp.concatenate([relayout_scr[j] for j in range(relayout_scr.shape[0])], axis=1))
        s = g * jax.nn.sigmoid(jnp.dot(g.astype(BF16), gw_ref[...], preferred_element_type=F32)
                               + gb_ref[...])
        h = h + gate_mix * jnp.dot(s.astype(BF16), ws_ref[...], preferred_element_type=F32)
    elif cfg.pre == "na":
        a_ref, wo_ref = take(2)
        h = h + gate_mix * jnp.dot(a_ref[...], wo_ref[...], preferred_element_type=F32)
    w1a_ref, w1b_ref, w2_ref = take(3)
    k = 1 if cfg.pre else 0
    mi = 6 * k
    xm = _modulated(h, m, g_ref[2 * k], mi).astype(BF16)
    a = jnp.dot(xm, w1a_ref[...], preferred_element_type=F32)
    b = jnp.dot(xm, w1b_ref[...], preferred_element_type=F32)
    hid = (a * jax.nn.sigmoid(a) * b).astype(BF16)
    h = h + (0.5 * m[mi + 2:mi + 3]) * jnp.dot(hid, w2_ref[...], preferred_element_type=F32)
    if cfg.post:
        (wp_ref,) = take(1)
    if cfg.post == "ab":
        cos_ref, sa_ref, sb_ref = take(3)
    if cfg.final:
        (fg_ref,) = take(1)
    outs = list(it)
    if "ab" in (cfg.pre, cfg.post):
        outs.pop()
    outs[0][...] = _rms(h) * fg_ref[...] if cfg.final else h
    if not cfg.post:
        return
    xm = _modulated(h, m, g_ref[1], 3).astype(BF16)
    if cfg.post == "na":
        outs[1][...] = jnp.dot(xm, wp_ref[...], preferred_element_type=F32).astype(BF16)
        return
    width = RET_HEADS * RET_HEAD_DIM
    pr_u = jnp.dot(xm, wp_ref[:, 4 * width:], preferred_element_type=F32)
    for j in range(relayout_scr.shape[0]):
        relayout_scr[j] = pr_u[:, j * 128:(j + 1) * 128]
    _to_chunk_rows(relayout_scr, outs[3])
    pr_qk = jnp.dot(xm, wp_ref[:, :2 * width], preferred_element_type=F32)
    cos, sa, sb = cos_ref[...], sa_ref[...], sb_ref[...]
    for hh in range(RET_HEADS):
        lo = hh * RET_HEAD_DIM
        hi = lo + RET_HEAD_DIM
        outs[1][:, lo:hi] = _rope(pr_qk[:, lo:hi], cos, sa, sb).astype(BF16)
        kr = _rope(pr_qk[:, width + lo:width + hi], cos, sa, sb) * (RET_HEAD_DIM ** -0.5)
        outs[1][:, width + lo:width + hi] = kr.astype(BF16)
    pr_vg = jnp.dot(xm, wp_ref[:, 2 * width:4 * width], preferred_element_type=F32)
    outs[1][:, 2 * width:] = pr_vg[:, :width].astype(BF16)
    outs[2][...] = pr_vg[:, width:]


def _half_layer(h_parts, mod, gains, w1, w2, *, layer, n_rows, lat_tiles, n_batch,
                pre="", pre_args=(), post="", post_w=None, post_wi=0, rope=None, final_g=None):
    d = h_parts[0].shape[1]
    f = w2.shape[2]
    k = 1 if pre else 0
    h_specs, n0 = _stream_specs(h_parts, d)
    _, m_spec = _tile_specs(lat_tiles, n_batch, d)
    in_specs = h_specs + [m_spec, _pinned((None, 3, 1, d), (layer, 0, 0, 0))]
    args = list(h_parts) + [mod, gains]
    n_r = nr0 = n_y = ny0 = 0
    scratch = []
    if pre == "ab":
        r_parts, ys_parts, glu_w, glu_b, w_out, wi = pre_args
        w = r_parts[0].shape[1]
        r_specs, nr0 = _stream_specs(r_parts, w)
        y_specs, ny0 = _stream_specs(ys_parts, S5_CHUNK * w, rows=S5_CHUNKS_PER_TILE)
        n_r, n_y = len(r_parts), len(ys_parts)
        scratch = [pltpu.VMEM((w // 128, TOKEN_TILE, 128), F32)]
        in_specs += r_specs + y_specs + [
            _pinned((None, w, w), (wi, 0, 0)), _pinned((None, 1, w), (wi, 0, 0)),
            _pinned((None, w, d), (wi, 0, 0)), _pinned((None, w, d), (wi, 1, 0))]
        args += [*r_parts, *ys_parts, glu_w, glu_b, w_out, w_out]
    elif pre == "na":
        att, w_o, wi = pre_args
        in_specs += [pl.BlockSpec((TOKEN_TILE, d), lambda i: (i, 0)), _pinned((None, d, d), (wi, 0, 0))]
        args += [att, w_o]
    in_specs += [_pinned((None, None, d, f), (layer, k, 0, 0)),
                 _pinned((None, None, d, f), (layer, k, 0, 1)),
                 _pinned((None, None, f, d), (layer, k, 0, 0))]
    args += [w1, w1, w2]
    out_specs = [pl.BlockSpec((TOKEN_TILE, d), lambda i: (i, 0))]
    out_shape = [jax.ShapeDtypeStruct((n_rows, d), F32)]
    if post:
        n = post_w.shape[2]
        in_specs.append(_pinned((None, d, n), (post_wi, 0, 0)))
        args.append(post_w)
    if post == "ab":
        n_lat_tiles = lat_tiles * n_batch
        tab = pl.BlockSpec((TOKEN_TILE, RET_HEAD_DIM),
                           lambda i: (jnp.where(i < n_lat_tiles, i % lat_tiles, lat_tiles), 0))
        in_specs += [tab, tab, tab]
        args += list(rope)
        width = RET_HEADS * RET_HEAD_DIM
        n_u = n - 4 * width
        out_specs += [pl.BlockSpec((TOKEN_TILE, 3 * width), lambda i: (i, 0)),
                      pl.BlockSpec((TOKEN_TILE, width), lambda i: (i, 0)),
                      pl.BlockSpec((S5_CHUNKS_PER_TILE, S5_CHUNK * n_u), lambda i: (i, 0))]
        out_shape += [jax.ShapeDtypeStruct((n_rows, 3 * width), BF16),
                      jax.ShapeDtypeStruct((n_rows, width), F32),
                      jax.ShapeDtypeStruct((n_rows // S5_CHUNK, S5_CHUNK * n_u), F32)]
        scratch = [pltpu.VMEM((n_u // 128, TOKEN_TILE, 128), F32)]
    elif post == "na":
        out_specs.append(pl.BlockSpec((TOKEN_TILE, n), lambda i: (i, 0)))
        out_shape.append(jax.ShapeDtypeStruct((n_rows, n), BF16))
    if final_g is not None:
        in_specs.append(_resident((1, d)))
        args.append(final_g.reshape(1, d))
    cfg = _HalfCfg(n_h=len(h_parts), n0=n0, pre=pre, n_r=n_r, nr0=nr0, n_y=n_y, ny0=ny0, post=post,
                   final=final_g is not None)
    return pl.pallas_call(
        functools.partial(_half_kernel, cfg=cfg),
        grid=(n_rows // TOKEN_TILE,),
        in_specs=in_specs,
        out_specs=out_specs,
        out_shape=out_shape,
        scratch_shapes=scratch,
        compiler_params=_cparams("parallel"),
        name="half_layer",
    )(*args)


def _rope_tables(seq):
    half = RET_HEAD_DIM // 2
    quarter = half // 2
    inv = ROPE_BASE ** (-np.arange(0, half, 2, dtype=np.float64) / half)
    t = np.arange(seq)
    lane = np.arange(RET_HEAD_DIM)
    pos = np.where(lane[None, :] < half, (t // GRID_W)[:, None], (t % GRID_W)[:, None])
    ang = pos.astype(np.float32).astype(np.float64) * inv.astype(np.float32)[lane % quarter][None, :]
    first = (lane % half) < quarter
    cos = np.cos(ang)
    sin = np.sin(ang)
    sa = np.where(first[None, :], -sin, 0.0)
    sb = np.where(first[None, :], 0.0, sin)
    pad1 = np.ones((TOKEN_TILE, RET_HEAD_DIM))
    pad0 = np.zeros((TOKEN_TILE, RET_HEAD_DIM))
    tabs = [np.concatenate([cos, pad1]), np.concatenate([sa, pad0]), np.concatenate([sb, pad0])]
    return [jnp.asarray(x, F32) for x in tabs]


def _rope(x, cos, sa, sb):
    quarter = RET_HEAD_DIM // 4
    up = pltpu.roll(x, RET_HEAD_DIM - quarter, axis=1)
    dn = pltpu.roll(x, quarter, axis=1)
    return x * cos + up * sa + dn * sb


RET_STEP_CHUNKS = 2


def _retention_kernel(dec_ref, ql_ref, kl_ref, vl_ref, qc_ref, kc_ref, vc_ref, gl_ref, gc_ref,
                      dmask_ref, qwf_ref, qwb_ref, kwf_ref, kwb_ref, ol_ref, oc_ref,
                      s_run, s_bwd, *, n_lat, n_ctx):
    i = pl.program_id(1)
    hd = RET_HEAD_DIM
    c = RET_CHUNK
    tn = (((0,), (0,)), ((), ()))
    nt = (((1,), (1,)), ((), ()))

    def bwd_chunk(k_ref, v_ref, rows, cid):
        for h in range(RET_HEADS):
            sl = slice(h * hd, (h + 1) * hd)
            s_old = s_run[h]
            s_bwd[cid, h] = s_old
            kw = (k_ref[rows, sl] * kwb_ref[h]).astype(BF16)
            kv = lax.dot_general(kw, v_ref[rows, sl], tn, preferred_element_type=F32)
            s_run[h] = dec_ref[1, h] * s_old + kv

    def fwd_chunk(q_ref, k_ref, v_ref, rows, cid, g_ref, o_ref, blk_rows):
        for h in range(RET_HEADS):
            sl = slice(h * hd, (h + 1) * hd)
            q = q_ref[rows, sl]
            k = k_ref[rows, sl]
            v = v_ref[rows, sl]
            s_old = s_run[h]
            a = lax.dot_general(q, k, nt, preferred_element_type=F32)
            o = jnp.dot((a * dmask_ref[h]).astype(BF16), v, preferred_element_type=F32)
            o += qwf_ref[h] * jnp.dot(q, s_old.astype(BF16), preferred_element_type=F32)
            o += qwb_ref[h] * jnp.dot(q, s_bwd[cid, h].astype(BF16), preferred_element_type=F32)
            kv = lax.dot_general((k * kwf_ref[h]).astype(BF16), v, tn, preferred_element_type=F32)
            s_run[h] = dec_ref[0, h] * s_old + kv
            o = o * lax.rsqrt(jnp.mean(o * o, axis=-1, keepdims=True) + EPS)
            g = g_ref[blk_rows, sl]
            o_ref[blk_rows, sl] = (o * (g * jax.nn.sigmoid(g))).astype(o_ref.dtype)

    def lat_rows(j):
        return pl.ds(pl.multiple_of(j * c, c), c)

    @pl.when(i == 0)
    def _():
        s_run[...] = jnp.zeros_like(s_run)
        for j in reversed(range(n_ctx)):
            bwd_chunk(kc_ref, vc_ref, slice(j * c, (j + 1) * c), j)

        def body(t, carry):
            j = n_lat - 1 - t
            bwd_chunk(kl_ref, vl_ref, lat_rows(j), n_ctx + j)
            return carry

        lax.fori_loop(0, n_lat, body, 0)
        s_run[...] = jnp.zeros_like(s_run)

    for j in range(n_ctx):
        @pl.when(i == 1 + j)
        def _(j=j):
            rows = slice(j * c, (j + 1) * c)
            fwd_chunk(qc_ref, kc_ref, vc_ref, rows, j, gc_ref, oc_ref, rows)

    @pl.when(i > n_ctx)
    def _():
        first = (i - 1 - n_ctx) * RET_STEP_CHUNKS
        for e in range(RET_STEP_CHUNKS):
            fwd_chunk(ql_ref, kl_ref, vl_ref, lat_rows(first + e), n_ctx + first + e,
                      gl_ref, ol_ref, slice(e * c, (e + 1) * c))


def _retention(pq, pg, decay_logit, *, n_batch, seq, ctx_len):
    c = RET_CHUNK
    width = RET_HEADS * RET_HEAD_DIM
    n_lat = seq // c
    n_ctx = ctx_len // c
    n_ch = n_lat + n_ctx

    log_gamma = jax.nn.log_sigmoid(decay_logit.astype(F32))
    pos = jnp.arange(c, dtype=F32)
    diff = pos[:, None] - pos[None, :]
    lf = log_gamma[0][:, None, None]
    lb = log_gamma[1][:, None, None]
    dmask = (jnp.where(diff >= 0, jnp.exp(lf * jnp.maximum(diff, 0.0)), 0.0)
             + jnp.where(diff <= 0, jnp.exp(lb * jnp.maximum(-diff, 0.0)), 0.0))
    ones = jnp.ones((1, 1, RET_HEAD_DIM), F32)
    col = pos[None, :, None]
    qwf = jnp.exp(lf * (col + 1.0)) * ones
    qwb = jnp.exp(lb * (c - col)) * ones
    kwf = jnp.exp(lf * (c - 1.0 - col)) * ones
    kwb = jnp.exp(lb * col) * ones
    chunk_decay = jnp.exp(log_gamma * c)

    ctx_blk0 = n_batch * seq // ctx_len
    step_rows = RET_STEP_CHUNKS * c
    lat_steps = n_lat // RET_STEP_CHUNKS
    assert n_lat % RET_STEP_CHUNKS == 0

    lat_step = pl.BlockSpec((step_rows, width),
                            lambda b, i: (b * lat_steps + jnp.maximum(i - 1 - n_ctx, 0), 0))
    lat = lambda part: pl.BlockSpec((seq, width), lambda b, i: (b, part))
    ctx = lambda part: pl.BlockSpec((ctx_len, width), lambda b, i: (ctx_blk0 + b, part))
    head_tab = _resident((RET_HEADS, c, RET_HEAD_DIM))
    kern = functools.partial(_retention_kernel, n_lat=n_lat, n_ctx=n_ctx)
    return pl.pallas_call(
        kern,
        grid=(n_batch, 1 + n_ctx + lat_steps),
        in_specs=[
            pl.BlockSpec(memory_space=pltpu.SMEM),
            lat(0), lat(1), lat(2), ctx(0), ctx(1), ctx(2), lat_step, ctx(0),
            _resident((RET_HEADS, c, c)), head_tab, head_tab, head_tab, head_tab,
        ],
        out_specs=[lat_step, pl.BlockSpec((ctx_len, width), lambda b, i: (b, 0))],
        out_shape=[jax.ShapeDtypeStruct((n_batch * seq, width), BF16),
                   jax.ShapeDtypeStruct((n_batch * ctx_len, width), BF16)],
        scratch_shapes=[
            pltpu.VMEM((RET_HEADS, RET_HEAD_DIM, RET_HEAD_DIM), F32),
            pltpu.VMEM((n_ch, RET_HEADS, RET_HEAD_DIM, RET_HEAD_DIM), F32),
        ],
        compiler_params=_cparams("parallel", "arbitrary"),
        name="retention",
    )(chunk_decay, pq, pq, pq, pq, pq, pq, pg, pg, dmask, qwf, qwb, kwf, kwb)


def _cmul(a, b):
    return a[0] * b[0] - a[1] * b[1], a[0] * b[1] + a[1] * b[0]


def _pair_blockdiag(a):
    a0, a1 = a[0::2], a[1::2]
    z = jnp.zeros_like(a0)
    return jnp.concatenate([jnp.concatenate([a0, z], axis=2), jnp.concatenate([z, a1], axis=2)],
                           axis=1)


def _pair_lanes(a):
    return jnp.concatenate([a[0::2], a[1::2]], axis=2)


def _powers(a, n):
    one = (jnp.ones_like(a[0]), jnp.zeros_like(a[0]))
    pw = (jnp.stack([one[0], a[0]], axis=1), jnp.stack([one[1], a[1]], axis=1))
    step = a
    while pw[0].shape[1] < n:
        step = _cmul(step, step)
        nxt = _cmul(pw, (step[0][:, None], step[1][:, None]))
        pw = (jnp.concatenate([pw[0], nxt[0]], axis=1), jnp.concatenate([pw[1], nxt[1]], axis=1))
    top = _cmul(step, step)
    return (jnp.concatenate([pw[0], top[0][:, None]], axis=1),
            jnp.concatenate([pw[1], top[1][:, None]], axis=1))


def _s5_weights(lam_re, lam_im, log_dt, b_re, b_im, c_re, c_im, d_skip):
    q = S5_CHUNK
    n_g = lam_re.shape[1]
    hi = lax.Precision.HIGHEST
    kall = None
    zw = []
    rw = []
    tabs = []
    for d in range(2):
        lr = jnp.minimum(lam_re[d].astype(F32), -1e-4)
        li = lam_im[d].astype(F32)
        dt = jnp.exp(log_dt[d].astype(F32))[:, None]
        mag = jnp.exp(lr * dt)
        a = (mag * jnp.cos(li * dt), mag * jnp.sin(li * dt))
        den = lr * lr + li * li
        am1 = (a[0] - 1.0, a[1])
        coef = ((am1[0] * lr + am1[1] * li) / den, (am1[1] * lr - am1[0] * li) / den)
        bt = (b_re[d].astype(F32).transpose(0, 2, 1), b_im[d].astype(F32).transpose(0, 2, 1))
        bb = _cmul((coef[0][:, None], coef[1][:, None]), bt)
        cm = (c_re[d].astype(F32), c_im[d].astype(F32))
        pw_re, pw_im = _powers(a, q)
        e = _cmul((pw_re[:, :q, None, :], pw_im[:, :q, None, :]),
                  (bb[0][:, None], bb[1][:, None]))
        taps = (jnp.einsum('gkp,gtjp->gjtk', cm[0], e[0], precision=hi)
                - jnp.einsum('gkp,gtjp->gjtk', cm[1], e[1], precision=hi))
        if d == 0:
            kall = jnp.concatenate([jnp.zeros_like(taps[:, :, 1:]), taps], axis=2)
        else:
            kall = kall + jnp.concatenate([taps[:, :, ::-1], jnp.zeros_like(taps[:, :, 1:])], axis=2)
        tsel = np.arange(q)[::-1] if d == 0 else np.arange(q)
        z = _cmul((pw_re[:, tsel][:, :, None, :], pw_im[:, tsel][:, :, None, :]),
                  (bb[0][:, None], bb[1][:, None]))
        zw.append((z[0].reshape(n_g, q * S5_GROUP, S5_STATE),
                   z[1].reshape(n_g, q * S5_GROUP, S5_STATE)))
        rsel = np.arange(1, q + 1) if d == 0 else np.arange(q, 0, -1)
        ct = (cm[0].transpose(0, 2, 1)[:, :, None, :], cm[1].transpose(0, 2, 1)[:, :, None, :])
        pt = (pw_re[:, rsel].transpose(0, 2, 1)[..., None], pw_im[:, rsel].transpose(0, 2, 1)[..., None])
        r = _cmul(ct, pt)
        rw.append((r[0].reshape(n_g, S5_STATE, q * S5_GROUP),
                   (-r[1]).reshape(n_g, S5_STATE, q * S5_GROUP)))
        a16 = (pw_re[:, q], pw_im[:, q])
        ramp_re, ramp_im = _powers(a16, S5_TILE)
        a32 = (ramp_re[:, 2], ramp_im[:, 2])
        a64 = (ramp_re[:, 4], ramp_im[:, 4])
        a128 = (ramp_re[:, 8], ramp_im[:, 8])
        rsl = slice(0, S5_TILE) if d == 0 else slice(S5_TILE - 1, None, -1)
        rows = [a16[0], a16[1], a32[0], a32[1], a64[0], a64[1], a128[0], a128[1]]
        tabs.append(jnp.concatenate([jnp.stack(rows, axis=1), ramp_re[:, rsl], ramp_im[:, rsl]],
                                    axis=1))
    dsk = d_skip.astype(F32).reshape(n_g, S5_GROUP)
    eye = jnp.eye(S5_GROUP, dtype=F32)
    kall = kall.at[:, :, q - 1].add(dsk[:, :, None] * eye[None])
    kall = jnp.concatenate([kall, jnp.zeros_like(kall[:, :, :1])], axis=2)
    kflat = kall.reshape(n_g, S5_GROUP, 2 * q * S5_GROUP)
    wz = jnp.concatenate([_pair_blockdiag(z) for pair in zw for z in pair], axis=2)
    w2 = jnp.concatenate([_pair_blockdiag(r) for pair in rw for r in pair], axis=1)
    tab = jnp.concatenate([_pair_lanes(t) for t in tabs], axis=1)
    return kflat, wz.astype(BF16), w2.astype(BF16), tab


def _tile_shift(x, s, down):
    row = lax.broadcasted_iota(jnp.int32, x.shape, 1)
    if down:
        return jnp.where(row >= s, pltpu.roll(x, s, axis=1), 0.0)
    return jnp.where(row < S5_TILE - s, pltpu.roll(x, S5_TILE - s, axis=1), 0.0)


S5_LANE_GROUPS = 128 // S5_GROUP


def _block_transpose(arrs):
    lane = lax.broadcasted_iota(jnp.int32, arrs[0].shape, 1)
    blk = lane // S5_GROUP
    a = list(arrs)
    for bit in range(3):
        s = 1 << bit
        hi = (blk & s) != 0
        new = list(a)
        for i in range(S5_LANE_GROUPS):
            if i & s:
                continue
            new[i] = jnp.where(hi, pltpu.roll(a[i + s], S5_GROUP * s, axis=1), a[i])
            new[i + s] = jnp.where(hi, a[i + s], pltpu.roll(a[i], 128 - S5_GROUP * s, axis=1))
        a = new
    return a


def _s5_kernel(vl_ref, vc_ref, kf_ref, wz_ref, w2_ref, tab_ref, yl_ref, yc_ref, m_scr, zx, xin,
               *, lat_tiles, ctx_tiles):
    q = S5_CHUNK
    lanes = 2 * S5_STATE
    pw = 2 * q * S5_GROUP
    gw = q * S5_GROUP

    @pl.when(pl.program_id(1) == 0)
    def _():
        m_scr[...] = jnp.zeros_like(m_scr)
        for g in range(S5_LANE_GROUPS):
            taps = kf_ref[g]
            base = (g % 2) * gw
            for ti in range(q):
                lo = S5_GROUP * (q - 1 - ti)
                m_scr[g // 2, base + ti * S5_GROUP:base + (ti + 1) * S5_GROUP, base:base + gw] = (
                    taps[:, lo:lo + gw].astype(BF16))

    n_tiles = ctx_tiles + lat_tiles
    ctx_rows = ctx_tiles * S5_TILE

    for p in range(m_scr.shape[0]):
        cols = slice(p * pw, (p + 1) * pw)
        v = jnp.concatenate([vc_ref[:, cols], vl_ref[:, cols]], axis=0).astype(BF16)
        y_intra = jnp.dot(v, m_scr[p], preferred_element_type=F32)
        y = jnp.dot(v, wz_ref[p], preferred_element_type=F32)

        def tab_row(r):
            return tab_ref[p, r:r + 1, :]

        for d in range(2):
            down = d == 0
            t0 = 24 * d
            lo = 2 * d * lanes
            z = (y[:, lo:lo + lanes].reshape(n_tiles, S5_TILE, lanes),
                 y[:, lo + lanes:lo + 2 * lanes].reshape(n_tiles, S5_TILE, lanes))
            loc = (_tile_shift(z[0], 1, down), _tile_shift(z[1], 1, down))
            for k, s in enumerate((1, 2, 4)):
                mul = (tab_row(t0 + 2 * k)[None], tab_row(t0 + 2 * k + 1)[None])
                inc = _cmul(mul, (_tile_shift(loc[0], s, down), _tile_shift(loc[1], s, down)))
                loc = (loc[0] + inc[0], loc[1] + inc[1])
            zx[2 * d] = loc[0]
            zx[2 * d + 1] = loc[1]
            zx[4 + 2 * d] = z[0]
            zx[4 + 2 * d + 1] = z[1]

        for d in range(2):
            t0 = 24 * d
            edge = S5_TILE - 1 if d == 0 else 0
            a16 = (tab_row(t0), tab_row(t0 + 1))
            a128 = (tab_row(t0 + 6), tab_row(t0 + 7))
            ramp = (tab_ref[p, t0 + 8:t0 + 16, :], tab_ref[p, t0 + 16:t0 + 24, :])

            def tile_step(j, carry, d=d, edge=edge, a16=a16, a128=a128, ramp=ramp):
                loc = (zx[2 * d, j], zx[2 * d + 1, j])
                z = (zx[4 + 2 * d, j], zx[4 + 2 * d + 1, j])
                inc = _cmul(ramp, carry)
                xin[j, :, (2 * d) * lanes:(2 * d + 1) * lanes] = loc[0] + inc[0]
                xin[j, :, (2 * d + 1) * lanes:(2 * d + 2) * lanes] = loc[1] + inc[1]
                e_loc = _cmul(a16, (loc[0][edge:edge + 1], loc[1][edge:edge + 1]))
                nxt = _cmul(a128, carry)
                return (nxt[0] + e_loc[0] + z[0][edge:edge + 1],
                        nxt[1] + e_loc[1] + z[1][edge:edge + 1])

            zero = (jnp.zeros((1, lanes), F32), jnp.zeros((1, lanes), F32))
            if d == 0:
                lax.fori_loop(0, n_tiles, tile_step, zero)
            else:
                mid = lax.fori_loop(0, ctx_tiles, lambda t, c: tile_step(ctx_tiles - 1 - t, c), zero)
                lax.fori_loop(0, lat_tiles, lambda t, c: tile_step(n_tiles - 1 - t, c), mid)

        x = xin[...].reshape(n_tiles * S5_TILE, 4 * lanes)
        y_all = y_intra + jnp.dot(x.astype(BF16), w2_ref[p], preferred_element_type=F32)
        yc_ref[:, cols] = y_all[:ctx_rows]
        yl_ref[:, cols] = y_all[ctx_rows:]


def _s5(v, weights, *, n_batch, seq, ctx_len):
    kflat, wz, w2, tab = weights
    q = S5_CHUNK
    lanes = 2 * S5_STATE
    pw = 2 * q * S5_GROUP
    n_pairs = w2.shape[0]
    ppb = S5_LANE_GROUPS // 2
    n_blocks = n_pairs // ppb
    bw = ppb * pw
    lat_rows = seq // q
    ctx_rows = ctx_len // q
    n_tiles = (lat_rows + ctx_rows) // S5_TILE
    ctx_blk0 = n_batch * lat_rows // ctx_rows
    kern = functools.partial(_s5_kernel, lat_tiles=lat_rows // S5_TILE, ctx_tiles=ctx_rows // S5_TILE)
    return pl.pallas_call(
        kern,
        grid=(n_blocks, n_batch),
        in_specs=[
            pl.BlockSpec((lat_rows, bw), lambda j, b: (b, j)),
            pl.BlockSpec((ctx_rows, bw), lambda j, b: (ctx_blk0 + b, j)),
            pl.BlockSpec((S5_LANE_GROUPS,) + kflat.shape[1:], lambda j, b: (j, 0, 0)),
            pl.BlockSpec((ppb, pw, 4 * lanes), lambda j, b: (j, 0, 0)),
            pl.BlockSpec((ppb, 4 * lanes, pw), lambda j, b: (j, 0, 0)),
            pl.BlockSpec((ppb, tab.shape[1], lanes), lambda j, b: (j, 0, 0)),
        ],
        out_specs=[pl.BlockSpec((lat_rows, bw), lambda j, b: (b, j)),
                   pl.BlockSpec((ctx_rows, bw), lambda j, b: (b, j))],
        out_shape=[jax.ShapeDtypeStruct((n_batch * lat_rows, n_blocks * bw), F32),
                   jax.ShapeDtypeStruct((n_batch * ctx_rows, n_blocks * bw), F32)],
        scratch_shapes=[
            pltpu.VMEM((ppb, pw, pw), BF16),
            pltpu.VMEM((8, n_tiles, S5_TILE, lanes), F32),
            pltpu.VMEM((n_tiles, S5_TILE, 4 * lanes), F32),
        ],
        compiler_params=_cparams("arbitrary", "arbitrary"),
        name="s5",
    )(v, v, kflat, wz, w2, tab)


NA_QROWS = 4
NA_KROWS = NA_QROWS + NA_KH


NA_REL_ROWS = 2 * NA_KH - 1


def _na_tile_index(kind, a, m):
    first, rel0 = ((0, NA_KH - 1 - a), (a, NA_QROWS - 1 - a), (NA_QROWS, -1 - a))[kind]
    return rel0 + m if first <= m < first + NA_KH else NA_REL_ROWS


def _na_bias_tiles(rpb):
    w = GRID_W
    qcol = np.arange(w)
    kcol = np.arange(w)
    wstart = np.clip(qcol - NA_KW // 2, 0, w - NA_KW)
    valid = (kcol[None, :] >= wstart[:, None]) & (kcol[None, :] < wstart[:, None] + NA_KW)
    rel = np.clip(kcol[None, :] - qcol[:, None], -(NA_KW - 1), NA_KW - 1) + NA_KW - 1
    onehot = (rel[None] == np.arange(2 * NA_KW - 1)[:, None, None]).astype(np.float32)
    tiles = jnp.einsum('hrj,jqk->hrqk', rpb.astype(F32), jnp.asarray(onehot),
                       precision=lax.Precision.HIGHEST)
    tiles = jnp.where(jnp.asarray(valid)[None, None], tiles, NEG_INF)
    return jnp.concatenate([tiles, jnp.full((NA_HEADS, 1, w, w), NEG_INF, F32)], axis=1)


def _na_kernel(q_ref, k_ref, v_ref, kc_ref, vc_ref, tiles_ref, o_ref, bias_ref, kt, kct, vx, vcx,
               s_even, s_odd, p_even, p_odd, *, rows):
    w = GRID_W
    dh = NA_HEAD_DIM
    lb = 2 * dh
    nq = NA_QROWS * w
    nk = NA_KROWS * w
    n_blocks = rows // NA_QROWS

    @pl.when(pl.program_id(1) == 0)
    def _():
        for kind in range(3):
            for hh in range(2):
                for a in range(NA_QROWS):
                    for m in range(0, NA_KROWS, 2):
                        pair = [tiles_ref[hh, _na_tile_index(kind, a, m + e)] for e in range(2)]
                        bias_ref[kind, hh * nq + a * w:hh * nq + (a + 1) * w, m * w:(m + 2) * w] = (
                            jnp.concatenate(pair, axis=1))

    scale = dh ** -0.5
    lane = lax.broadcasted_iota(jnp.int32, (nq, lb), 1)
    first = lane < dh

    kt[...] = k_ref[...].T
    kct[...] = kc_ref[...].T

    for dst, src in ((vx, v_ref), (vcx, vc_ref)):
        n = src.shape[0]
        dst[:, :lb] = src[...]
        dst[:, lb:] = (lax.broadcasted_iota(jnp.int32, (n, lb), 1) == 0).astype(BF16)

    def key_rows(i):
        r0 = min(max(NA_QROWS * i - NA_KH // 2, 0), rows - NA_KROWS)
        return slice(r0 * w, r0 * w + nk)

    def scores(i, s_ref):
        kind = 0 if i == 0 else (2 if i == n_blocks - 1 else 1)
        q = q_ref[i * nq:(i + 1) * nq, :] * scale
        zero = jnp.zeros_like(q)
        qs = jnp.concatenate([jnp.where(first, q, zero), jnp.where(first, zero, q)], axis=0)
        s_ref[:, :nk] = jnp.dot(qs, kt[:, key_rows(i)], preferred_element_type=F32) + bias_ref[kind]
        s_ref[:, nk:] = jnp.dot(qs, kct[...], preferred_element_type=F32)

    def softmax(s_ref, p_ref):
        s = s_ref[...]
        p_ref[...] = jnp.exp(s - jnp.max(s, axis=-1, keepdims=True)).astype(BF16)

    def attend(i, p_ref):
        o = jnp.dot(p_ref[:, :nk], vx[key_rows(i), :], preferred_element_type=F32)
        o += jnp.dot(p_ref[:, nk:], vcx[...], preferred_element_type=F32)
        o = o[:, :lb] / o[:, lb:lb + 1]
        o_ref[i * nq:(i + 1) * nq, :] = jnp.where(first, o[:nq], o[nq:]).astype(o_ref.dtype)

    scores(0, s_even)
    softmax(s_even, p_even)
    scores(1, s_odd)
    for j in range(1, n_blocks // 2):
        attend(2 * j - 2, p_even)
        softmax(s_odd, p_odd)
        scores(2 * j, s_even)
        attend(2 * j - 1, p_odd)
        softmax(s_even, p_even)
        scores(2 * j + 1, s_odd)
    attend(n_blocks - 2, p_even)
    softmax(s_odd, p_odd)
    attend(n_blocks - 1, p_odd)


def _natten(p, tiles, *, n_batch, seq, ctx_len):
    d = NA_HEADS * NA_HEAD_DIM
    lb = 2 * NA_HEAD_DIM
    n_pairs = NA_HEADS // 2
    ctx_blk0 = n_batch * seq // ctx_len
    rows = seq // GRID_W
    assert rows % (2 * NA_QROWS) == 0 and rows >= NA_KROWS + NA_QROWS
    stacked = 2 * NA_QROWS * GRID_W
    n_keys = NA_KROWS * GRID_W + ctx_len
    lat = lambda part: pl.BlockSpec((seq, lb), lambda j, b: (b, part * n_pairs + j))
    ctx = lambda part: pl.BlockSpec((ctx_len, lb), lambda j, b: (ctx_blk0 + b, part * n_pairs + j))
    return pl.pallas_call(
        functools.partial(_na_kernel, rows=rows),
        grid=(n_pairs, n_batch),
        in_specs=[lat(0), lat(1), lat(2), ctx(1), ctx(2),
                  pl.BlockSpec((2,) + tiles.shape[1:], lambda j, b: (j, 0, 0, 0))],
        out_specs=pl.BlockSpec((seq, lb), lambda j, b: (b, j)),
        out_shape=jax.ShapeDtypeStruct((n_batch * seq, d), BF16),
        scratch_shapes=[pltpu.VMEM((3, stacked, NA_KROWS * GRID_W), F32),
                        pltpu.VMEM((lb, seq), BF16), pltpu.VMEM((lb, ctx_len), BF16),
                        pltpu.VMEM((seq, 2 * lb), BF16), pltpu.VMEM((ctx_len, 2 * lb), BF16),
                        pltpu.VMEM((stacked, n_keys), F32), pltpu.VMEM((stacked, n_keys), F32),
                        pltpu.VMEM((stacked, n_keys), BF16), pltpu.VMEM((stacked, n_keys), BF16)],
        compiler_params=_cparams("arbitrary", "arbitrary"),
        name="natten",
    )(p, p, p, p, p, tiles)


def kernel(x, c, ctx, c_ctx, w_mod, b_mod, norm_g, ffn_w1, ffn_w2, w_in_ab, w_out_ab, ret_decay_logit, s5_lam_re, s5_lam_im, s5_log_dt, s5_b_re, s5_b_im, s5_c_re, s5_c_im, s5_d, s5_glu_w, s5_glu_b, na_w_qkv, na_w_o, na_rpb, final_g):
    n_batch, seq, d = x.shape
    ctx_len = ctx.shape[1]
    depth = w_mod.shape[0]
    n_lat = n_batch * seq
    n_all = n_lat + n_batch * ctx_len
    lat_tiles = seq // TOKEN_TILE
    assert seq % TOKEN_TILE == 0 and (n_batch * ctx_len) % TOKEN_TILE == 0
    assert n_batch + 1 <= MOD_ROWS and seq % (GRID_W * NA_KH) == 0

    cvec = jnp.concatenate([c, c_ctx[None], jnp.zeros((MOD_ROWS - n_batch - 1, d), F32)], axis=0)
    mod = _modulation(cvec, w_mod, b_mod).reshape(depth, MOD_ROWS, N_MOD, d)
    h_parts = (x.reshape(n_lat, d), ctx.reshape(n_batch * ctx_len, d))
    common = dict(lat_tiles=lat_tiles, n_batch=n_batch)
    dims = dict(n_batch=n_batch, seq=seq, ctx_len=ctx_len)
    gains = norm_g.astype(F32).reshape(depth, 3, 1, d)
    w1 = ffn_w1.astype(BF16)
    w2 = ffn_w2.astype(BF16)

    for layer in range(depth):
        last = layer == depth - 1
        i = layer // 2
        half = functools.partial(_half_layer, mod=mod[layer], gains=gains, w1=w1, w2=w2, layer=layer,
                                 **common)
        if layer % 2 == 0:
            h, pq, pg, pv = half(h_parts, n_rows=n_all, post="ab", post_w=w_in_ab.astype(BF16),
                                 post_wi=i, rope=_rope_tables(seq))
            r_parts = _retention(pq, pg, ret_decay_logit[i], **dims)
            weights = _s5_weights(s5_lam_re[i], s5_lam_im[i], s5_log_dt[i], s5_b_re[i], s5_b_im[i],
                                  s5_c_re[i], s5_c_im[i], s5_d[i])
            ys_parts = _s5(pv, weights, **dims)
            pre = dict(pre="ab", pre_args=(r_parts, ys_parts, s5_glu_w.astype(BF16),
                                           s5_glu_b.astype(F32)[:, None, :], w_out_ab.astype(BF16), i))
        else:
            assert last
            h, p = half(h_parts, n_rows=n_all, post="na", post_w=na_w_qkv.astype(BF16), post_wi=i)
            att = _natten(p, _na_bias_tiles(na_rpb[i]), **dims)
            pre = dict(pre="na", pre_args=(att, na_w_o.astype(BF16), i))
        (h,) = half((h,), n_rows=n_lat if last else n_all, final_g=final_g if last else None, **pre)
        h_parts = (h,)
    return h[:n_lat].reshape(n_batch, seq, d)
```

```python
import functools
import math
from typing import NamedTuple

import numpy as np
import jax
import jax.numpy as jnp
from jax import lax
from jax.experimental import pallas as pl
from jax.experimental.pallas import tpu as pltpu

F32 = jnp.float32
BF16 = jnp.bfloat16

EPS = 1e-6
ROPE_BASE = 10000.0
GRID_W = 64
N_MOD = 9
RET_HEADS = 4
RET_HEAD_DIM = 128
RET_CHUNK = 256
S5_GROUP = 16
S5_STATE = 64
S5_CHUNK = 16
S5_TILE = 8
NA_HEADS = 16
NA_HEAD_DIM = 64
NA_KH = 8
NA_KW = 16
NEG_INF = -1e30

TOKEN_TILE = 512
VMEM_LIMIT = 56 * 1024 * 1024
MOD_ROWS = 8


def _cparams(*sem):
    return pltpu.CompilerParams(dimension_semantics=sem, vmem_limit_bytes=VMEM_LIMIT)


def _resident(shape):
    nd = len(shape)
    return pl.BlockSpec(shape, lambda *_: (0,) * nd, pipeline_mode=pl.Buffered(1))


def _split_bf16(x):
    hi = x.astype(BF16)
    return hi, (x - hi.astype(F32)).astype(BF16)


def _mod_kernel(c_ref, w_ref, b_ref, o_ref):
    c = c_ref[...]
    s_hi, s_lo = _split_bf16(c * jax.nn.sigmoid(c))
    w_hi, w_lo = _split_bf16(w_ref[0])
    acc = jnp.dot(s_hi, w_lo, preferred_element_type=F32)
    acc += jnp.dot(s_lo, w_hi, preferred_element_type=F32)
    acc += jnp.dot(s_hi, w_hi, preferred_element_type=F32)
    o_ref[0] = acc + b_ref[0]


def _modulation(cvec, w_mod, b_mod):
    depth, d, nd = w_mod.shape
    tn = nd // 4 if nd % 512 == 0 else d
    return pl.pallas_call(
        _mod_kernel,
        grid=(depth, nd // tn),
        in_specs=[
            pl.BlockSpec((MOD_ROWS, d), lambda l, j: (0, 0)),
            pl.BlockSpec((1, d, tn), lambda l, j: (l, 0, j)),
            pl.BlockSpec((1, 1, tn), lambda l, j: (l, 0, j)),
        ],
        out_specs=pl.BlockSpec((1, MOD_ROWS, tn), lambda l, j: (l, 0, j)),
        out_shape=jax.ShapeDtypeStruct((depth, MOD_ROWS, nd), F32),
        compiler_params=_cparams("parallel", "parallel"),
        name="modulation",
    )(cvec, w_mod, b_mod.reshape(depth, 1, nd))


def _rms(x):
    return x * lax.rsqrt(jnp.mean(x * x, axis=-1, keepdims=True) + EPS)


def _modulated(h, m, g, mi):
    return (_rms(h) * g) * (1.0 + m[mi + 1:mi + 2]) + m[mi:mi + 1]


def _tile_specs(n_lat_tiles_per_batch, n_batch, d):
    def mod_idx(i):
        return (jnp.minimum(i // n_lat_tiles_per_batch, n_batch), 0, 0)
    h_spec = pl.BlockSpec((TOKEN_TILE, d), lambda i: (i, 0))
    m_spec = pl.BlockSpec((1, N_MOD, d), mod_idx)
    return h_spec, m_spec


def _pinned(block_shape, index):
    return pl.BlockSpec(block_shape, lambda *_: index, pipeline_mode=pl.Buffered(1))


def _stream_specs(parts, width, rows=TOKEN_TILE):
    if len(parts) == 1:
        return [pl.BlockSpec((rows, width), lambda i: (i, 0))], 0
    n0 = parts[0].shape[0] // rows
    return [pl.BlockSpec((rows, width), lambda i: (jnp.minimum(i, n0 - 1), 0)),
            pl.BlockSpec((rows, width), lambda i: (jnp.maximum(i - n0, 0), 0))], n0


def _stream_tile(refs, n0):
    if len(refs) == 1:
        return refs[0][...]
    return jnp.where(pl.program_id(0) < n0, refs[0][...], refs[1][...])


class _HalfCfg(NamedTuple):
    n_h: int
    n0: int
    pre: str
    n_r: int
    nr0: int
    n_y: int
    ny0: int
    post: str
    final: bool


def _gelu_tanh(y):
    return 0.5 * y * (1.0 + jnp.tanh(math.sqrt(2.0 / math.pi) * (y + 0.044715 * (y * y * y))))


S5_CHUNKS_PER_TILE = TOKEN_TILE // S5_CHUNK


def _to_chunk_rows(u_scr, out_ref):
    half = 128 // S5_GROUP
    gw = S5_CHUNK * S5_GROUP
    for blk in range(u_scr.shape[0]):
        for th in range(S5_CHUNK // half):
            arrs = [u_scr[blk, pl.ds(th * half + tl, S5_CHUNKS_PER_TILE, stride=S5_CHUNK), :]
                    for tl in range(half)]
            for g, x in enumerate(_block_transpose(arrs)):
                lo = (blk * half + g) * gw + th * 128
                out_ref[:, lo:lo + 128] = x


def _from_chunk_rows(y, y_scr):
    half = 128 // S5_GROUP
    gw = S5_CHUNK * S5_GROUP
    for blk in range(y_scr.shape[0]):
        for th in range(S5_CHUNK // half):
            arrs = [y[:, (blk * half + g) * gw + th * 128:(blk * half + g) * gw + (th + 1) * 128]
                    for g in range(half)]
            for tl, x in enumerate(_block_transpose(arrs)):
                y_scr[blk, pl.ds(th * half + tl, S5_CHUNKS_PER_TILE, stride=S5_CHUNK), :] = x


def _half_kernel(*refs, cfg):
    it = iter(refs)
    take = lambda n: [next(it) for _ in range(n)]
    h_refs = take(cfg.n_h)
    m_ref, g_ref = take(2)
    h = _stream_tile(h_refs, cfg.n0)
    m = m_ref[0]
    gate_mix = m[5:6]
    relayout_scr = refs[-1]
    if cfg.pre == "ab":
        r_refs = take(cfg.n_r)
        y_refs = take(cfg.n_y)
        gw_ref, gb_ref, wr_ref, ws_ref = take(4)
        h = h + gate_mix * jnp.dot(_stream_tile(r_refs, cfg.nr0), wr_ref[...],
                                   preferred_element_type=F32)
        _from_chunk_rows(_stream_tile(y_refs, cfg.ny0), relayout_scr)
        g = _gelu_tanh(jnp.concatenate([relayout_scr[j] for j in range(relayout_scr.shape[0])], axis=1))
        s = g * jax.nn.sigmoid(jnp.dot(g.astype(BF16), gw_ref[...], preferred_element_type=F32)
                               + gb_ref[...])
        h = h + gate_mix * jnp.dot(s.astype(BF16), ws_ref[...], preferred_element_type=F32)
    elif cfg.pre == "na":
        a_ref, wo_ref = take(2)
        h = h + gate_mix * jnp.dot(a_ref[...], wo_ref[...], preferred_element_type=F32)
    w1a_ref, w1b_ref, w2_ref = take(3)
    k = 1 if cfg.pre else 0
    mi = 6 * k
    xm = _modulated(h, m, g_ref[2 * k], mi).astype(BF16)
    a = jnp.dot(xm, w1a_ref[...], preferred_element_type=F32)
    b = jnp.dot(xm, w1b_ref[...], preferred_element_type=F32)
    hid = (a * jax.nn.sigmoid(a) * b).astype(BF16)
    h = h + (0.5 * m[mi + 2:mi + 3]) * jnp.dot(hid, w2_ref[...], preferred_element_type=F32)
    if cfg.post:
        (wp_ref,) = take(1)
    if cfg.post == "ab":
        cos_ref, sa_ref, sb_ref = take(3)
    if cfg.final:
        (fg_ref,) = take(1)
    outs = list(it)
    if "ab" in (cfg.pre, cfg.post):
        outs.pop()
    outs[0][...] = _rms(h) * fg_ref[...] if cfg.final else h
    if not cfg.post:
        return
    xm = _modulated(h, m, g_ref[1], 3).astype(BF16)
    if cfg.post == "na":
        outs[1][...] = jnp.dot(xm, wp_ref[...], preferred_element_type=F32).astype(BF16)
        return
    width = RET_HEADS * RET_HEAD_DIM
    pr_u = jnp.dot(xm, wp_ref[:, 4 * width:], preferred_element_type=F32)
    for j in range(relayout_scr.shape[0]):
        relayout_scr[j] = pr_u[:, j * 128:(j + 1) * 128]
    _to_chunk_rows(relayout_scr, outs[3])
    pr_qk = jnp.dot(xm, wp_ref[:, :2 * width], preferred_element_type=F32)
    cos, sa, sb = cos_ref[...], sa_ref[...], sb_ref[...]
    for hh in range(RET_HEADS):
        lo = hh * RET_HEAD_DIM
        hi = lo + RET_HEAD_DIM
        outs[1][:, lo:hi] = _rope(pr_qk[:, lo:hi], cos, sa, sb).astype(BF16)
        kr = _rope(pr_qk[:, width + lo:width + hi], cos, sa, sb) * (RET_HEAD_DIM ** -0.5)
        outs[1][:, width + lo:width + hi] = kr.astype(BF16)
    pr_vg = jnp.dot(xm, wp_ref[:, 2 * width:4 * width], preferred_element_type=F32)
    outs[1][:, 2 * width:] = pr_vg[:, :width].astype(BF16)
    outs[2][...] = pr_vg[:, width:]


def _half_layer(h_parts, mod, gains, w1, w2, *, layer, n_rows, lat_tiles, n_batch,
                pre="", pre_args=(), post="", post_w=None, post_wi=0, rope=None, final_g=None):
    d = h_parts[0].shape[1]
    f = w2.shape[2]
    k = 1 if pre else 0
    h_specs, n0 = _stream_specs(h_parts, d)
    _, m_spec = _tile_specs(lat_tiles, n_batch, d)
    in_specs = h_specs + [m_spec, _pinned((None, 3, 1, d), (layer, 0, 0, 0))]
    args = list(h_parts) + [mod, gains]
    n_r = nr0 = n_y = ny0 = 0
    scratch = []
    if pre == "ab":
        r_parts, ys_parts, glu_w, glu_b, w_out, wi = pre_args
        w = r_parts[0].shape[1]
        r_specs, nr0 = _stream_specs(r_parts, w)
        y_specs, ny0 = _stream_specs(ys_parts, S5_CHUNK * w, rows=S5_CHUNKS_PER_TILE)
        n_r, n_y = len(r_parts), len(ys_parts)
        scratch = [pltpu.VMEM((w // 128, TOKEN_TILE, 128), F32)]
        in_specs += r_specs + y_specs + [
            _pinned((None, w, w), (wi, 0, 0)), _pinned((None, 1, w), (wi, 0, 0)),
            _pinned((None, w, d), (wi, 0, 0)), _pinned((None, w, d), (wi, 1, 0))]
        args += [*r_parts, *ys_parts, glu_w, glu_b, w_out, w_out]
    elif pre == "na":
        att, w_o, wi = pre_args
        in_specs += [pl.BlockSpec((TOKEN_TILE, d), lambda i: (i, 0)), _pinned((None, d, d), (wi, 0, 0))]
        args += [att, w_o]
    in_specs += [_pinned((None, None, d, f), (layer, k, 0, 0)),
                 _pinned((None, None, d, f), (layer, k, 0, 1)),
                 _pinned((None, None, f, d), (layer, k, 0, 0))]
    args += [w1, w1, w2]
    out_specs = [pl.BlockSpec((TOKEN_TILE, d), lambda i: (i, 0))]
    out_shape = [jax.ShapeDtypeStruct((n_rows, d), F32)]
    if post:
        n = post_w.shape[2]
        in_specs.append(_pinned((None, d, n), (post_wi, 0, 0)))
        args.append(post_w)
    if post == "ab":
        n_lat_tiles = lat_tiles * n_batch
        tab = pl.BlockSpec((TOKEN_TILE, RET_HEAD_DIM),
                           lambda i: (jnp.where(i < n_lat_tiles, i % lat_tiles, lat_tiles), 0))
        in_specs += [tab, tab, tab]
        args += list(rope)
        width = RET_HEADS * RET_HEAD_DIM
        n_u = n - 4 * width
        out_specs += [pl.BlockSpec((TOKEN_TILE, 3 * width), lambda i: (i, 0)),
                      pl.BlockSpec((TOKEN_TILE, width), lambda i: (i, 0)),
                      pl.BlockSpec((S5_CHUNKS_PER_TILE, S5_CHUNK * n_u), lambda i: (i, 0))]
        out_shape += [jax.ShapeDtypeStruct((n_rows, 3 * width), BF16),
                      jax.ShapeDtypeStruct((n_rows, width), F32),
                      jax.ShapeDtypeStruct((n_rows // S5_CHUNK, S5_CHUNK * n_u), F32)]
        scratch = [pltpu.VMEM((n_u // 128, TOKEN_TILE, 128), F32)]
    elif post == "na":
        out_specs.append(pl.BlockSpec((TOKEN_TILE, n), lambda i: (i, 0)))
        out_shape.append(jax.ShapeDtypeStruct((n_rows, n), BF16))
    if final_g is not None:
        in_specs.append(_resident((1, d)))
        args.append(final_g.reshape(1, d))
    cfg = _HalfCfg(n_h=len(h_parts), n0=n0, pre=pre, n_r=n_r, nr0=nr0, n_y=n_y, ny0=ny0, post=post,
                   final=final_g is not None)
    return pl.pallas_call(
        functools.partial(_half_kernel, cfg=cfg),
        grid=(n_rows // TOKEN_TILE,),
        in_specs=in_specs,
        out_specs=out_specs,
        out_shape=out_shape,
        scratch_shapes=scratch,
        compiler_params=_cparams("parallel"),
        name="half_layer",
    )(*args)


def _rope_tables(seq):
    half = RET_HEAD_DIM // 2
    quarter = half // 2
    inv = ROPE_BASE ** (-np.arange(0, half, 2, dtype=np.float64) / half)
    t = np.arange(seq)
    lane = np.arange(RET_HEAD_DIM)
    pos = np.where(lane[None, :] < half, (t // GRID_W)[:, None], (t % GRID_W)[:, None])
    ang = pos.astype(np.float32).astype(np.float64) * inv.astype(np.float32)[lane % quarter][None, :]
    first = (lane % half) < quarter
    cos = np.cos(ang)
    sin = np.sin(ang)
    sa = np.where(first[None, :], -sin, 0.0)
    sb = np.where(first[None, :], 0.0, sin)
    pad1 = np.ones((TOKEN_TILE, RET_HEAD_DIM))
    pad0 = np.zeros((TOKEN_TILE, RET_HEAD_DIM))
    tabs = [np.concatenate([cos, pad1]), np.concatenate([sa, pad0]), np.concatenate([sb, pad0])]
    return [jnp.asarray(x, F32) for x in tabs]


def _rope(x, cos, sa, sb):
    quarter = RET_HEAD_DIM // 4
    up = pltpu.roll(x, RET_HEAD_DIM - quarter, axis=1)
    dn = pltpu.roll(x, quarter, axis=1)
    return x * cos + up * sa + dn * sb


RET_STEP_CHUNKS = 4


def _retention_kernel(dec_ref, ql_ref, kl_ref, vl_ref, qc_ref, kc_ref, vc_ref, gl_ref, gc_ref,
                      dmask_ref, qwf_ref, qwb_ref, kwf_ref, kwb_ref, ol_ref, oc_ref,
                      s_run, s_bwd, *, n_lat, n_ctx):
    i = pl.program_id(1)
    hd = RET_HEAD_DIM
    c = RET_CHUNK
    tn = (((0,), (0,)), ((), ()))
    nt = (((1,), (1,)), ((), ()))

    def bwd_chunk(k_ref, v_ref, rows, cid):
        for h in range(RET_HEADS):
            sl = slice(h * hd, (h + 1) * hd)
            s_old = s_run[h]
            s_bwd[cid, h] = s_old
            kw = (k_ref[rows, sl] * kwb_ref[h]).astype(BF16)
            kv = lax.dot_general(kw, v_ref[rows, sl], tn, preferred_element_type=F32)
            s_run[h] = dec_ref[1, h] * s_old + kv

    def fwd_chunk(q_ref, k_ref, v_ref, rows, cid, g_ref, o_ref, blk_rows):
        for h in range(RET_HEADS):
            sl = slice(h * hd, (h + 1) * hd)
            q = q_ref[rows, sl]
            k = k_ref[rows, sl]
            v = v_ref[rows, sl]
            s_old = s_run[h]
            a = lax.dot_general(q, k, nt, preferred_element_type=F32)
            o = jnp.dot((a * dmask_ref[h]).astype(BF16), v, preferred_element_type=F32)
            o += qwf_ref[h] * jnp.dot(q, s_old.astype(BF16), preferred_element_type=F32)
            o += qwb_ref[h] * jnp.dot(q, s_bwd[cid, h].astype(BF16), preferred_element_type=F32)
            kv = lax.dot_general((k * kwf_ref[h]).astype(BF16), v, tn, preferred_element_type=F32)
            s_run[h] = dec_ref[0, h] * s_old + kv
            o = o * lax.rsqrt(jnp.mean(o * o, axis=-1, keepdims=True) + EPS)
            g = g_ref[blk_rows, sl]
            o_ref[blk_rows, sl] = (o * (g * jax.nn.sigmoid(g))).astype(o_ref.dtype)

    def lat_rows(j):
        return pl.ds(pl.multiple_of(j * c, c), c)

    @pl.when(i == 0)
    def _():
        s_run[...] = jnp.zeros_like(s_run)
        for j in reversed(range(n_ctx)):
            bwd_chunk(kc_ref, vc_ref, slice(j * c, (j + 1) * c), j)

        def body(t, carry):
            j = n_lat - 1 - t
            bwd_chunk(kl_ref, vl_ref, lat_rows(j), n_ctx + j)
            return carry

        lax.fori_loop(0, n_lat, body, 0)
        s_run[...] = jnp.zeros_like(s_run)

    for j in range(n_ctx):
        @pl.when(i == 1 + j)
        def _(j=j):
            rows = slice(j * c, (j + 1) * c)
            fwd_chunk(qc_ref, kc_ref, vc_ref, rows, j, gc_ref, oc_ref, rows)

    @pl.when(i > n_ctx)
    def _():
        first = (i - 1 - n_ctx) * RET_STEP_CHUNKS
        for e in range(RET_STEP_CHUNKS):
            fwd_chunk(ql_ref, kl_ref, vl_ref, lat_rows(first + e), n_ctx + first + e,
                      gl_ref, ol_ref, slice(e * c, (e + 1) * c))


def _retention(pq, pg, decay_logit, *, n_batch, seq, ctx_len):
    c = RET_CHUNK
    width = RET_HEADS * RET_HEAD_DIM
    n_lat = seq // c
    n_ctx = ctx_len // c
    n_ch = n_lat + n_ctx

    log_gamma = jax.nn.log_sigmoid(decay_logit.astype(F32))
    pos = jnp.arange(c, dtype=F32)
    diff = pos[:, None] - pos[None, :]
    lf = log_gamma[0][:, None, None]
    lb = log_gamma[1][:, None, None]
    dmask = (jnp.where(diff >= 0, jnp.exp(lf * jnp.maximum(diff, 0.0)), 0.0)
             + jnp.where(diff <= 0, jnp.exp(lb * jnp.maximum(-diff, 0.0)), 0.0))
    ones = jnp.ones((1, 1, RET_HEAD_DIM), F32)
    col = pos[None, :, None]
    qwf = jnp.exp(lf * (col + 1.0)) * ones
    qwb = jnp.exp(lb * (c - col)) * ones
    kwf = jnp.exp(lf * (c - 1.0 - col)) * ones
    kwb = jnp.exp(lb * col) * ones
    chunk_decay = jnp.exp(log_gamma * c)

    ctx_blk0 = n_batch * seq // ctx_len
    step_rows = RET_STEP_CHUNKS * c
    lat_steps = n_lat // RET_STEP_CHUNKS
    assert n_lat % RET_STEP_CHUNKS == 0

    lat_step = pl.BlockSpec((step_rows, width),
                            lambda b, i: (b * lat_steps + jnp.maximum(i - 1 - n_ctx, 0), 0))
    lat = lambda part: pl.BlockSpec((seq, width), lambda b, i: (b, part))
    ctx = lambda part: pl.BlockSpec((ctx_len, width), lambda b, i: (ctx_blk0 + b, part))
    head_tab = _resident((RET_HEADS, c, RET_HEAD_DIM))
    kern = functools.partial(_retention_kernel, n_lat=n_lat, n_ctx=n_ctx)
    return pl.pallas_call(
        kern,
        grid=(n_batch, 1 + n_ctx + lat_steps),
        in_specs=[
            pl.BlockSpec(memory_space=pltpu.SMEM),
            lat(0), lat(1), lat(2), ctx(0), ctx(1), ctx(2), lat_step, ctx(0),
            _resident((RET_HEADS, c, c)), head_tab, head_tab, head_tab, head_tab,
        ],
        out_specs=[lat_step, pl.BlockSpec((ctx_len, width), lambda b, i: (b, 0))],
        out_shape=[jax.ShapeDtypeStruct((n_batch * seq, width), BF16),
                   jax.ShapeDtypeStruct((n_batch * ctx_len, width), BF16)],
        scratch_shapes=[
            pltpu.VMEM((RET_HEADS, RET_HEAD_DIM, RET_HEAD_DIM), F32),
            pltpu.VMEM((n_ch, RET_HEADS, RET_HEAD_DIM, RET_HEAD_DIM), F32),
        ],
        compiler_params=_cparams("parallel", "arbitrary"),
        name="retention",
    )(chunk_decay, pq, pq, pq, pq, pq, pq, pg, pg, dmask, qwf, qwb, kwf, kwb)


def _cmul(a, b):
    return a[0] * b[0] - a[1] * b[1], a[0] * b[1] + a[1] * b[0]


def _pair_blockdiag(a):
    a0, a1 = a[0::2], a[1::2]
    z = jnp.zeros_like(a0)
    return jnp.concatenate([jnp.concatenate([a0, z], axis=2), jnp.concatenate([z, a1], axis=2)],
                           axis=1)


def _pair_lanes(a):
    return jnp.concatenate([a[0::2], a[1::2]], axis=2)


def _powers(a, n):
    one = (jnp.ones_like(a[0]), jnp.zeros_like(a[0]))
    pw = (jnp.stack([one[0], a[0]], axis=1), jnp.stack([one[1], a[1]], axis=1))
    step = a
    while pw[0].shape[1] < n:
        step = _cmul(step, step)
        nxt = _cmul(pw, (step[0][:, None], step[1][:, None]))
        pw = (jnp.concatenate([pw[0], nxt[0]], axis=1), jnp.concatenate([pw[1], nxt[1]], axis=1))
    top = _cmul(step, step)
    return (jnp.concatenate([pw[0], top[0][:, None]], axis=1),
            jnp.concatenate([pw[1], top[1][:, None]], axis=1))


def _s5_weights(lam_re, lam_im, log_dt, b_re, b_im, c_re, c_im, d_skip):
    q = S5_CHUNK
    n_g = lam_re.shape[1]
    hi = lax.Precision.HIGHEST
    kall = None
    zw = []
    rw = []
    tabs = []
    for d in range(2):
        lr = jnp.minimum(lam_re[d].astype(F32), -1e-4)
        li = lam_im[d].astype(F32)
        dt = jnp.exp(log_dt[d].astype(F32))[:, None]
        mag = jnp.exp(lr * dt)
        a = (mag * jnp.cos(li * dt), mag * jnp.sin(li * dt))
        den = lr * lr + li * li
        am1 = (a[0] - 1.0, a[1])
        coef = ((am1[0] * lr + am1[1] * li) / den, (am1[1] * lr - am1[0] * li) / den)
        bt = (b_re[d].astype(F32).transpose(0, 2, 1), b_im[d].astype(F32).transpose(0, 2, 1))
        bb = _cmul((coef[0][:, None], coef[1][:, None]), bt)
        cm = (c_re[d].astype(F32), c_im[d].astype(F32))
        pw_re, pw_im = _powers(a, q)
        e = _cmul((pw_re[:, :q, None, :], pw_im[:, :q, None, :]),
                  (bb[0][:, None], bb[1][:, None]))
        taps = (jnp.einsum('gkp,gtjp->gjtk', cm[0], e[0], precision=hi)
                - jnp.einsum('gkp,gtjp->gjtk', cm[1], e[1], precision=hi))
        if d == 0:
            kall = jnp.concatenate([jnp.zeros_like(taps[:, :, 1:]), taps], axis=2)
        else:
            kall = kall + jnp.concatenate([taps[:, :, ::-1], jnp.zeros_like(taps[:, :, 1:])], axis=2)
        tsel = np.arange(q)[::-1] if d == 0 else np.arange(q)
        z = _cmul((pw_re[:, tsel][:, :, None, :], pw_im[:, tsel][:, :, None, :]),
                  (bb[0][:, None], bb[1][:, None]))
        zw.append((z[0].reshape(n_g, q * S5_GROUP, S5_STATE),
                   z[1].reshape(n_g, q * S5_GROUP, S5_STATE)))
        rsel = np.arange(1, q + 1) if d == 0 else np.arange(q, 0, -1)
        ct = (cm[0].transpose(0, 2, 1)[:, :, None, :], cm[1].transpose(0, 2, 1)[:, :, None, :])
        pt = (pw_re[:, rsel].transpose(0, 2, 1)[..., None], pw_im[:, rsel].transpose(0, 2, 1)[..., None])
        r = _cmul(ct, pt)
        rw.append((r[0].reshape(n_g, S5_STATE, q * S5_GROUP),
                   (-r[1]).reshape(n_g, S5_STATE, q * S5_GROUP)))
        a16 = (pw_re[:, q], pw_im[:, q])
        ramp_re, ramp_im = _powers(a16, S5_TILE)
        a32 = (ramp_re[:, 2], ramp_im[:, 2])
        a64 = (ramp_re[:, 4], ramp_im[:, 4])
        a128 = (ramp_re[:, 8], ramp_im[:, 8])
        rsl = slice(0, S5_TILE) if d == 0 else slice(S5_TILE - 1, None, -1)
        rows = [a16[0], a16[1], a32[0], a32[1], a64[0], a64[1], a128[0], a128[1]]
        tabs.append(jnp.concatenate([jnp.stack(rows, axis=1), ramp_re[:, rsl], ramp_im[:, rsl]],
                                    axis=1))
    dsk = d_skip.astype(F32).reshape(n_g, S5_GROUP)
    eye = jnp.eye(S5_GROUP, dtype=F32)
    kall = kall.at[:, :, q - 1].add(dsk[:, :, None] * eye[None])
    kall = jnp.concatenate([kall, jnp.zeros_like(kall[:, :, :1])], axis=2)
    kflat = kall.reshape(n_g, S5_GROUP, 2 * q * S5_GROUP)
    wz = jnp.concatenate([_pair_blockdiag(z) for pair in zw for z in pair], axis=2)
    w2 = jnp.concatenate([_pair_blockdiag(r) for pair in rw for r in pair], axis=1)
    tab = jnp.concatenate([_pair_lanes(t) for t in tabs], axis=1)
    return kflat, wz.astype(BF16), w2.astype(BF16), tab


def _tile_shift(x, s, down):
    row = lax.broadcasted_iota(jnp.int32, x.shape, 1)
    if down:
        return jnp.where(row >= s, pltpu.roll(x, s, axis=1), 0.0)
    return jnp.where(row < S5_TILE - s, pltpu.roll(x, S5_TILE - s, axis=1), 0.0)


S5_LANE_GROUPS = 128 // S5_GROUP


def _block_transpose(arrs):
    lane = lax.broadcasted_iota(jnp.int32, arrs[0].shape, 1)
    blk = lane // S5_GROUP
    a = list(arrs)
    for bit in range(3):
        s = 1 << bit
        hi = (blk & s) != 0
        new = list(a)
        for i in range(S5_LANE_GROUPS):
            if i & s:
                continue
            new[i] = jnp.where(hi, pltpu.roll(a[i + s], S5_GROUP * s, axis=1), a[i])
            new[i + s] = jnp.where(hi, a[i + s], pltpu.roll(a[i], 128 - S5_GROUP * s, axis=1))
        a = new
    return a


def _s5_kernel(vl_ref, vc_ref, kf_ref, wz_ref, w2_ref, tab_ref, yl_ref, yc_ref, m_scr, zx, xin,
               *, lat_tiles, ctx_tiles):
    q = S5_CHUNK
    lanes = 2 * S5_STATE
    pw = 2 * q * S5_GROUP
    gw = q * S5_GROUP

    @pl.when(pl.program_id(1) == 0)
    def _():
        m_scr[...] = jnp.zeros_like(m_scr)
        for g in range(S5_LANE_GROUPS):
            taps = kf_ref[g]
            base = (g % 2) * gw
            for ti in range(q):
                lo = S5_GROUP * (q - 1 - ti)
                m_scr[g // 2, base + ti * S5_GROUP:base + (ti + 1) * S5_GROUP, base:base + gw] = (
                    taps[:, lo:lo + gw].astype(BF16))

    n_tiles = ctx_tiles + lat_tiles
    ctx_rows = ctx_tiles * S5_TILE

    for p in range(m_scr.shape[0]):
        cols = slice(p * pw, (p + 1) * pw)
        v = jnp.concatenate([vc_ref[:, cols], vl_ref[:, cols]], axis=0).astype(BF16)
        y_intra = jnp.dot(v, m_scr[p], preferred_element_type=F32)
        y = jnp.dot(v, wz_ref[p], preferred_element_type=F32)

        def tab_row(r):
            return tab_ref[p, r:r + 1, :]

        for d in range(2):
            down = d == 0
            t0 = 24 * d
            lo = 2 * d * lanes
            z = (y[:, lo:lo + lanes].reshape(n_tiles, S5_TILE, lanes),
                 y[:, lo + lanes:lo + 2 * lanes].reshape(n_tiles, S5_TILE, lanes))
            loc = (_tile_shift(z[0], 1, down), _tile_shift(z[1], 1, down))
            for k, s in enumerate((1, 2, 4)):
                mul = (tab_row(t0 + 2 * k)[None], tab_row(t0 + 2 * k + 1)[None])
                inc = _cmul(mul, (_tile_shift(loc[0], s, down), _tile_shift(loc[1], s, down)))
                loc = (loc[0] + inc[0], loc[1] + inc[1])
            zx[2 * d] = loc[0]
            zx[2 * d + 1] = loc[1]
            zx[4 + 2 * d] = z[0]
            zx[4 + 2 * d + 1] = z[1]

        for d in range(2):
            t0 = 24 * d
            edge = S5_TILE - 1 if d == 0 else 0
            a16 = (tab_row(t0), tab_row(t0 + 1))
            a128 = (tab_row(t0 + 6), tab_row(t0 + 7))
            ramp = (tab_ref[p, t0 + 8:t0 + 16, :], tab_ref[p, t0 + 16:t0 + 24, :])

            def tile_step(j, carry, d=d, edge=edge, a16=a16, a128=a128, ramp=ramp):
                loc = (zx[2 * d, j], zx[2 * d + 1, j])
                z = (zx[4 + 2 * d, j], zx[4 + 2 * d + 1, j])
                inc = _cmul(ramp, carry)
                xin[j, :, (2 * d) * lanes:(2 * d + 1) * lanes] = loc[0] + inc[0]
                xin[j, :, (2 * d + 1) * lanes:(2 * d + 2) * lanes] = loc[1] + inc[1]
                e_loc = _cmul(a16, (loc[0][edge:edge + 1], loc[1][edge:edge + 1]))
                nxt = _cmul(a128, carry)
                return (nxt[0] + e_loc[0] + z[0][edge:edge + 1],
                        nxt[1] + e_loc[1] + z[1][edge:edge + 1])

            zero = (jnp.zeros((1, lanes), F32), jnp.zeros((1, lanes), F32))
            if d == 0:
                lax.fori_loop(0, n_tiles, tile_step, zero)
            else:
                mid = lax.fori_loop(0, ctx_tiles, lambda t, c: tile_step(ctx_tiles - 1 - t, c), zero)
                lax.fori_loop(0, lat_tiles, lambda t, c: tile_step(n_tiles - 1 - t, c), mid)

        x = xin[...].reshape(n_tiles * S5_TILE, 4 * lanes)
        y_all = y_intra + jnp.dot(x.astype(BF16), w2_ref[p], preferred_element_type=F32)
        yc_ref[:, cols] = y_all[:ctx_rows]
        yl_ref[:, cols] = y_all[ctx_rows:]


def _s5(v, weights, *, n_batch, seq, ctx_len):
    kflat, wz, w2, tab = weights
    q = S5_CHUNK
    lanes = 2 * S5_STATE
    pw = 2 * q * S5_GROUP
    n_pairs = w2.shape[0]
    ppb = S5_LANE_GROUPS // 2
    n_blocks = n_pairs // ppb
    bw = ppb * pw
    lat_rows = seq // q
    ctx_rows = ctx_len // q
    n_tiles = (lat_rows + ctx_rows) // S5_TILE
    ctx_blk0 = n_batch * lat_rows // ctx_rows
    kern = functools.partial(_s5_kernel, lat_tiles=lat_rows // S5_TILE, ctx_tiles=ctx_rows // S5_TILE)
    return pl.pallas_call(
        kern,
        grid=(n_blocks, n_batch),
        in_specs=[
            pl.BlockSpec((lat_rows, bw), lambda j, b: (b, j)),
            pl.BlockSpec((ctx_rows, bw), lambda j, b: (ctx_blk0 + b, j)),
            pl.BlockSpec((S5_LANE_GROUPS,) + kflat.shape[1:], lambda j, b: (j, 0, 0)),
            pl.BlockSpec((ppb, pw, 4 * lanes), lambda j, b: (j, 0, 0)),
            pl.BlockSpec((ppb, 4 * lanes, pw), lambda j, b: (j, 0, 0)),
            pl.BlockSpec((ppb, tab.shape[1], lanes), lambda j, b: (j, 0, 0)),
        ],
        out_specs=[pl.BlockSpec((lat_rows, bw), lambda j, b: (b, j)),
                   pl.BlockSpec((ctx_rows, bw), lambda j, b: (b, j))],
        out_shape=[jax.ShapeDtypeStruct((n_batch * lat_rows, n_blocks * bw), F32),
                   jax.ShapeDtypeStruct((n_batch * ctx_rows, n_blocks * bw), F32)],
        scratch_shapes=[
            pltpu.VMEM((ppb, pw, pw), BF16),
            pltpu.VMEM((8, n_tiles, S5_TILE, lanes), F32),
            pltpu.VMEM((n_tiles, S5_TILE, 4 * lanes), F32),
        ],
        compiler_params=_cparams("arbitrary", "arbitrary"),
        name="s5",
    )(v, v, kflat, wz, w2, tab)


NA_QROWS = 4
NA_KROWS = NA_QROWS + NA_KH


NA_REL_ROWS = 2 * NA_KH - 1


def _na_tile_index(kind, a, m):
    first, rel0 = ((0, NA_KH - 1 - a), (a, NA_QROWS - 1 - a), (NA_QROWS, -1 - a))[kind]
    return rel0 + m if first <= m < first + NA_KH else NA_REL_ROWS


def _na_bias_tiles(rpb):
    w = GRID_W
    qcol = np.arange(w)
    kcol = np.arange(w)
    wstart = np.clip(qcol - NA_KW // 2, 0, w - NA_KW)
    valid = (kcol[None, :] >= wstart[:, None]) & (kcol[None, :] < wstart[:, None] + NA_KW)
    rel = np.clip(kcol[None, :] - qcol[:, None], -(NA_KW - 1), NA_KW - 1) + NA_KW - 1
    onehot = (rel[None] == np.arange(2 * NA_KW - 1)[:, None, None]).astype(np.float32)
    tiles = jnp.einsum('hrj,jqk->hrqk', rpb.astype(F32), jnp.asarray(onehot),
                       precision=lax.Precision.HIGHEST)
    tiles = jnp.where(jnp.asarray(valid)[None, None], tiles, NEG_INF)
    return jnp.concatenate([tiles, jnp.full((NA_HEADS, 1, w, w), NEG_INF, F32)], axis=1)


def _na_kernel(q_ref, k_ref, v_ref, kc_ref, vc_ref, tiles_ref, o_ref, bias_ref, kt, kct, vx, vcx,
               s_even, s_odd, p_even, p_odd, *, rows):
    w = GRID_W
    dh = NA_HEAD_DIM
    lb = 2 * dh
    nq = NA_QROWS * w
    nk = NA_KROWS * w
    n_blocks = rows // NA_QROWS

    @pl.when(pl.program_id(1) == 0)
    def _():
        for kind in range(3):
            for hh in range(2):
                for a in range(NA_QROWS):
                    for m in range(0, NA_KROWS, 2):
                        pair = [tiles_ref[hh, _na_tile_index(kind, a, m + e)] for e in range(2)]
                        bias_ref[kind, hh * nq + a * w:hh * nq + (a + 1) * w, m * w:(m + 2) * w] = (
                            jnp.concatenate(pair, axis=1))

    scale = dh ** -0.5
    lane = lax.broadcasted_iota(jnp.int32, (nq, lb), 1)
    first = lane < dh

    kt[...] = k_ref[...].T
    kct[...] = kc_ref[...].T

    for dst, src in ((vx, v_ref), (vcx, vc_ref)):
        n = src.shape[0]
        dst[:, :lb] = src[...]
        dst[:, lb:] = (lax.broadcasted_iota(jnp.int32, (n, lb), 1) == 0).astype(BF16)

    def key_rows(i):
        r0 = min(max(NA_QROWS * i - NA_KH // 2, 0), rows - NA_KROWS)
        return slice(r0 * w, r0 * w + nk)

    def scores(i, s_ref):
        kind = 0 if i == 0 else (2 if i == n_blocks - 1 else 1)
        q = q_ref[i * nq:(i + 1) * nq, :] * scale
        zero = jnp.zeros_like(q)
        qs = jnp.concatenate([jnp.where(first, q, zero), jnp.where(first, zero, q)], axis=0)
        s_ref[:, :nk] = jnp.dot(qs, kt[:, key_rows(i)], preferred_element_type=F32) + bias_ref[kind]
        s_ref[:, nk:] = jnp.dot(qs, kct[...], preferred_element_type=F32)

    def softmax(s_ref, p_ref):
        s = s_ref[...]
        p_ref[...] = jnp.exp(s - jnp.max(s, axis=-1, keepdims=True)).astype(BF16)

    def attend(i, p_ref):
        o = jnp.dot(p_ref[:, :nk], vx[key_rows(i), :], preferred_element_type=F32)
        o += jnp.dot(p_ref[:, nk:], vcx[...], preferred_element_type=F32)
        o = o[:, :lb] / o[:, lb:lb + 1]
        o_ref[i * nq:(i + 1) * nq, :] = jnp.where(first, o[:nq], o[nq:]).astype(o_ref.dtype)

    scores(0, s_even)
    softmax(s_even, p_even)
    scores(1, s_odd)
    for j in range(1, n_blocks // 2):
        attend(2 * j - 2, p_even)
        softmax(s_odd, p_odd)
        scores(2 * j, s_even)
        attend(2 * j - 1, p_odd)
        softmax(s_even, p_even)
        scores(2 * j + 1, s_odd)
    attend(n_blocks - 2, p_even)
    softmax(s_odd, p_odd)
    attend(n_blocks - 1, p_odd)


def _natten(p, tiles, *, n_batch, seq, ctx_len):
    d = NA_HEADS * NA_HEAD_DIM
    lb = 2 * NA_HEAD_DIM
    n_pairs = NA_HEADS // 2
    ctx_blk0 = n_batch * seq // ctx_len
    rows = seq // GRID_W
    assert rows % (2 * NA_QROWS) == 0 and rows >= NA_KROWS + NA_QROWS
    stacked = 2 * NA_QROWS * GRID_W
    n_keys = NA_KROWS * GRID_W + ctx_len
    lat = lambda part: pl.BlockSpec((seq, lb), lambda j, b: (b, part * n_pairs + j))
    ctx = lambda part: pl.BlockSpec((ctx_len, lb), lambda j, b: (ctx_blk0 + b, part * n_pairs + j))
    return pl.pallas_call(
        functools.partial(_na_kernel, rows=rows),
        grid=(n_pairs, n_batch),
        in_specs=[lat(0), lat(1), lat(2), ctx(1), ctx(2),
                  pl.BlockSpec((2,) + tiles.shape[1:], lambda j, b: (j, 0, 0, 0))],
        out_specs=pl.BlockSpec((seq, lb), lambda j, b: (b, j)),
        out_shape=jax.ShapeDtypeStruct((n_batch * seq, d), BF16),
        scratch_shapes=[pltpu.VMEM((3, stacked, NA_KROWS * GRID_W), F32),
                        pltpu.VMEM((lb, seq), BF16), pltpu.VMEM((lb, ctx_len), BF16),
                        pltpu.VMEM((seq, 2 * lb), BF16), pltpu.VMEM((ctx_len, 2 * lb), BF16),
                        pltpu.VMEM((stacked, n_keys), F32), pltpu.VMEM((stacked, n_keys), F32),
                        pltpu.VMEM((stacked, n_keys), BF16), pltpu.VMEM((stacked, n_keys), BF16)],
        compiler_params=_cparams("arbitrary", "arbitrary"),
        name="natten",
    )(p, p, p, p, p, tiles)


def kernel(x, c, ctx, c_ctx, w_mod, b_mod, norm_g, ffn_w1, ffn_w2, w_in_ab, w_out_ab, ret_decay_logit, s5_lam_re, s5_lam_im, s5_log_dt, s5_b_re, s5_b_im, s5_c_re, s5_c_im, s5_d, s5_glu_w, s5_glu_b, na_w_qkv, na_w_o, na_rpb, final_g):
    n_batch, seq, d = x.shape
    ctx_len = ctx.shape[1]
    depth = w_mod.shape[0]
    n_lat = n_batch * seq
    n_all = n_lat + n_batch * ctx_len
    lat_tiles = seq // TOKEN_TILE
    assert seq % TOKEN_TILE == 0 and (n_batch * ctx_len) % TOKEN_TILE == 0
    assert n_batch + 1 <= MOD_ROWS and seq % (GRID_W * NA_KH) == 0

    cvec = jnp.concatenate([c, c_ctx[None], jnp.zeros((MOD_ROWS - n_batch - 1, d), F32)], axis=0)
    mod = _modulation(cvec, w_mod, b_mod).reshape(depth, MOD_ROWS, N_MOD, d)
    h_parts = (x.reshape(n_lat, d), ctx.reshape(n_batch * ctx_len, d))
    common = dict(lat_tiles=lat_tiles, n_batch=n_batch)
    dims = dict(n_batch=n_batch, seq=seq, ctx_len=ctx_len)
    gains = norm_g.astype(F32).reshape(depth, 3, 1, d)
    w1 = ffn_w1.astype(BF16)
    w2 = ffn_w2.astype(BF16)

    for layer in range(depth):
        last = layer == depth - 1
        i = layer // 2
        half = functools.partial(_half_layer, mod=mod[layer], gains=gains, w1=w1, w2=w2, layer=layer,
                                 **common)
        if layer % 2 == 0:
            h, pq, pg, pv = half(h_parts, n_rows=n_all, post="ab", post_w=w_in_ab.astype(BF16),
                                 post_wi=i, rope=_rope_tables(seq))
            r_parts = _retention(pq, pg, ret_decay_logit[i], **dims)
            weights = _s5_weights(s5_lam_re[i], s5_lam_im[i], s5_log_dt[i], s5_b_re[i], s5_b_im[i],
                                  s5_c_re[i], s5_c_im[i], s5_d[i])
            ys_parts = _s5(pv, weights, **dims)
            pre = dict(pre="ab", pre_args=(r_parts, ys_parts, s5_glu_w.astype(BF16),
                                           s5_glu_b.astype(F32)[:, None, :], w_out_ab.astype(BF16), i))
        else:
            assert last
            h, p = half(h_parts, n_rows=n_all, post="na", post_w=na_w_qkv.astype(BF16), post_wi=i)
            att = _natten(p, _na_bias_tiles(na_rpb[i]), **dims)
            pre = dict(pre="na", pre_args=(att, na_w_o.astype(BF16), i))
        (h,) = half((h,), n_rows=n_lat if last else n_all, final_g=final_g if last else None, **pre)
        h_parts = (h,)
    return h[:n_lat].reshape(n_batch, seq, d)
```

```python
import functools
import math
from typing import NamedTuple

import numpy as np
import jax
import jax.numpy as jnp
from jax import lax
from jax.experimental import pallas as pl
from jax.experimental.pallas import tpu as pltpu

F32 = jnp.float32
BF16 = jnp.bfloat16

EPS = 1e-6
ROPE_BASE = 10000.0
GRID_W = 64
N_MOD = 9
RET_HEADS = 4
RET_HEAD_DIM = 128
RET_CHUNK = 256
S5_GROUP = 16
S5_STATE = 64
S5_CHUNK = 16
S5_TILE = 8
NA_HEADS = 16
NA_HEAD_DIM = 64
NA_KH = 8
NA_KW = 16
NEG_INF = -1e30

TOKEN_TILE = 512
VMEM_LIMIT = 56 * 1024 * 1024
MOD_ROWS = 8


def _cparams(*sem):
    return pltpu.CompilerParams(dimension_semantics=sem, vmem_limit_bytes=VMEM_LIMIT)


def _resident(shape):
    nd = len(shape)
    return pl.BlockSpec(shape, lambda *_: (0,) * nd, pipeline_mode=pl.Buffered(1))


def _split_bf16(x):
    hi = x.astype(BF16)
    return hi, (x - hi.astype(F32)).astype(BF16)


def _mod_kernel(c_ref, w_ref, b_ref, o_ref):
    c = c_ref[...]
    s_hi, s_lo = _split_bf16(c * jax.nn.sigmoid(c))
    w_hi, w_lo = _split_bf16(w_ref[0])
    acc = jnp.dot(s_hi, w_lo, preferred_element_type=F32)
    acc += jnp.dot(s_lo, w_hi, preferred_element_type=F32)
    acc += jnp.dot(s_hi, w_hi, preferred_element_type=F32)
    o_ref[0] = acc + b_ref[0]


def _modulation(cvec, w_mod, b_mod):
    depth, d, nd = w_mod.shape
    tn = nd // 4 if nd % 512 == 0 else d
    return pl.pallas_call(
        _mod_kernel,
        grid=(depth, nd // tn),
        in_specs=[
            pl.BlockSpec((MOD_ROWS, d), lambda l, j: (0, 0)),
            pl.BlockSpec((1, d, tn), lambda l, j: (l, 0, j)),
            pl.BlockSpec((1, 1, tn), lambda l, j: (l, 0, j)),
        ],
        out_specs=pl.BlockSpec((1, MOD_ROWS, tn), lambda l, j: (l, 0, j)),
        out_shape=jax.ShapeDtypeStruct((depth, MOD_ROWS, nd), F32),
        compiler_params=_cparams("parallel", "parallel"),
        name="modulation",
    )(cvec, w_mod, b_mod.reshape(depth, 1, nd))


def _rms(x):
    return x * lax.rsqrt(jnp.mean(x * x, axis=-1, keepdims=True) + EPS)


def _modulated(h, m, g, mi):
    return (_rms(h) * g) * (1.0 + m[mi + 1:mi + 2]) + m[mi:mi + 1]


def _tile_specs(n_lat_tiles_per_batch, n_batch, d):
    def mod_idx(i):
        return (jnp.minimum(i // n_lat_tiles_per_batch, n_batch), 0, 0)
    h_spec = pl.BlockSpec((TOKEN_TILE, d), lambda i: (i, 0))
    m_spec = pl.BlockSpec((1, N_MOD, d), mod_idx)
    return h_spec, m_spec


def _pinned(block_shape, index):
    return pl.BlockSpec(block_shape, lambda *_: index, pipeline_mode=pl.Buffered(1))


def _stream_specs(parts, width, rows=TOKEN_TILE):
    if len(parts) == 1:
        return [pl.BlockSpec((rows, width), lambda i: (i, 0))], 0
    n0 = parts[0].shape[0] // rows
    return [pl.BlockSpec((rows, width), lambda i: (jnp.minimum(i, n0 - 1), 0)),
            pl.BlockSpec((rows, width), lambda i: (jnp.maximum(i - n0, 0), 0))], n0


def _stream_tile(refs, n0):
    if len(refs) == 1:
        return refs[0][...]
    return jnp.where(pl.program_id(0) < n0, refs[0][...], refs[1][...])


class _HalfCfg(NamedTuple):
    n_h: int
    n0: int
    pre: str
    n_r: int
    nr0: int
    n_y: int
    ny0: int
    post: str
    final: bool


def _gelu_tanh(y):
    return 0.5 * y * (1.0 + jnp.tanh(math.sqrt(2.0 / math.pi) * (y + 0.044715 * (y * y * y))))


S5_CHUNKS_PER_TILE = TOKEN_TILE // S5_CHUNK


def _to_chunk_rows(u_scr, out_ref):
    half = 128 // S5_GROUP
    gw = S5_CHUNK * S5_GROUP
    for blk in range(u_scr.shape[0]):
        for th in range(S5_CHUNK // half):
            arrs = [u_scr[blk, pl.ds(th * half + tl, S5_CHUNKS_PER_TILE, stride=S5_CHUNK), :]
                    for tl in range(half)]
            for g, x in enumerate(_block_transpose(arrs)):
                lo = (blk * half + g) * gw + th * 128
                out_ref[:, lo:lo + 128] = x


def _from_chunk_rows(y, y_scr):
    half = 128 // S5_GROUP
    gw = S5_CHUNK * S5_GROUP
    for blk in range(y_scr.shape[0]):
        for th in range(S5_CHUNK // half):
            arrs = [y[:, (blk * half + g) * gw + th * 128:(blk * half + g) * gw + (th + 1) * 128]
                    for g in range(half)]
            for tl, x in enumerate(_block_transpose(arrs)):
                y_scr[blk, pl.ds(th * half + tl, S5_CHUNKS_PER_TILE, stride=S5_CHUNK), :] = x


def _half_kernel(*refs, cfg):
    it = iter(refs)
    take = lambda n: [next(it) for _ in range(n)]
    h_refs = take(cfg.n_h)
    m_ref, g_ref = take(2)
    h = _stream_tile(h_refs, cfg.n0)
    m = m_ref[0]
    gate_mix = m[5:6]
    relayout_scr = refs[-1]
    if cfg.pre == "ab":
        r_refs = take(cfg.n_r)
        y_refs = take(cfg.n_y)
        gw_ref, gb_ref, wr_ref, ws_ref = take(4)
        h = h + gate_mix * jnp.dot(_stream_tile(r_refs, cfg.nr0), wr_ref[...],
                                   preferred_element_type=F32)
        _from_chunk_rows(_stream_tile(y_refs, cfg.ny0), relayout_scr)
        g = _gelu_tanh(jnp.concatenate([relayout_scr[j] for j in range(relayout_scr.shape[0])], axis=1))
        s = g * jax.nn.sigmoid(jnp.dot(g.astype(BF16), gw_ref[...], preferred_element_type=F32)
                               + gb_ref[...])
        h = h + gate_mix * jnp.dot(s.astype(BF16), ws_ref[...], preferred_element_type=F32)
    elif cfg.pre == "na":
        a_ref, wo_ref = take(2)
        h = h + gate_mix * jnp.dot(a_ref[...], wo_ref[...], preferred_element_type=F32)
    w1a_ref, w1b_ref, w2_ref = take(3)
    k = 1 if cfg.pre else 0
    mi = 6 * k
    xm = _modulated(h, m, g_ref[2 * k], mi).astype(BF16)
    a = jnp.dot(xm, w1a_ref[...], preferred_element_type=F32)
    b = jnp.dot(xm, w1b_ref[...], preferred_element_type=F32)
    hid = (a * jax.nn.sigmoid(a) * b).astype(BF16)
    h = h + (0.5 * m[mi + 2:mi + 3]) * jnp.dot(hid, w2_ref[...], preferred_element_type=F32)
    if cfg.post:
        (wp_ref,) = take(1)
    if cfg.post == "ab":
        cos_ref, sa_ref, sb_ref = take(3)
    if cfg.final:
        (fg_ref,) = take(1)
    outs = list(it)
    if "ab" in (cfg.pre, cfg.post):
        outs.pop()
    outs[0][...] = _rms(h) * fg_ref[...] if cfg.final else h
    if not cfg.post:
        return
    xm = _modulated(h, m, g_ref[1], 3).astype(BF16)
    if cfg.post == "na":
        outs[1][...] = jnp.dot(xm, wp_ref[...], preferred_element_type=F32).astype(BF16)
        return
    width = RET_HEADS * RET_HEAD_DIM
    pr_u = jnp.dot(xm, wp_ref[:, 4 * width:], preferred_element_type=F32)
    for j in range(relayout_scr.shape[0]):
        relayout_scr[j] = pr_u[:, j * 128:(j + 1) * 128]
    _to_chunk_rows(relayout_scr, outs[3])
    pr_qk = jnp.dot(xm, wp_ref[:, :2 * width], preferred_element_type=F32)
    cos, sa, sb = cos_ref[...], sa_ref[...], sb_ref[...]
    for hh in range(RET_HEADS):
        lo = hh * RET_HEAD_DIM
        hi = lo + RET_HEAD_DIM
        outs[1][:, lo:hi] = _rope(pr_qk[:, lo:hi], cos, sa, sb).astype(BF16)
        kr = _rope(pr_qk[:, width + lo:width + hi], cos, sa, sb) * (RET_HEAD_DIM ** -0.5)
        outs[1][:, width + lo:width + hi] = kr.astype(BF16)
    pr_vg = jnp.dot(xm, wp_ref[:, 2 * width:4 * width], preferred_element_type=F32)
    outs[1][:, 2 * width:] = pr_vg[:, :width].astype(BF16)
    outs[2][...] = pr_vg[:, width:]


def _half_layer(h_parts, mod, gains, w1, w2, *, layer, n_rows, lat_tiles, n_batch,
                pre="", pre_args=(), post="", post_w=None, post_wi=0, rope=None, final_g=None):
    d = h_parts[0].shape[1]
    f = w2.shape[2]
    k = 1 if pre else 0
    h_specs, n0 = _stream_specs(h_parts, d)
    _, m_spec = _tile_specs(lat_tiles, n_batch, d)
    in_specs = h_specs + [m_spec, _pinned((None, 3, 1, d), (layer, 0, 0, 0))]
    args = list(h_parts) + [mod, gains]
    n_r = nr0 = n_y = ny0 = 0
    scratch = []
    if pre == "ab":
        r_parts, ys_parts, glu_w, glu_b, w_out, wi = pre_args
        w = r_parts[0].shape[1]
        r_specs, nr0 = _stream_specs(r_parts, w)
        y_specs, ny0 = _stream_specs(ys_parts, S5_CHUNK * w, rows=S5_CHUNKS_PER_TILE)
        n_r, n_y = len(r_parts), len(ys_parts)
        scratch = [pltpu.VMEM((w // 128, TOKEN_TILE, 128), F32)]
        in_specs += r_specs + y_specs + [
            _pinned((None, w, w), (wi, 0, 0)), _pinned((None, 1, w), (wi, 0, 0)),
            _pinned((None, w, d), (wi, 0, 0)), _pinned((None, w, d), (wi, 1, 0))]
        args += [*r_parts, *ys_parts, glu_w, glu_b, w_out, w_out]
    elif pre == "na":
        att, w_o, wi = pre_args
        in_specs += [pl.BlockSpec((TOKEN_TILE, d), lambda i: (i, 0)), _pinned((None, d, d), (wi, 0, 0))]
        args += [att, w_o]
    in_specs += [_pinned((None, None, d, f), (layer, k, 0, 0)),
                 _pinned((None, None, d, f), (layer, k, 0, 1)),
                 _pinned((None, None, f, d), (layer, k, 0, 0))]
    args += [w1, w1, w2]
    out_specs = [pl.BlockSpec((TOKEN_TILE, d), lambda i: (i, 0))]
    out_shape = [jax.ShapeDtypeStruct((n_rows, d), F32)]
    if post:
        n = post_w.shape[2]
        in_specs.append(_pinned((None, d, n), (post_wi, 0, 0)))
        args.append(post_w)
    if post == "ab":
        n_lat_tiles = lat_tiles * n_batch
        tab = pl.BlockSpec((TOKEN_TILE, RET_HEAD_DIM),
                           lambda i: (jnp.where(i < n_lat_tiles, i % lat_tiles, lat_tiles), 0))
        in_specs += [tab, tab, tab]
        args += list(rope)
        width = RET_HEADS * RET_HEAD_DIM
        n_u = n - 4 * width
        out_specs += [pl.BlockSpec((TOKEN_TILE, 3 * width), lambda i: (i, 0)),
                      pl.BlockSpec((TOKEN_TILE, width), lambda i: (i, 0)),
                      pl.BlockSpec((S5_CHUNKS_PER_TILE, S5_CHUNK * n_u), lambda i: (i, 0))]
        out_shape += [jax.ShapeDtypeStruct((n_rows, 3 * width), BF16),
                      jax.ShapeDtypeStruct((n_rows, width), F32),
                      jax.ShapeDtypeStruct((n_rows // S5_CHUNK, S5_CHUNK * n_u), F32)]
        scratch = [pltpu.VMEM((n_u // 128, TOKEN_TILE, 128), F32)]
    elif post == "na":
        out_specs.append(pl.BlockSpec((TOKEN_TILE, n), lambda i: (i, 0)))
        out_shape.append(jax.ShapeDtypeStruct((n_rows, n), BF16))
    if final_g is not None:
        in_specs.append(_resident((1, d)))
        args.append(final_g.reshape(1, d))
    cfg = _HalfCfg(n_h=len(h_parts), n0=n0, pre=pre, n_r=n_r, nr0=nr0, n_y=n_y, ny0=ny0, post=post,
                   final=final_g is not None)
    return pl.pallas_call(
        functools.partial(_half_kernel, cfg=cfg),
        grid=(n_rows // TOKEN_TILE,),
        in_specs=in_specs,
        out_specs=out_specs,
        out_shape=out_shape,
        scratch_shapes=scratch,
        compiler_params=_cparams("parallel"),
        name="half_layer",
    )(*args)


def _rope_tables(seq):
    half = RET_HEAD_DIM // 2
    quarter = half // 2
    inv = ROPE_BASE ** (-np.arange(0, half, 2, dtype=np.float64) / half)
    t = np.arange(seq)
    lane = np.arange(RET_HEAD_DIM)
    pos = np.where(lane[None, :] < half, (t // GRID_W)[:, None], (t % GRID_W)[:, None])
    ang = pos.astype(np.float32).astype(np.float64) * inv.astype(np.float32)[lane % quarter][None, :]
    first = (lane % half) < quarter
    cos = np.cos(ang)
    sin = np.sin(ang)
    sa = np.where(first[None, :], -sin, 0.0)
    sb = np.where(first[None, :], 0.0, sin)
    pad1 = np.ones((TOKEN_TILE, RET_HEAD_DIM))
    pad0 = np.zeros((TOKEN_TILE, RET_HEAD_DIM))
    tabs = [np.concatenate([cos, pad1]), np.concatenate([sa, pad0]), np.concatenate([sb, pad0])]
    return [jnp.asarray(x, F32) for x in tabs]


def _rope(x, cos, sa, sb):
    quarter = RET_HEAD_DIM // 4
    up = pltpu.roll(x, RET_HEAD_DIM - quarter, axis=1)
    dn = pltpu.roll(x, quarter, axis=1)
    return x * cos + up * sa + dn * sb


RET_STEP_CHUNKS = 8


def _retention_kernel(dec_ref, ql_ref, kl_ref, vl_ref, qc_ref, kc_ref, vc_ref, gl_ref, gc_ref,
                      dmask_ref, qwf_ref, qwb_ref, kwf_ref, kwb_ref, ol_ref, oc_ref,
                      s_run, s_bwd, *, n_lat, n_ctx):
    i = pl.program_id(1)
    hd = RET_HEAD_DIM
    c = RET_CHUNK
    tn = (((0,), (0,)), ((), ()))
    nt = (((1,), (1,)), ((), ()))

    def bwd_chunk(k_ref, v_ref, rows, cid):
        for h in range(RET_HEADS):
            sl = slice(h * hd, (h + 1) * hd)
            s_old = s_run[h]
            s_bwd[cid, h] = s_old
            kw = (k_ref[rows, sl] * kwb_ref[h]).astype(BF16)
            kv = lax.dot_general(kw, v_ref[rows, sl], tn, preferred_element_type=F32)
            s_run[h] = dec_ref[1, h] * s_old + kv

    def fwd_chunk(q_ref, k_ref, v_ref, rows, cid, g_ref, o_ref, blk_rows):
        for h in range(RET_HEADS):
            sl = slice(h * hd, (h + 1) * hd)
            q = q_ref[rows, sl]
            k = k_ref[rows, sl]
            v = v_ref[rows, sl]
            s_old = s_run[h]
            a = lax.dot_general(q, k, nt, preferred_element_type=F32)
            o = jnp.dot((a * dmask_ref[h]).astype(BF16), v, preferred_element_type=F32)
            o += qwf_ref[h] * jnp.dot(q, s_old.astype(BF16), preferred_element_type=F32)
            o += qwb_ref[h] * jnp.dot(q, s_bwd[cid, h].astype(BF16), preferred_element_type=F32)
            kv = lax.dot_general((k * kwf_ref[h]).astype(BF16), v, tn, preferred_element_type=F32)
            s_run[h] = dec_ref[0, h] * s_old + kv
            o = o * lax.rsqrt(jnp.mean(o * o, axis=-1, keepdims=True) + EPS)
            g = g_ref[blk_rows, sl]
            o_ref[blk_rows, sl] = (o * (g * jax.nn.sigmoid(g))).astype(o_ref.dtype)

    def lat_rows(j):
        return pl.ds(pl.multiple_of(j * c, c), c)

    @pl.when(i == 0)
    def _():
        s_run[...] = jnp.zeros_like(s_run)
        for j in reversed(range(n_ctx)):
            bwd_chunk(kc_ref, vc_ref, slice(j * c, (j + 1) * c), j)

        def body(t, carry):
            j = n_lat - 1 - t
            bwd_chunk(kl_ref, vl_ref, lat_rows(j), n_ctx + j)
            return carry

        lax.fori_loop(0, n_lat, body, 0)
        s_run[...] = jnp.zeros_like(s_run)

    for j in range(n_ctx):
        @pl.when(i == 1 + j)
        def _(j=j):
            rows = slice(j * c, (j + 1) * c)
            fwd_chunk(qc_ref, kc_ref, vc_ref, rows, j, gc_ref, oc_ref, rows)

    @pl.when(i > n_ctx)
    def _():
        first = (i - 1 - n_ctx) * RET_STEP_CHUNKS
        for e in range(RET_STEP_CHUNKS):
            fwd_chunk(ql_ref, kl_ref, vl_ref, lat_rows(first + e), n_ctx + first + e,
                      gl_ref, ol_ref, slice(e * c, (e + 1) * c))


def _retention(pq, pg, decay_logit, *, n_batch, seq, ctx_len):
    c = RET_CHUNK
    width = RET_HEADS * RET_HEAD_DIM
    n_lat = seq // c
    n_ctx = ctx_len // c
    n_ch = n_lat + n_ctx

    log_gamma = jax.nn.log_sigmoid(decay_logit.astype(F32))
    pos = jnp.arange(c, dtype=F32)
    diff = pos[:, None] - pos[None, :]
    lf = log_gamma[0][:, None, None]
    lb = log_gamma[1][:, None, None]
    dmask = (jnp.where(diff >= 0, jnp.exp(lf * jnp.maximum(diff, 0.0)), 0.0)
             + jnp.where(diff <= 0, jnp.exp(lb * jnp.maximum(-diff, 0.0)), 0.0))
    ones = jnp.ones((1, 1, RET_HEAD_DIM), F32)
    col = pos[None, :, None]
    qwf = jnp.exp(lf * (col + 1.0)) * ones
    qwb = jnp.exp(lb * (c - col)) * ones
    kwf = jnp.exp(lf * (c - 1.0 - col)) * ones
    kwb = jnp.exp(lb * col) * ones
    chunk_decay = jnp.exp(log_gamma * c)

    ctx_blk0 = n_batch * seq // ctx_len
    step_rows = RET_STEP_CHUNKS * c
    lat_steps = n_lat // RET_STEP_CHUNKS
    assert n_lat % RET_STEP_CHUNKS == 0

    lat_step = pl.BlockSpec((step_rows, width),
                            lambda b, i: (b * lat_steps + jnp.maximum(i - 1 - n_ctx, 0), 0))
    lat = lambda part: pl.BlockSpec((seq, width), lambda b, i: (b, part))
    ctx = lambda part: pl.BlockSpec((ctx_len, width), lambda b, i: (ctx_blk0 + b, part))
    head_tab = _resident((RET_HEADS, c, RET_HEAD_DIM))
    kern = functools.partial(_retention_kernel, n_lat=n_lat, n_ctx=n_ctx)
    return pl.pallas_call(
        kern,
        grid=(n_batch, 1 + n_ctx + lat_steps),
        in_specs=[
            pl.BlockSpec(memory_space=pltpu.SMEM),
            lat(0), lat(1), lat(2), ctx(0), ctx(1), ctx(2), lat_step, ctx(0),
            _resident((RET_HEADS, c, c)), head_tab, head_tab, head_tab, head_tab,
        ],
        out_specs=[lat_step, pl.BlockSpec((ctx_len, width), lambda b, i: (b, 0))],
        out_shape=[jax.ShapeDtypeStruct((n_batch * seq, width), BF16),
                   jax.ShapeDtypeStruct((n_batch * ctx_len, width), BF16)],
        scratch_shapes=[
            pltpu.VMEM((RET_HEADS, RET_HEAD_DIM, RET_HEAD_DIM), F32),
            pltpu.VMEM((n_ch, RET_HEADS, RET_HEAD_DIM, RET_HEAD_DIM), F32),
        ],
        compiler_params=_cparams("parallel", "arbitrary"),
        name="retention",
    )(chunk_decay, pq, pq, pq, pq, pq, pq, pg, pg, dmask, qwf, qwb, kwf, kwb)


def _cmul(a, b):
    return a[0] * b[0] - a[1] * b[1], a[0] * b[1] + a[1] * b[0]


def _pair_blockdiag(a):
    a0, a1 = a[0::2], a[1::2]
    z = jnp.zeros_like(a0)
    return jnp.concatenate([jnp.concatenate([a0, z], axis=2), jnp.concatenate([z, a1], axis=2)],
                           axis=1)


def _pair_lanes(a):
    return jnp.concatenate([a[0::2], a[1::2]], axis=2)


def _powers(a, n):
    one = (jnp.ones_like(a[0]), jnp.zeros_like(a[0]))
    pw = (jnp.stack([one[0], a[0]], axis=1), jnp.stack([one[1], a[1]], axis=1))
    step = a
    while pw[0].shape[1] < n:
        step = _cmul(step, step)
        nxt = _cmul(pw, (step[0][:, None], step[1][:, None]))
        pw = (jnp.concatenate([pw[0], nxt[0]], axis=1), jnp.concatenate([pw[1], nxt[1]], axis=1))
    top = _cmul(step, step)
    return (jnp.concatenate([pw[0], top[0][:, None]], axis=1),
            jnp.concatenate([pw[1], top[1][:, None]], axis=1))


def _s5_weights(lam_re, lam_im, log_dt, b_re, b_im, c_re, c_im, d_skip):
    q = S5_CHUNK
    n_g = lam_re.shape[1]
    hi = lax.Precision.HIGHEST
    kall = None
    zw = []
    rw = []
    tabs = []
    for d in range(2):
        lr = jnp.minimum(lam_re[d].astype(F32), -1e-4)
        li = lam_im[d].astype(F32)
        dt = jnp.exp(log_dt[d].astype(F32))[:, None]
        mag = jnp.exp(lr * dt)
        a = (mag * jnp.cos(li * dt), mag * jnp.sin(li * dt))
        den = lr * lr + li * li
        am1 = (a[0] - 1.0, a[1])
        coef = ((am1[0] * lr + am1[1] * li) / den, (am1[1] * lr - am1[0] * li) / den)
        bt = (b_re[d].astype(F32).transpose(0, 2, 1), b_im[d].astype(F32).transpose(0, 2, 1))
        bb = _cmul((coef[0][:, None], coef[1][:, None]), bt)
        cm = (c_re[d].astype(F32), c_im[d].astype(F32))
        pw_re, pw_im = _powers(a, q)
        e = _cmul((pw_re[:, :q, None, :], pw_im[:, :q, None, :]),
                  (bb[0][:, None], bb[1][:, None]))
        taps = (jnp.einsum('gkp,gtjp->gjtk', cm[0], e[0], precision=hi)
                - jnp.einsum('gkp,gtjp->gjtk', cm[1], e[1], precision=hi))
        if d == 0:
            kall = jnp.concatenate([jnp.zeros_like(taps[:, :, 1:]), taps], axis=2)
        else:
            kall = kall + jnp.concatenate([taps[:, :, ::-1], jnp.zeros_like(taps[:, :, 1:])], axis=2)
        tsel = np.arange(q)[::-1] if d == 0 else np.arange(q)
        z = _cmul((pw_re[:, tsel][:, :, None, :], pw_im[:, tsel][:, :, None, :]),
                  (bb[0][:, None], bb[1][:, None]))
        zw.append((z[0].reshape(n_g, q * S5_GROUP, S5_STATE),
                   z[1].reshape(n_g, q * S5_GROUP, S5_STATE)))
        rsel = np.arange(1, q + 1) if d == 0 else np.arange(q, 0, -1)
        ct = (cm[0].transpose(0, 2, 1)[:, :, None, :], cm[1].transpose(0, 2, 1)[:, :, None, :])
        pt = (pw_re[:, rsel].transpose(0, 2, 1)[..., None], pw_im[:, rsel].transpose(0, 2, 1)[..., None])
        r = _cmul(ct, pt)
        rw.append((r[0].reshape(n_g, S5_STATE, q * S5_GROUP),
                   (-r[1]).reshape(n_g, S5_STATE, q * S5_GROUP)))
        a16 = (pw_re[:, q], pw_im[:, q])
        ramp_re, ramp_im = _powers(a16, S5_TILE)
        a32 = (ramp_re[:, 2], ramp_im[:, 2])
        a64 = (ramp_re[:, 4], ramp_im[:, 4])
        a128 = (ramp_re[:, 8], ramp_im[:, 8])
        rsl = slice(0, S5_TILE) if d == 0 else slice(S5_TILE - 1, None, -1)
        rows = [a16[0], a16[1], a32[0], a32[1], a64[0], a64[1], a128[0], a128[1]]
        tabs.append(jnp.concatenate([jnp.stack(rows, axis=1), ramp_re[:, rsl], ramp_im[:, rsl]],
                                    axis=1))
    dsk = d_skip.astype(F32).reshape(n_g, S5_GROUP)
    eye = jnp.eye(S5_GROUP, dtype=F32)
    kall = kall.at[:, :, q - 1].add(dsk[:, :, None] * eye[None])
    kall = jnp.concatenate([kall, jnp.zeros_like(kall[:, :, :1])], axis=2)
    kflat = kall.reshape(n_g, S5_GROUP, 2 * q * S5_GROUP)
    wz = jnp.concatenate([_pair_blockdiag(z) for pair in zw for z in pair], axis=2)
    w2 = jnp.concatenate([_pair_blockdiag(r) for pair in rw for r in pair], axis=1)
    tab = jnp.concatenate([_pair_lanes(t) for t in tabs], axis=1)
    return kflat, wz.astype(BF16), w2.astype(BF16), tab


def _tile_shift(x, s, down):
    row = lax.broadcasted_iota(jnp.int32, x.shape, 1)
    if down:
        return jnp.where(row >= s, pltpu.roll(x, s, axis=1), 0.0)
    return jnp.where(row < S5_TILE - s, pltpu.roll(x, S5_TILE - s, axis=1), 0.0)


S5_LANE_GROUPS = 128 // S5_GROUP


def _block_transpose(arrs):
    lane = lax.broadcasted_iota(jnp.int32, arrs[0].shape, 1)
    blk = lane // S5_GROUP
    a = list(arrs)
    for bit in range(3):
        s = 1 << bit
        hi = (blk & s) != 0
        new = list(a)
        for i in range(S5_LANE_GROUPS):
            if i & s:
                continue
            new[i] = jnp.where(hi, pltpu.roll(a[i + s], S5_GROUP * s, axis=1), a[i])
            new[i + s] = jnp.where(hi, a[i + s], pltpu.roll(a[i], 128 - S5_GROUP * s, axis=1))
        a = new
    return a


def _s5_kernel(vl_ref, vc_ref, kf_ref, wz_ref, w2_ref, tab_ref, yl_ref, yc_ref, m_scr, zx, xin,
               *, lat_tiles, ctx_tiles):
    q = S5_CHUNK
    lanes = 2 * S5_STATE
    pw = 2 * q * S5_GROUP
    gw = q * S5_GROUP

    @pl.when(pl.program_id(1) == 0)
    def _():
        m_scr[...] = jnp.zeros_like(m_scr)
        for g in range(S5_LANE_GROUPS):
            taps = kf_ref[g]
            base = (g % 2) * gw
            for ti in range(q):
                lo = S5_GROUP * (q - 1 - ti)
                m_scr[g // 2, base + ti * S5_GROUP:base + (ti + 1) * S5_GROUP, base:base + gw] = (
                    taps[:, lo:lo + gw].astype(BF16))

    n_tiles = ctx_tiles + lat_tiles
    ctx_rows = ctx_tiles * S5_TILE

    for p in range(m_scr.shape[0]):
        cols = slice(p * pw, (p + 1) * pw)
        v = jnp.concatenate([vc_ref[:, cols], vl_ref[:, cols]], axis=0).astype(BF16)
        y_intra = jnp.dot(v, m_scr[p], preferred_element_type=F32)
        y = jnp.dot(v, wz_ref[p], preferred_element_type=F32)

        def tab_row(r):
            return tab_ref[p, r:r + 1, :]

        for d in range(2):
            down = d == 0
            t0 = 24 * d
            lo = 2 * d * lanes
            z = (y[:, lo:lo + lanes].reshape(n_tiles, S5_TILE, lanes),
                 y[:, lo + lanes:lo + 2 * lanes].reshape(n_tiles, S5_TILE, lanes))
            loc = (_tile_shift(z[0], 1, down), _tile_shift(z[1], 1, down))
            for k, s in enumerate((1, 2, 4)):
                mul = (tab_row(t0 + 2 * k)[None], tab_row(t0 + 2 * k + 1)[None])
                inc = _cmul(mul, (_tile_shift(loc[0], s, down), _tile_shift(loc[1], s, down)))
                loc = (loc[0] + inc[0], loc[1] + inc[1])
            zx[2 * d] = loc[0]
            zx[2 * d + 1] = loc[1]
            zx[4 + 2 * d] = z[0]
            zx[4 + 2 * d + 1] = z[1]

        for d in range(2):
            t0 = 24 * d
            edge = S5_TILE - 1 if d == 0 else 0
            a16 = (tab_row(t0), tab_row(t0 + 1))
            a128 = (tab_row(t0 + 6), tab_row(t0 + 7))
            ramp = (tab_ref[p, t0 + 8:t0 + 16, :], tab_ref[p, t0 + 16:t0 + 24, :])

            def tile_step(j, carry, d=d, edge=edge, a16=a16, a128=a128, ramp=ramp):
                loc = (zx[2 * d, j], zx[2 * d + 1, j])
                z = (zx[4 + 2 * d, j], zx[4 + 2 * d + 1, j])
                inc = _cmul(ramp, carry)
                xin[j, :, (2 * d) * lanes:(2 * d + 1) * lanes] = loc[0] + inc[0]
                xin[j, :, (2 * d + 1) * lanes:(2 * d + 2) * lanes] = loc[1] + inc[1]
                e_loc = _cmul(a16, (loc[0][edge:edge + 1], loc[1][edge:edge + 1]))
                nxt = _cmul(a128, carry)
                return (nxt[0] + e_loc[0] + z[0][edge:edge + 1],
                        nxt[1] + e_loc[1] + z[1][edge:edge + 1])

            zero = (jnp.zeros((1, lanes), F32), jnp.zeros((1, lanes), F32))
            if d == 0:
                lax.fori_loop(0, n_tiles, tile_step, zero)
            else:
                mid = lax.fori_loop(0, ctx_tiles, lambda t, c: tile_step(ctx_tiles - 1 - t, c), zero)
                lax.fori_loop(0, lat_tiles, lambda t, c: tile_step(n_tiles - 1 - t, c), mid)

        x = xin[...].reshape(n_tiles * S5_TILE, 4 * lanes)
        y_all = y_intra + jnp.dot(x.astype(BF16), w2_ref[p], preferred_element_type=F32)
        yc_ref[:, cols] = y_all[:ctx_rows]
        yl_ref[:, cols] = y_all[ctx_rows:]


def _s5(v, weights, *, n_batch, seq, ctx_len):
    kflat, wz, w2, tab = weights
    q = S5_CHUNK
    lanes = 2 * S5_STATE
    pw = 2 * q * S5_GROUP
    n_pairs = w2.shape[0]
    ppb = S5_LANE_GROUPS // 2
    n_blocks = n_pairs // ppb
    bw = ppb * pw
    lat_rows = seq // q
    ctx_rows = ctx_len // q
    n_tiles = (lat_rows + ctx_rows) // S5_TILE
    ctx_blk0 = n_batch * lat_rows // ctx_rows
    kern = functools.partial(_s5_kernel, lat_tiles=lat_rows // S5_TILE, ctx_tiles=ctx_rows // S5_TILE)
    return pl.pallas_call(
        kern,
        grid=(n_blocks, n_batch),
        in_specs=[
            pl.BlockSpec((lat_rows, bw), lambda j, b: (b, j)),
            pl.BlockSpec((ctx_rows, bw), lambda j, b: (ctx_blk0 + b, j)),
            pl.BlockSpec((S5_LANE_GROUPS,) + kflat.shape[1:], lambda j, b: (j, 0, 0)),
            pl.BlockSpec((ppb, pw, 4 * lanes), lambda j, b: (j, 0, 0)),
            pl.BlockSpec((ppb, 4 * lanes, pw), lambda j, b: (j, 0, 0)),
            pl.BlockSpec((ppb, tab.shape[1], lanes), lambda j, b: (j, 0, 0)),
        ],
        out_specs=[pl.BlockSpec((lat_rows, bw), lambda j, b: (b, j)),
                   pl.BlockSpec((ctx_rows, bw), lambda j, b: (b, j))],
        out_shape=[jax.ShapeDtypeStruct((n_batch * lat_rows, n_blocks * bw), F32),
                   jax.ShapeDtypeStruct((n_batch * ctx_rows, n_blocks * bw), F32)],
        scratch_shapes=[
            pltpu.VMEM((ppb, pw, pw), BF16),
            pltpu.VMEM((8, n_tiles, S5_TILE, lanes), F32),
            pltpu.VMEM((n_tiles, S5_TILE, 4 * lanes), F32),
        ],
        compiler_params=_cparams("arbitrary", "arbitrary"),
        name="s5",
    )(v, v, kflat, wz, w2, tab)


NA_QROWS = 4
NA_KROWS = NA_QROWS + NA_KH


NA_REL_ROWS = 2 * NA_KH - 1


def _na_tile_index(kind, a, m):
    first, rel0 = ((0, NA_KH - 1 - a), (a, NA_QROWS - 1 - a), (NA_QROWS, -1 - a))[kind]
    return rel0 + m if first <= m < first + NA_KH else NA_REL_ROWS


def _na_bias_tiles(rpb):
    w = GRID_W
    qcol = np.arange(w)
    kcol = np.arange(w)
    wstart = np.clip(qcol - NA_KW // 2, 0, w - NA_KW)
    valid = (kcol[None, :] >= wstart[:, None]) & (kcol[None, :] < wstart[:, None] + NA_KW)
    rel = np.clip(kcol[None, :] - qcol[:, None], -(NA_KW - 1), NA_KW - 1) + NA_KW - 1
    onehot = (rel[None] == np.arange(2 * NA_KW - 1)[:, None, None]).astype(np.float32)
    tiles = jnp.einsum('hrj,jqk->hrqk', rpb.astype(F32), jnp.asarray(onehot),
                       precision=lax.Precision.HIGHEST)
    tiles = jnp.where(jnp.asarray(valid)[None, None], tiles, NEG_INF)
    return jnp.concatenate([tiles, jnp.full((NA_HEADS, 1, w, w), NEG_INF, F32)], axis=1)


def _na_kernel(q_ref, k_ref, v_ref, kc_ref, vc_ref, tiles_ref, o_ref, bias_ref, kt, kct, vx, vcx,
               s_even, s_odd, p_even, p_odd, *, rows):
    w = GRID_W
    dh = NA_HEAD_DIM
    lb = 2 * dh
    nq = NA_QROWS * w
    nk = NA_KROWS * w
    n_blocks = rows // NA_QROWS

    @pl.when(pl.program_id(1) == 0)
    def _():
        for kind in range(3):
            for hh in range(2):
                for a in range(NA_QROWS):
                    for m in range(0, NA_KROWS, 2):
                        pair = [tiles_ref[hh, _na_tile_index(kind, a, m + e)] for e in range(2)]
                        bias_ref[kind, hh * nq + a * w:hh * nq + (a + 1) * w, m * w:(m + 2) * w] = (
                            jnp.concatenate(pair, axis=1))

    scale = dh ** -0.5
    lane = lax.broadcasted_iota(jnp.int32, (nq, lb), 1)
    first = lane < dh

    kt[...] = k_ref[...].T
    kct[...] = kc_ref[...].T

    for dst, src in ((vx, v_ref), (vcx, vc_ref)):
        n = src.shape[0]
        dst[:, :lb] = src[...]
        dst[:, lb:] = (lax.broadcasted_iota(jnp.int32, (n, lb), 1) == 0).astype(BF16)

    def key_rows(i):
        r0 = min(max(NA_QROWS * i - NA_KH // 2, 0), rows - NA_KROWS)
        return slice(r0 * w, r0 * w + nk)

    def scores(i, s_ref):
        kind = 0 if i == 0 else (2 if i == n_blocks - 1 else 1)
        q = q_ref[i * nq:(i + 1) * nq, :] * scale
        zero = jnp.zeros_like(q)
        qs = jnp.concatenate([jnp.where(first, q, zero), jnp.where(first, zero, q)], axis=0)
        s_ref[:, :nk] = jnp.dot(qs, kt[:, key_rows(i)], preferred_element_type=F32) + bias_ref[kind]
        s_ref[:, nk:] = jnp.dot(qs, kct[...], preferred_element_type=F32)

    def softmax(s_ref, p_ref):
        s = s_ref[...]
        p_ref[...] = jnp.exp(s - jnp.max(s, axis=-1, keepdims=True)).astype(BF16)

    def attend(i, p_ref):
        o = jnp.dot(p_ref[:, :nk], vx[key_rows(i), :], preferred_element_type=F32)
        o += jnp.dot(p_ref[:, nk:], vcx[...], preferred_element_type=F32)
        o = o[:, :lb] / o[:, lb:lb + 1]
        o_ref[i * nq:(i + 1) * nq, :] = jnp.where(first, o[:nq], o[nq:]).astype(o_ref.dtype)

    scores(0, s_even)
    softmax(s_even, p_even)
    scores(1, s_odd)
    for j in range(1, n_blocks // 2):
        attend(2 * j - 2, p_even)
        softmax(s_odd, p_odd)
        scores(2 * j, s_even)
        attend(2 * j - 1, p_odd)
        softmax(s_even, p_even)
        scores(2 * j + 1, s_odd)
    attend(n_blocks - 2, p_even)
    softmax(s_odd, p_odd)
    attend(n_blocks - 1, p_odd)


def _natten(p, tiles, *, n_batch, seq, ctx_len):
    d = NA_HEADS * NA_HEAD_DIM
    lb = 2 * NA_HEAD_DIM
    n_pairs = NA_HEADS // 2
    ctx_blk0 = n_batch * seq // ctx_len
    rows = seq // GRID_W
    assert rows % (2 * NA_QROWS) == 0 and rows >= NA_KROWS + NA_QROWS
    stacked = 2 * NA_QROWS * GRID_W
    n_keys = NA_KROWS * GRID_W + ctx_len
    lat = lambda part: pl.BlockSpec((seq, lb), lambda j, b: (b, part * n_pairs + j))
    ctx = lambda part: pl.BlockSpec((ctx_len, lb), lambda j, b: (ctx_blk0 + b, part * n_pairs + j))
    return pl.pallas_call(
        functools.partial(_na_kernel, rows=rows),
        grid=(n_pairs, n_batch),
        in_specs=[lat(0), lat(1), lat(2), ctx(1), ctx(2),
                  pl.BlockSpec((2,) + tiles.shape[1:], lambda j, b: (j, 0, 0, 0))],
        out_specs=pl.BlockSpec((seq, lb), lambda j, b: (b, j)),
        out_shape=jax.ShapeDtypeStruct((n_batch * seq, d), BF16),
        scratch_shapes=[pltpu.VMEM((3, stacked, NA_KROWS * GRID_W), F32),
                        pltpu.VMEM((lb, seq), BF16), pltpu.VMEM((lb, ctx_len), BF16),
                        pltpu.VMEM((seq, 2 * lb), BF16), pltpu.VMEM((ctx_len, 2 * lb), BF16),
                        pltpu.VMEM((stacked, n_keys), F32), pltpu.VMEM((stacked, n_keys), F32),
                        pltpu.VMEM((stacked, n_keys), BF16), pltpu.VMEM((stacked, n_keys), BF16)],
        compiler_params=_cparams("arbitrary", "arbitrary"),
        name="natten",
    )(p, p, p, p, p, tiles)


def kernel(x, c, ctx, c_ctx, w_mod, b_mod, norm_g, ffn_w1, ffn_w2, w_in_ab, w_out_ab, ret_decay_logit, s5_lam_re, s5_lam_im, s5_log_dt, s5_b_re, s5_b_im, s5_c_re, s5_c_im, s5_d, s5_glu_w, s5_glu_b, na_w_qkv, na_w_o, na_rpb, final_g):
    n_batch, seq, d = x.shape
    ctx_len = ctx.shape[1]
    depth = w_mod.shape[0]
    n_lat = n_batch * seq
    n_all = n_lat + n_batch * ctx_len
    lat_tiles = seq // TOKEN_TILE
    assert seq % TOKEN_TILE == 0 and (n_batch * ctx_len) % TOKEN_TILE == 0
    assert n_batch + 1 <= MOD_ROWS and seq % (GRID_W * NA_KH) == 0

    cvec = jnp.concatenate([c, c_ctx[None], jnp.zeros((MOD_ROWS - n_batch - 1, d), F32)], axis=0)
    mod = _modulation(cvec, w_mod, b_mod).reshape(depth, MOD_ROWS, N_MOD, d)
    h_parts = (x.reshape(n_lat, d), ctx.reshape(n_batch * ctx_len, d))
    common = dict(lat_tiles=lat_tiles, n_batch=n_batch)
    dims = dict(n_batch=n_batch, seq=seq, ctx_len=ctx_len)
    gains = norm_g.astype(F32).reshape(depth, 3, 1, d)
    w1 = ffn_w1.astype(BF16)
    w2 = ffn_w2.astype(BF16)

    for layer in range(depth):
        last = layer == depth - 1
        i = layer // 2
        half = functools.partial(_half_layer, mod=mod[layer], gains=gains, w1=w1, w2=w2, layer=layer,
                                 **common)
        if layer % 2 == 0:
            h, pq, pg, pv = half(h_parts, n_rows=n_all, post="ab", post_w=w_in_ab.astype(BF16),
                                 post_wi=i, rope=_rope_tables(seq))
            r_parts = _retention(pq, pg, ret_decay_logit[i], **dims)
            weights = _s5_weights(s5_lam_re[i], s5_lam_im[i], s5_log_dt[i], s5_b_re[i], s5_b_im[i],
                                  s5_c_re[i], s5_c_im[i], s5_d[i])
            ys_parts = _s5(pv, weights, **dims)
            pre = dict(pre="ab", pre_args=(r_parts, ys_parts, s5_glu_w.astype(BF16),
                                           s5_glu_b.astype(F32)[:, None, :], w_out_ab.astype(BF16), i))
        else:
            assert last
            h, p = half(h_parts, n_rows=n_all, post="na", post_w=na_w_qkv.astype(BF16), post_wi=i)
            att = _natten(p, _na_bias_tiles(na_rpb[i]), **dims)
            pre = dict(pre="na", pre_args=(att, na_w_o.astype(BF16), i))
        (h,) = half((h,), n_rows=n_lat if last else n_all, final_g=final_g if last else None, **pre)
        h_parts = (h,)
    return h[:n_lat].reshape(n_batch, seq, d)
```
